```python
import math
import jax
import jax.numpy as jnp
from jax import lax
import numpy as np

D_MODEL = 1024
BATCH = 2
SEQ = 16384
DEPTH = 2

GRID_W = 64
CTX_LEN = 256
Q_BLOCK = 128
HEAD_DIM = 64
ROPE_THETA = 10000.0
A_HEADS = 4
A_KV_HEADS = 2
DIFF_HEADS = 4
DIFF_DIM = 32
WIN_HEADS = 4
WIN_KV_HEADS = 2
WINDOW = 128
RET_HEADS = 4
RET_DK = 64
RET_DV = 64
RET_CHUNK = 128
N_BRANCH = 4
BRANCH_W = 256
PIECES = (
    A_HEADS * HEAD_DIM, A_KV_HEADS * HEAD_DIM, A_KV_HEADS * HEAD_DIM,
    2 * DIFF_HEADS * DIFF_DIM, 2 * DIFF_HEADS * DIFF_DIM, DIFF_HEADS * 2 * DIFF_DIM,
    WIN_HEADS * HEAD_DIM, WIN_KV_HEADS * HEAD_DIM, WIN_KV_HEADS * HEAD_DIM,
    RET_HEADS * RET_DK, RET_HEADS * RET_DK, RET_HEADS * RET_DV, RET_HEADS * RET_DV,
)
MIX_COLS = sum(PIECES)
IN_COLS = MIX_COLS + N_BRANCH * D_MODEL
PEER_HEADS = 8
PEER_NK = 128
PEER_N = PEER_NK * PEER_NK
PEER_TOPK = 16
PEER_DQ = 128
PEER_BLOCK = 128
DEEPNORM_ALPHA = (2 * DEPTH) ** 0.25
DEEPNORM_BETA = (8 * DEPTH) ** -0.25
LN_EPS = 1e-5
RMS_EPS = 1e-6
F32 = jnp.float32

kernel_name = 'hybrid_dit_gqa_diff_window_retention_peer'


def layer_norm(x, p):
    xf = x.astype(F32)
    mu = jnp.mean(xf, -1, keepdims=True)
    var = jnp.mean(jnp.square(xf - mu), -1, keepdims=True)
    return ((xf - mu) * lax.rsqrt(var + LN_EPS) * p[0].astype(F32) + p[1].astype(F32)).astype(x.dtype)


def rms_norm(x, g):
    xf = x.astype(F32)
    return (xf * lax.rsqrt(jnp.mean(xf * xf, -1, keepdims=True) + RMS_EPS) * g.astype(F32)).astype(x.dtype)


def to_heads(t, n):
    b, l, _ = t.shape
    return t.reshape(b, l, n, -1).transpose(0, 2, 1, 3)


def from_heads(t):
    b, h, l, d = t.shape
    return t.transpose(0, 2, 1, 3).reshape(b, l, h * d)


def split_pieces(p):
    out, start = [], 0
    for w in PIECES:
        out.append(p[..., start:start + w])
        start += w
    return out


def axial_tables(row, col, d):
    quarter = d // 4
    inv = ROPE_THETA ** (-jnp.arange(quarter, dtype=F32) / quarter)
    ang = jnp.stack([row[:, None] * inv, col[:, None] * inv], axis=1)
    return jnp.cos(ang), jnp.sin(ang)


def apply_axial(x, cs):
    cos, sin = cs
    d = x.shape[-1]
    xs = x.astype(F32).reshape(x.shape[:-1] + (2, 2, d // 4))
    x1, x2 = xs[..., 0, :], xs[..., 1, :]
    out = jnp.stack([x1 * cos - x2 * sin, x2 * cos + x1 * sin], axis=-2)
    return out.reshape(x.shape).astype(x.dtype)


def rope1d_tables(pos, d):
    half = d // 2
    inv = ROPE_THETA ** (-jnp.arange(half, dtype=F32) / half)
    ang = pos[:, None] * inv
    return jnp.cos(ang), jnp.sin(ang)


def apply_rope1d(x, cs):
    cos, sin = cs
    xf = x.astype(F32)
    half = x.shape[-1] // 2
    x1, x2 = xf[..., :half], xf[..., half:]
    return jnp.concatenate([x1 * cos - x2 * sin, x2 * cos + x1 * sin], -1).astype(x.dtype)


def sink_softmax(s, sink):
    m = jnp.maximum(jnp.max(s, -1, keepdims=True), sink)
    e = jnp.exp(s - m)
    return e / (jnp.sum(e, -1, keepdims=True) + jnp.exp(sink - m))


def gqa_blocked(q, k, v):
    b, hq, s, d = q.shape
    hkv = k.shape[1]
    g = hq // hkv
    nb = s // Q_BLOCK
    scale = d ** -0.5
    qb = q.reshape(b, hkv, g, nb, Q_BLOCK, d).transpose(3, 0, 1, 2, 4, 5)

    def one(qblk):
        sc = jnp.einsum('bkgqd,bkld->bkgql', qblk, k, preferred_element_type=F32) * scale
        p = jax.nn.softmax(sc, axis=-1).astype(v.dtype)
        return jnp.einsum('bkgql,bkle->bkgqe', p, v)

    o = lax.map(one, qb)
    return o.transpose(1, 2, 3, 0, 4, 5).reshape(b, hq, s, v.shape[-1])


def diff_blocked(q1, q2, k1, k2, v, lam):
    b, h, s, d = q1.shape
    nb = s // Q_BLOCK
    scale = d ** -0.5

    def blocks(t):
        return t.reshape(b, h, nb, Q_BLOCK, d).transpose(2, 0, 1, 3, 4)

    def one(qs):
        a1, a2 = qs
        p1 = jax.nn.softmax(jnp.einsum('bhqd,bhkd->bhqk', a1, k1, preferred_element_type=F32) * scale, axis=-1)
        p2 = jax.nn.softmax(jnp.einsum('bhqd,bhkd->bhqk', a2, k2, preferred_element_type=F32) * scale, axis=-1)
        return jnp.einsum('bhqk,bhke->bhqe', (p1 - lam * p2).astype(v.dtype), v)

    o = lax.map(one, (blocks(q1), blocks(q2)))
    return o.transpose(1, 2, 0, 3, 4).reshape(b, h, s, v.shape[-1])


def window_attn(q, k, v, kc, vc, sink):
    b, hq, s, d = q.shape
    hkv = k.shape[1]
    g = hq // hkv
    nb = s // Q_BLOCK
    scale = d ** -0.5
    qb = q.reshape(b, hkv, g, nb, Q_BLOCK, d)

    def band(t):
        tp = jnp.pad(t, ((0, 0), (0, 0), (Q_BLOCK, Q_BLOCK), (0, 0)))
        tp = tp.reshape(b, hkv, nb + 2, Q_BLOCK, t.shape[-1])
        return jnp.concatenate([tp[:, :, :-2], tp[:, :, 1:-1], tp[:, :, 2:]], axis=3)

    kw, vw = band(k), band(v)
    qi = jnp.arange(Q_BLOCK)[:, None]
    kj = jnp.arange(3 * Q_BLOCK)[None, :]
    in_band = jnp.abs(Q_BLOCK + qi - kj) <= WINDOW
    kpos = (jnp.arange(nb)[:, None] - 1) * Q_BLOCK + kj
    mask = in_band[None] & ((kpos >= 0) & (kpos < s))[:, None, :]
    s_loc = jnp.einsum('bkgnqd,bknld->bkgnql', qb, kw, preferred_element_type=F32) * scale
    s_loc = jnp.where(mask, s_loc, -jnp.inf)
    s_ctx = jnp.einsum('bkgnqd,bkcd->bkgnqc', qb, kc, preferred_element_type=F32) * scale
    p = sink_softmax(jnp.concatenate([s_loc, s_ctx], -1), sink.astype(F32).reshape(1, hkv, g, 1, 1, 1))
    p = p.astype(v.dtype)
    nl = 3 * Q_BLOCK
    o = (jnp.einsum('bkgnql,bknle->bkgnqe', p[..., :nl], vw)
         + jnp.einsum('bkgnqc,bkce->bkgnqe', p[..., nl:], vc))
    return o.reshape(b, hq, s, v.shape[-1])


def ctx_sink_attn(q, kc, vc, sink):
    b, hq, l, d = q.shape
    hkv = kc.shape[1]
    g = hq // hkv
    sc = jnp.einsum('bkgqd,bkcd->bkgqc', q.reshape(b, hkv, g, l, d), kc, preferred_element_type=F32) * d ** -0.5
    p = sink_softmax(sc, sink.astype(F32).reshape(1, hkv, g, 1, 1)).astype(vc.dtype)
    return jnp.einsum('bkgqc,bkce->bkgqe', p, vc).reshape(b, hq, l, vc.shape[-1])


def retention_chunked(q, k, v, log_gamma, s0):
    b, h, l, _ = q.shape
    n = l // RET_CHUNK
    q, k, v = q.astype(F32), k.astype(F32), v.astype(F32)
    idx = jnp.arange(RET_CHUNK, dtype=F32)
    diff = idx[:, None] - idx[None, :]
    lg = log_gamma.astype(F32)
    dmat = jnp.exp(jnp.where(diff[None] >= 0, diff[None] * lg[:, None, None], -jnp.inf))
    xi = jnp.exp((idx + 1.0)[None] * lg[:, None])
    zeta = jnp.exp((RET_CHUNK - 1.0 - idx)[None] * lg[:, None])
    g_chunk = jnp.exp(RET_CHUNK * lg)

    def chunks(t):
        return t.reshape(b, h, n, RET_CHUNK, t.shape[-1]).transpose(2, 0, 1, 3, 4)

    def step(st, inp):
        qc, kc, vc = inp
        inner = jnp.einsum('bhid,bhjd->bhij', qc, kc) * dmat
        o = jnp.einsum('bhij,bhjv->bhiv', inner, vc) + jnp.einsum('bhid,bhdv->bhiv', qc, st) * xi[..., None]
        st = st * g_chunk[:, None, None] + jnp.einsum('bhjd,bhjv->bhdv', kc * zeta[..., None], vc)
        return st, o

    s_fin, o = lax.scan(step, s0.astype(F32), (chunks(q), chunks(k), chunks(v)))
    return o.transpose(1, 2, 0, 3, 4).reshape(b, h, l, v.shape[-1]), s_fin


def mixer_gqa(lat, cx, gain, ax, need_ctx):
    q = apply_axial(rms_norm(to_heads(lat[0], A_HEADS), gain[0]), ax)
    k = apply_axial(rms_norm(to_heads(lat[1], A_KV_HEADS), gain[1]), ax)
    v = to_heads(lat[2], A_KV_HEADS)
    kc = rms_norm(to_heads(cx[1], A_KV_HEADS), gain[1])
    vc = to_heads(cx[2], A_KV_HEADS)
    o = gqa_blocked(q, jnp.concatenate([kc, k], 2), jnp.concatenate([vc, v], 2))
    oc = None
    if need_ctx:
        oc = from_heads(gqa_blocked(rms_norm(to_heads(cx[0], A_HEADS), gain[0]), kc, vc))
    return from_heads(o), oc


def diff_heads(t):
    b, l, _ = t.shape
    t = t.reshape(b, l, DIFF_HEADS, 2, DIFF_DIM).transpose(3, 0, 2, 1, 4)
    return t[0], t[1]


def mixer_diff(lat, cx, lam_p, subln, layer, ax, need_ctx):
    lam_init = 0.8 - 0.6 * math.exp(-0.3 * layer)
    lp = lam_p.astype(F32)
    lam = jnp.exp(jnp.sum(lp[0] * lp[1])) - jnp.exp(jnp.sum(lp[2] * lp[3])) + lam_init
    q1, q2 = diff_heads(lat[0])
    k1, k2 = diff_heads(lat[1])
    q1, q2, k1, k2 = apply_axial(q1, ax), apply_axial(q2, ax), apply_axial(k1, ax), apply_axial(k2, ax)
    v = to_heads(lat[2], DIFF_HEADS)
    ck1, ck2 = diff_heads(cx[1])
    cv = to_heads(cx[2], DIFF_HEADS)

    def post(o):
        return from_heads(rms_norm(o, subln) * (1.0 - lam_init))

    o = diff_blocked(q1, q2, jnp.concatenate([ck1, k1], 2), jnp.concatenate([ck2, k2], 2),
                     jnp.concatenate([cv, v], 2), lam)
    oc = None
    if need_ctx:
        cq1, cq2 = diff_heads(cx[0])
        oc = post(diff_blocked(cq1, cq2, ck1, ck2, cv, lam))
    return post(o), oc


def mixer_window(lat, cx, sink, ax, need_ctx):
    q = apply_axial(to_heads(lat[0], WIN_HEADS), ax)
    k = apply_axial(to_heads(lat[1], WIN_KV_HEADS), ax)
    v = to_heads(lat[2], WIN_KV_HEADS)
    kc = to_heads(cx[1], WIN_KV_HEADS)
    vc = to_heads(cx[2], WIN_KV_HEADS)
    o = from_heads(window_attn(q, k, v, kc, vc, sink))
    oc = None
    if need_ctx:
        oc = from_heads(ctx_sink_attn(to_heads(cx[0], WIN_HEADS), kc, vc, sink))
    return o, oc


def mixer_retention(lat, cx, decay, norm, rope_lat, rope_ctx, need_ctx):
    lg = jax.nn.log_sigmoid(decay.astype(F32))

    def qkv(p, rope):
        q = apply_rope1d(to_heads(p[0], RET_HEADS), rope)
        k = apply_rope1d(to_heads(p[1], RET_HEADS), rope) * (RET_DK ** -0.5)
        return q, k, to_heads(p[2], RET_HEADS)

    def flip(t):
        return jnp.flip(t, axis=2)

    def finish(o, g):
        mu = jnp.mean(o, -1, keepdims=True)
        var = jnp.mean(jnp.square(o - mu), -1, keepdims=True)
        on = from_heads((o - mu) * lax.rsqrt(var + LN_EPS)) * norm[0].astype(F32) + norm[1].astype(F32)
        return (on * jax.nn.silu(g.astype(F32))).astype(g.dtype)

    qc, kc, vc = qkv(cx, rope_ctx)
    ql, kl, vl = qkv(lat, rope_lat)
    zero = jnp.zeros((qc.shape[0], RET_HEADS, RET_DK, RET_DV), F32)
    ocf, sf = retention_chunked(qc, kc, vc, lg[0], zero)
    ocb, sb = retention_chunked(flip(qc), flip(kc), flip(vc), lg[1], zero)
    olf, _ = retention_chunked(ql, kl, vl, lg[0], sf)
    olb, _ = retention_chunked(flip(ql), flip(kl), flip(vl), lg[1], sb)
    o = finish(olf + flip(olb), lat[3])
    oc = finish(ocf + flip(ocb), cx[3]) if need_ctx else None
    return o, oc


def merge_branches(u, outs, w_gate, w_branch, w_out):
    d = u.shape[-1]
    m = sum(jax.nn.sigmoid(u @ w_gate[:, i * d:(i + 1) * d]) * (o @ w_branch[i]) for i, o in enumerate(outs))
    return m @ w_out


def peer_ffn(u, wq, subkeys, tab_u, tab_v):
    b, l, d = u.shape
    tok = u.reshape(-1, PEER_BLOCK, d)

    def block(xb):
        q = (xb @ wq).reshape(PEER_BLOCK, PEER_HEADS, 2, PEER_DQ)
        s = jnp.einsum('thpk,hpnk->thpn', q, subkeys, preferred_element_type=F32)
        sv, si = lax.top_k(s, PEER_TOPK)
        cand = (sv[:, :, 0, :, None] + sv[:, :, 1, None, :]).reshape(PEER_BLOCK, PEER_HEADS, PEER_TOPK * PEER_TOPK)
        cidx = (si[:, :, 0, :, None] * PEER_NK + si[:, :, 1, None, :]).reshape(PEER_BLOCK, PEER_HEADS, PEER_TOPK * PEER_TOPK)
        fv, fi = lax.top_k(cand, PEER_TOPK)
        e = jnp.take_along_axis(cidx, fi, axis=-1)
        gate = jax.nn.softmax(fv, axis=-1)
        act = jax.nn.gelu(jnp.einsum('thkd,td->thk', tab_u[e], xb, preferred_element_type=F32))
        return jnp.einsum('thk,thkd->td', (gate * act).astype(xb.dtype), tab_v[e])

    return lax.map(block, tok).reshape(b, l, d)


def setup_inputs(seed: int = 0) -> dict:
    key = jax.random.key(seed)
    ks = jax.random.split(key, 24)
    d = D_MODEL

    def nrm(k, shape, s):
        return s * jax.random.normal(k, shape, F32)

    base_decay = np.log(2.0 ** (5 + np.arange(RET_HEADS)) - 1.0).astype(np.float32)
    return {
        'x': nrm(ks[0], (BATCH, SEQ, d), 1.0),
        'c': nrm(ks[1], (BATCH, d), 1.0),
        'ctx': nrm(ks[2], (BATCH, CTX_LEN, d), 1.0),
        'c_ctx': nrm(ks[3], (d,), 1.0),
        'w_mod': nrm(ks[4], (DEPTH, d, 6 * d), 0.5 * d ** -0.5),
        'b_mod': nrm(ks[5], (DEPTH, 6 * d), 0.01),
        'w_in': nrm(ks[6], (DEPTH, d, IN_COLS), d ** -0.5),
        'qk_gain': 1.0 + nrm(ks[7], (DEPTH, 2, HEAD_DIM), 0.05),
        'diff_lambda': nrm(ks[8], (DEPTH, 4, DIFF_DIM), 0.1),
        'diff_subln': 1.0 + nrm(ks[9], (DEPTH, 2 * DIFF_DIM), 0.05),
        'win_sink': nrm(ks[10], (DEPTH, WIN_HEADS), 0.5),
        'ret_decay': jnp.asarray(base_decay) + nrm(ks[11], (DEPTH, 2, RET_HEADS), 0.1),
        'ret_norm': jnp.stack([1.0 + nrm(ks[12], (DEPTH, RET_HEADS * RET_DV), 0.05),
                               nrm(ks[13], (DEPTH, RET_HEADS * RET_DV), 0.02)], axis=1),
        'w_branch': nrm(ks[14], (DEPTH, N_BRANCH, BRANCH_W, d), DEEPNORM_BETA * BRANCH_W ** -0.5),
        'w_out': nrm(ks[15], (DEPTH, d, d), DEEPNORM_BETA * d ** -0.5),
        'ln_attn': jnp.stack([1.0 + nrm(ks[16], (DEPTH, d), 0.05), nrm(ks[17], (DEPTH, d), 0.02)], axis=1),
        'ln_ffn': jnp.stack([1.0 + nrm(ks[18], (DEPTH, d), 0.05), nrm(ks[19], (DEPTH, d), 0.02)], axis=1),
        'peer_wq': nrm(ks[20], (DEPTH, d, PEER_HEADS * 2 * PEER_DQ), d ** -0.5),
        'peer_subkeys': nrm(ks[21], (DEPTH, PEER_HEADS, 2, PEER_NK, PEER_DQ), PEER_DQ ** -0.5),
        'peer_u': nrm(ks[22], (DEPTH, PEER_N, d), d ** -0.5),
        'peer_v': nrm(ks[23], (DEPTH, PEER_N, d), DEEPNORM_BETA),
    }


def reference(x, c, ctx, c_ctx, w_mod, b_mod, w_in, qk_gain, diff_lambda, diff_subln, win_sink,
              ret_decay, ret_norm, w_branch, w_out, ln_attn, ln_ffn, peer_wq, peer_subkeys, peer_u, peer_v):
    b, s, d = x.shape
    n_ctx = ctx.shape[1]
    rows = s // GRID_W
    row = jnp.broadcast_to(jnp.arange(rows, dtype=F32)[:, None], (rows, GRID_W)).reshape(-1)
    col = jnp.broadcast_to(jnp.arange(GRID_W, dtype=F32)[None, :], (rows, GRID_W)).reshape(-1)
    ax64 = axial_tables(row, col, HEAD_DIM)
    ax32 = axial_tables(row, col, DIFF_DIM)
    rope_ctx = rope1d_tables(jnp.arange(n_ctx, dtype=F32), RET_DK)
    rope_lat = rope1d_tables(n_ctx + jnp.arange(s, dtype=F32), RET_DK)
    cond = jax.nn.silu(c)
    cond_ctx = jax.nn.silu(c_ctx)
    h, hc = x, ctx
    for l in range(DEPTH):
        need_ctx = l < DEPTH - 1
        mod = (cond @ w_mod[l] + b_mod[l]).reshape(b, 6, 1, d)
        modc = (cond_ctx @ w_mod[l] + b_mod[l]).reshape(6, d)
        u = h * (1 + mod[:, 1]) + mod[:, 0]
        uc = hc * (1 + modc[1]) + modc[0]
        w_mix = w_in[l, :, :MIX_COLS]
        w_gate = w_in[l, :, MIX_COLS:]
        pl = split_pieces(u @ w_mix)
        pc = split_pieces(uc @ w_mix)
        oa, oac = mixer_gqa(pl[0:3], pc[0:3], qk_gain[l], ax64, need_ctx)
        ob, obc = mixer_diff(pl[3:6], pc[3:6], diff_lambda[l], diff_subln[l], l, ax32, need_ctx)
        oc, occ = mixer_window(pl[6:9], pc[6:9], win_sink[l], ax64, need_ctx)
        od, odc = mixer_retention(pl[9:13], pc[9:13], ret_decay[l], ret_norm[l], rope_lat, rope_ctx, need_ctx)
        y = merge_branches(u, (oa, ob, oc, od), w_gate, w_branch[l], w_out[l])
        h = layer_norm(DEEPNORM_ALPHA * h + mod[:, 2] * y, ln_attn[l])
        if need_ctx:
            yc = merge_branches(uc, (oac, obc, occ, odc), w_gate, w_branch[l], w_out[l])
            hc = layer_norm(DEEPNORM_ALPHA * hc + modc[2] * yc, ln_attn[l])
        u2 = h * (1 + mod[:, 4]) + mod[:, 3]
        y2 = peer_ffn(u2, peer_wq[l], peer_subkeys[l], peer_u[l], peer_v[l])
        h = layer_norm(DEEPNORM_ALPHA * h + mod[:, 5] * y2, ln_ffn[l])
        if need_ctx:
            uc2 = hc * (1 + modc[4]) + modc[3]
            yc2 = peer_ffn(uc2, peer_wq[l], peer_subkeys[l], peer_u[l], peer_v[l])
            hc = layer_norm(DEEPNORM_ALPHA * hc + modc[5] * yc2, ln_ffn[l])
    return h
```

```python
import functools
import math

import numpy as np
import jax
import jax.numpy as jnp
from jax import lax
from jax.experimental import pallas as pl
from jax.experimental.pallas import tpu as pltpu

GRID_W = 64
HEAD_DIM = 64
ROPE_THETA = 10000.0
A_HEADS = 4
A_KV_HEADS = 2
DIFF_HEADS = 4
DIFF_DIM = 32
WIN_HEADS = 4
WIN_KV_HEADS = 2
WINDOW = 128
RET_HEADS = 4
RET_DK = 64
RET_DV = 64
RET_CHUNK = 128
N_BRANCH = 4
BRANCH_W = 256
PIECES = (
    A_HEADS * HEAD_DIM, A_KV_HEADS * HEAD_DIM, A_KV_HEADS * HEAD_DIM,
    2 * DIFF_HEADS * DIFF_DIM, 2 * DIFF_HEADS * DIFF_DIM, DIFF_HEADS * 2 * DIFF_DIM,
    WIN_HEADS * HEAD_DIM, WIN_KV_HEADS * HEAD_DIM, WIN_KV_HEADS * HEAD_DIM,
    RET_HEADS * RET_DK, RET_HEADS * RET_DK, RET_HEADS * RET_DV, RET_HEADS * RET_DV,
)
MIX_COLS = sum(PIECES)
PIECE_OFF = tuple(int(v) for v in np.cumsum((0,) + PIECES))
PEER_HEADS = 8
PEER_NK = 128
PEER_TOPK = 16
PEER_DQ = 128
PEER_CAND_ROWS = -(-sum((PEER_TOPK + 1) // (p + 1) for p in range(PEER_TOPK + 1)) // 8) * 8
LN_EPS = 1e-5
RMS_EPS = 1e-6
LOG2E = 1.4426950408889634

F32 = jnp.float32
BF16 = jnp.bfloat16

TOKEN_BLOCK = 256
ATT_TQ = 256
ATT_TK = 256
PEER_TOKENS = 512
PEER_EXPERTS = 1024
VMEM_LIMIT = 56 * 1024 * 1024


def _cparams(sem):
    return pltpu.CompilerParams(dimension_semantics=sem, vmem_limit_bytes=VMEM_LIMIT)


def _split_bf16(a):
    hi = a.astype(BF16)
    lo = (a - hi.astype(F32)).astype(BF16)
    return hi, lo


def _dot3(a, b, dims=(((1,), (0,)), ((), ()))):
    ah, al = _split_bf16(a)
    bh, bl = _split_bf16(b)
    d = functools.partial(lax.dot_general, dimension_numbers=dims, preferred_element_type=F32)
    return d(ah, bh) + (d(al, bh) + d(ah, bl))


def _mod_kernel(c_ref, w_ref, b_ref, o_ref):
    o_ref[...] = _dot3(c_ref[...], w_ref[...]) + b_ref[...]


def _modulation(cond8, w, b):
    d, n = w.shape
    tn = 1536
    return pl.pallas_call(
        _mod_kernel,
        out_shape=jax.ShapeDtypeStruct((8, n), F32),
        grid=(n // tn,),
        in_specs=[pl.BlockSpec((8, d), lambda j: (0, 0)),
                  pl.BlockSpec((d, tn), lambda j: (0, j)),
                  pl.BlockSpec((1, tn), lambda j: (0, j))],
        out_specs=pl.BlockSpec((8, tn), lambda j: (0, j)),
        compiler_params=_cparams(("arbitrary",)),
    )(cond8, w, b.reshape(1, n))


def _inproj_kernel(x_ref, mod_ref, w_ref, o_ref, *, sigmoid, chunk):
    m = mod_ref[...]
    xm = (x_ref[...] * (1.0 + m[1:2, :]) + m[0:1, :]).astype(BF16)
    n = w_ref.shape[1]
    for j in range(n // chunk):
        acc = jnp.dot(xm, w_ref[:, j * chunk:(j + 1) * chunk], preferred_element_type=F32)
        if sigmoid:
            acc = jax.nn.sigmoid(acc)
        o_ref[:, j * chunk:(j + 1) * chunk] = acc.astype(o_ref.dtype)


def _inproj(h, mod, group_of_block, w, *, sigmoid, out_dtype):
    t, d = h.shape
    n = w.shape[1]
    tb = TOKEN_BLOCK
    return pl.pallas_call(
        functools.partial(_inproj_kernel, sigmoid=sigmoid, chunk=512),
        out_shape=jax.ShapeDtypeStruct((t, n), out_dtype),
        grid=(t // tb,),
        in_specs=[pl.BlockSpec((tb, d), lambda i: (i, 0)),
                  pl.BlockSpec((None, 6, d), lambda i: (group_of_block(i), 0, 0)),
                  pl.BlockSpec((d, n), lambda i: (0, 0))],
        out_specs=pl.BlockSpec((tb, n), lambda i: (i, 0)),
        compiler_params=_cparams(("parallel",)),
    )(h, mod, w)


def _flash_kernel(qt_ref, k_ref, vt_ref, o_ref, *, n_ctx_qblocks, n_ctx_chunks, n_chunks):
    i = pl.program_id(2)
    qt = qt_ref[...]
    w = qt.shape[1]
    n = jnp.where(i < n_ctx_qblocks, n_ctx_chunks, n_chunks)

    def body(c, carry):
        m, l, acc = carry
        s = jnp.dot(k_ref[c], qt, preferred_element_type=F32)
        m_new = jnp.maximum(m, jnp.max(s, axis=0, keepdims=True))
        alpha = jnp.exp2(m - m_new)
        p = jnp.exp2(s - m_new)
        l = l * alpha + jnp.sum(p, axis=0, keepdims=True)
        acc = acc * alpha + jnp.dot(vt_ref[c], p.astype(BF16), preferred_element_type=F32)
        return m_new, l, acc

    init = (jnp.full((1, w), -jnp.inf, F32), jnp.zeros((1, w), F32), jnp.zeros((HEAD_DIM, w), F32))
    _, l, acc = lax.fori_loop(0, n, body, init)
    o_ref[...] = acc / l


def _flash(qt, k, vt, n_ctx_qblocks, n_ctx_chunks):
    b, g, nqb, dh, w = qt.shape
    nch, tk = k.shape[2], k.shape[3]
    return pl.pallas_call(
        functools.partial(_flash_kernel, n_ctx_qblocks=n_ctx_qblocks, n_ctx_chunks=n_ctx_chunks,
                          n_chunks=nch),
        out_shape=jax.ShapeDtypeStruct((b, g, nqb, dh, w), F32),
        grid=(b, g, nqb),
        in_specs=[pl.BlockSpec((None, None, None, dh, w), lambda bi, gi, i: (bi, gi, i, 0, 0)),
                  pl.BlockSpec((None, None, nch, tk, dh), lambda bi, gi, i: (bi, gi, 0, 0, 0)),
                  pl.BlockSpec((None, None, nch, dh, tk), lambda bi, gi, i: (bi, gi, 0, 0, 0))],
        out_specs=pl.BlockSpec((None, None, None, dh, w), lambda bi, gi, i: (bi, gi, i, 0, 0)),
        compiler_params=_cparams(("parallel", "parallel", "arbitrary")),
    )(qt, k, vt)


def _window_kernel(q_ref, kp_ref, kc_ref, kn_ref, vp_ref, vc_ref, vn_ref, kx_ref, vx_ref, sink_ref,
                   o_ref, *, n_ctx_blocks, n_blocks):
    qb = pl.program_id(2)
    q = q_ref[...]
    rows = q.shape[0]
    nt = (((1,), (1,)), ((), ()))
    sdot = functools.partial(lax.dot_general, dimension_numbers=nt, preferred_element_type=F32)
    qi = lax.broadcasted_iota(jnp.int32, (rows, WINDOW), 0) % WINDOW
    kj = lax.broadcasted_iota(jnp.int32, (rows, WINDOW), 1)
    neg = -jnp.inf
    off_p = jnp.where(qb >= n_ctx_blocks + 1, 0, 2 * WINDOW)
    off_c = jnp.where(qb >= n_ctx_blocks, 0, 2 * WINDOW)
    off_n = jnp.where(jnp.logical_and(qb >= n_ctx_blocks, qb <= n_blocks - 2), 0, 2 * WINDOW)
    s_p = jnp.where(kj >= qi + off_p, sdot(q, kp_ref[...]), neg)
    s_c = jnp.where(kj >= off_c, sdot(q, kc_ref[...]), neg)
    s_n = jnp.where(kj <= qi - off_n, sdot(q, kn_ref[...]), neg)
    kx = kx_ref[...].reshape(-1, HEAD_DIM)
    vx = vx_ref[...].reshape(-1, HEAD_DIM)
    s_x = sdot(q, kx)
    sink = sink_ref[...]
    m = jnp.maximum(jnp.maximum(jnp.max(s_p, axis=1, keepdims=True), jnp.max(s_c, axis=1, keepdims=True)),
                    jnp.maximum(jnp.max(s_n, axis=1, keepdims=True), jnp.max(s_x, axis=1, keepdims=True)))
    m = jnp.maximum(m, sink)
    e_p, e_c, e_n, e_x = (jnp.exp2(s - m) for s in (s_p, s_c, s_n, s_x))
    den = (jnp.sum(e_p, axis=1, keepdims=True) + jnp.sum(e_c, axis=1, keepdims=True)
           + jnp.sum(e_n, axis=1, keepdims=True) + jnp.sum(e_x, axis=1, keepdims=True)
           + jnp.exp2(sink - m))
    pv = functools.partial(jnp.dot, preferred_element_type=F32)
    o = (pv(e_p.astype(BF16), vp_ref[...]) + pv(e_c.astype(BF16), vc_ref[...])
         + pv(e_n.astype(BF16), vn_ref[...]) + pv(e_x.astype(BF16), vx))
    o_ref[...] = o / den


def _window(q, k, v, sink, n_ctx_blocks):
    b, g, nb, rows, dh = q.shape
    blk = k.shape[3]
    lo, hi = n_ctx_blocks, nb - 1

    def nbr(delta):
        return lambda bi, gi, i: (bi, gi, jnp.clip(i + delta, lo, hi), 0, 0)

    kv_spec = lambda d: pl.BlockSpec((None, None, None, blk, dh), nbr(d))
    ctx_spec = pl.BlockSpec((None, None, n_ctx_blocks, blk, dh), lambda bi, gi, i: (bi, gi, 0, 0, 0))
    return pl.pallas_call(
        functools.partial(_window_kernel, n_ctx_blocks=n_ctx_blocks, n_blocks=nb),
        out_shape=jax.ShapeDtypeStruct((b, g, nb, rows, dh), F32),
        grid=(b, g, nb),
        in_specs=[pl.BlockSpec((None, None, None, rows, dh), lambda bi, gi, i: (bi, gi, i, 0, 0)),
                  kv_spec(-1), kv_spec(0), kv_spec(1), kv_spec(-1), kv_spec(0), kv_spec(1),
                  ctx_spec, ctx_spec,
                  pl.BlockSpec((None, rows, 1), lambda bi, gi, i: (gi, 0, 0))],
        out_specs=pl.BlockSpec((None, None, None, rows, dh), lambda bi, gi, i: (bi, gi, i, 0, 0)),
        compiler_params=_cparams(("parallel", "parallel", "arbitrary")),
    )(q, k, k, k, v, v, v, k, v, sink)


def _retention_kernel(q_ref, kt_ref, v_ref, dmat_ref, xi_ref, zeta_ref, gch_ref, o_ref, st_ref):
    t = pl.program_id(2)

    @pl.when(t == 0)
    def _():
        st_ref[...] = jnp.zeros_like(st_ref)

    for hd in range(RET_HEADS):
        q = q_ref[hd]
        kt = kt_ref[hd]
        v = v_ref[hd]
        st = st_ref[hd]
        inner = _dot3(q, kt) * dmat_ref[hd]
        o_ref[hd] = _dot3(inner, v) + _dot3(q, st) * xi_ref[hd]
        st_ref[hd] = st * gch_ref[hd] + _dot3(kt * zeta_ref[hd], v)


def _retention_call(q, kt, v, dmat, xi, zeta, gch, n_ctx_chunks):
    b, hh, lt, dk = q.shape
    dv = v.shape[-1]
    c = RET_CHUNK
    nch = lt // c

    def blk(d, ti):
        back = jnp.where(ti < n_ctx_chunks, n_ctx_chunks - 1 - ti, nch - 1 - (ti - n_ctx_chunks))
        return jnp.where(d == 0, ti, back)

    tab = lambda shape: pl.BlockSpec((None,) + shape, lambda d, bi, ti: (d,) + (0,) * len(shape))
    return pl.pallas_call(
        _retention_kernel,
        out_shape=jax.ShapeDtypeStruct((2, b, hh, lt, dv), F32),
        grid=(2, b, nch),
        in_specs=[pl.BlockSpec((None, hh, c, dk), lambda d, bi, ti: (bi, 0, blk(d, ti), 0)),
                  pl.BlockSpec((None, hh, dk, c), lambda d, bi, ti: (bi, 0, 0, blk(d, ti))),
                  pl.BlockSpec((None, hh, c, dv), lambda d, bi, ti: (bi, 0, blk(d, ti), 0)),
                  tab((hh, c, c)), tab((hh, c, 1)), tab((hh, 1, c)), tab((hh, 1, 1))],
        out_specs=pl.BlockSpec((None, None, hh, c, dv), lambda d, bi, ti: (d, bi, 0, blk(d, ti), 0)),
        scratch_shapes=[pltpu.VMEM((hh, dk, dv), F32)],
        compiler_params=_cparams(("parallel", "parallel", "arbitrary")),
    )(q, kt, v, dmat, xi, zeta, gch)


def _layer_norm_rows(z, ln):
    mu = jnp.mean(z, axis=-1, keepdims=True)
    zc = z - mu
    var = jnp.mean(zc * zc, axis=-1, keepdims=True)
    return zc * lax.rsqrt(var + LN_EPS) * ln[0:1, :] + ln[1:2, :]


def _merge_kernel(oa_ref, ob_ref, oc_ref, od_ref, g_ref, h_ref, mod_ref, wb_ref, wo_ref, ln_ref, o_ref,
                  *, alpha):
    d = h_ref.shape[1]
    m = None
    for i, o_i in enumerate((oa_ref, ob_ref, oc_ref, od_ref)):
        t = g_ref[:, i * d:(i + 1) * d].astype(F32) * jnp.dot(o_i[...], wb_ref[i], preferred_element_type=F32)
        m = t if m is None else m + t
    y = jnp.dot(m.astype(BF16), wo_ref[...], preferred_element_type=F32)
    z = alpha * h_ref[...] + mod_ref[2:3, :] * y
    o_ref[...] = _layer_norm_rows(z, ln_ref[...])


def _merge(outs, gates, h, mod, group_of_block, wb, wo, ln, alpha):
    t, d = h.shape
    tb = TOKEN_BLOCK
    bw = outs[0].shape[1]
    row = lambda n: pl.BlockSpec((tb, n), lambda i: (i, 0))
    return pl.pallas_call(
        functools.partial(_merge_kernel, alpha=alpha),
        out_shape=jax.ShapeDtypeStruct((t, d), F32),
        grid=(t // tb,),
        in_specs=[row(bw), row(bw), row(bw), row(bw), row(N_BRANCH * d), row(d),
                  pl.BlockSpec((None, 6, d), lambda i: (group_of_block(i), 0, 0)),
                  pl.BlockSpec((N_BRANCH, bw, d), lambda i: (0, 0, 0)),
                  pl.BlockSpec((d, d), lambda i: (0, 0)),
                  pl.BlockSpec((2, d), lambda i: (0, 0))],
        out_specs=row(d),
        compiler_params=_cparams(("parallel",)),
    )(*outs, gates, h, mod, wb, wo, ln)


def _top_rows(s, n):
    out = []
    cur = s
    for r in range(n):
        mx = jnp.max(cur, axis=0, keepdims=True)
        out.append(mx)
        if r + 1 < n:
            cur = jnp.where(cur == mx, -jnp.inf, cur)
    return out


def _peer_route_kernel(h_ref, mod_ref, wh_ref, wl_ref, sk_ref, x_ref, s1_ref, b1_ref, th_ref, az_ref,
                       cand_ref):
    m = mod_ref[...]
    u = h_ref[...] * (1.0 + m[4:5, :]) + m[3:4, :]
    x_ref[...] = u.astype(BF16)
    uh, ul = _split_bf16(u)
    d = functools.partial(jnp.dot, preferred_element_type=F32)
    nk = PEER_NK
    nt = (((1,), (1,)), ((), ()))
    k1 = PEER_TOPK + 1
    for hd in range(PEER_HEADS):
        st = []
        for p in range(2):
            c0 = (2 * hd + p) * PEER_DQ
            wh = wh_ref[:, c0:c0 + PEER_DQ]
            wl = wl_ref[:, c0:c0 + PEER_DQ]
            q = d(uh, wh) + (d(ul, wh) + d(uh, wl))
            st.append(_dot3(sk_ref[hd, p], q, nt))
        top0 = _top_rows(st[0], k1)
        top1 = _top_rows(st[1], k1)
        r = 0
        for p0 in range(k1):
            for p1 in range(k1 // (p0 + 1)):
                cand_ref[r:r + 1, :] = top0[p0] + top1[p1]
                r += 1
        cand_ref[r:, :] = jnp.full((cand_ref.shape[0] - r, cand_ref.shape[1]), -jnp.inf, F32)
        cand = cand_ref[...]
        ctop = _top_rows(cand, k1)
        tau = 0.5 * (ctop[PEER_TOPK - 1] + ctop[PEER_TOPK])
        mx = top0[0] + top1[0]
        z = jnp.sum(jnp.where(cand >= tau, jnp.exp(cand - mx), 0.0), axis=0, keepdims=True)
        s1_ref[hd] = st[1]
        b1_ref[hd] = jnp.exp(st[1] - top1[0])
        th_ref[hd] = tau - st[0]
        az_ref[hd] = jnp.exp(st[0] - top0[0]) / z


def _peer_route(h, mod, group_of_block, wq_hi, wq_lo, subkeys):
    t, d = h.shape
    tb = TOKEN_BLOCK
    hh, nk = PEER_HEADS, PEER_NK
    st_shape = jax.ShapeDtypeStruct((hh, nk, t), F32)
    st_spec = pl.BlockSpec((hh, nk, tb), lambda i: (0, 0, i))
    nq = wq_hi.shape[1]
    return pl.pallas_call(
        _peer_route_kernel,
        out_shape=(jax.ShapeDtypeStruct((t, d), BF16), st_shape, st_shape, st_shape, st_shape),
        grid=(t // tb,),
        in_specs=[pl.BlockSpec((tb, d), lambda i: (i, 0)),
                  pl.BlockSpec((None, 6, d), lambda i: (group_of_block(i), 0, 0)),
                  pl.BlockSpec((d, nq), lambda i: (0, 0)),
                  pl.BlockSpec((d, nq), lambda i: (0, 0)),
                  pl.BlockSpec((hh, 2, nk, PEER_DQ), lambda i: (0, 0, 0, 0))],
        out_specs=(pl.BlockSpec((tb, d), lambda i: (i, 0)), st_spec, st_spec, st_spec, st_spec),
        scratch_shapes=[pltpu.VMEM((PEER_CAND_ROWS, tb), F32)],
        compiler_params=_cparams(("parallel",)),
    )(h, mod, wq_hi, wq_lo, subkeys)


def _gelu_tanh(x):
    return 0.5 * x * (1.0 + jnp.tanh(math.sqrt(2.0 / math.pi) * (x + 0.044715 * (x * x * x))))


def _peer_dense_kernel(xt_ref, u_ref, vt_ref, s1_ref, b1_ref, th_ref, az_ref, yt_ref, g_ref):
    c = pl.program_id(1)
    nk = PEER_NK
    rows_per_step = u_ref.shape[0] // nk

    @pl.when(c == 0)
    def _():
        yt_ref[...] = jnp.zeros_like(yt_ref)

    act = _gelu_tanh(jnp.dot(u_ref[...], xt_ref[...], preferred_element_type=F32))
    for ii in range(rows_per_step):
        i = c * rows_per_step + ii
        w = None
        for hd in range(PEER_HEADS):
            th = th_ref[hd, pl.ds(i, 1), :]
            az = az_ref[hd, pl.ds(i, 1), :]
            t = jnp.where(s1_ref[hd] >= th, b1_ref[hd], 0.0) * az
            w = t if w is None else w + t
        g_ref[ii * nk:(ii + 1) * nk, :] = (w * act[ii * nk:(ii + 1) * nk, :]).astype(BF16)
    yt_ref[...] += jnp.dot(vt_ref[...], g_ref[...], preferred_element_type=F32)


def _peer_dense(xt, u, vt, s1, b1, th, az):
    d, t = xt.shape
    n = u.shape[0]
    tp = PEER_TOKENS
    ec = PEER_EXPERTS
    hh, nk = PEER_HEADS, PEER_NK
    st_spec = pl.BlockSpec((hh, nk, tp), lambda i, c: (0, 0, i))
    return pl.pallas_call(
        _peer_dense_kernel,
        out_shape=jax.ShapeDtypeStruct((d, t), F32),
        grid=(t // tp, n // ec),
        in_specs=[pl.BlockSpec((d, tp), lambda i, c: (0, i)),
                  pl.BlockSpec((ec, d), lambda i, c: (c, 0)),
                  pl.BlockSpec((d, ec), lambda i, c: (0, c)),
                  st_spec, st_spec, st_spec, st_spec],
        out_specs=pl.BlockSpec((d, tp), lambda i, c: (0, i)),
        scratch_shapes=[pltpu.VMEM((ec, tp), BF16)],
        compiler_params=_cparams(("parallel", "arbitrary")),
    )(xt, u, vt, s1, b1, th, az)


def _resid_ln_kernel(h_ref, y_ref, mod_ref, ln_ref, o_ref, *, alpha, gate_row):
    z = alpha * h_ref[...] + mod_ref[gate_row:gate_row + 1, :] * y_ref[...]
    o_ref[...] = _layer_norm_rows(z, ln_ref[...])


def _resid_ln(h, y, mod, group_of_block, ln, alpha, gate_row):
    t, d = h.shape
    tb = TOKEN_BLOCK
    row = pl.BlockSpec((tb, d), lambda i: (i, 0))
    return pl.pallas_call(
        functools.partial(_resid_ln_kernel, alpha=alpha, gate_row=gate_row),
        out_shape=jax.ShapeDtypeStruct((t, d), F32),
        grid=(t // tb,),
        in_specs=[row, row,
                  pl.BlockSpec((None, 6, d), lambda i: (group_of_block(i), 0, 0)),
                  pl.BlockSpec((2, d), lambda i: (0, 0))],
        out_specs=row,
        compiler_params=_cparams(("parallel",)),
    )(h, y, mod, ln)


def _axial_tables(s, n_ctx, d):
    rows = s // GRID_W
    row = jnp.broadcast_to(jnp.arange(rows, dtype=F32)[:, None], (rows, GRID_W)).reshape(-1)
    col = jnp.broadcast_to(jnp.arange(GRID_W, dtype=F32)[None, :], (rows, GRID_W)).reshape(-1)
    quarter = d // 4
    inv = ROPE_THETA ** (-jnp.arange(quarter, dtype=F32) / quarter)
    ar, ac = row[:, None] * inv, col[:, None] * inv
    cos = jnp.concatenate([jnp.cos(ar), jnp.cos(ar), jnp.cos(ac), jnp.cos(ac)], axis=-1)
    sin = jnp.concatenate([-jnp.sin(ar), jnp.sin(ar), -jnp.sin(ac), jnp.sin(ac)], axis=-1)
    cos = jnp.concatenate([jnp.ones((n_ctx, d), F32), cos], axis=0)
    sin = jnp.concatenate([jnp.zeros((n_ctx, d), F32), sin], axis=0)
    return cos, sin


def _rope1d_tables(lt, d):
    half = d // 2
    inv = ROPE_THETA ** (-jnp.arange(half, dtype=F32) / half)
    ang = jnp.arange(lt, dtype=F32)[:, None] * inv
    return (jnp.concatenate([jnp.cos(ang), jnp.cos(ang)], axis=-1),
            jnp.concatenate([-jnp.sin(ang), jnp.sin(ang)], axis=-1))


def _swap_perm():
    cols = []
    for piece, flip in ((0, 16), (1, 16), (3, 8), (4, 8), (6, 16), (7, 16), (9, 32), (10, 32)):
        base = np.arange(PIECE_OFF[piece], PIECE_OFF[piece + 1])
        cols.append(PIECE_OFF[piece] + ((base - PIECE_OFF[piece]) ^ flip))
    return np.concatenate(cols)


SWAP_COLS = _swap_perm()
SWAP_PIECES = (0, 1, 3, 4, 6, 7, 9, 10)
SWAP_OFF = dict(zip(SWAP_PIECES, np.cumsum([0] + [PIECES[p] for p in SWAP_PIECES])[:-1] + MIX_COLS))


def _rms_heads(x, xs, gain, gain_s):
    r = lax.rsqrt(jnp.mean(x * x, axis=-1, keepdims=True) + RMS_EPS)
    return x * r * gain, xs * r * gain_s


def kernel(x, c, ctx, c_ctx, w_mod, b_mod, w_in, qk_gain, diff_lambda, diff_subln, win_sink, ret_decay,
           ret_norm, w_branch, w_out, ln_attn, ln_ffn, peer_wq, peer_subkeys, peer_u, peer_v):
    b, s, d = x.shape
    n_ctx = ctx.shape[1]
    depth = w_mod.shape[0]
    lt = n_ctx + s
    t = b * lt
    tb = TOKEN_BLOCK
    assert n_ctx % tb == 0 and s % tb == 0 and t % PEER_TOKENS == 0
    alpha = (2 * depth) ** 0.25
    blocks_per_batch = lt // tb
    ctx_blocks = n_ctx // tb

    def group_of_block(i):
        return jnp.where(i % blocks_per_batch < ctx_blocks, b, i // blocks_per_batch)

    cos64, sin64 = _axial_tables(s, n_ctx, HEAD_DIM)
    cos32, sin32 = _axial_tables(s, n_ctx, DIFF_DIM)
    cos1d, sin1d = _rope1d_tables(lt, RET_DK)
    cond8 = jnp.zeros((8, d), F32).at[:b].set(jax.nn.silu(c)).at[b].set(jax.nn.silu(c_ctx))

    h = jnp.concatenate([ctx, x], axis=1).reshape(t, d)
    nqb = lt // ATT_TQ
    nkc = lt // ATT_TK
    nwb = lt // WINDOW

    for l in range(depth):
        mod = _modulation(cond8, w_mod[l], b_mod[l]).reshape(8, 6, d)
        w_mix = w_in[l, :, :MIX_COLS]
        w_ext = jnp.concatenate([w_mix, w_mix[:, SWAP_COLS]], axis=1).astype(BF16)
        w_gate = w_in[l, :, MIX_COLS:].astype(BF16)
        proj = _inproj(h, mod, group_of_block, w_ext, sigmoid=False, out_dtype=F32)
        gates = _inproj(h, mod, group_of_block, w_gate, sigmoid=True, out_dtype=BF16)
        proj = proj.reshape(b, lt, -1)

        def piece(p, heads):
            return proj[..., PIECE_OFF[p]:PIECE_OFF[p + 1]].reshape(b, lt, heads, -1)

        def swapped(p, heads):
            return proj[..., SWAP_OFF[p]:SWAP_OFF[p] + PIECES[p]].reshape(b, lt, heads, -1)

        def rope(p, heads, cos, sin):
            return piece(p, heads) * cos[:, None, :] + swapped(p, heads) * sin[:, None, :]

        g0, g1 = qk_gain[l, 0], qk_gain[l, 1]
        swap64 = np.arange(HEAD_DIM) ^ 16
        qa, qas = _rms_heads(piece(0, A_HEADS), swapped(0, A_HEADS), g0, g0[swap64])
        ka, kas = _rms_heads(piece(1, A_KV_HEADS), swapped(1, A_KV_HEADS), g1, g1[swap64])
        qa = qa * cos64[:, None, :] + qas * sin64[:, None, :]
        ka = ka * cos64[:, None, :] + kas * sin64[:, None, :]
        va = piece(2, A_KV_HEADS)
        cos32x2 = jnp.concatenate([cos32, cos32], axis=-1)
        sin32x2 = jnp.concatenate([sin32, sin32], axis=-1)
        qd = rope(3, DIFF_HEADS, cos32x2, sin32x2)
        kd = rope(4, DIFF_HEADS, cos32x2, sin32x2)
        vd = piece(5, DIFF_HEADS)
        first = (jnp.arange(2 * DIFF_DIM) < DIFF_DIM).astype(F32)

        sc_a = HEAD_DIM ** -0.5 * LOG2E
        sc_d = DIFF_DIM ** -0.5 * LOG2E
        qa_t = (qa * sc_a).reshape(b, nqb, ATT_TQ, A_KV_HEADS, 2, HEAD_DIM).transpose(0, 3, 1, 5, 4, 2)
        qd_s = jnp.stack([qd * first, qd * (1.0 - first)], axis=3) * sc_d
        qd_t = qd_s.reshape(b, nqb, ATT_TQ, DIFF_HEADS, 2, HEAD_DIM).transpose(0, 3, 1, 5, 4, 2)
        qt = jnp.concatenate([qa_t, qd_t], axis=1).reshape(b, -1, nqb, HEAD_DIM, 2 * ATT_TQ).astype(BF16)
        k_all = jnp.concatenate([ka, kd], axis=2).transpose(0, 2, 1, 3).astype(BF16)
        v_all = jnp.concatenate([va, vd], axis=2).transpose(0, 2, 1, 3).astype(BF16)
        n_groups = k_all.shape[1]
        k_ch = k_all.reshape(b, n_groups, nkc, ATT_TK, HEAD_DIM)
        vt_ch = v_all.reshape(b, n_groups, nkc, ATT_TK, HEAD_DIM).transpose(0, 1, 2, 4, 3)
        ot = _flash(qt, k_ch, vt_ch, n_ctx // ATT_TQ, n_ctx // ATT_TK)
        ot = ot.reshape(b, n_groups, nqb, HEAD_DIM, 2, ATT_TQ)
        oa = ot[:, :A_KV_HEADS].transpose(0, 2, 5, 1, 4, 3).reshape(b, lt, A_HEADS * HEAD_DIM)
        lam_init = 0.8 - 0.6 * math.exp(-0.3 * l)
        lp = diff_lambda[l].astype(F32)
        lam = jnp.exp(jnp.sum(lp[0] * lp[1])) - jnp.exp(jnp.sum(lp[2] * lp[3])) + lam_init
        od_ = ot[:, A_KV_HEADS:, :, :, 0, :] - lam * ot[:, A_KV_HEADS:, :, :, 1, :]
        od_ = od_.transpose(0, 2, 4, 1, 3).reshape(b, lt, DIFF_HEADS, 2 * DIFF_DIM)
        od_ = od_ * lax.rsqrt(jnp.mean(od_ * od_, axis=-1, keepdims=True) + RMS_EPS) * diff_subln[l]
        ob = (od_ * (1.0 - lam_init)).reshape(b, lt, -1)

        qw = rope(6, WIN_HEADS, cos64, sin64) * sc_a
        kw = rope(7, WIN_KV_HEADS, cos64, sin64)
        vw = piece(8, WIN_KV_HEADS)
        qw = (qw.reshape(b, nwb, WINDOW, WIN_KV_HEADS, 2, HEAD_DIM).transpose(0, 3, 1, 4, 2, 5)
              .reshape(b, WIN_KV_HEADS, nwb, 2 * WINDOW, HEAD_DIM).astype(BF16))
        kw = kw.transpose(0, 2, 1, 3).reshape(b, WIN_KV_HEADS, nwb, WINDOW, HEAD_DIM).astype(BF16)
        vw = vw.transpose(0, 2, 1, 3).reshape(b, WIN_KV_HEADS, nwb, WINDOW, HEAD_DIM).astype(BF16)
        sink = jnp.repeat(win_sink[l].astype(F32) * LOG2E, WINDOW).reshape(WIN_KV_HEADS, 2 * WINDOW, 1)
        ow = _window(qw, kw, vw, sink, n_ctx // WINDOW)
        oc = (ow.reshape(b, WIN_KV_HEADS, nwb, 2, WINDOW, HEAD_DIM).transpose(0, 2, 4, 1, 3, 5)
              .reshape(b, lt, -1))

        qr = rope(9, RET_HEADS, cos1d, sin1d).transpose(0, 2, 1, 3)
        krt = (rope(10, RET_HEADS, cos1d, sin1d) * RET_DK ** -0.5).transpose(0, 2, 3, 1)
        vr = piece(11, RET_HEADS).transpose(0, 2, 1, 3)
        lg = jax.nn.log_sigmoid(ret_decay[l].astype(F32))
        idx = jnp.arange(RET_CHUNK, dtype=F32)
        diff = idx[:, None] - idx[None, :]
        lg3 = lg[:, :, None, None]
        dm_f = jnp.exp(jnp.where(diff >= 0, diff * lg3[0], -jnp.inf))
        dm_b = jnp.exp(jnp.where(diff <= 0, -diff * lg3[1], -jnp.inf))
        dmat = jnp.stack([dm_f, dm_b])
        xi = jnp.stack([jnp.exp((idx + 1.0) * lg[0][:, None]), jnp.exp((RET_CHUNK - idx) * lg[1][:, None])])
        zeta = jnp.stack([jnp.exp((RET_CHUNK - 1.0 - idx) * lg[0][:, None]), jnp.exp(idx * lg[1][:, None])])
        gch = jnp.exp(RET_CHUNK * lg)
        o_ret = _retention_call(qr, krt, vr, dmat, xi[..., None], zeta[:, :, None, :],
                                gch[:, :, None, None], n_ctx // RET_CHUNK)
        o_ret = (o_ret[0] + o_ret[1]).transpose(0, 2, 1, 3)
        mu = jnp.mean(o_ret, -1, keepdims=True)
        var = jnp.mean(jnp.square(o_ret - mu), -1, keepdims=True)
        on = ((o_ret - mu) * lax.rsqrt(var + LN_EPS)).reshape(b, lt, -1) * ret_norm[l, 0] + ret_norm[l, 1]
        od = on * jax.nn.silu(proj[..., PIECE_OFF[12]:PIECE_OFF[13]])

        outs = [o.reshape(t, BRANCH_W).astype(BF16) for o in (oa, ob, oc, od)]
        h = _merge(outs, gates, h, mod, group_of_block, w_branch[l].astype(BF16), w_out[l].astype(BF16),
                   ln_attn[l], alpha)

        wq_hi, wq_lo = _split_bf16(peer_wq[l])
        xb, s1, b1, th, az = _peer_route(h, mod, group_of_block, wq_hi, wq_lo, peer_subkeys[l])
        yt = _peer_dense(xb.T, peer_u[l].astype(BF16), peer_v[l].T.astype(BF16), s1, b1, th, az)
        h = _resid_ln(h, yt.T, mod, group_of_block, ln_ffn[l], alpha, 5)

    return h.reshape(b, lt, d)[:, n_ctx:, :]
```

```python
import functools
import math

import numpy as np
import jax
import jax.numpy as jnp
from jax import lax
from jax.experimental import pallas as pl
from jax.experimental.pallas import tpu as pltpu

GRID_W = 64
HEAD_DIM = 64
ROPE_THETA = 10000.0
A_HEADS = 4
A_KV_HEADS = 2
DIFF_HEADS = 4
DIFF_DIM = 32
WIN_HEADS = 4
WIN_KV_HEADS = 2
WINDOW = 128
RET_HEADS = 4
RET_DK = 64
RET_DV = 64
RET_CHUNK = 128
N_BRANCH = 4
BRANCH_W = 256
PIECES = (
    A_HEADS * HEAD_DIM, A_KV_HEADS * HEAD_DIM, A_KV_HEADS * HEAD_DIM,
    2 * DIFF_HEADS * DIFF_DIM, 2 * DIFF_HEADS * DIFF_DIM, DIFF_HEADS * 2 * DIFF_DIM,
    WIN_HEADS * HEAD_DIM, WIN_KV_HEADS * HEAD_DIM, WIN_KV_HEADS * HEAD_DIM,
    RET_HEADS * RET_DK, RET_HEADS * RET_DK, RET_HEADS * RET_DV, RET_HEADS * RET_DV,
)
MIX_COLS = sum(PIECES)
PIECE_OFF = tuple(int(v) for v in np.cumsum((0,) + PIECES))
PEER_HEADS = 8
PEER_NK = 128
PEER_TOPK = 16
PEER_DQ = 128
PEER_CAND_ROWS = -(-sum((PEER_TOPK + 1) // (p + 1) for p in range(PEER_TOPK + 1)) // 8) * 8
LN_EPS = 1e-5
RMS_EPS = 1e-6
LOG2E = 1.4426950408889634

F32 = jnp.float32
BF16 = jnp.bfloat16

TOKEN_BLOCK = 256
ATT_TQ = 256
ATT_TK = 1024
ATT_VPAD = 80
PEER_TOKENS = 512
PEER_EXPERTS = 1024
VMEM_LIMIT = 56 * 1024 * 1024


def _cparams(sem):
    return pltpu.CompilerParams(dimension_semantics=sem, vmem_limit_bytes=VMEM_LIMIT)


def _split_bf16(a):
    hi = a.astype(BF16)
    lo = (a - hi.astype(F32)).astype(BF16)
    return hi, lo


def _dot3(a, b, dims=(((1,), (0,)), ((), ()))):
    ah, al = _split_bf16(a)
    bh, bl = _split_bf16(b)
    d = functools.partial(lax.dot_general, dimension_numbers=dims, preferred_element_type=F32)
    return d(ah, bh) + (d(al, bh) + d(ah, bl))


def _mod_kernel(c_ref, w_ref, b_ref, o_ref):
    o_ref[...] = _dot3(c_ref[...], w_ref[...]) + b_ref[...]


def _modulation(cond8, w, b):
    d, n = w.shape
    tn = 1536
    return pl.pallas_call(
        _mod_kernel,
        out_shape=jax.ShapeDtypeStruct((8, n), F32),
        grid=(n // tn,),
        in_specs=[pl.BlockSpec((8, d), lambda j: (0, 0)),
                  pl.BlockSpec((d, tn), lambda j: (0, j)),
                  pl.BlockSpec((1, tn), lambda j: (0, j))],
        out_specs=pl.BlockSpec((8, tn), lambda j: (0, j)),
        compiler_params=_cparams(("arbitrary",)),
    )(cond8, w, b.reshape(1, n))


def _inproj_kernel(x_ref, mod_ref, w_ref, o_ref, *, sigmoid, chunk):
    m = mod_ref[...]
    xm = (x_ref[...] * (1.0 + m[1:2, :]) + m[0:1, :]).astype(BF16)
    n = w_ref.shape[1]
    for j in range(n // chunk):
        acc = jnp.dot(xm, w_ref[:, j * chunk:(j + 1) * chunk], preferred_element_type=F32)
        if sigmoid:
            acc = jax.nn.sigmoid(acc)
        o_ref[:, j * chunk:(j + 1) * chunk] = acc.astype(o_ref.dtype)


def _inproj(h, mod, group_of_block, w, *, sigmoid, out_dtype):
    t, d = h.shape
    n = w.shape[1]
    tb = TOKEN_BLOCK
    return pl.pallas_call(
        functools.partial(_inproj_kernel, sigmoid=sigmoid, chunk=512),
        out_shape=jax.ShapeDtypeStruct((t, n), out_dtype),
        grid=(t // tb,),
        in_specs=[pl.BlockSpec((tb, d), lambda i: (i, 0)),
                  pl.BlockSpec((None, 6, d), lambda i: (group_of_block(i), 0, 0)),
                  pl.BlockSpec((d, n), lambda i: (0, 0))],
        out_specs=pl.BlockSpec((tb, n), lambda i: (i, 0)),
        compiler_params=_cparams(("parallel",)),
    )(h, mod, w)


def _flash_kernel(qt_ref, kx_ref, vtx_ref, kl_ref, vtl_ref, o_ref, s_ref, *, n_ctx_qblocks):
    i = pl.program_id(2)
    qt = qt_ref[...]
    w = qt.shape[1]
    nlc = kl_ref.shape[0]

    def scores(k):
        return jnp.dot(k, qt, preferred_element_type=F32)

    def absorb(s, vt, carry):
        m, acc = carry
        m_new = jnp.maximum(m, jnp.max(s, axis=0, keepdims=True))
        p = jnp.exp2(s - m_new).astype(BF16)
        acc = acc * jnp.exp2(m - m_new) + jnp.dot(vt, p, preferred_element_type=F32)
        return m_new, acc

    carry = (jnp.full((1, w), -jnp.inf, F32), jnp.zeros((vtx_ref.shape[0], w), F32))
    carry = absorb(scores(kx_ref[...]), vtx_ref[...], carry)
    s_ref[0] = scores(kl_ref[0])

    def pair(j, carry):
        c0 = 2 * j
        s_ref[1] = scores(kl_ref[c0 + 1])
        carry = absorb(s_ref[0], vtl_ref[c0], carry)
        s_ref[0] = scores(kl_ref[jnp.minimum(c0 + 2, nlc - 1)])
        return absorb(s_ref[1], vtl_ref[c0 + 1], carry)

    n_pairs = jnp.where(i < n_ctx_qblocks, 0, nlc // 2)
    _, acc = lax.fori_loop(0, n_pairs, pair, carry)
    o_ref[...] = acc[:HEAD_DIM] * (1.0 / acc[HEAD_DIM:HEAD_DIM + 1])


def _flash(qt, k, v, n_ctx):
    b, g, nqb, dh, w = qt.shape
    lt = k.shape[2]
    s = lt - n_ctx
    tk = next(c for c in (ATT_TK, 512, 256, 128) if s % (2 * c) == 0)
    nlc = s // tk
    ones = jnp.concatenate([jnp.ones((b, g, lt, 1), BF16), jnp.zeros((b, g, lt, ATT_VPAD - dh - 1), BF16)], -1)
    vt = jnp.concatenate([v, ones], axis=-1).transpose(0, 1, 3, 2)
    dv = vt.shape[2]
    kx, kl = k[:, :, :n_ctx], k[:, :, n_ctx:].reshape(b, g, nlc, tk, dh)
    vtx = vt[..., :n_ctx]
    vtl = vt[..., n_ctx:].reshape(b, g, dv, nlc, tk).transpose(0, 1, 3, 2, 4)
    per_group = lambda shape: pl.BlockSpec((None, None) + shape, lambda bi, gi, i: (bi, gi) + (0,) * len(shape))
    return pl.pallas_call(
        functools.partial(_flash_kernel, n_ctx_qblocks=n_ctx // (w // 2)),
        out_shape=jax.ShapeDtypeStruct((b, g, nqb, dh, w), F32),
        grid=(b, g, nqb),
        in_specs=[pl.BlockSpec((None, None, None, dh, w), lambda bi, gi, i: (bi, gi, i, 0, 0)),
                  per_group((n_ctx, dh)), per_group((dv, n_ctx)),
                  per_group((nlc, tk, dh)), per_group((nlc, dv, tk))],
        out_specs=pl.BlockSpec((None, None, None, dh, w), lambda bi, gi, i: (bi, gi, i, 0, 0)),
        scratch_shapes=[pltpu.VMEM((2, tk, w), F32)],
        compiler_params=_cparams(("parallel", "parallel", "arbitrary")),
    )(qt, kx, vtx, kl, vtl)


def _window_kernel(q_ref, kp_ref, kc_ref, kn_ref, vp_ref, vc_ref, vn_ref, kx_ref, vx_ref, sink_ref,
                   o_ref, *, n_ctx_blocks, n_blocks):
    qb = pl.program_id(2)
    q = q_ref[...]
    rows = q.shape[0]
    nt = (((1,), (1,)), ((), ()))
    sdot = functools.partial(lax.dot_general, dimension_numbers=nt, preferred_element_type=F32)
    qi = lax.broadcasted_iota(jnp.int32, (rows, WINDOW), 0) % WINDOW
    kj = lax.broadcasted_iota(jnp.int32, (rows, WINDOW), 1)
    neg = -jnp.inf
    off_p = jnp.where(qb >= n_ctx_blocks + 1, 0, 2 * WINDOW)
    off_c = jnp.where(qb >= n_ctx_blocks, 0, 2 * WINDOW)
    off_n = jnp.where(jnp.logical_and(qb >= n_ctx_blocks, qb <= n_blocks - 2), 0, 2 * WINDOW)
    s_p = jnp.where(kj >= qi + off_p, sdot(q, kp_ref[...]), neg)
    s_c = jnp.where(kj >= off_c, sdot(q, kc_ref[...]), neg)
    s_n = jnp.where(kj <= qi - off_n, sdot(q, kn_ref[...]), neg)
    kx = kx_ref[...].reshape(-1, HEAD_DIM)
    vx = vx_ref[...].reshape(-1, HEAD_DIM)
    s_x = sdot(q, kx)
    sink = sink_ref[...]
    m = jnp.maximum(jnp.maximum(jnp.max(s_p, axis=1, keepdims=True), jnp.max(s_c, axis=1, keepdims=True)),
                    jnp.maximum(jnp.max(s_n, axis=1, keepdims=True), jnp.max(s_x, axis=1, keepdims=True)))
    m = jnp.maximum(m, sink)
    e_p, e_c, e_n, e_x = (jnp.exp2(s - m) for s in (s_p, s_c, s_n, s_x))
    den = (jnp.sum(e_p, axis=1, keepdims=True) + jnp.sum(e_c, axis=1, keepdims=True)
           + jnp.sum(e_n, axis=1, keepdims=True) + jnp.sum(e_x, axis=1, keepdims=True)
           + jnp.exp2(sink - m))
    pv = functools.partial(jnp.dot, preferred_element_type=F32)
    o = (pv(e_p.astype(BF16), vp_ref[...]) + pv(e_c.astype(BF16), vc_ref[...])
         + pv(e_n.astype(BF16), vn_ref[...]) + pv(e_x.astype(BF16), vx))
    o_ref[...] = o / den


def _window(q, k, v, sink, n_ctx_blocks):
    b, g, nb, rows, dh = q.shape
    blk = k.shape[3]
    lo, hi = n_ctx_blocks, nb - 1

    def nbr(delta):
        return lambda bi, gi, i: (bi, gi, jnp.clip(i + delta, lo, hi), 0, 0)

    kv_spec = lambda d: pl.BlockSpec((None, None, None, blk, dh), nbr(d))
    ctx_spec = pl.BlockSpec((None, None, n_ctx_blocks, blk, dh), lambda bi, gi, i: (bi, gi, 0, 0, 0))
    return pl.pallas_call(
        functools.partial(_window_kernel, n_ctx_blocks=n_ctx_blocks, n_blocks=nb),
        out_shape=jax.ShapeDtypeStruct((b, g, nb, rows, dh), F32),
        grid=(b, g, nb),
        in_specs=[pl.BlockSpec((None, None, None, rows, dh), lambda bi, gi, i: (bi, gi, i, 0, 0)),
                  kv_spec(-1), kv_spec(0), kv_spec(1), kv_spec(-1), kv_spec(0), kv_spec(1),
                  ctx_spec, ctx_spec,
                  pl.BlockSpec((None, rows, 1), lambda bi, gi, i: (gi, 0, 0))],
        out_specs=pl.BlockSpec((None, None, None, rows, dh), lambda bi, gi, i: (bi, gi, i, 0, 0)),
        compiler_params=_cparams(("parallel", "parallel", "arbitrary")),
    )(q, k, k, k, v, v, v, k, v, sink)


def _retention_kernel(q_ref, kt_ref, v_ref, dmat_ref, xi_ref, zeta_ref, gch_ref, o_ref, st_ref):
    t = pl.program_id(2)

    @pl.when(t == 0)
    def _():
        st_ref[...] = jnp.zeros_like(st_ref)

    for hd in range(RET_HEADS):
        q = q_ref[hd]
        kt = kt_ref[hd]
        v = v_ref[hd]
        st = st_ref[hd]
        inner = _dot3(q, kt) * dmat_ref[hd]
        o_ref[hd] = _dot3(inner, v) + _dot3(q, st) * xi_ref[hd]
        st_ref[hd] = st * gch_ref[hd] + _dot3(kt * zeta_ref[hd], v)


def _retention_call(q, kt, v, dmat, xi, zeta, gch, n_ctx_chunks):
    b, hh, lt, dk = q.shape
    dv = v.shape[-1]
    c = RET_CHUNK
    nch = lt // c

    def blk(d, ti):
        back = jnp.where(ti < n_ctx_chunks, n_ctx_chunks - 1 - ti, nch - 1 - (ti - n_ctx_chunks))
        return jnp.where(d == 0, ti, back)

    tab = lambda shape: pl.BlockSpec((None,) + shape, lambda d, bi, ti: (d,) + (0,) * len(shape))
    return pl.pallas_call(
        _retention_kernel,
        out_shape=jax.ShapeDtypeStruct((2, b, hh, lt, dv), F32),
        grid=(2, b, nch),
        in_specs=[pl.BlockSpec((None, hh, c, dk), lambda d, bi, ti: (bi, 0, blk(d, ti), 0)),
                  pl.BlockSpec((None, hh, dk, c), lambda d, bi, ti: (bi, 0, 0, blk(d, ti))),
                  pl.BlockSpec((None, hh, c, dv), lambda d, bi, ti: (bi, 0, blk(d, ti), 0)),
                  tab((hh, c, c)), tab((hh, c, 1)), tab((hh, 1, c)), tab((hh, 1, 1))],
        out_specs=pl.BlockSpec((None, None, hh, c, dv), lambda d, bi, ti: (d, bi, 0, blk(d, ti), 0)),
        scratch_shapes=[pltpu.VMEM((hh, dk, dv), F32)],
        compiler_params=_cparams(("parallel", "parallel", "arbitrary")),
    )(q, kt, v, dmat, xi, zeta, gch)


def _layer_norm_rows(z, ln):
    mu = jnp.mean(z, axis=-1, keepdims=True)
    zc = z - mu
    var = jnp.mean(zc * zc, axis=-1, keepdims=True)
    return zc * lax.rsqrt(var + LN_EPS) * ln[0:1, :] + ln[1:2, :]


def _merge_kernel(oa_ref, ob_ref, oc_ref, od_ref, g_ref, h_ref, mod_ref, wb_ref, wo_ref, ln_ref, o_ref,
                  *, alpha):
    d = h_ref.shape[1]
    m = None
    for i, o_i in enumerate((oa_ref, ob_ref, oc_ref, od_ref)):
        t = g_ref[:, i * d:(i + 1) * d].astype(F32) * jnp.dot(o_i[...], wb_ref[i], preferred_element_type=F32)
        m = t if m is None else m + t
    y = jnp.dot(m.astype(BF16), wo_ref[...], preferred_element_type=F32)
    z = alpha * h_ref[...] + mod_ref[2:3, :] * y
    o_ref[...] = _layer_norm_rows(z, ln_ref[...])


def _merge(outs, gates, h, mod, group_of_block, wb, wo, ln, alpha):
    t, d = h.shape
    tb = TOKEN_BLOCK
    bw = outs[0].shape[1]
    row = lambda n: pl.BlockSpec((tb, n), lambda i: (i, 0))
    return pl.pallas_call(
        functools.partial(_merge_kernel, alpha=alpha),
        out_shape=jax.ShapeDtypeStruct((t, d), F32),
        grid=(t // tb,),
        in_specs=[row(bw), row(bw), row(bw), row(bw), row(N_BRANCH * d), row(d),
                  pl.BlockSpec((None, 6, d), lambda i: (group_of_block(i), 0, 0)),
                  pl.BlockSpec((N_BRANCH, bw, d), lambda i: (0, 0, 0)),
                  pl.BlockSpec((d, d), lambda i: (0, 0)),
                  pl.BlockSpec((2, d), lambda i: (0, 0))],
        out_specs=row(d),
        compiler_params=_cparams(("parallel",)),
    )(*outs, gates, h, mod, wb, wo, ln)


def _top_rows(s, n):
    out = []
    cur = s
    for r in range(n):
        mx = jnp.max(cur, axis=0, keepdims=True)
        out.append(mx)
        if r + 1 < n:
            cur = jnp.where(cur == mx, -jnp.inf, cur)
    return out


def _peer_route_kernel(h_ref, mod_ref, wh_ref, wl_ref, sk_ref, x_ref, s1_ref, b1_ref, th_ref, az_ref,
                       cand_ref):
    m = mod_ref[...]
    u = h_ref[...] * (1.0 + m[4:5, :]) + m[3:4, :]
    x_ref[...] = u.astype(BF16)
    uh, ul = _split_bf16(u)
    d = functools.partial(jnp.dot, preferred_element_type=F32)
    nk = PEER_NK
    nt = (((1,), (1,)), ((), ()))
    k1 = PEER_TOPK + 1
    for hd in range(PEER_HEADS):
        st = []
        for p in range(2):
            c0 = (2 * hd + p) * PEER_DQ
            wh = wh_ref[:, c0:c0 + PEER_DQ]
            wl = wl_ref[:, c0:c0 + PEER_DQ]
            q = d(uh, wh) + (d(ul, wh) + d(uh, wl))
            st.append(_dot3(sk_ref[hd, p], q, nt))
        top0 = _top_rows(st[0], k1)
        top1 = _top_rows(st[1], k1)
        r = 0
        for p0 in range(k1):
            for p1 in range(k1 // (p0 + 1)):
                cand_ref[r:r + 1, :] = top0[p0] + top1[p1]
                r += 1
        cand_ref[r:, :] = jnp.full((cand_ref.shape[0] - r, cand_ref.shape[1]), -jnp.inf, F32)
        cand = cand_ref[...]
        ctop = _top_rows(cand, k1)
        tau = 0.5 * (ctop[PEER_TOPK - 1] + ctop[PEER_TOPK])
        mx = top0[0] + top1[0]
        z = jnp.sum(jnp.where(cand >= tau, jnp.exp(cand - mx), 0.0), axis=0, keepdims=True)
        s1_ref[hd] = st[1]
        b1_ref[hd] = jnp.exp(st[1] - top1[0])
        th_ref[hd] = tau - st[0]
        az_ref[hd] = jnp.exp(st[0] - top0[0]) / z


def _peer_route(h, mod, group_of_block, wq_hi, wq_lo, subkeys):
    t, d = h.shape
    tb = TOKEN_BLOCK
    hh, nk = PEER_HEADS, PEER_NK
    st_shape = jax.ShapeDtypeStruct((hh, nk, t), F32)
    st_spec = pl.BlockSpec((hh, nk, tb), lambda i: (0, 0, i))
    nq = wq_hi.shape[1]
    return pl.pallas_call(
        _peer_route_kernel,
        out_shape=(jax.ShapeDtypeStruct((t, d), BF16), st_shape, st_shape, st_shape, st_shape),
        grid=(t // tb,),
        in_specs=[pl.BlockSpec((tb, d), lambda i: (i, 0)),
                  pl.BlockSpec((None, 6, d), lambda i: (group_of_block(i), 0, 0)),
                  pl.BlockSpec((d, nq), lambda i: (0, 0)),
                  pl.BlockSpec((d, nq), lambda i: (0, 0)),
                  pl.BlockSpec((hh, 2, nk, PEER_DQ), lambda i: (0, 0, 0, 0))],
        out_specs=(pl.BlockSpec((tb, d), lambda i: (i, 0)), st_spec, st_spec, st_spec, st_spec),
        scratch_shapes=[pltpu.VMEM((PEER_CAND_ROWS, tb), F32)],
        compiler_params=_cparams(("parallel",)),
    )(h, mod, wq_hi, wq_lo, subkeys)


def _gelu_tanh(x):
    return 0.5 * x * (1.0 + jnp.tanh(math.sqrt(2.0 / math.pi) * (x + 0.044715 * (x * x * x))))


def _peer_dense_kernel(xt_ref, u_ref, vt_ref, s1_ref, b1_ref, th_ref, az_ref, yt_ref, g_ref):
    c = pl.program_id(1)
    nk = PEER_NK
    rows_per_step = u_ref.shape[0] // nk

    @pl.when(c == 0)
    def _():
        yt_ref[...] = jnp.zeros_like(yt_ref)

    act = _gelu_tanh(jnp.dot(u_ref[...], xt_ref[...], preferred_element_type=F32))
    for ii in range(rows_per_step):
        i = c * rows_per_step + ii
        w = None
        for hd in range(PEER_HEADS):
            th = th_ref[hd, pl.ds(i, 1), :]
            az = az_ref[hd, pl.ds(i, 1), :]
            t = jnp.where(s1_ref[hd] >= th, b1_ref[hd], 0.0) * az
            w = t if w is None else w + t
        g_ref[ii * nk:(ii + 1) * nk, :] = (w * act[ii * nk:(ii + 1) * nk, :]).astype(BF16)
    yt_ref[...] += jnp.dot(vt_ref[...], g_ref[...], preferred_element_type=F32)


def _peer_dense(xt, u, vt, s1, b1, th, az):
    d, t = xt.shape
    n = u.shape[0]
    tp = PEER_TOKENS
    ec = PEER_EXPERTS
    hh, nk = PEER_HEADS, PEER_NK
    st_spec = pl.BlockSpec((hh, nk, tp), lambda i, c: (0, 0, i))
    return pl.pallas_call(
        _peer_dense_kernel,
        out_shape=jax.ShapeDtypeStruct((d, t), F32),
        grid=(t // tp, n // ec),
        in_specs=[pl.BlockSpec((d, tp), lambda i, c: (0, i)),
                  pl.BlockSpec((ec, d), lambda i, c: (c, 0)),
                  pl.BlockSpec((d, ec), lambda i, c: (0, c)),
                  st_spec, st_spec, st_spec, st_spec],
        out_specs=pl.BlockSpec((d, tp), lambda i, c: (0, i)),
        scratch_shapes=[pltpu.VMEM((ec, tp), BF16)],
        compiler_params=_cparams(("parallel", "arbitrary")),
    )(xt, u, vt, s1, b1, th, az)


def _resid_ln_kernel(h_ref, y_ref, mod_ref, ln_ref, o_ref, *, alpha, gate_row):
    z = alpha * h_ref[...] + mod_ref[gate_row:gate_row + 1, :] * y_ref[...]
    o_ref[...] = _layer_norm_rows(z, ln_ref[...])


def _resid_ln(h, y, mod, group_of_block, ln, alpha, gate_row):
    t, d = h.shape
    tb = TOKEN_BLOCK
    row = pl.BlockSpec((tb, d), lambda i: (i, 0))
    return pl.pallas_call(
        functools.partial(_resid_ln_kernel, alpha=alpha, gate_row=gate_row),
        out_shape=jax.ShapeDtypeStruct((t, d), F32),
        grid=(t // tb,),
        in_specs=[row, row,
                  pl.BlockSpec((None, 6, d), lambda i: (group_of_block(i), 0, 0)),
                  pl.BlockSpec((2, d), lambda i: (0, 0))],
        out_specs=row,
        compiler_params=_cparams(("parallel",)),
    )(h, y, mod, ln)


def _axial_tables(s, n_ctx, d):
    rows = s // GRID_W
    row = jnp.broadcast_to(jnp.arange(rows, dtype=F32)[:, None], (rows, GRID_W)).reshape(-1)
    col = jnp.broadcast_to(jnp.arange(GRID_W, dtype=F32)[None, :], (rows, GRID_W)).reshape(-1)
    quarter = d // 4
    inv = ROPE_THETA ** (-jnp.arange(quarter, dtype=F32) / quarter)
    ar, ac = row[:, None] * inv, col[:, None] * inv
    cos = jnp.concatenate([jnp.cos(ar), jnp.cos(ar), jnp.cos(ac), jnp.cos(ac)], axis=-1)
    sin = jnp.concatenate([-jnp.sin(ar), jnp.sin(ar), -jnp.sin(ac), jnp.sin(ac)], axis=-1)
    cos = jnp.concatenate([jnp.ones((n_ctx, d), F32), cos], axis=0)
    sin = jnp.concatenate([jnp.zeros((n_ctx, d), F32), sin], axis=0)
    return cos, sin


def _rope1d_tables(lt, d):
    half = d // 2
    inv = ROPE_THETA ** (-jnp.arange(half, dtype=F32) / half)
    ang = jnp.arange(lt, dtype=F32)[:, None] * inv
    return (jnp.concatenate([jnp.cos(ang), jnp.cos(ang)], axis=-1),
            jnp.concatenate([-jnp.sin(ang), jnp.sin(ang)], axis=-1))


def _swap_perm():
    cols = []
    for piece, flip in ((0, 16), (1, 16), (3, 8), (4, 8), (6, 16), (7, 16), (9, 32), (10, 32)):
        base = np.arange(PIECE_OFF[piece], PIECE_OFF[piece + 1])
        cols.append(PIECE_OFF[piece] + ((base - PIECE_OFF[piece]) ^ flip))
    return np.concatenate(cols)


SWAP_COLS = _swap_perm()
SWAP_PIECES = (0, 1, 3, 4, 6, 7, 9, 10)
SWAP_OFF = dict(zip(SWAP_PIECES, np.cumsum([0] + [PIECES[p] for p in SWAP_PIECES])[:-1] + MIX_COLS))


def _rms_heads(x, xs, gain, gain_s):
    r = lax.rsqrt(jnp.mean(x * x, axis=-1, keepdims=True) + RMS_EPS)
    return x * r * gain, xs * r * gain_s


def kernel(x, c, ctx, c_ctx, w_mod, b_mod, w_in, qk_gain, diff_lambda, diff_subln, win_sink, ret_decay,
           ret_norm, w_branch, w_out, ln_attn, ln_ffn, peer_wq, peer_subkeys, peer_u, peer_v):
    b, s, d = x.shape
    n_ctx = ctx.shape[1]
    depth = w_mod.shape[0]
    lt = n_ctx + s
    t = b * lt
    tb = TOKEN_BLOCK
    assert n_ctx % tb == 0 and s % tb == 0 and t % PEER_TOKENS == 0
    alpha = (2 * depth) ** 0.25
    blocks_per_batch = lt // tb
    ctx_blocks = n_ctx // tb

    def group_of_block(i):
        return jnp.where(i % blocks_per_batch < ctx_blocks, b, i // blocks_per_batch)

    cos64, sin64 = _axial_tables(s, n_ctx, HEAD_DIM)
    cos32, sin32 = _axial_tables(s, n_ctx, DIFF_DIM)
    cos1d, sin1d = _rope1d_tables(lt, RET_DK)
    cond8 = jnp.zeros((8, d), F32).at[:b].set(jax.nn.silu(c)).at[b].set(jax.nn.silu(c_ctx))

    h = jnp.concatenate([ctx, x], axis=1).reshape(t, d)
    nqb = lt // ATT_TQ
    nwb = lt // WINDOW

    for l in range(depth):
        mod = _modulation(cond8, w_mod[l], b_mod[l]).reshape(8, 6, d)
        w_mix = w_in[l, :, :MIX_COLS]
        w_ext = jnp.concatenate([w_mix, w_mix[:, SWAP_COLS]], axis=1).astype(BF16)
        w_gate = w_in[l, :, MIX_COLS:].astype(BF16)
        proj = _inproj(h, mod, group_of_block, w_ext, sigmoid=False, out_dtype=F32)
        gates = _inproj(h, mod, group_of_block, w_gate, sigmoid=True, out_dtype=BF16)
        proj = proj.reshape(b, lt, -1)

        def piece(p, heads):
            return proj[..., PIECE_OFF[p]:PIECE_OFF[p + 1]].reshape(b, lt, heads, -1)

        def swapped(p, heads):
            return proj[..., SWAP_OFF[p]:SWAP_OFF[p] + PIECES[p]].reshape(b, lt, heads, -1)

        def rope(p, heads, cos, sin):
            return piece(p, heads) * cos[:, None, :] + swapped(p, heads) * sin[:, None, :]

        g0, g1 = qk_gain[l, 0], qk_gain[l, 1]
        swap64 = np.arange(HEAD_DIM) ^ 16
        qa, qas = _rms_heads(piece(0, A_HEADS), swapped(0, A_HEADS), g0, g0[swap64])
        ka, kas = _rms_heads(piece(1, A_KV_HEADS), swapped(1, A_KV_HEADS), g1, g1[swap64])
        qa = qa * cos64[:, None, :] + qas * sin64[:, None, :]
        ka = ka * cos64[:, None, :] + kas * sin64[:, None, :]
        va = piece(2, A_KV_HEADS)
        cos32x2 = jnp.concatenate([cos32, cos32], axis=-1)
        sin32x2 = jnp.concatenate([sin32, sin32], axis=-1)
        qd = rope(3, DIFF_HEADS, cos32x2, sin32x2)
        kd = rope(4, DIFF_HEADS, cos32x2, sin32x2)
        vd = piece(5, DIFF_HEADS)
        first = (jnp.arange(2 * DIFF_DIM) < DIFF_DIM).astype(F32)

        sc_a = HEAD_DIM ** -0.5 * LOG2E
        sc_d = DIFF_DIM ** -0.5 * LOG2E
        qa_t = (qa * sc_a).reshape(b, nqb, ATT_TQ, A_KV_HEADS, 2, HEAD_DIM).transpose(0, 3, 1, 5, 4, 2)
        qd_s = jnp.stack([qd * first, qd * (1.0 - first)], axis=3) * sc_d
        qd_t = qd_s.reshape(b, nqb, ATT_TQ, DIFF_HEADS, 2, HEAD_DIM).transpose(0, 3, 1, 5, 4, 2)
        qt = jnp.concatenate([qa_t, qd_t], axis=1).reshape(b, -1, nqb, HEAD_DIM, 2 * ATT_TQ).astype(BF16)
        k_all = jnp.concatenate([ka, kd], axis=2).transpose(0, 2, 1, 3).astype(BF16)
        v_all = jnp.concatenate([va, vd], axis=2).transpose(0, 2, 1, 3).astype(BF16)
        n_groups = k_all.shape[1]
        ot = _flash(qt, k_all, v_all, n_ctx)
        ot = ot.reshape(b, n_groups, nqb, HEAD_DIM, 2, ATT_TQ)
        oa = ot[:, :A_KV_HEADS].transpose(0, 2, 5, 1, 4, 3).reshape(b, lt, A_HEADS * HEAD_DIM)
        lam_init = 0.8 - 0.6 * math.exp(-0.3 * l)
        lp = diff_lambda[l].astype(F32)
        lam = jnp.exp(jnp.sum(lp[0] * lp[1])) - jnp.exp(jnp.sum(lp[2] * lp[3])) + lam_init
        od_ = ot[:, A_KV_HEADS:, :, :, 0, :] - lam * ot[:, A_KV_HEADS:, :, :, 1, :]
        od_ = od_.transpose(0, 2, 4, 1, 3).reshape(b, lt, DIFF_HEADS, 2 * DIFF_DIM)
        od_ = od_ * lax.rsqrt(jnp.mean(od_ * od_, axis=-1, keepdims=True) + RMS_EPS) * diff_subln[l]
        ob = (od_ * (1.0 - lam_init)).reshape(b, lt, -1)

        qw = rope(6, WIN_HEADS, cos64, sin64) * sc_a
        kw = rope(7, WIN_KV_HEADS, cos64, sin64)
        vw = piece(8, WIN_KV_HEADS)
        qw = (qw.reshape(b, nwb, WINDOW, WIN_KV_HEADS, 2, HEAD_DIM).transpose(0, 3, 1, 4, 2, 5)
              .reshape(b, WIN_KV_HEADS, nwb, 2 * WINDOW, HEAD_DIM).astype(BF16))
        kw = kw.transpose(0, 2, 1, 3).reshape(b, WIN_KV_HEADS, nwb, WINDOW, HEAD_DIM).astype(BF16)
        vw = vw.transpose(0, 2, 1, 3).reshape(b, WIN_KV_HEADS, nwb, WINDOW, HEAD_DIM).astype(BF16)
        sink = jnp.repeat(win_sink[l].astype(F32) * LOG2E, WINDOW).reshape(WIN_KV_HEADS, 2 * WINDOW, 1)
        ow = _window(qw, kw, vw, sink, n_ctx // WINDOW)
        oc = (ow.reshape(b, WIN_KV_HEADS, nwb, 2, WINDOW, HEAD_DIM).transpose(0, 2, 4, 1, 3, 5)
              .reshape(b, lt, -1))

        qr = rope(9, RET_HEADS, cos1d, sin1d).transpose(0, 2, 1, 3)
        krt = (rope(10, RET_HEADS, cos1d, sin1d) * RET_DK ** -0.5).transpose(0, 2, 3, 1)
        vr = piece(11, RET_HEADS).transpose(0, 2, 1, 3)
        lg = jax.nn.log_sigmoid(ret_decay[l].astype(F32))
        idx = jnp.arange(RET_CHUNK, dtype=F32)
        diff = idx[:, None] - idx[None, :]
        lg3 = lg[:, :, None, None]
        dm_f = jnp.exp(jnp.where(diff >= 0, diff * lg3[0], -jnp.inf))
        dm_b = jnp.exp(jnp.where(diff <= 0, -diff * lg3[1], -jnp.inf))
        dmat = jnp.stack([dm_f, dm_b])
        xi = jnp.stack([jnp.exp((idx + 1.0) * lg[0][:, None]), jnp.exp((RET_CHUNK - idx) * lg[1][:, None])])
        zeta = jnp.stack([jnp.exp((RET_CHUNK - 1.0 - idx) * lg[0][:, None]), jnp.exp(idx * lg[1][:, None])])
        gch = jnp.exp(RET_CHUNK * lg)
        o_ret = _retention_call(qr, krt, vr, dmat, xi[..., None], zeta[:, :, None, :],
                                gch[:, :, None, None], n_ctx // RET_CHUNK)
        o_ret = (o_ret[0] + o_ret[1]).transpose(0, 2, 1, 3)
        mu = jnp.mean(o_ret, -1, keepdims=True)
        var = jnp.mean(jnp.square(o_ret - mu), -1, keepdims=True)
        on = ((o_ret - mu) * lax.rsqrt(var + LN_EPS)).reshape(b, lt, -1) * ret_norm[l, 0] + ret_norm[l, 1]
        od = on * jax.nn.silu(proj[..., PIECE_OFF[12]:PIECE_OFF[13]])

        outs = [o.reshape(t, BRANCH_W).astype(BF16) for o in (oa, ob, oc, od)]
        h = _merge(outs, gates, h, mod, group_of_block, w_branch[l].astype(BF16), w_out[l].astype(BF16),
                   ln_attn[l], alpha)

        wq_hi, wq_lo = _split_bf16(peer_wq[l])
        xb, s1, b1, th, az = _peer_route(h, mod, group_of_block, wq_hi, wq_lo, peer_subkeys[l])
        yt = _peer_dense(xb.T, peer_u[l].astype(BF16), peer_v[l].T.astype(BF16), s1, b1, th, az)
        h = _resid_ln(h, yt.T, mod, group_of_block, ln_ffn[l], alpha, 5)

    return h.reshape(b, lt, d)[:, n_ctx:, :]
```

```python
import functools
import math

import numpy as np
import jax
import jax.numpy as jnp
from jax import lax
from jax.experimental import pallas as pl
from jax.experimental.pallas import tpu as pltpu

GRID_W = 64
HEAD_DIM = 64
ROPE_THETA = 10000.0
A_HEADS = 4
A_KV_HEADS = 2
DIFF_HEADS = 4
DIFF_DIM = 32
WIN_HEADS = 4
WIN_KV_HEADS = 2
WINDOW = 128
RET_HEADS = 4
RET_DK = 64
RET_DV = 64
RET_CHUNK = 128
N_BRANCH = 4
BRANCH_W = 256
PIECES = (
    A_HEADS * HEAD_DIM, A_KV_HEADS * HEAD_DIM, A_KV_HEADS * HEAD_DIM,
    2 * DIFF_HEADS * DIFF_DIM, 2 * DIFF_HEADS * DIFF_DIM, DIFF_HEADS * 2 * DIFF_DIM,
    WIN_HEADS * HEAD_DIM, WIN_KV_HEADS * HEAD_DIM, WIN_KV_HEADS * HEAD_DIM,
    RET_HEADS * RET_DK, RET_HEADS * RET_DK, RET_HEADS * RET_DV, RET_HEADS * RET_DV,
)
MIX_COLS = sum(PIECES)
PIECE_OFF = tuple(int(v) for v in np.cumsum((0,) + PIECES))
PEER_HEADS = 8
PEER_NK = 128
PEER_TOPK = 16
PEER_DQ = 128
PEER_CAND_ROWS = -(-sum((PEER_TOPK + 1) // (p + 1) for p in range(PEER_TOPK + 1)) // 8) * 8
LN_EPS = 1e-5
RMS_EPS = 1e-6
LOG2E = 1.4426950408889634

F32 = jnp.float32
BF16 = jnp.bfloat16

TOKEN_BLOCK = 256
ATT_TQ = 256
ATT_TK = 1024
ATT_VPAD = 80
PEER_TOKENS = 512
PEER_EXPERTS = 1024
VMEM_LIMIT = 56 * 1024 * 1024
BF16_ROWS = 16


def _cparams(sem):
    return pltpu.CompilerParams(dimension_semantics=sem, vmem_limit_bytes=VMEM_LIMIT)


def _split_bf16(a):
    hi = a.astype(BF16)
    lo = (a - hi.astype(F32)).astype(BF16)
    return hi, lo


def _dot3(a, b, dims=(((1,), (0,)), ((), ()))):
    ah, al = _split_bf16(a)
    bh, bl = _split_bf16(b)
    d = functools.partial(lax.dot_general, dimension_numbers=dims, preferred_element_type=F32)
    return d(ah, bh) + (d(al, bh) + d(ah, bl))


def _mod_kernel(c_ref, w_ref, b_ref, o_ref):
    o_ref[...] = _dot3(c_ref[...], w_ref[...]) + b_ref[...]


def _modulation(cond8, w, b):
    d, n = w.shape
    tn = 1536
    return pl.pallas_call(
        _mod_kernel,
        out_shape=jax.ShapeDtypeStruct((8, n), F32),
        grid=(n // tn,),
        in_specs=[pl.BlockSpec((8, d), lambda j: (0, 0)),
                  pl.BlockSpec((d, tn), lambda j: (0, j)),
                  pl.BlockSpec((1, tn), lambda j: (0, j))],
        out_specs=pl.BlockSpec((8, tn), lambda j: (0, j)),
        compiler_params=_cparams(("arbitrary",)),
    )(cond8, w, b.reshape(1, n))


def _inproj_kernel(x_ref, mod_ref, w_ref, o_ref, *, sigmoid, chunk):
    m = mod_ref[...]
    xm = (x_ref[...] * (1.0 + m[1:2, :]) + m[0:1, :]).astype(BF16)
    n = w_ref.shape[1]
    for j in range(n // chunk):
        acc = jnp.dot(xm, w_ref[:, j * chunk:(j + 1) * chunk], preferred_element_type=F32)
        if sigmoid:
            acc = jax.nn.sigmoid(acc)
        o_ref[:, j * chunk:(j + 1) * chunk] = acc.astype(o_ref.dtype)


def _inproj(h, mod, group_of_block, w, *, sigmoid, out_dtype):
    t, d = h.shape
    n = w.shape[1]
    tb = TOKEN_BLOCK
    return pl.pallas_call(
        functools.partial(_inproj_kernel, sigmoid=sigmoid, chunk=512),
        out_shape=jax.ShapeDtypeStruct((t, n), out_dtype),
        grid=(t // tb,),
        in_specs=[pl.BlockSpec((tb, d), lambda i: (i, 0)),
                  pl.BlockSpec((None, 6, d), lambda i: (group_of_block(i), 0, 0)),
                  pl.BlockSpec((d, n), lambda i: (0, 0))],
        out_specs=pl.BlockSpec((tb, n), lambda i: (i, 0)),
        compiler_params=_cparams(("parallel",)),
    )(h, mod, w)


def _flash_kernel(qt_ref, kx_ref, vtx_ref, kl_ref, vtl_ref, o_ref, s_ref, *, n_ctx_qblocks):
    i = pl.program_id(2)
    qt = qt_ref[...]
    w = qt.shape[1]
    nlc = kl_ref.shape[0]

    def scores(k):
        return jnp.dot(k, qt, preferred_element_type=F32)

    def absorb(s, vt, carry):
        m, acc = carry
        m_new = jnp.maximum(m, jnp.max(s, axis=0, keepdims=True))
        p = jnp.exp2(s - m_new).astype(BF16)
        acc = acc * jnp.exp2(m - m_new) + jnp.dot(vt, p, preferred_element_type=F32)
        return m_new, acc

    carry = (jnp.full((1, w), -jnp.inf, F32), jnp.zeros((vtx_ref.shape[0], w), F32))
    carry = absorb(scores(kx_ref[...]), vtx_ref[...], carry)
    s_ref[0] = scores(kl_ref[0])

    def pair(j, carry):
        c0 = 2 * j
        s_ref[1] = scores(kl_ref[c0 + 1])
        carry = absorb(s_ref[0], vtl_ref[c0], carry)
        s_ref[0] = scores(kl_ref[jnp.minimum(c0 + 2, nlc - 1)])
        return absorb(s_ref[1], vtl_ref[c0 + 1], carry)

    n_pairs = jnp.where(i < n_ctx_qblocks, 0, nlc // 2)
    _, acc = lax.fori_loop(0, n_pairs, pair, carry)
    o_ref[...] = acc[:HEAD_DIM] * (1.0 / acc[HEAD_DIM:HEAD_DIM + 1])


def _flash(qt, k, v, n_ctx):
    b, g, nqb, dh, w = qt.shape
    lt = k.shape[2]
    s = lt - n_ctx
    tk = next(c for c in (ATT_TK, 512, 256, 128) if s % (2 * c) == 0)
    nlc = s // tk
    ones = jnp.concatenate([jnp.ones((b, g, lt, 1), BF16), jnp.zeros((b, g, lt, ATT_VPAD - dh - 1), BF16)], -1)
    vt = jnp.concatenate([v, ones], axis=-1).transpose(0, 1, 3, 2)
    dv = vt.shape[2]
    kx, kl = k[:, :, :n_ctx], k[:, :, n_ctx:].reshape(b, g, nlc, tk, dh)
    vtx = vt[..., :n_ctx]
    vtl = vt[..., n_ctx:].reshape(b, g, dv, nlc, tk).transpose(0, 1, 3, 2, 4)
    per_group = lambda shape: pl.BlockSpec((None, None) + shape, lambda bi, gi, i: (bi, gi) + (0,) * len(shape))
    return pl.pallas_call(
        functools.partial(_flash_kernel, n_ctx_qblocks=n_ctx // (w // 2)),
        out_shape=jax.ShapeDtypeStruct((b, g, nqb, dh, w), F32),
        grid=(b, g, nqb),
        in_specs=[pl.BlockSpec((None, None, None, dh, w), lambda bi, gi, i: (bi, gi, i, 0, 0)),
                  per_group((n_ctx, dh)), per_group((dv, n_ctx)),
                  per_group((nlc, tk, dh)), per_group((nlc, dv, tk))],
        out_specs=pl.BlockSpec((None, None, None, dh, w), lambda bi, gi, i: (bi, gi, i, 0, 0)),
        scratch_shapes=[pltpu.VMEM((2, tk, w), F32)],
        compiler_params=_cparams(("parallel", "parallel", "arbitrary")),
    )(qt, kx, vtx, kl, vtl)


def _window_kernel(q_ref, kp_ref, kc_ref, kn_ref, vp_ref, vc_ref, vn_ref, kx_ref, vx_ref, sink_ref,
                   o_ref, *, n_ctx_blocks, n_blocks):
    qb = pl.program_id(2)
    q = q_ref[...]
    rows = q.shape[0]
    nt = (((1,), (1,)), ((), ()))
    sdot = functools.partial(lax.dot_general, dimension_numbers=nt, preferred_element_type=F32)
    qi = lax.broadcasted_iota(jnp.int32, (rows, WINDOW), 0) % WINDOW
    kj = lax.broadcasted_iota(jnp.int32, (rows, WINDOW), 1)
    neg = -jnp.inf
    off_p = jnp.where(qb >= n_ctx_blocks + 1, 0, 2 * WINDOW)
    off_c = jnp.where(qb >= n_ctx_blocks, 0, 2 * WINDOW)
    off_n = jnp.where(jnp.logical_and(qb >= n_ctx_blocks, qb <= n_blocks - 2), 0, 2 * WINDOW)
    s_p = jnp.where(kj >= qi + off_p, sdot(q, kp_ref[...]), neg)
    s_c = jnp.where(kj >= off_c, sdot(q, kc_ref[...]), neg)
    s_n = jnp.where(kj <= qi - off_n, sdot(q, kn_ref[...]), neg)
    kx = kx_ref[...].reshape(-1, HEAD_DIM)
    vx = vx_ref[...].reshape(-1, HEAD_DIM)
    s_x = sdot(q, kx)
    sink = sink_ref[...]
    m = jnp.maximum(jnp.maximum(jnp.max(s_p, axis=1, keepdims=True), jnp.max(s_c, axis=1, keepdims=True)),
                    jnp.maximum(jnp.max(s_n, axis=1, keepdims=True), jnp.max(s_x, axis=1, keepdims=True)))
    m = jnp.maximum(m, sink)
    e_p, e_c, e_n, e_x = (jnp.exp2(s - m) for s in (s_p, s_c, s_n, s_x))
    den = (jnp.sum(e_p, axis=1, keepdims=True) + jnp.sum(e_c, axis=1, keepdims=True)
           + jnp.sum(e_n, axis=1, keepdims=True) + jnp.sum(e_x, axis=1, keepdims=True)
           + jnp.exp2(sink - m))
    pv = functools.partial(jnp.dot, preferred_element_type=F32)
    o = (pv(e_p.astype(BF16), vp_ref[...]) + pv(e_c.astype(BF16), vc_ref[...])
         + pv(e_n.astype(BF16), vn_ref[...]) + pv(e_x.astype(BF16), vx))
    o_ref[...] = o / den


def _window(q, k, v, sink, n_ctx_blocks):
    b, g, nb, rows, dh = q.shape
    blk = k.shape[3]
    lo, hi = n_ctx_blocks, nb - 1

    def nbr(delta):
        return lambda bi, gi, i: (bi, gi, jnp.clip(i + delta, lo, hi), 0, 0)

    kv_spec = lambda d: pl.BlockSpec((None, None, None, blk, dh), nbr(d))
    ctx_spec = pl.BlockSpec((None, None, n_ctx_blocks, blk, dh), lambda bi, gi, i: (bi, gi, 0, 0, 0))
    return pl.pallas_call(
        functools.partial(_window_kernel, n_ctx_blocks=n_ctx_blocks, n_blocks=nb),
        out_shape=jax.ShapeDtypeStruct((b, g, nb, rows, dh), F32),
        grid=(b, g, nb),
        in_specs=[pl.BlockSpec((None, None, None, rows, dh), lambda bi, gi, i: (bi, gi, i, 0, 0)),
                  kv_spec(-1), kv_spec(0), kv_spec(1), kv_spec(-1), kv_spec(0), kv_spec(1),
                  ctx_spec, ctx_spec,
                  pl.BlockSpec((None, rows, 1), lambda bi, gi, i: (gi, 0, 0))],
        out_specs=pl.BlockSpec((None, None, None, rows, dh), lambda bi, gi, i: (bi, gi, i, 0, 0)),
        compiler_params=_cparams(("parallel", "parallel", "arbitrary")),
    )(q, k, k, k, v, v, v, k, v, sink)


def _retention_kernel(q_ref, kt_ref, v_ref, dmat_ref, xi_ref, zeta_ref, gch_ref, o_ref, st_ref):
    t = pl.program_id(2)

    @pl.when(t == 0)
    def _():
        st_ref[...] = jnp.zeros_like(st_ref)

    for hd in range(RET_HEADS):
        q = q_ref[hd]
        kt = kt_ref[hd]
        v = v_ref[hd]
        st = st_ref[hd]
        inner = _dot3(q, kt) * dmat_ref[hd]
        o_ref[hd] = _dot3(inner, v) + _dot3(q, st) * xi_ref[hd]
        st_ref[hd] = st * gch_ref[hd] + _dot3(kt * zeta_ref[hd], v)


def _retention_call(q, kt, v, dmat, xi, zeta, gch, n_ctx_chunks):
    b, hh, lt, dk = q.shape
    dv = v.shape[-1]
    c = RET_CHUNK
    nch = lt // c

    def blk(d, ti):
        back = jnp.where(ti < n_ctx_chunks, n_ctx_chunks - 1 - ti, nch - 1 - (ti - n_ctx_chunks))
        return jnp.where(d == 0, ti, back)

    tab = lambda shape: pl.BlockSpec((None,) + shape, lambda d, bi, ti: (d,) + (0,) * len(shape))
    return pl.pallas_call(
        _retention_kernel,
        out_shape=jax.ShapeDtypeStruct((2, b, hh, lt, dv), F32),
        grid=(2, b, nch),
        in_specs=[pl.BlockSpec((None, hh, c, dk), lambda d, bi, ti: (bi, 0, blk(d, ti), 0)),
                  pl.BlockSpec((None, hh, dk, c), lambda d, bi, ti: (bi, 0, 0, blk(d, ti))),
                  pl.BlockSpec((None, hh, c, dv), lambda d, bi, ti: (bi, 0, blk(d, ti), 0)),
                  tab((hh, c, c)), tab((hh, c, 1)), tab((hh, 1, c)), tab((hh, 1, 1))],
        out_specs=pl.BlockSpec((None, None, hh, c, dv), lambda d, bi, ti: (d, bi, 0, blk(d, ti), 0)),
        scratch_shapes=[pltpu.VMEM((hh, dk, dv), F32)],
        compiler_params=_cparams(("parallel", "parallel", "arbitrary")),
    )(q, kt, v, dmat, xi, zeta, gch)


def _layer_norm_rows(z, ln):
    mu = jnp.mean(z, axis=-1, keepdims=True)
    zc = z - mu
    var = jnp.mean(zc * zc, axis=-1, keepdims=True)
    return zc * lax.rsqrt(var + LN_EPS) * ln[0:1, :] + ln[1:2, :]


def _merge_kernel(oa_ref, ob_ref, oc_ref, od_ref, g_ref, h_ref, mod_ref, wb_ref, wo_ref, ln_ref, o_ref,
                  *, alpha):
    d = h_ref.shape[1]
    m = None
    for i, o_i in enumerate((oa_ref, ob_ref, oc_ref, od_ref)):
        t = g_ref[:, i * d:(i + 1) * d].astype(F32) * jnp.dot(o_i[...], wb_ref[i], preferred_element_type=F32)
        m = t if m is None else m + t
    y = jnp.dot(m.astype(BF16), wo_ref[...], preferred_element_type=F32)
    z = alpha * h_ref[...] + mod_ref[2:3, :] * y
    o_ref[...] = _layer_norm_rows(z, ln_ref[...])


def _merge(outs, gates, h, mod, group_of_block, wb, wo, ln, alpha):
    t, d = h.shape
    tb = TOKEN_BLOCK
    bw = outs[0].shape[1]
    row = lambda n: pl.BlockSpec((tb, n), lambda i: (i, 0))
    return pl.pallas_call(
        functools.partial(_merge_kernel, alpha=alpha),
        out_shape=jax.ShapeDtypeStruct((t, d), F32),
        grid=(t // tb,),
        in_specs=[row(bw), row(bw), row(bw), row(bw), row(N_BRANCH * d), row(d),
                  pl.BlockSpec((None, 6, d), lambda i: (group_of_block(i), 0, 0)),
                  pl.BlockSpec((N_BRANCH, bw, d), lambda i: (0, 0, 0)),
                  pl.BlockSpec((d, d), lambda i: (0, 0)),
                  pl.BlockSpec((2, d), lambda i: (0, 0))],
        out_specs=row(d),
        compiler_params=_cparams(("parallel",)),
    )(*outs, gates, h, mod, wb, wo, ln)


def _top_rows(s, n, with_rank=False):
    out = []
    cur = s
    rank = jnp.full(s.shape, float(n), F32) if with_rank else None
    for r in range(n):
        mx = jnp.max(cur, axis=0, keepdims=True)
        out.append(mx)
        if with_rank:
            rank = jnp.where(cur == mx, float(r), rank)
        if r + 1 < n:
            cur = jnp.where(cur == mx, -jnp.inf, cur)
    return (out, rank) if with_rank else out


def _peer_route_kernel(h_ref, mod_ref, wh_ref, wl_ref, sk_ref, x_ref, rk_ref, b1_ref, nn_ref, az_ref,
                       cand_ref):
    m = mod_ref[...]
    u = h_ref[...] * (1.0 + m[4:5, :]) + m[3:4, :]
    x_ref[...] = u.astype(BF16)
    uh, ul = _split_bf16(u)
    d = functools.partial(jnp.dot, preferred_element_type=F32)
    nk = PEER_NK
    nt = (((1,), (1,)), ((), ()))
    k1 = PEER_TOPK + 1
    for hd in range(PEER_HEADS):
        c0 = 2 * hd * PEER_DQ
        wh = wh_ref[:, c0:c0 + 2 * PEER_DQ]
        wl = wl_ref[:, c0:c0 + 2 * PEER_DQ]
        q = d(uh, wh) + (d(ul, wh) + d(uh, wl))
        st = [_dot3(sk_ref[hd, p], q[:, p * PEER_DQ:(p + 1) * PEER_DQ], nt) for p in range(2)]
        top0 = _top_rows(st[0], k1)
        top1, rank1 = _top_rows(st[1], k1, with_rank=True)
        r = 0
        for p0 in range(k1):
            for p1 in range(k1 // (p0 + 1)):
                cand_ref[r:r + 1, :] = top0[p0] + top1[p1]
                r += 1
        cand_ref[r:, :] = jnp.full((cand_ref.shape[0] - r, cand_ref.shape[1]), -jnp.inf, F32)
        cand = cand_ref[...]
        ctop = _top_rows(cand, k1)
        tau = 0.5 * (ctop[PEER_TOPK - 1] + ctop[PEER_TOPK])
        mx = top0[0] + top1[0]
        z = jnp.sum(jnp.where(cand >= tau, jnp.exp(cand - mx), 0.0), axis=0, keepdims=True)
        th = tau - st[0]
        nn = jnp.zeros_like(th)
        for q in range(PEER_TOPK):
            nn = nn + jnp.where(top1[q] >= th, 1.0, 0.0)
        rk_ref[hd] = rank1.astype(BF16)
        b1_ref[hd] = jnp.exp(st[1] - top1[0]).astype(BF16)
        nn_ref[hd] = nn
        az_ref[hd] = jnp.exp(st[0] - top0[0]) / z


def _peer_route(h, mod, group_of_block, wq_hi, wq_lo, subkeys):
    t, d = h.shape
    tb = TOKEN_BLOCK
    hh, nk = PEER_HEADS, PEER_NK
    st_shape = lambda dt: jax.ShapeDtypeStruct((hh, nk, t), dt)
    st_spec = pl.BlockSpec((hh, nk, tb), lambda i: (0, 0, i))
    nq = wq_hi.shape[1]
    return pl.pallas_call(
        _peer_route_kernel,
        out_shape=(jax.ShapeDtypeStruct((t, d), BF16), st_shape(BF16), st_shape(BF16), st_shape(F32),
                   st_shape(F32)),
        grid=(t // tb,),
        in_specs=[pl.BlockSpec((tb, d), lambda i: (i, 0)),
                  pl.BlockSpec((None, 6, d), lambda i: (group_of_block(i), 0, 0)),
                  pl.BlockSpec((d, nq), lambda i: (0, 0)),
                  pl.BlockSpec((d, nq), lambda i: (0, 0)),
                  pl.BlockSpec((hh, 2, nk, PEER_DQ), lambda i: (0, 0, 0, 0))],
        out_specs=(pl.BlockSpec((tb, d), lambda i: (i, 0)), st_spec, st_spec, st_spec, st_spec),
        scratch_shapes=[pltpu.VMEM((PEER_CAND_ROWS, tb), F32)],
        compiler_params=_cparams(("parallel",)),
    )(h, mod, wq_hi, wq_lo, subkeys)


GELU_K1 = -2.0 * math.sqrt(2.0 / math.pi) * LOG2E
GELU_K2 = GELU_K1 * 0.044715


def _gelu_tanh(x):
    return x / (1.0 + jnp.exp2(x * (GELU_K1 + GELU_K2 * (x * x))))


def _peer_dense_kernel(xt_ref, u_ref, vt_ref, rk_ref, b1_ref, nn_ref, az_ref, yt_ref, g_ref):
    c = pl.program_id(1)
    nk = PEER_NK
    tp = xt_ref.shape[1]
    rows_per_step = u_ref.shape[0] // nk
    n_chunks = pl.num_programs(1) - 1

    @pl.when(c == 0)
    def _():
        yt_ref[...] = jnp.zeros_like(yt_ref)
        g_ref[...] = jnp.zeros_like(g_ref)

    xt = xt_ref[...]
    row0 = jnp.minimum(c, n_chunks - 1) * rows_per_step
    pre = lambda ii: jnp.dot(u_ref[ii * nk:(ii + 1) * nk, :], xt, preferred_element_type=F32)
    act_next = pre(0)
    for ii in range(rows_per_step):
        if ii % 2 == 0:
            pair = slice(ii * nk, (ii + 2) * nk)
            yt_ref[...] += jnp.dot(vt_ref[:, pair], g_ref[pair, :], preferred_element_type=F32)
        act = act_next
        if ii + 1 < rows_per_step:
            act_next = pre(ii + 1)
        i = row0 + ii
        w = None
        tile = (nk // BF16_ROWS, BF16_ROWS, tp)
        row = lambda ref, hd: jnp.broadcast_to(ref[hd, pl.ds(i, 1), :], (BF16_ROWS, tp)).astype(BF16)[None]
        for hd in range(PEER_HEADS):
            t = jnp.where(rk_ref[hd].reshape(tile) < row(nn_ref, hd), b1_ref[hd].reshape(tile),
                          jnp.zeros((), BF16)) * row(az_ref, hd)
            w = t if w is None else w + t
        g_ref[ii * nk:(ii + 1) * nk, :] = w.reshape(nk, tp) * _gelu_tanh(act.astype(BF16))


def _peer_dense(xt, u, vt, rk, b1, nn, az):
    d, t = xt.shape
    n = u.shape[0]
    tp = PEER_TOKENS
    ec = PEER_EXPERTS
    nc = n // ec
    hh, nk = PEER_HEADS, PEER_NK
    st_spec = pl.BlockSpec((hh, nk, tp), lambda i, c: (0, 0, i))
    return pl.pallas_call(
        _peer_dense_kernel,
        out_shape=jax.ShapeDtypeStruct((d, t), F32),
        grid=(t // tp, nc + 1),
        in_specs=[pl.BlockSpec((d, tp), lambda i, c: (0, i)),
                  pl.BlockSpec((ec, d), lambda i, c: (jnp.minimum(c, nc - 1), 0)),
                  pl.BlockSpec((d, ec), lambda i, c: (0, jnp.maximum(c - 1, 0))),
                  st_spec, st_spec, st_spec, st_spec],
        out_specs=pl.BlockSpec((d, tp), lambda i, c: (0, i)),
        scratch_shapes=[pltpu.VMEM((ec, tp), BF16)],
        compiler_params=_cparams(("parallel", "arbitrary")),
    )(xt, u, vt, rk, b1, nn, az)


def _resid_ln_kernel(h_ref, y_ref, mod_ref, ln_ref, o_ref, *, alpha, gate_row):
    z = alpha * h_ref[...] + mod_ref[gate_row:gate_row + 1, :] * y_ref[...]
    o_ref[...] = _layer_norm_rows(z, ln_ref[...])


def _resid_ln(h, y, mod, group_of_block, ln, alpha, gate_row):
    t, d = h.shape
    tb = TOKEN_BLOCK
    row = pl.BlockSpec((tb, d), lambda i: (i, 0))
    return pl.pallas_call(
        functools.partial(_resid_ln_kernel, alpha=alpha, gate_row=gate_row),
        out_shape=jax.ShapeDtypeStruct((t, d), F32),
        grid=(t // tb,),
        in_specs=[row, row,
                  pl.BlockSpec((None, 6, d), lambda i: (group_of_block(i), 0, 0)),
                  pl.BlockSpec((2, d), lambda i: (0, 0))],
        out_specs=row,
        compiler_params=_cparams(("parallel",)),
    )(h, y, mod, ln)


def _axial_tables(s, n_ctx, d):
    rows = s // GRID_W
    row = jnp.broadcast_to(jnp.arange(rows, dtype=F32)[:, None], (rows, GRID_W)).reshape(-1)
    col = jnp.broadcast_to(jnp.arange(GRID_W, dtype=F32)[None, :], (rows, GRID_W)).reshape(-1)
    quarter = d // 4
    inv = ROPE_THETA ** (-jnp.arange(quarter, dtype=F32) / quarter)
    ar, ac = row[:, None] * inv, col[:, None] * inv
    cos = jnp.concatenate([jnp.cos(ar), jnp.cos(ar), jnp.cos(ac), jnp.cos(ac)], axis=-1)
    sin = jnp.concatenate([-jnp.sin(ar), jnp.sin(ar), -jnp.sin(ac), jnp.sin(ac)], axis=-1)
    cos = jnp.concatenate([jnp.ones((n_ctx, d), F32), cos], axis=0)
    sin = jnp.concatenate([jnp.zeros((n_ctx, d), F32), sin], axis=0)
    return cos, sin


def _rope1d_tables(lt, d):
    half = d // 2
    inv = ROPE_THETA ** (-jnp.arange(half, dtype=F32) / half)
    ang = jnp.arange(lt, dtype=F32)[:, None] * inv
    return (jnp.concatenate([jnp.cos(ang), jnp.cos(ang)], axis=-1),
            jnp.concatenate([-jnp.sin(ang), jnp.sin(ang)], axis=-1))


def _swap_perm():
    cols = []
    for piece, flip in ((0, 16), (1, 16), (3, 8), (4, 8), (6, 16), (7, 16), (9, 32), (10, 32)):
        base = np.arange(PIECE_OFF[piece], PIECE_OFF[piece + 1])
        cols.append(PIECE_OFF[piece] + ((base - PIECE_OFF[piece]) ^ flip))
    return np.concatenate(cols)


SWAP_COLS = _swap_perm()
SWAP_PIECES = (0, 1, 3, 4, 6, 7, 9, 10)
SWAP_OFF = dict(zip(SWAP_PIECES, np.cumsum([0] + [PIECES[p] for p in SWAP_PIECES])[:-1] + MIX_COLS))


def _rms_heads(x, xs, gain, gain_s):
    r = lax.rsqrt(jnp.mean(x * x, axis=-1, keepdims=True) + RMS_EPS)
    return x * r * gain, xs * r * gain_s


def kernel(x, c, ctx, c_ctx, w_mod, b_mod, w_in, qk_gain, diff_lambda, diff_subln, win_sink, ret_decay,
           ret_norm, w_branch, w_out, ln_attn, ln_ffn, peer_wq, peer_subkeys, peer_u, peer_v):
    b, s, d = x.shape
    n_ctx = ctx.shape[1]
    depth = w_mod.shape[0]
    lt = n_ctx + s
    t = b * lt
    tb = TOKEN_BLOCK
    assert n_ctx % tb == 0 and s % tb == 0 and t % PEER_TOKENS == 0
    alpha = (2 * depth) ** 0.25
    blocks_per_batch = lt // tb
    ctx_blocks = n_ctx // tb

    def group_of_block(i):
        return jnp.where(i % blocks_per_batch < ctx_blocks, b, i // blocks_per_batch)

    cos64, sin64 = _axial_tables(s, n_ctx, HEAD_DIM)
    cos32, sin32 = _axial_tables(s, n_ctx, DIFF_DIM)
    cos1d, sin1d = _rope1d_tables(lt, RET_DK)
    cond8 = jnp.zeros((8, d), F32).at[:b].set(jax.nn.silu(c)).at[b].set(jax.nn.silu(c_ctx))

    h = jnp.concatenate([ctx, x], axis=1).reshape(t, d)
    nqb = lt // ATT_TQ
    nwb = lt // WINDOW

    for l in range(depth):
        mod = _modulation(cond8, w_mod[l], b_mod[l]).reshape(8, 6, d)
        w_mix = w_in[l, :, :MIX_COLS]
        w_ext = jnp.concatenate([w_mix, w_mix[:, SWAP_COLS]], axis=1).astype(BF16)
        w_gate = w_in[l, :, MIX_COLS:].astype(BF16)
        proj = _inproj(h, mod, group_of_block, w_ext, sigmoid=False, out_dtype=F32)
        gates = _inproj(h, mod, group_of_block, w_gate, sigmoid=True, out_dtype=BF16)
        proj = proj.reshape(b, lt, -1)

        def piece(p, heads):
            return proj[..., PIECE_OFF[p]:PIECE_OFF[p + 1]].reshape(b, lt, heads, -1)

        def swapped(p, heads):
            return proj[..., SWAP_OFF[p]:SWAP_OFF[p] + PIECES[p]].reshape(b, lt, heads, -1)

        def rope(p, heads, cos, sin):
            return piece(p, heads) * cos[:, None, :] + swapped(p, heads) * sin[:, None, :]

        g0, g1 = qk_gain[l, 0], qk_gain[l, 1]
        swap64 = np.arange(HEAD_DIM) ^ 16
        qa, qas = _rms_heads(piece(0, A_HEADS), swapped(0, A_HEADS), g0, g0[swap64])
        ka, kas = _rms_heads(piece(1, A_KV_HEADS), swapped(1, A_KV_HEADS), g1, g1[swap64])
        qa = qa * cos64[:, None, :] + qas * sin64[:, None, :]
        ka = ka * cos64[:, None, :] + kas * sin64[:, None, :]
        va = piece(2, A_KV_HEADS)
        cos32x2 = jnp.concatenate([cos32, cos32], axis=-1)
        sin32x2 = jnp.concatenate([sin32, sin32], axis=-1)
        qd = rope(3, DIFF_HEADS, cos32x2, sin32x2)
        kd = rope(4, DIFF_HEADS, cos32x2, sin32x2)
        vd = piece(5, DIFF_HEADS)
        first = (jnp.arange(2 * DIFF_DIM) < DIFF_DIM).astype(F32)

        sc_a = HEAD_DIM ** -0.5 * LOG2E
        sc_d = DIFF_DIM ** -0.5 * LOG2E
        qa_t = (qa * sc_a).reshape(b, nqb, ATT_TQ, A_KV_HEADS, 2, HEAD_DIM).transpose(0, 3, 1, 5, 4, 2)
        qd_s = jnp.stack([qd * first, qd * (1.0 - first)], axis=3) * sc_d
        qd_t = qd_s.reshape(b, nqb, ATT_TQ, DIFF_HEADS, 2, HEAD_DIM).transpose(0, 3, 1, 5, 4, 2)
        qt = jnp.concatenate([qa_t, qd_t], axis=1).reshape(b, -1, nqb, HEAD_DIM, 2 * ATT_TQ).astype(BF16)
        k_all = jnp.concatenate([ka, kd], axis=2).transpose(0, 2, 1, 3).astype(BF16)
        v_all = jnp.concatenate([va, vd], axis=2).transpose(0, 2, 1, 3).astype(BF16)
        n_groups = k_all.shape[1]
        ot = _flash(qt, k_all, v_all, n_ctx)
        ot = ot.reshape(b, n_groups, nqb, HEAD_DIM, 2, ATT_TQ)
        oa = ot[:, :A_KV_HEADS].transpose(0, 2, 5, 1, 4, 3).reshape(b, lt, A_HEADS * HEAD_DIM)
        lam_init = 0.8 - 0.6 * math.exp(-0.3 * l)
        lp = diff_lambda[l].astype(F32)
        lam = jnp.exp(jnp.sum(lp[0] * lp[1])) - jnp.exp(jnp.sum(lp[2] * lp[3])) + lam_init
        od_ = ot[:, A_KV_HEADS:, :, :, 0, :] - lam * ot[:, A_KV_HEADS:, :, :, 1, :]
        od_ = od_.transpose(0, 2, 4, 1, 3).reshape(b, lt, DIFF_HEADS, 2 * DIFF_DIM)
        od_ = od_ * lax.rsqrt(jnp.mean(od_ * od_, axis=-1, keepdims=True) + RMS_EPS) * diff_subln[l]
        ob = (od_ * (1.0 - lam_init)).reshape(b, lt, -1)

        qw = rope(6, WIN_HEADS, cos64, sin64) * sc_a
        kw = rope(7, WIN_KV_HEADS, cos64, sin64)
        vw = piece(8, WIN_KV_HEADS)
        qw = (qw.reshape(b, nwb, WINDOW, WIN_KV_HEADS, 2, HEAD_DIM).transpose(0, 3, 1, 4, 2, 5)
              .reshape(b, WIN_KV_HEADS, nwb, 2 * WINDOW, HEAD_DIM).astype(BF16))
        kw = kw.transpose(0, 2, 1, 3).reshape(b, WIN_KV_HEADS, nwb, WINDOW, HEAD_DIM).astype(BF16)
        vw = vw.transpose(0, 2, 1, 3).reshape(b, WIN_KV_HEADS, nwb, WINDOW, HEAD_DIM).astype(BF16)
        sink = jnp.repeat(win_sink[l].astype(F32) * LOG2E, WINDOW).reshape(WIN_KV_HEADS, 2 * WINDOW, 1)
        ow = _window(qw, kw, vw, sink, n_ctx // WINDOW)
        oc = (ow.reshape(b, WIN_KV_HEADS, nwb, 2, WINDOW, HEAD_DIM).transpose(0, 2, 4, 1, 3, 5)
              .reshape(b, lt, -1))

        qr = rope(9, RET_HEADS, cos1d, sin1d).transpose(0, 2, 1, 3)
        krt = (rope(10, RET_HEADS, cos1d, sin1d) * RET_DK ** -0.5).transpose(0, 2, 3, 1)
        vr = piece(11, RET_HEADS).transpose(0, 2, 1, 3)
        lg = jax.nn.log_sigmoid(ret_decay[l].astype(F32))
        idx = jnp.arange(RET_CHUNK, dtype=F32)
        diff = idx[:, None] - idx[None, :]
        lg3 = lg[:, :, None, None]
        dm_f = jnp.exp(jnp.where(diff >= 0, diff * lg3[0], -jnp.inf))
        dm_b = jnp.exp(jnp.where(diff <= 0, -diff * lg3[1], -jnp.inf))
        dmat = jnp.stack([dm_f, dm_b])
        xi = jnp.stack([jnp.exp((idx + 1.0) * lg[0][:, None]), jnp.exp((RET_CHUNK - idx) * lg[1][:, None])])
        zeta = jnp.stack([jnp.exp((RET_CHUNK - 1.0 - idx) * lg[0][:, None]), jnp.exp(idx * lg[1][:, None])])
        gch = jnp.exp(RET_CHUNK * lg)
        o_ret = _retention_call(qr, krt, vr, dmat, xi[..., None], zeta[:, :, None, :],
                                gch[:, :, None, None], n_ctx // RET_CHUNK)
        o_ret = (o_ret[0] + o_ret[1]).transpose(0, 2, 1, 3)
        mu = jnp.mean(o_ret, -1, keepdims=True)
        var = jnp.mean(jnp.square(o_ret - mu), -1, keepdims=True)
        on = ((o_ret - mu) * lax.rsqrt(var + LN_EPS)).reshape(b, lt, -1) * ret_norm[l, 0] + ret_norm[l, 1]
        od = on * jax.nn.silu(proj[..., PIECE_OFF[12]:PIECE_OFF[13]])

        outs = [o.reshape(t, BRANCH_W).astype(BF16) for o in (oa, ob, oc, od)]
        h = _merge(outs, gates, h, mod, group_of_block, w_branch[l].astype(BF16), w_out[l].astype(BF16),
                   ln_attn[l], alpha)

        wq_hi, wq_lo = _split_bf16(peer_wq[l])
        xb, s1, b1, th, az = _peer_route(h, mod, group_of_block, wq_hi, wq_lo, peer_subkeys[l])
        yt = _peer_dense(xb.T, peer_u[l].astype(BF16), peer_v[l].T.astype(BF16), s1, b1, th, az)
        h = _resid_ln(h, yt.T, mod, group_of_block, ln_ffn[l], alpha, 5)

    return h.reshape(b, lt, d)[:, n_ctx:, :]
```

```python
import functools
import math

import numpy as np
import jax
import jax.numpy as jnp
from jax import lax
from jax.experimental import pallas as pl
from jax.experimental.pallas import tpu as pltpu

GRID_W = 64
HEAD_DIM = 64
ROPE_THETA = 10000.0
A_HEADS = 4
A_KV_HEADS = 2
DIFF_HEADS = 4
DIFF_DIM = 32
WIN_HEADS = 4
WIN_KV_HEADS = 2
WINDOW = 128
RET_HEADS = 4
RET_DK = 64
RET_DV = 64
RET_CHUNK = 128
N_BRANCH = 4
BRANCH_W = 256
PIECES = (
    A_HEADS * HEAD_DIM, A_KV_HEADS * HEAD_DIM, A_KV_HEADS * HEAD_DIM,
    2 * DIFF_HEADS * DIFF_DIM, 2 * DIFF_HEADS * DIFF_DIM, DIFF_HEADS * 2 * DIFF_DIM,
    WIN_HEADS * HEAD_DIM, WIN_KV_HEADS * HEAD_DIM, WIN_KV_HEADS * HEAD_DIM,
    RET_HEADS * RET_DK, RET_HEADS * RET_DK, RET_HEADS * RET_DV, RET_HEADS * RET_DV,
)
MIX_COLS = sum(PIECES)
PIECE_OFF = tuple(int(v) for v in np.cumsum((0,) + PIECES))
PEER_HEADS = 8
PEER_NK = 128
PEER_TOPK = 16
PEER_DQ = 128
PEER_CAND_ROWS = -(-sum((PEER_TOPK + 1) // (p + 1) for p in range(PEER_TOPK + 1)) // 8) * 8
LN_EPS = 1e-5
RMS_EPS = 1e-6
LOG2E = 1.4426950408889634

F32 = jnp.float32
BF16 = jnp.bfloat16

TOKEN_BLOCK = 256
ATT_TQ = 256
ATT_TK = 1024
ATT_VPAD = 80
PEER_TOKENS = 512
PEER_EXPERTS = 1024
VMEM_LIMIT = 56 * 1024 * 1024
BF16_ROWS = 16


def _cparams(sem):
    return pltpu.CompilerParams(dimension_semantics=sem, vmem_limit_bytes=VMEM_LIMIT)


def _split_bf16(a):
    hi = a.astype(BF16)
    lo = (a - hi.astype(F32)).astype(BF16)
    return hi, lo


def _dot3(a, b, dims=(((1,), (0,)), ((), ()))):
    ah, al = _split_bf16(a)
    bh, bl = _split_bf16(b)
    d = functools.partial(lax.dot_general, dimension_numbers=dims, preferred_element_type=F32)
    return d(ah, bh) + (d(al, bh) + d(ah, bl))


def _mod_kernel(c_ref, w_ref, b_ref, o_ref):
    o_ref[...] = _dot3(c_ref[...], w_ref[...]) + b_ref[...]


def _modulation(cond8, w, b):
    d, n = w.shape
    tn = 1536
    return pl.pallas_call(
        _mod_kernel,
        out_shape=jax.ShapeDtypeStruct((8, n), F32),
        grid=(n // tn,),
        in_specs=[pl.BlockSpec((8, d), lambda j: (0, 0)),
                  pl.BlockSpec((d, tn), lambda j: (0, j)),
                  pl.BlockSpec((1, tn), lambda j: (0, j))],
        out_specs=pl.BlockSpec((8, tn), lambda j: (0, j)),
        compiler_params=_cparams(("arbitrary",)),
    )(cond8, w, b.reshape(1, n))


def _inproj_kernel(x_ref, mod_ref, w_ref, o_ref, *, sigmoid, chunk):
    m = mod_ref[...]
    xm = (x_ref[...] * (1.0 + m[1:2, :]) + m[0:1, :]).astype(BF16)
    n = w_ref.shape[1]
    for j in range(n // chunk):
        acc = jnp.dot(xm, w_ref[:, j * chunk:(j + 1) * chunk], preferred_element_type=F32)
        if sigmoid:
            acc = jax.nn.sigmoid(acc)
        o_ref[:, j * chunk:(j + 1) * chunk] = acc.astype(o_ref.dtype)


def _inproj(h, mod, group_of_block, w, *, sigmoid, out_dtype):
    t, d = h.shape
    n = w.shape[1]
    tb = TOKEN_BLOCK
    return pl.pallas_call(
        functools.partial(_inproj_kernel, sigmoid=sigmoid, chunk=512),
        out_shape=jax.ShapeDtypeStruct((t, n), out_dtype),
        grid=(t // tb,),
        in_specs=[pl.BlockSpec((tb, d), lambda i: (i, 0)),
                  pl.BlockSpec((None, 6, d), lambda i: (group_of_block(i), 0, 0)),
                  pl.BlockSpec((d, n), lambda i: (0, 0))],
        out_specs=pl.BlockSpec((tb, n), lambda i: (i, 0)),
        compiler_params=_cparams(("parallel",)),
    )(h, mod, w)


TOK_PIECES = ((1, True), (4, True), (6, True), (7, True), (8, False), (9, True), (10, True), (11, False),
              (12, False))
FEAT_PIECES = ((0, True), (3, True), (2, False), (5, False))


def _piece_cols(pieces):
    flips = {0: 16, 1: 16, 3: 8, 4: 8, 6: 16, 7: 16, 9: 32, 10: 32}
    cols, offs = [], {}
    n = 0
    for p, rotary in pieces:
        base = np.arange(PIECE_OFF[p], PIECE_OFF[p + 1])
        offs[p] = n
        cols.append(base)
        n += len(base)
        if rotary:
            cols.append(PIECE_OFF[p] + ((base - PIECE_OFF[p]) ^ flips[p]))
            n += len(base)
    return np.concatenate(cols), offs


TOK_COLS, TOK_OFF = _piece_cols(TOK_PIECES)
FEAT_COLS, FEAT_OFF = _piece_cols(FEAT_PIECES)
TTOK_OFF = {1: 0, 4: 128, 6: 384, 7: 640, 9: 768}
TTOK_W = 1024
TFEAT_OFF = {0: 0, 3: 256}
TFEAT_W = 512


def _mixer_in_kernel(x_ref, mod_ref, wt_ref, wf_ref, ttok_ref, tfeat_ref, gtok_ref, gfeat_ref, avg_ref,
                     kall_ref, qt_ref, vt_ref, wq_ref, wk_ref, wv_ref, rq_ref, rk_ref, rv_ref, rg_ref):
    m = mod_ref[...]
    u = x_ref[...] * (1.0 + m[1:2, :]) + m[0:1, :]
    xm = u.astype(BF16)
    xmt = u.T.astype(BF16)
    tb = xm.shape[0]

    def tok(p, swapped=False):
        a = TOK_OFF[p] + (PIECES[p] if swapped else 0)
        return jnp.dot(xm, wt_ref[:, a:a + PIECES[p]], preferred_element_type=F32)

    def feat(p, swapped=False):
        a = FEAT_OFF[p] + (PIECES[p] if swapped else 0)
        return jnp.dot(wf_ref[a:a + PIECES[p], :], xmt, preferred_element_type=F32)

    def rope_tok(p, x, xs):
        a = TTOK_OFF[p]
        return x * ttok_ref[:, a:a + PIECES[p]] + xs * ttok_ref[:, TTOK_W + a:TTOK_W + a + PIECES[p]]

    def rope_feat(p, x, xs):
        a = TFEAT_OFF[p]
        return x * tfeat_ref[a:a + PIECES[p], :] + xs * tfeat_ref[TFEAT_W + a:TFEAT_W + a + PIECES[p], :]

    x, xs = tok(1), tok(1, True)
    sq_hi, sq_lo = _split_bf16(x * x)
    avg = avg_ref[...]
    ms = jnp.dot(sq_hi, avg, preferred_element_type=F32) + jnp.dot(sq_lo, avg, preferred_element_type=F32)
    r = lax.rsqrt(ms + RMS_EPS)
    ka = r * rope_tok(1, x * gtok_ref[0:1, :], xs * gtok_ref[1:2, :])
    kd = rope_tok(4, tok(4), tok(4, True))
    kall_ref[:, :PIECES[1]] = ka.astype(BF16)
    kall_ref[:, PIECES[1]:] = kd.astype(BF16)
    wq_ref[...] = rope_tok(6, tok(6), tok(6, True))
    wk_ref[...] = rope_tok(7, tok(7), tok(7, True))
    wv_ref[...] = tok(8)
    rq_ref[...] = rope_tok(9, tok(9), tok(9, True))
    a9 = TTOK_OFF[9]
    rk_ref[...] = (tok(10) * ttok_ref[:, a9:a9 + PIECES[10]]
                   + tok(10, True) * ttok_ref[:, TTOK_W + a9:TTOK_W + a9 + PIECES[10]]) * RET_DK ** -0.5
    rv_ref[...] = tok(11)
    rg_ref[...] = tok(12)

    zero64 = jnp.zeros((HEAD_DIM, tb), F32)
    zero32 = jnp.zeros((DIFF_DIM, tb), F32)

    def place(q, upper):
        return jnp.concatenate([zero64, q] if upper else [q, zero64], axis=0).astype(BF16)

    xq, xqs = feat(0), feat(0, True)
    for hd in range(A_HEADS):
        rows = slice(hd * HEAD_DIM, (hd + 1) * HEAD_DIM)
        xh = xq[rows]
        rh = lax.rsqrt(jnp.mean(xh * xh, axis=0, keepdims=True) + RMS_EPS)
        swapped_rows = slice(A_HEADS * HEAD_DIM + hd * HEAD_DIM, A_HEADS * HEAD_DIM + (hd + 1) * HEAD_DIM)
        qh = rh * (xh * gfeat_ref[rows, :] * tfeat_ref[rows, :]
                   + xqs[rows] * gfeat_ref[swapped_rows, :]
                   * tfeat_ref[TFEAT_W + hd * HEAD_DIM:TFEAT_W + (hd + 1) * HEAD_DIM, :])
        qt_ref[hd // 2, hd % 2] = place(qh, hd // 2 == 1)
    qd = rope_feat(3, feat(3), feat(3, True))
    for hd in range(DIFF_HEADS):
        qh = qd[hd * HEAD_DIM:(hd + 1) * HEAD_DIM]
        q1 = jnp.concatenate([qh[:DIFF_DIM], zero32], axis=0)
        q2 = jnp.concatenate([zero32, qh[DIFF_DIM:]], axis=0)
        qt_ref[A_KV_HEADS + hd, 0] = place(q1, hd % 2 == 1)
        qt_ref[A_KV_HEADS + hd, 1] = place(q2, hd % 2 == 1)

    pad_rows = vt_ref.shape[1] - HEAD_DIM
    tail = (lax.broadcasted_iota(jnp.int32, (pad_rows, tb), 0) == 0).astype(F32)
    va, vd = feat(2), feat(5)
    for g in range(A_KV_HEADS + DIFF_HEADS):
        v = va[g * HEAD_DIM:(g + 1) * HEAD_DIM] if g < A_KV_HEADS else \
            vd[(g - A_KV_HEADS) * HEAD_DIM:(g - A_KV_HEADS + 1) * HEAD_DIM]
        vt_ref[g] = jnp.concatenate([v, tail], axis=0).astype(BF16)


def _mixer_in(h, mod, group_of_block, wt, wf, ttok, tfeat, gtok, gfeat, avg, b, lt):
    t, d = h.shape
    tb = TOKEN_BLOCK
    bpb = lt // tb
    n_groups = A_KV_HEADS + DIFF_HEADS
    kw = PIECES[1] + PIECES[4]
    f32_out = lambda w: jax.ShapeDtypeStruct((t, w), F32)
    row = lambda w: pl.BlockSpec((tb, w), lambda i: (i, 0))
    const = lambda a: pl.BlockSpec(a.shape, lambda i: (0,) * a.ndim)
    widths = (PIECES[6], PIECES[7], PIECES[8], PIECES[9], PIECES[10], PIECES[11], PIECES[12])
    return pl.pallas_call(
        _mixer_in_kernel,
        out_shape=(jax.ShapeDtypeStruct((b, lt, kw), BF16),
                   jax.ShapeDtypeStruct((b, n_groups, bpb, 2, 2 * HEAD_DIM, tb), BF16),
                   jax.ShapeDtypeStruct((b, n_groups, bpb, ATT_VPAD, tb), BF16))
        + tuple(f32_out(w) for w in widths),
        grid=(t // tb,),
        in_specs=[row(d),
                  pl.BlockSpec((None, 6, d), lambda i: (group_of_block(i), 0, 0)),
                  const(wt), const(wf),
                  pl.BlockSpec((tb, 2 * TTOK_W), lambda i: (i % bpb, 0)),
                  pl.BlockSpec((2 * TFEAT_W, tb), lambda i: (0, i % bpb)),
                  const(gtok), const(gfeat), const(avg)],
        out_specs=(pl.BlockSpec((None, tb, kw), lambda i: (i // bpb, i % bpb, 0)),
                   pl.BlockSpec((None, n_groups, None, 2, 2 * HEAD_DIM, tb),
                                lambda i: (i // bpb, 0, i % bpb, 0, 0, 0)),
                   pl.BlockSpec((None, n_groups, None, ATT_VPAD, tb), lambda i: (i // bpb, 0, i % bpb, 0, 0)))
        + tuple(row(w) for w in widths),
        compiler_params=_cparams(("parallel",)),
    )(h, mod, wt, wf, ttok, tfeat, gtok, gfeat, avg)


def _flash_kernel(qt_ref, k_ref, vt_ref, o_ref, s_ref, *, n_ctx, tk):
    i = pl.program_id(2)
    qt = jnp.concatenate([qt_ref[0], qt_ref[1]], axis=1)
    w = qt.shape[1]
    tile = vt_ref.shape[2]
    nlc = (k_ref.shape[0] - n_ctx) // tk

    def scores(row0, rows):
        return jnp.dot(k_ref[pl.ds(row0, rows), :], qt, preferred_element_type=F32)

    def absorb(s, tile0, carry):
        m, acc = carry
        m_new = jnp.maximum(m, jnp.max(s, axis=0, keepdims=True))
        p = jnp.exp2(s - m_new).astype(BF16)
        acc = acc * jnp.exp2(m - m_new)
        for j in range(s.shape[0] // tile):
            acc = acc + jnp.dot(vt_ref[tile0 + j], p[j * tile:(j + 1) * tile], preferred_element_type=F32)
        return m_new, acc

    carry = (jnp.full((1, w), -jnp.inf, F32), jnp.zeros((vt_ref.shape[1], w), F32))
    carry = absorb(scores(0, n_ctx), 0, carry)
    lat_row = lambda c: pl.multiple_of(n_ctx + c * tk, tile)
    lat_tile = lambda c: (n_ctx + c * tk) // tile
    s_ref[0] = scores(lat_row(0), tk)

    def pair(j, carry):
        c0 = 2 * j
        s_ref[1] = scores(lat_row(c0 + 1), tk)
        carry = absorb(s_ref[0], lat_tile(c0), carry)
        s_ref[0] = scores(lat_row(jnp.minimum(c0 + 2, nlc - 1)), tk)
        return absorb(s_ref[1], lat_tile(c0 + 1), carry)

    n_pairs = jnp.where(i * (w // 2) < n_ctx, 0, nlc // 2)
    _, acc = lax.fori_loop(0, n_pairs, pair, carry)
    o_ref[...] = acc[:HEAD_DIM] * (1.0 / acc[HEAD_DIM:HEAD_DIM + 1])


def _flash(qt, k_all, vt, n_ctx):
    b, g, nqb, _, dk, tq = qt.shape
    lt = k_all.shape[1]
    s = lt - n_ctx
    tk = next(c for c in (ATT_TK, 512, 256) if s % (2 * c) == 0)
    assert n_ctx % tq == 0 and tk % tq == 0
    key_block = lambda gi: jnp.where(gi < A_KV_HEADS, 0, 1 + (gi - A_KV_HEADS) // 2)
    return pl.pallas_call(
        functools.partial(_flash_kernel, n_ctx=n_ctx, tk=tk),
        out_shape=jax.ShapeDtypeStruct((b, g, nqb, HEAD_DIM, 2 * tq), F32),
        grid=(b, g, nqb),
        in_specs=[pl.BlockSpec((None, None, None, 2, dk, tq), lambda bi, gi, i: (bi, gi, i, 0, 0, 0)),
                  pl.BlockSpec((None, lt, dk), lambda bi, gi, i: (bi, 0, key_block(gi))),
                  pl.BlockSpec((None, None) + vt.shape[2:], lambda bi, gi, i: (bi, gi, 0, 0, 0))],
        out_specs=pl.BlockSpec((None, None, None, HEAD_DIM, 2 * tq), lambda bi, gi, i: (bi, gi, i, 0, 0)),
        scratch_shapes=[pltpu.VMEM((2, tk, 2 * tq), F32)],
        compiler_params=_cparams(("parallel", "parallel", "arbitrary")),
    )(qt, k_all, vt)


def _window_kernel(q_ref, kp_ref, kc_ref, kn_ref, vp_ref, vc_ref, vn_ref, kx_ref, vx_ref, sink_ref,
                   o_ref, *, n_ctx_blocks, n_blocks):
    qb = pl.program_id(2)
    q = q_ref[...]
    rows = q.shape[0]
    nt = (((1,), (1,)), ((), ()))
    sdot = functools.partial(lax.dot_general, dimension_numbers=nt, preferred_element_type=F32)
    qi = lax.broadcasted_iota(jnp.int32, (rows, WINDOW), 0) % WINDOW
    kj = lax.broadcasted_iota(jnp.int32, (rows, WINDOW), 1)
    neg = -jnp.inf
    off_p = jnp.where(qb >= n_ctx_blocks + 1, 0, 2 * WINDOW)
    off_c = jnp.where(qb >= n_ctx_blocks, 0, 2 * WINDOW)
    off_n = jnp.where(jnp.logical_and(qb >= n_ctx_blocks, qb <= n_blocks - 2), 0, 2 * WINDOW)
    s_p = jnp.where(kj >= qi + off_p, sdot(q, kp_ref[...]), neg)
    s_c = jnp.where(kj >= off_c, sdot(q, kc_ref[...]), neg)
    s_n = jnp.where(kj <= qi - off_n, sdot(q, kn_ref[...]), neg)
    kx = kx_ref[...].reshape(-1, HEAD_DIM)
    vx = vx_ref[...].reshape(-1, HEAD_DIM)
    s_x = sdot(q, kx)
    sink = sink_ref[...]
    m = jnp.maximum(jnp.maximum(jnp.max(s_p, axis=1, keepdims=True), jnp.max(s_c, axis=1, keepdims=True)),
                    jnp.maximum(jnp.max(s_n, axis=1, keepdims=True), jnp.max(s_x, axis=1, keepdims=True)))
    m = jnp.maximum(m, sink)
    e_p, e_c, e_n, e_x = (jnp.exp2(s - m) for s in (s_p, s_c, s_n, s_x))
    den = (jnp.sum(e_p, axis=1, keepdims=True) + jnp.sum(e_c, axis=1, keepdims=True)
           + jnp.sum(e_n, axis=1, keepdims=True) + jnp.sum(e_x, axis=1, keepdims=True)
           + jnp.exp2(sink - m))
    pv = functools.partial(jnp.dot, preferred_element_type=F32)
    o = (pv(e_p.astype(BF16), vp_ref[...]) + pv(e_c.astype(BF16), vc_ref[...])
         + pv(e_n.astype(BF16), vn_ref[...]) + pv(e_x.astype(BF16), vx))
    o_ref[...] = o / den


def _window(q, k, v, sink, n_ctx_blocks):
    b, g, nb, rows, dh = q.shape
    blk = k.shape[3]
    lo, hi = n_ctx_blocks, nb - 1

    def nbr(delta):
        return lambda bi, gi, i: (bi, gi, jnp.clip(i + delta, lo, hi), 0, 0)

    kv_spec = lambda d: pl.BlockSpec((None, None, None, blk, dh), nbr(d))
    ctx_spec = pl.BlockSpec((None, None, n_ctx_blocks, blk, dh), lambda bi, gi, i: (bi, gi, 0, 0, 0))
    return pl.pallas_call(
        functools.partial(_window_kernel, n_ctx_blocks=n_ctx_blocks, n_blocks=nb),
        out_shape=jax.ShapeDtypeStruct((b, g, nb, rows, dh), F32),
        grid=(b, g, nb),
        in_specs=[pl.BlockSpec((None, None, None, rows, dh), lambda bi, gi, i: (bi, gi, i, 0, 0)),
                  kv_spec(-1), kv_spec(0), kv_spec(1), kv_spec(-1), kv_spec(0), kv_spec(1),
                  ctx_spec, ctx_spec,
                  pl.BlockSpec((None, rows, 1), lambda bi, gi, i: (gi, 0, 0))],
        out_specs=pl.BlockSpec((None, None, None, rows, dh), lambda bi, gi, i: (bi, gi, i, 0, 0)),
        compiler_params=_cparams(("parallel", "parallel", "arbitrary")),
    )(q, k, k, k, v, v, v, k, v, sink)


def _retention_kernel(q_ref, kt_ref, v_ref, dmat_ref, xi_ref, zeta_ref, gch_ref, o_ref, st_ref):
    t = pl.program_id(2)

    @pl.when(t == 0)
    def _():
        st_ref[...] = jnp.zeros_like(st_ref)

    for hd in range(RET_HEADS):
        q = q_ref[hd]
        kt = kt_ref[hd]
        v = v_ref[hd]
        st = st_ref[hd]
        inner = _dot3(q, kt) * dmat_ref[hd]
        o_ref[hd] = _dot3(inner, v) + _dot3(q, st) * xi_ref[hd]
        st_ref[hd] = st * gch_ref[hd] + _dot3(kt * zeta_ref[hd], v)


def _retention_call(q, kt, v, dmat, xi, zeta, gch, n_ctx_chunks):
    b, hh, lt, dk = q.shape
    dv = v.shape[-1]
    c = RET_CHUNK
    nch = lt // c

    def blk(d, ti):
        back = jnp.where(ti < n_ctx_chunks, n_ctx_chunks - 1 - ti, nch - 1 - (ti - n_ctx_chunks))
        return jnp.where(d == 0, ti, back)

    tab = lambda shape: pl.BlockSpec((None,) + shape, lambda d, bi, ti: (d,) + (0,) * len(shape))
    return pl.pallas_call(
        _retention_kernel,
        out_shape=jax.ShapeDtypeStruct((2, b, hh, lt, dv), F32),
        grid=(2, b, nch),
        in_specs=[pl.BlockSpec((None, hh, c, dk), lambda d, bi, ti: (bi, 0, blk(d, ti), 0)),
                  pl.BlockSpec((None, hh, dk, c), lambda d, bi, ti: (bi, 0, 0, blk(d, ti))),
                  pl.BlockSpec((None, hh, c, dv), lambda d, bi, ti: (bi, 0, blk(d, ti), 0)),
                  tab((hh, c, c)), tab((hh, c, 1)), tab((hh, 1, c)), tab((hh, 1, 1))],
        out_specs=pl.BlockSpec((None, None, hh, c, dv), lambda d, bi, ti: (d, bi, 0, blk(d, ti), 0)),
        scratch_shapes=[pltpu.VMEM((hh, dk, dv), F32)],
        compiler_params=_cparams(("parallel", "parallel", "arbitrary")),
    )(q, kt, v, dmat, xi, zeta, gch)


def _layer_norm_rows(z, ln):
    mu = jnp.mean(z, axis=-1, keepdims=True)
    zc = z - mu
    var = jnp.mean(zc * zc, axis=-1, keepdims=True)
    return zc * lax.rsqrt(var + LN_EPS) * ln[0:1, :] + ln[1:2, :]


def _merge_kernel(oa_ref, ob_ref, oc_ref, od_ref, g_ref, h_ref, mod_ref, wb_ref, wo_ref, ln_ref, o_ref,
                  *, alpha):
    d = h_ref.shape[1]
    m = None
    for i, o_i in enumerate((oa_ref, ob_ref, oc_ref, od_ref)):
        t = g_ref[:, i * d:(i + 1) * d].astype(F32) * jnp.dot(o_i[...], wb_ref[i], preferred_element_type=F32)
        m = t if m is None else m + t
    y = jnp.dot(m.astype(BF16), wo_ref[...], preferred_element_type=F32)
    z = alpha * h_ref[...] + mod_ref[2:3, :] * y
    o_ref[...] = _layer_norm_rows(z, ln_ref[...])


def _merge(outs, gates, h, mod, group_of_block, wb, wo, ln, alpha):
    t, d = h.shape
    tb = TOKEN_BLOCK
    bw = outs[0].shape[1]
    row = lambda n: pl.BlockSpec((tb, n), lambda i: (i, 0))
    return pl.pallas_call(
        functools.partial(_merge_kernel, alpha=alpha),
        out_shape=jax.ShapeDtypeStruct((t, d), F32),
        grid=(t // tb,),
        in_specs=[row(bw), row(bw), row(bw), row(bw), row(N_BRANCH * d), row(d),
                  pl.BlockSpec((None, 6, d), lambda i: (group_of_block(i), 0, 0)),
                  pl.BlockSpec((N_BRANCH, bw, d), lambda i: (0, 0, 0)),
                  pl.BlockSpec((d, d), lambda i: (0, 0)),
                  pl.BlockSpec((2, d), lambda i: (0, 0))],
        out_specs=row(d),
        compiler_params=_cparams(("parallel",)),
    )(*outs, gates, h, mod, wb, wo, ln)


def _top_rows(s, n, with_rank=False):
    out = []
    cur = s
    rank = jnp.full(s.shape, float(n), F32) if with_rank else None
    for r in range(n):
        mx = jnp.max(cur, axis=0, keepdims=True)
        out.append(mx)
        if with_rank:
            rank = jnp.where(cur == mx, float(r), rank)
        if r + 1 < n:
            cur = jnp.where(cur == mx, -jnp.inf, cur)
    return (out, rank) if with_rank else out


def _peer_route_kernel(h_ref, mod_ref, wh_ref, wl_ref, sk_ref, x_ref, rk_ref, b1_ref, nn_ref, az_ref,
                       cand_ref):
    m = mod_ref[...]
    u = h_ref[...] * (1.0 + m[4:5, :]) + m[3:4, :]
    x_ref[...] = u.astype(BF16)
    uh, ul = _split_bf16(u)
    d = functools.partial(jnp.dot, preferred_element_type=F32)
    nk = PEER_NK
    nt = (((1,), (1,)), ((), ()))
    k1 = PEER_TOPK + 1
    for hd in range(PEER_HEADS):
        c0 = 2 * hd * PEER_DQ
        wh = wh_ref[:, c0:c0 + 2 * PEER_DQ]
        wl = wl_ref[:, c0:c0 + 2 * PEER_DQ]
        q = d(uh, wh) + (d(ul, wh) + d(uh, wl))
        st = [_dot3(sk_ref[hd, p], q[:, p * PEER_DQ:(p + 1) * PEER_DQ], nt) for p in range(2)]
        top0 = _top_rows(st[0], k1)
        top1, rank1 = _top_rows(st[1], k1, with_rank=True)
        r = 0
        for p0 in range(k1):
            for p1 in range(k1 // (p0 + 1)):
                cand_ref[r:r + 1, :] = top0[p0] + top1[p1]
                r += 1
        cand_ref[r:, :] = jnp.full((cand_ref.shape[0] - r, cand_ref.shape[1]), -jnp.inf, F32)
        cand = cand_ref[...]
        ctop = _top_rows(cand, k1)
        tau = 0.5 * (ctop[PEER_TOPK - 1] + ctop[PEER_TOPK])
        mx = top0[0] + top1[0]
        z = jnp.sum(jnp.where(cand >= tau, jnp.exp(cand - mx), 0.0), axis=0, keepdims=True)
        th = tau - st[0]
        nn = jnp.zeros_like(th)
        for q in range(PEER_TOPK):
            nn = nn + jnp.where(top1[q] >= th, 1.0, 0.0)
        rk_ref[hd] = rank1.astype(BF16)
        b1_ref[hd] = jnp.exp(st[1] - top1[0]).astype(BF16)
        nn_ref[hd] = nn
        az_ref[hd] = jnp.exp(st[0] - top0[0]) / z


def _peer_route(h, mod, group_of_block, wq_hi, wq_lo, subkeys):
    t, d = h.shape
    tb = TOKEN_BLOCK
    hh, nk = PEER_HEADS, PEER_NK
    st_shape = lambda dt: jax.ShapeDtypeStruct((hh, nk, t), dt)
    st_spec = pl.BlockSpec((hh, nk, tb), lambda i: (0, 0, i))
    nq = wq_hi.shape[1]
    return pl.pallas_call(
        _peer_route_kernel,
        out_shape=(jax.ShapeDtypeStruct((t, d), BF16), st_shape(BF16), st_shape(BF16), st_shape(F32),
                   st_shape(F32)),
        grid=(t // tb,),
        in_specs=[pl.BlockSpec((tb, d), lambda i: (i, 0)),
                  pl.BlockSpec((None, 6, d), lambda i: (group_of_block(i), 0, 0)),
                  pl.BlockSpec((d, nq), lambda i: (0, 0)),
                  pl.BlockSpec((d, nq), lambda i: (0, 0)),
                  pl.BlockSpec((hh, 2, nk, PEER_DQ), lambda i: (0, 0, 0, 0))],
        out_specs=(pl.BlockSpec((tb, d), lambda i: (i, 0)), st_spec, st_spec, st_spec, st_spec),
        scratch_shapes=[pltpu.VMEM((PEER_CAND_ROWS, tb), F32)],
        compiler_params=_cparams(("parallel",)),
    )(h, mod, wq_hi, wq_lo, subkeys)


GELU_K1 = -2.0 * math.sqrt(2.0 / math.pi) * LOG2E
GELU_K2 = GELU_K1 * 0.044715


def _gelu_tanh(x):
    return x / (1.0 + jnp.exp2(x * (GELU_K1 + GELU_K2 * (x * x))))


def _peer_dense_kernel(xt_ref, u_ref, vt_ref, rk_ref, b1_ref, nn_ref, az_ref, yt_ref, g_ref):
    c = pl.program_id(1)
    nk = PEER_NK
    tp = xt_ref.shape[1]
    rows_per_step = u_ref.shape[0] // nk
    n_chunks = pl.num_programs(1) - 1

    @pl.when(c == 0)
    def _():
        yt_ref[...] = jnp.zeros_like(yt_ref)
        g_ref[...] = jnp.zeros_like(g_ref)

    xt = xt_ref[...]
    row0 = jnp.minimum(c, n_chunks - 1) * rows_per_step
    pre = lambda ii: jnp.dot(u_ref[ii * nk:(ii + 1) * nk, :], xt, preferred_element_type=F32)
    act_next = pre(0)
    for ii in range(rows_per_step):
        if ii % 2 == 0:
            pair = slice(ii * nk, (ii + 2) * nk)
            yt_ref[...] += jnp.dot(vt_ref[:, pair], g_ref[pair, :], preferred_element_type=F32)
        act = act_next
        if ii + 1 < rows_per_step:
            act_next = pre(ii + 1)
        i = row0 + ii
        w = None
        tile = (nk // BF16_ROWS, BF16_ROWS, tp)
        row = lambda ref, hd: jnp.broadcast_to(ref[hd, pl.ds(i, 1), :], (BF16_ROWS, tp)).astype(BF16)[None]
        for hd in range(PEER_HEADS):
            t = jnp.where(rk_ref[hd].reshape(tile) < row(nn_ref, hd), b1_ref[hd].reshape(tile),
                          jnp.zeros((), BF16)) * row(az_ref, hd)
            w = t if w is None else w + t
        g_ref[ii * nk:(ii + 1) * nk, :] = w.reshape(nk, tp) * _gelu_tanh(act.astype(BF16))


def _peer_dense(xt, u, vt, rk, b1, nn, az):
    d, t = xt.shape
    n = u.shape[0]
    tp = PEER_TOKENS
    ec = PEER_EXPERTS
    nc = n // ec
    hh, nk = PEER_HEADS, PEER_NK
    st_spec = pl.BlockSpec((hh, nk, tp), lambda i, c: (0, 0, i))
    return pl.pallas_call(
        _peer_dense_kernel,
        out_shape=jax.ShapeDtypeStruct((d, t), F32),
        grid=(t // tp, nc + 1),
        in_specs=[pl.BlockSpec((d, tp), lambda i, c: (0, i)),
                  pl.BlockSpec((ec, d), lambda i, c: (jnp.minimum(c, nc - 1), 0)),
                  pl.BlockSpec((d, ec), lambda i, c: (0, jnp.maximum(c - 1, 0))),
                  st_spec, st_spec, st_spec, st_spec],
        out_specs=pl.BlockSpec((d, tp), lambda i, c: (0, i)),
        scratch_shapes=[pltpu.VMEM((ec, tp), BF16)],
        compiler_params=_cparams(("parallel", "arbitrary")),
    )(xt, u, vt, rk, b1, nn, az)


def _resid_ln_kernel(h_ref, y_ref, mod_ref, ln_ref, o_ref, *, alpha, gate_row):
    z = alpha * h_ref[...] + mod_ref[gate_row:gate_row + 1, :] * y_ref[...]
    o_ref[...] = _layer_norm_rows(z, ln_ref[...])


def _resid_ln(h, y, mod, group_of_block, ln, alpha, gate_row):
    t, d = h.shape
    tb = TOKEN_BLOCK
    row = pl.BlockSpec((tb, d), lambda i: (i, 0))
    return pl.pallas_call(
        functools.partial(_resid_ln_kernel, alpha=alpha, gate_row=gate_row),
        out_shape=jax.ShapeDtypeStruct((t, d), F32),
        grid=(t // tb,),
        in_specs=[row, row,
                  pl.BlockSpec((None, 6, d), lambda i: (group_of_block(i), 0, 0)),
                  pl.BlockSpec((2, d), lambda i: (0, 0))],
        out_specs=row,
        compiler_params=_cparams(("parallel",)),
    )(h, y, mod, ln)


def _axial_tables(s, n_ctx, d):
    rows = s // GRID_W
    row = jnp.broadcast_to(jnp.arange(rows, dtype=F32)[:, None], (rows, GRID_W)).reshape(-1)
    col = jnp.broadcast_to(jnp.arange(GRID_W, dtype=F32)[None, :], (rows, GRID_W)).reshape(-1)
    quarter = d // 4
    inv = ROPE_THETA ** (-jnp.arange(quarter, dtype=F32) / quarter)
    ar, ac = row[:, None] * inv, col[:, None] * inv
    cos = jnp.concatenate([jnp.cos(ar), jnp.cos(ar), jnp.cos(ac), jnp.cos(ac)], axis=-1)
    sin = jnp.concatenate([-jnp.sin(ar), jnp.sin(ar), -jnp.sin(ac), jnp.sin(ac)], axis=-1)
    cos = jnp.concatenate([jnp.ones((n_ctx, d), F32), cos], axis=0)
    sin = jnp.concatenate([jnp.zeros((n_ctx, d), F32), sin], axis=0)
    return cos, sin


def _rope1d_tables(lt, d):
    half = d // 2
    inv = ROPE_THETA ** (-jnp.arange(half, dtype=F32) / half)
    ang = jnp.arange(lt, dtype=F32)[:, None] * inv
    return (jnp.concatenate([jnp.cos(ang), jnp.cos(ang)], axis=-1),
            jnp.concatenate([-jnp.sin(ang), jnp.sin(ang)], axis=-1))


def _swap_perm():
    cols = []
    for piece, flip in ((0, 16), (1, 16), (3, 8), (4, 8), (6, 16), (7, 16), (9, 32), (10, 32)):
        base = np.arange(PIECE_OFF[piece], PIECE_OFF[piece + 1])
        cols.append(PIECE_OFF[piece] + ((base - PIECE_OFF[piece]) ^ flip))
    return np.concatenate(cols)


SWAP_COLS = _swap_perm()
SWAP_PIECES = (0, 1, 3, 4, 6, 7, 9, 10)
SWAP_OFF = dict(zip(SWAP_PIECES, np.cumsum([0] + [PIECES[p] for p in SWAP_PIECES])[:-1] + MIX_COLS))


def _rms_heads(x, xs, gain, gain_s):
    r = lax.rsqrt(jnp.mean(x * x, axis=-1, keepdims=True) + RMS_EPS)
    return x * r * gain, xs * r * gain_s


def kernel(x, c, ctx, c_ctx, w_mod, b_mod, w_in, qk_gain, diff_lambda, diff_subln, win_sink, ret_decay,
           ret_norm, w_branch, w_out, ln_attn, ln_ffn, peer_wq, peer_subkeys, peer_u, peer_v):
    b, s, d = x.shape
    n_ctx = ctx.shape[1]
    depth = w_mod.shape[0]
    lt = n_ctx + s
    t = b * lt
    tb = TOKEN_BLOCK
    assert n_ctx % tb == 0 and s % tb == 0 and t % PEER_TOKENS == 0
    alpha = (2 * depth) ** 0.25
    blocks_per_batch = lt // tb
    ctx_blocks = n_ctx // tb

    def group_of_block(i):
        return jnp.where(i % blocks_per_batch < ctx_blocks, b, i // blocks_per_batch)

    cos64, sin64 = _axial_tables(s, n_ctx, HEAD_DIM)
    cos32, sin32 = _axial_tables(s, n_ctx, DIFF_DIM)
    cos1d, sin1d = _rope1d_tables(lt, RET_DK)
    sc_a = HEAD_DIM ** -0.5 * LOG2E
    sc_d = DIFF_DIM ** -0.5 * LOG2E
    rep = lambda a, n: jnp.tile(a, (1, n))
    tok_part = lambda t64, t32, t1d: [rep(t64, 2), rep(t32, 8), rep(t64, 4) * sc_a, rep(t64, 2), rep(t1d, 4)]
    feat_part = lambda t64, t32: [rep(t64, 4) * sc_a, rep(t32, 8) * sc_d]
    ttok = jnp.concatenate(tok_part(cos64, cos32, cos1d) + tok_part(sin64, sin32, sin1d), axis=1)
    tfeat = jnp.concatenate(feat_part(cos64, cos32) + feat_part(sin64, sin32), axis=1).T
    swap64 = np.arange(HEAD_DIM) ^ 16
    head_ids = np.arange(2 * HEAD_DIM) // HEAD_DIM
    avg = jnp.asarray((head_ids[:, None] == head_ids[None, :]) / HEAD_DIM, BF16)
    cond8 = jnp.zeros((8, d), F32).at[:b].set(jax.nn.silu(c)).at[b].set(jax.nn.silu(c_ctx))

    h = jnp.concatenate([ctx, x], axis=1).reshape(t, d)
    nqb = lt // ATT_TQ
    nwb = lt // WINDOW
    n_groups = A_KV_HEADS + DIFF_HEADS

    for l in range(depth):
        mod = _modulation(cond8, w_mod[l], b_mod[l]).reshape(8, 6, d)
        w_mix = w_in[l, :, :MIX_COLS]
        w_gate = w_in[l, :, MIX_COLS:].astype(BF16)
        g0, g1 = qk_gain[l, 0].astype(F32), qk_gain[l, 1].astype(F32)
        gtok = jnp.stack([jnp.tile(g1, 2), jnp.tile(g1[swap64], 2)])
        gfeat = jnp.broadcast_to(jnp.concatenate([jnp.tile(g0, A_HEADS), jnp.tile(g0[swap64], A_HEADS)])[:, None],
                                 (2 * A_HEADS * HEAD_DIM, tb))
        kall, qt, vt, wq, wk, wv, rq, rk, rv, rg = _mixer_in(
            h, mod, group_of_block, w_mix[:, TOK_COLS].astype(BF16), w_mix[:, FEAT_COLS].T.astype(BF16),
            ttok, tfeat, gtok, gfeat, avg, b, lt)
        gates = _inproj(h, mod, group_of_block, w_gate, sigmoid=True, out_dtype=BF16)

        ot = _flash(qt, kall, vt, n_ctx)
        ot = ot.reshape(b, n_groups, nqb, HEAD_DIM, 2, ATT_TQ)
        oa = ot[:, :A_KV_HEADS].transpose(0, 2, 5, 1, 4, 3).reshape(b, lt, A_HEADS * HEAD_DIM)
        lam_init = 0.8 - 0.6 * math.exp(-0.3 * l)
        lp = diff_lambda[l].astype(F32)
        lam = jnp.exp(jnp.sum(lp[0] * lp[1])) - jnp.exp(jnp.sum(lp[2] * lp[3])) + lam_init
        od_ = ot[:, A_KV_HEADS:, :, :, 0, :] - lam * ot[:, A_KV_HEADS:, :, :, 1, :]
        od_ = od_.transpose(0, 2, 4, 1, 3).reshape(b, lt, DIFF_HEADS, 2 * DIFF_DIM)
        od_ = od_ * lax.rsqrt(jnp.mean(od_ * od_, axis=-1, keepdims=True) + RMS_EPS) * diff_subln[l]
        ob = (od_ * (1.0 - lam_init)).reshape(b, lt, -1)

        heads = lambda a, n: a.astype(BF16).reshape(b, lt, n, HEAD_DIM)
        qw = (heads(wq, WIN_HEADS).reshape(b, nwb, WINDOW, WIN_KV_HEADS, 2, HEAD_DIM)
              .transpose(0, 3, 1, 4, 2, 5).reshape(b, WIN_KV_HEADS, nwb, 2 * WINDOW, HEAD_DIM))
        kw = heads(wk, WIN_KV_HEADS).transpose(0, 2, 1, 3).reshape(b, WIN_KV_HEADS, nwb, WINDOW, HEAD_DIM)
        vw = heads(wv, WIN_KV_HEADS).transpose(0, 2, 1, 3).reshape(b, WIN_KV_HEADS, nwb, WINDOW, HEAD_DIM)
        sink = jnp.repeat(win_sink[l].astype(F32) * LOG2E, WINDOW).reshape(WIN_KV_HEADS, 2 * WINDOW, 1)
        ow = _window(qw, kw, vw, sink, n_ctx // WINDOW)
        oc = (ow.reshape(b, WIN_KV_HEADS, nwb, 2, WINDOW, HEAD_DIM).transpose(0, 2, 4, 1, 3, 5)
              .reshape(b, lt, -1))

        qr = rq.reshape(b, lt, RET_HEADS, RET_DK).transpose(0, 2, 1, 3)
        krt = rk.reshape(b, lt, RET_HEADS, RET_DK).transpose(0, 2, 3, 1)
        vr = rv.reshape(b, lt, RET_HEADS, RET_DV).transpose(0, 2, 1, 3)
        lg = jax.nn.log_sigmoid(ret_decay[l].astype(F32))
        idx = jnp.arange(RET_CHUNK, dtype=F32)
        diff = idx[:, None] - idx[None, :]
        lg3 = lg[:, :, None, None]
        dm_f = jnp.exp(jnp.where(diff >= 0, diff * lg3[0], -jnp.inf))
        dm_b = jnp.exp(jnp.where(diff <= 0, -diff * lg3[1], -jnp.inf))
        dmat = jnp.stack([dm_f, dm_b])
        xi = jnp.stack([jnp.exp((idx + 1.0) * lg[0][:, None]), jnp.exp((RET_CHUNK - idx) * lg[1][:, None])])
        zeta = jnp.stack([jnp.exp((RET_CHUNK - 1.0 - idx) * lg[0][:, None]), jnp.exp(idx * lg[1][:, None])])
        gch = jnp.exp(RET_CHUNK * lg)
        o_ret = _retention_call(qr, krt, vr, dmat, xi[..., None], zeta[:, :, None, :],
                                gch[:, :, None, None], n_ctx // RET_CHUNK)
        o_ret = (o_ret[0] + o_ret[1]).transpose(0, 2, 1, 3)
        mu = jnp.mean(o_ret, -1, keepdims=True)
        var = jnp.mean(jnp.square(o_ret - mu), -1, keepdims=True)
        on = ((o_ret - mu) * lax.rsqrt(var + LN_EPS)).reshape(b, lt, -1) * ret_norm[l, 0] + ret_norm[l, 1]
        od = on * jax.nn.silu(rg.reshape(b, lt, -1))

        outs = [o.reshape(t, BRANCH_W).astype(BF16) for o in (oa, ob, oc, od)]
        h = _merge(outs, gates, h, mod, group_of_block, w_branch[l].astype(BF16), w_out[l].astype(BF16),
                   ln_attn[l], alpha)

        wq_hi, wq_lo = _split_bf16(peer_wq[l])
        xb, s1, b1, th, az = _peer_route(h, mod, group_of_block, wq_hi, wq_lo, peer_subkeys[l])
        yt = _peer_dense(xb.T, peer_u[l].astype(BF16), peer_v[l].T.astype(BF16), s1, b1, th, az)
        h = _resid_ln(h, yt.T, mod, group_of_block, ln_ffn[l], alpha, 5)

    return h.reshape(b, lt, d)[:, n_ctx:, :]
```

```python
import functools
import math

import numpy as np
import jax
import jax.numpy as jnp
from jax import lax
from jax.experimental import pallas as pl
from jax.experimental.pallas import tpu as pltpu

GRID_W = 64
HEAD_DIM = 64
ROPE_THETA = 10000.0
A_HEADS = 4
A_KV_HEADS = 2
DIFF_HEADS = 4
DIFF_DIM = 32
WIN_HEADS = 4
WIN_KV_HEADS = 2
WINDOW = 128
RET_HEADS = 4
RET_DK = 64
RET_DV = 64
RET_CHUNK = 128
N_BRANCH = 4
BRANCH_W = 256
PIECES = (
    A_HEADS * HEAD_DIM, A_KV_HEADS * HEAD_DIM, A_KV_HEADS * HEAD_DIM,
    2 * DIFF_HEADS * DIFF_DIM, 2 * DIFF_HEADS * DIFF_DIM, DIFF_HEADS * 2 * DIFF_DIM,
    WIN_HEADS * HEAD_DIM, WIN_KV_HEADS * HEAD_DIM, WIN_KV_HEADS * HEAD_DIM,
    RET_HEADS * RET_DK, RET_HEADS * RET_DK, RET_HEADS * RET_DV, RET_HEADS * RET_DV,
)
MIX_COLS = sum(PIECES)
PIECE_OFF = tuple(int(v) for v in np.cumsum((0,) + PIECES))
PEER_HEADS = 8
PEER_NK = 128
PEER_TOPK = 16
PEER_DQ = 128
PEER_CAND_ROWS = -(-sum((PEER_TOPK + 1) // (p + 1) for p in range(PEER_TOPK + 1)) // 8) * 8
LN_EPS = 1e-5
RMS_EPS = 1e-6
LOG2E = 1.4426950408889634

F32 = jnp.float32
BF16 = jnp.bfloat16

TOKEN_BLOCK = 256
ATT_TQ = 256
ATT_TK = 1024
ATT_VPAD = 80
PEER_TOKENS = 512
PEER_EXPERTS = 1024
VMEM_LIMIT = 56 * 1024 * 1024
BF16_ROWS = 16


def _cparams(sem):
    return pltpu.CompilerParams(dimension_semantics=sem, vmem_limit_bytes=VMEM_LIMIT)


def _split_bf16(a):
    hi = a.astype(BF16)
    lo = (a - hi.astype(F32)).astype(BF16)
    return hi, lo


def _dot3(a, b, dims=(((1,), (0,)), ((), ()))):
    ah, al = _split_bf16(a)
    bh, bl = _split_bf16(b)
    d = functools.partial(lax.dot_general, dimension_numbers=dims, preferred_element_type=F32)
    return d(ah, bh) + (d(al, bh) + d(ah, bl))


def _mod_kernel(c_ref, w_ref, b_ref, o_ref):
    o_ref[...] = _dot3(c_ref[...], w_ref[...]) + b_ref[...]


def _modulation(cond8, w, b):
    d, n = w.shape
    tn = 1536
    return pl.pallas_call(
        _mod_kernel,
        out_shape=jax.ShapeDtypeStruct((8, n), F32),
        grid=(n // tn,),
        in_specs=[pl.BlockSpec((8, d), lambda j: (0, 0)),
                  pl.BlockSpec((d, tn), lambda j: (0, j)),
                  pl.BlockSpec((1, tn), lambda j: (0, j))],
        out_specs=pl.BlockSpec((8, tn), lambda j: (0, j)),
        compiler_params=_cparams(("arbitrary",)),
    )(cond8, w, b.reshape(1, n))


def _inproj_kernel(x_ref, mod_ref, w_ref, o_ref, *, sigmoid, chunk):
    m = mod_ref[...]
    xm = (x_ref[...] * (1.0 + m[1:2, :]) + m[0:1, :]).astype(BF16)
    n = w_ref.shape[1]
    for j in range(n // chunk):
        acc = jnp.dot(xm, w_ref[:, j * chunk:(j + 1) * chunk], preferred_element_type=F32)
        if sigmoid:
            acc = jax.nn.sigmoid(acc)
        o_ref[:, j * chunk:(j + 1) * chunk] = acc.astype(o_ref.dtype)


def _inproj(h, mod, group_of_block, w, *, sigmoid, out_dtype):
    t, d = h.shape
    n = w.shape[1]
    tb = TOKEN_BLOCK
    return pl.pallas_call(
        functools.partial(_inproj_kernel, sigmoid=sigmoid, chunk=512),
        out_shape=jax.ShapeDtypeStruct((t, n), out_dtype),
        grid=(t // tb,),
        in_specs=[pl.BlockSpec((tb, d), lambda i: (i, 0)),
                  pl.BlockSpec((None, 6, d), lambda i: (group_of_block(i), 0, 0)),
                  pl.BlockSpec((d, n), lambda i: (0, 0))],
        out_specs=pl.BlockSpec((tb, n), lambda i: (i, 0)),
        compiler_params=_cparams(("parallel",)),
    )(h, mod, w)


TOK_PIECES = ((1, True), (4, True), (6, True), (7, True), (8, False), (9, True), (11, False), (12, False))
FEAT_PIECES = ((0, True), (3, True), (2, False), (5, False), (10, True))


def _piece_cols(pieces):
    flips = {0: 16, 1: 16, 3: 8, 4: 8, 6: 16, 7: 16, 9: 32, 10: 32}
    cols, offs = [], {}
    n = 0
    for p, rotary in pieces:
        base = np.arange(PIECE_OFF[p], PIECE_OFF[p + 1])
        offs[p] = n
        cols.append(base)
        n += len(base)
        if rotary:
            cols.append(PIECE_OFF[p] + ((base - PIECE_OFF[p]) ^ flips[p]))
            n += len(base)
    return np.concatenate(cols), offs


TOK_COLS, TOK_OFF = _piece_cols(TOK_PIECES)
FEAT_COLS, FEAT_OFF = _piece_cols(FEAT_PIECES)
TTOK_OFF = {1: 0, 4: 128, 6: 384, 7: 640, 9: 768}
TTOK_W = 1024
TFEAT_OFF = {0: 0, 3: 256, 10: 512}
TFEAT_W = 768


def _mixer_in_kernel(x_ref, mod_ref, wt_ref, wf_ref, ttok_ref, tfeat_ref, gtok_ref, gfeat_ref, avg_ref,
                     kall_ref, qt_ref, vt_ref, wq_ref, wk_ref, wv_ref, rq_ref, rkt_ref, rv_ref, rg_ref):
    m = mod_ref[...]
    u = x_ref[...] * (1.0 + m[1:2, :]) + m[0:1, :]
    xm = u.astype(BF16)
    xmt = u.T.astype(BF16)
    tb = xm.shape[0]

    def tok(p, swapped=False):
        a = TOK_OFF[p] + (PIECES[p] if swapped else 0)
        return jnp.dot(xm, wt_ref[:, a:a + PIECES[p]], preferred_element_type=F32)

    def feat(p, swapped=False):
        a = FEAT_OFF[p] + (PIECES[p] if swapped else 0)
        return jnp.dot(wf_ref[a:a + PIECES[p], :], xmt, preferred_element_type=F32)

    def rope_tok(p, x, xs):
        a = TTOK_OFF[p]
        return x * ttok_ref[:, a:a + PIECES[p]] + xs * ttok_ref[:, TTOK_W + a:TTOK_W + a + PIECES[p]]

    def rope_feat(p, x, xs):
        a = TFEAT_OFF[p]
        return x * tfeat_ref[a:a + PIECES[p], :] + xs * tfeat_ref[TFEAT_W + a:TFEAT_W + a + PIECES[p], :]

    x, xs = tok(1), tok(1, True)
    sq_hi, sq_lo = _split_bf16(x * x)
    avg = avg_ref[...]
    ms = jnp.dot(sq_hi, avg, preferred_element_type=F32) + jnp.dot(sq_lo, avg, preferred_element_type=F32)
    r = lax.rsqrt(ms + RMS_EPS)
    ka = r * rope_tok(1, x * gtok_ref[0:1, :], xs * gtok_ref[1:2, :])
    kd = rope_tok(4, tok(4), tok(4, True))
    kall_ref[:, :PIECES[1]] = ka.astype(BF16)
    kall_ref[:, PIECES[1]:] = kd.astype(BF16)

    def split_heads(ref, val):
        for hd in range(ref.shape[0]):
            ref[hd] = val[:, hd * HEAD_DIM:(hd + 1) * HEAD_DIM].astype(ref.dtype)

    split_heads(wq_ref, rope_tok(6, tok(6), tok(6, True)))
    split_heads(wk_ref, rope_tok(7, tok(7), tok(7, True)))
    split_heads(wv_ref, tok(8))
    split_heads(rq_ref, rope_tok(9, tok(9), tok(9, True)))
    split_heads(rv_ref, tok(11))
    rg_ref[...] = tok(12)
    rkt = rope_feat(10, feat(10), feat(10, True))
    for hd in range(RET_HEADS):
        rkt_ref[hd] = rkt[hd * RET_DK:(hd + 1) * RET_DK]

    zero64 = jnp.zeros((HEAD_DIM, tb), F32)
    zero32 = jnp.zeros((DIFF_DIM, tb), F32)

    def place(q, upper):
        return jnp.concatenate([zero64, q] if upper else [q, zero64], axis=0).astype(BF16)

    xq, xqs = feat(0), feat(0, True)
    for hd in range(A_HEADS):
        rows = slice(hd * HEAD_DIM, (hd + 1) * HEAD_DIM)
        xh = xq[rows]
        rh = lax.rsqrt(jnp.mean(xh * xh, axis=0, keepdims=True) + RMS_EPS)
        swapped_rows = slice(A_HEADS * HEAD_DIM + hd * HEAD_DIM, A_HEADS * HEAD_DIM + (hd + 1) * HEAD_DIM)
        qh = rh * (xh * gfeat_ref[rows, :] * tfeat_ref[rows, :]
                   + xqs[rows] * gfeat_ref[swapped_rows, :]
                   * tfeat_ref[TFEAT_W + hd * HEAD_DIM:TFEAT_W + (hd + 1) * HEAD_DIM, :])
        qt_ref[hd // 2, hd % 2] = place(qh, hd // 2 == 1)
    qd = rope_feat(3, feat(3), feat(3, True))
    for hd in range(DIFF_HEADS):
        qh = qd[hd * HEAD_DIM:(hd + 1) * HEAD_DIM]
        q1 = jnp.concatenate([qh[:DIFF_DIM], zero32], axis=0)
        q2 = jnp.concatenate([zero32, qh[DIFF_DIM:]], axis=0)
        qt_ref[A_KV_HEADS + hd, 0] = place(q1, hd % 2 == 1)
        qt_ref[A_KV_HEADS + hd, 1] = place(q2, hd % 2 == 1)

    pad_rows = vt_ref.shape[1] - HEAD_DIM
    tail = (lax.broadcasted_iota(jnp.int32, (pad_rows, tb), 0) == 0).astype(F32)
    va, vd = feat(2), feat(5)
    for g in range(A_KV_HEADS + DIFF_HEADS):
        v = va[g * HEAD_DIM:(g + 1) * HEAD_DIM] if g < A_KV_HEADS else \
            vd[(g - A_KV_HEADS) * HEAD_DIM:(g - A_KV_HEADS + 1) * HEAD_DIM]
        vt_ref[g] = jnp.concatenate([v, tail], axis=0).astype(BF16)


def _mixer_in(h, mod, group_of_block, wt, wf, ttok, tfeat, gtok, gfeat, avg, b, lt):
    t, d = h.shape
    tb = TOKEN_BLOCK
    bpb = lt // tb
    n_groups = A_KV_HEADS + DIFF_HEADS
    kw = PIECES[1] + PIECES[4]
    row = lambda w: pl.BlockSpec((tb, w), lambda i: (i, 0))
    const = lambda a: pl.BlockSpec(a.shape, lambda i: (0,) * a.ndim)
    heads_shape = lambda n, dt: jax.ShapeDtypeStruct((b, n, lt, HEAD_DIM), dt)
    heads_spec = lambda n: pl.BlockSpec((None, n, tb, HEAD_DIM), lambda i: (i // bpb, 0, i % bpb, 0))
    return pl.pallas_call(
        _mixer_in_kernel,
        out_shape=(jax.ShapeDtypeStruct((b, lt, kw), BF16),
                   jax.ShapeDtypeStruct((b, n_groups, bpb, 2, 2 * HEAD_DIM, tb), BF16),
                   jax.ShapeDtypeStruct((b, n_groups, bpb, ATT_VPAD, tb), BF16),
                   heads_shape(WIN_HEADS, BF16), heads_shape(WIN_KV_HEADS, BF16), heads_shape(WIN_KV_HEADS, BF16),
                   heads_shape(RET_HEADS, F32), jax.ShapeDtypeStruct((b, RET_HEADS, RET_DK, lt), F32),
                   heads_shape(RET_HEADS, F32), jax.ShapeDtypeStruct((t, PIECES[12]), F32)),
        grid=(t // tb,),
        in_specs=[row(d),
                  pl.BlockSpec((None, 6, d), lambda i: (group_of_block(i), 0, 0)),
                  const(wt), const(wf),
                  pl.BlockSpec((tb, 2 * TTOK_W), lambda i: (i % bpb, 0)),
                  pl.BlockSpec((2 * TFEAT_W, tb), lambda i: (0, i % bpb)),
                  const(gtok), const(gfeat), const(avg)],
        out_specs=(pl.BlockSpec((None, tb, kw), lambda i: (i // bpb, i % bpb, 0)),
                   pl.BlockSpec((None, n_groups, None, 2, 2 * HEAD_DIM, tb),
                                lambda i: (i // bpb, 0, i % bpb, 0, 0, 0)),
                   pl.BlockSpec((None, n_groups, None, ATT_VPAD, tb), lambda i: (i // bpb, 0, i % bpb, 0, 0)),
                   heads_spec(WIN_HEADS), heads_spec(WIN_KV_HEADS), heads_spec(WIN_KV_HEADS),
                   heads_spec(RET_HEADS),
                   pl.BlockSpec((None, RET_HEADS, RET_DK, tb), lambda i: (i // bpb, 0, 0, i % bpb)),
                   heads_spec(RET_HEADS), row(PIECES[12])),
        compiler_params=_cparams(("parallel",)),
    )(h, mod, wt, wf, ttok, tfeat, gtok, gfeat, avg)


def _flash_kernel(qt_ref, k_ref, vt_ref, o_ref, s_ref, *, n_ctx, tk):
    i = pl.program_id(2)
    qt = jnp.concatenate([qt_ref[0], qt_ref[1]], axis=1)
    w = qt.shape[1]
    tile = vt_ref.shape[2]
    nlc = (k_ref.shape[0] - n_ctx) // tk

    def scores(row0, rows):
        return jnp.dot(k_ref[pl.ds(row0, rows), :], qt, preferred_element_type=F32)

    def absorb(s, tile0, carry):
        m, acc = carry
        m_new = jnp.maximum(m, jnp.max(s, axis=0, keepdims=True))
        p = jnp.exp2(s - m_new).astype(BF16)
        acc = acc * jnp.exp2(m - m_new)
        for j in range(s.shape[0] // tile):
            acc = acc + jnp.dot(vt_ref[tile0 + j], p[j * tile:(j + 1) * tile], preferred_element_type=F32)
        return m_new, acc

    carry = (jnp.full((1, w), -jnp.inf, F32), jnp.zeros((vt_ref.shape[1], w), F32))
    carry = absorb(scores(0, n_ctx), 0, carry)
    lat_row = lambda c: pl.multiple_of(n_ctx + c * tk, tile)
    lat_tile = lambda c: (n_ctx + c * tk) // tile
    s_ref[0] = scores(lat_row(0), tk)

    def pair(j, carry):
        c0 = 2 * j
        s_ref[1] = scores(lat_row(c0 + 1), tk)
        carry = absorb(s_ref[0], lat_tile(c0), carry)
        s_ref[0] = scores(lat_row(jnp.minimum(c0 + 2, nlc - 1)), tk)
        return absorb(s_ref[1], lat_tile(c0 + 1), carry)

    n_pairs = jnp.where(i * (w // 2) < n_ctx, 0, nlc // 2)
    _, acc = lax.fori_loop(0, n_pairs, pair, carry)
    o_ref[...] = acc[:HEAD_DIM] * (1.0 / acc[HEAD_DIM:HEAD_DIM + 1])


def _flash(qt, k_all, vt, n_ctx):
    b, g, nqb, _, dk, tq = qt.shape
    lt = k_all.shape[1]
    s = lt - n_ctx
    tk = next(c for c in (ATT_TK, 512, 256) if s % (2 * c) == 0)
    assert n_ctx % tq == 0 and tk % tq == 0
    key_block = lambda gi: jnp.where(gi < A_KV_HEADS, 0, 1 + (gi - A_KV_HEADS) // 2)
    return pl.pallas_call(
        functools.partial(_flash_kernel, n_ctx=n_ctx, tk=tk),
        out_shape=jax.ShapeDtypeStruct((b, g, nqb, HEAD_DIM, 2 * tq), F32),
        grid=(b, g, nqb),
        in_specs=[pl.BlockSpec((None, None, None, 2, dk, tq), lambda bi, gi, i: (bi, gi, i, 0, 0, 0)),
                  pl.BlockSpec((None, lt, dk), lambda bi, gi, i: (bi, 0, key_block(gi))),
                  pl.BlockSpec((None, None) + vt.shape[2:], lambda bi, gi, i: (bi, gi, 0, 0, 0))],
        out_specs=pl.BlockSpec((None, None, None, HEAD_DIM, 2 * tq), lambda bi, gi, i: (bi, gi, i, 0, 0)),
        scratch_shapes=[pltpu.VMEM((2, tk, 2 * tq), F32)],
        compiler_params=_cparams(("parallel", "parallel", "arbitrary")),
    )(qt, k_all, vt)


def _window_kernel(q_ref, kp_ref, kc_ref, kn_ref, vp_ref, vc_ref, vn_ref, kx_ref, vx_ref, sink_ref,
                   o_ref, *, n_ctx_blocks, n_blocks):
    qb = pl.program_id(2)
    q = q_ref[...].reshape(-1, HEAD_DIM)
    rows = q.shape[0]
    nt = (((1,), (1,)), ((), ()))
    sdot = functools.partial(lax.dot_general, dimension_numbers=nt, preferred_element_type=F32)
    qi = lax.broadcasted_iota(jnp.int32, (rows, WINDOW), 0) % WINDOW
    kj = lax.broadcasted_iota(jnp.int32, (rows, WINDOW), 1)
    neg = -jnp.inf
    off_p = jnp.where(qb >= n_ctx_blocks + 1, 0, 2 * WINDOW)
    off_c = jnp.where(qb >= n_ctx_blocks, 0, 2 * WINDOW)
    off_n = jnp.where(jnp.logical_and(qb >= n_ctx_blocks, qb <= n_blocks - 2), 0, 2 * WINDOW)
    s_p = jnp.where(kj >= qi + off_p, sdot(q, kp_ref[...]), neg)
    s_c = jnp.where(kj >= off_c, sdot(q, kc_ref[...]), neg)
    s_n = jnp.where(kj <= qi - off_n, sdot(q, kn_ref[...]), neg)
    kx = kx_ref[...]
    vx = vx_ref[...]
    s_x = sdot(q, kx)
    sink = sink_ref[...]
    m = jnp.maximum(jnp.maximum(jnp.max(s_p, axis=1, keepdims=True), jnp.max(s_c, axis=1, keepdims=True)),
                    jnp.maximum(jnp.max(s_n, axis=1, keepdims=True), jnp.max(s_x, axis=1, keepdims=True)))
    m = jnp.maximum(m, sink)
    e_p, e_c, e_n, e_x = (jnp.exp2(s - m) for s in (s_p, s_c, s_n, s_x))
    den = (jnp.sum(e_p, axis=1, keepdims=True) + jnp.sum(e_c, axis=1, keepdims=True)
           + jnp.sum(e_n, axis=1, keepdims=True) + jnp.sum(e_x, axis=1, keepdims=True)
           + jnp.exp2(sink - m))
    pv = functools.partial(jnp.dot, preferred_element_type=F32)
    o = (pv(e_p.astype(BF16), vp_ref[...]) + pv(e_c.astype(BF16), vc_ref[...])
         + pv(e_n.astype(BF16), vn_ref[...]) + pv(e_x.astype(BF16), vx))
    o_ref[...] = o / den


def _window(q, k, v, sink, n_ctx):
    b, hq, lt, dh = q.shape
    g = k.shape[1]
    blk = WINDOW
    nb = lt // blk
    rows = (hq // g) * blk
    n_ctx_blocks = n_ctx // blk
    lo, hi = n_ctx_blocks, nb - 1

    def nbr(delta):
        return lambda bi, gi, i: (bi, gi, jnp.clip(i + delta, lo, hi), 0)

    kv_spec = lambda d: pl.BlockSpec((None, None, blk, dh), nbr(d))
    ctx_spec = pl.BlockSpec((None, None, n_ctx, dh), lambda bi, gi, i: (bi, gi, 0, 0))
    return pl.pallas_call(
        functools.partial(_window_kernel, n_ctx_blocks=n_ctx_blocks, n_blocks=nb),
        out_shape=jax.ShapeDtypeStruct((b, g, nb, rows, dh), F32),
        grid=(b, g, nb),
        in_specs=[pl.BlockSpec((None, hq // g, blk, dh), lambda bi, gi, i: (bi, gi, i, 0)),
                  kv_spec(-1), kv_spec(0), kv_spec(1), kv_spec(-1), kv_spec(0), kv_spec(1),
                  ctx_spec, ctx_spec,
                  pl.BlockSpec((None, rows, 1), lambda bi, gi, i: (gi, 0, 0))],
        out_specs=pl.BlockSpec((None, None, None, rows, dh), lambda bi, gi, i: (bi, gi, i, 0, 0)),
        compiler_params=_cparams(("parallel", "parallel", "arbitrary")),
    )(q, k, k, k, v, v, v, k, v, sink)


def _retention_kernel(q_ref, kt_ref, v_ref, dmat_ref, xi_ref, zeta_ref, gch_ref, o_ref, st_ref):
    t = pl.program_id(2)

    @pl.when(t == 0)
    def _():
        st_ref[...] = jnp.zeros_like(st_ref)

    for hd in range(RET_HEADS):
        q = q_ref[hd]
        kt = kt_ref[hd]
        v = v_ref[hd]
        st = st_ref[hd]
        inner = _dot3(q, kt) * dmat_ref[hd]
        o_ref[hd] = _dot3(inner, v) + _dot3(q, st) * xi_ref[hd]
        st_ref[hd] = st * gch_ref[hd] + _dot3(kt * zeta_ref[hd], v)


def _retention_call(q, kt, v, dmat, xi, zeta, gch, n_ctx_chunks):
    b, hh, lt, dk = q.shape
    dv = v.shape[-1]
    c = RET_CHUNK
    nch = lt // c

    def blk(d, ti):
        back = jnp.where(ti < n_ctx_chunks, n_ctx_chunks - 1 - ti, nch - 1 - (ti - n_ctx_chunks))
        return jnp.where(d == 0, ti, back)

    tab = lambda shape: pl.BlockSpec((None,) + shape, lambda d, bi, ti: (d,) + (0,) * len(shape))
    return pl.pallas_call(
        _retention_kernel,
        out_shape=jax.ShapeDtypeStruct((2, b, hh, lt, dv), F32),
        grid=(2, b, nch),
        in_specs=[pl.BlockSpec((None, hh, c, dk), lambda d, bi, ti: (bi, 0, blk(d, ti), 0)),
                  pl.BlockSpec((None, hh, dk, c), lambda d, bi, ti: (bi, 0, 0, blk(d, ti))),
                  pl.BlockSpec((None, hh, c, dv), lambda d, bi, ti: (bi, 0, blk(d, ti), 0)),
                  tab((hh, c, c)), tab((hh, c, 1)), tab((hh, 1, c)), tab((hh, 1, 1))],
        out_specs=pl.BlockSpec((None, None, hh, c, dv), lambda d, bi, ti: (d, bi, 0, blk(d, ti), 0)),
        scratch_shapes=[pltpu.VMEM((hh, dk, dv), F32)],
        compiler_params=_cparams(("parallel", "parallel", "arbitrary")),
    )(q, kt, v, dmat, xi, zeta, gch)


def _layer_norm_rows(z, ln):
    mu = jnp.mean(z, axis=-1, keepdims=True)
    zc = z - mu
    var = jnp.mean(zc * zc, axis=-1, keepdims=True)
    return zc * lax.rsqrt(var + LN_EPS) * ln[0:1, :] + ln[1:2, :]


def _merge_kernel(fa_ref, fd0_ref, fd1_ref, win_ref, ret_ref, rg_ref, g_ref, h_ref, mod_ref, lam_ref,
                  subln_ref, rnorm_ref, wb_ref, wo_ref, ln_ref, o_ref, *, alpha):
    d = h_ref.shape[1]
    tb = h_ref.shape[0]
    hd = HEAD_DIM
    proj = functools.partial(jnp.dot, preferred_element_type=F32)
    gate = lambda i: g_ref[:, i * d:(i + 1) * d].astype(F32)

    oat = jnp.concatenate([fa_ref[g][:, st * tb:(st + 1) * tb] for g in range(A_KV_HEADS) for st in range(2)],
                          axis=0)
    m = gate(0) * proj(oat.T.astype(BF16), wb_ref[0])

    lam = lam_ref[...]
    heads = []
    for h4 in range(DIFF_HEADS):
        f = (fd0_ref if h4 < 2 else fd1_ref)[h4 % 2]
        o = f[:, :tb] - lam * f[:, tb:]
        heads.append(o * lax.rsqrt(jnp.mean(o * o, axis=0, keepdims=True) + RMS_EPS))
    obt = jnp.concatenate(heads, axis=0) * subln_ref[...]
    m = m + gate(1) * proj(obt.T.astype(BF16), wb_ref[1])

    half = tb // 2
    acc = None
    for h4 in range(WIN_HEADS):
        g, st = h4 // 2, h4 % 2
        o = jnp.concatenate([win_ref[g, 0][st * half:(st + 1) * half], win_ref[g, 1][st * half:(st + 1) * half]],
                            axis=0)
        t = proj(o.astype(BF16), wb_ref[2, h4 * hd:(h4 + 1) * hd, :])
        acc = t if acc is None else acc + t
    m = m + gate(2) * acc

    acc = None
    for h4 in range(RET_HEADS):
        cols = slice(h4 * RET_DV, (h4 + 1) * RET_DV)
        o = ret_ref[0, h4] + ret_ref[1, h4]
        mu = jnp.mean(o, axis=-1, keepdims=True)
        oc = o - mu
        var = jnp.mean(oc * oc, axis=-1, keepdims=True)
        on = oc * lax.rsqrt(var + LN_EPS) * rnorm_ref[0:1, cols] + rnorm_ref[1:2, cols]
        gt = rg_ref[:, cols]
        t = proj((on * (gt / (1.0 + jnp.exp(-gt)))).astype(BF16), wb_ref[3, cols, :])
        acc = t if acc is None else acc + t
    m = m + gate(3) * acc

    y = proj(m.astype(BF16), wo_ref[...])
    z = alpha * h_ref[...] + mod_ref[2:3, :] * y
    o_ref[...] = _layer_norm_rows(z, ln_ref[...])


def _merge(flash_out, win_out, ret_out, rg, gates, h, mod, group_of_block, lam, subln, rnorm, wb, wo, ln,
           alpha, b, lt):
    t, d = h.shape
    tb = TOKEN_BLOCK
    bpb = lt // tb
    row = lambda n: pl.BlockSpec((tb, n), lambda i: (i, 0))
    const = lambda a: pl.BlockSpec(a.shape, lambda i: (0,) * a.ndim)
    fspec = lambda gb: pl.BlockSpec((None, 2, None) + flash_out.shape[3:], lambda i: (i // bpb, gb, i % bpb, 0, 0))
    return pl.pallas_call(
        functools.partial(_merge_kernel, alpha=alpha),
        out_shape=jax.ShapeDtypeStruct((t, d), F32),
        grid=(t // tb,),
        in_specs=[fspec(0), fspec(1), fspec(2),
                  pl.BlockSpec((None, WIN_KV_HEADS, 2) + win_out.shape[3:], lambda i: (i // bpb, 0, i % bpb, 0, 0)),
                  pl.BlockSpec((2, None, RET_HEADS, tb, RET_DV), lambda i: (0, i // bpb, 0, i % bpb, 0)),
                  row(rg.shape[1]), row(N_BRANCH * d), row(d),
                  pl.BlockSpec((None, 6, d), lambda i: (group_of_block(i), 0, 0)),
                  const(lam), const(subln), const(rnorm), const(wb), const(wo), const(ln)],
        out_specs=row(d),
        compiler_params=_cparams(("parallel",)),
    )(flash_out, flash_out, flash_out, win_out, ret_out, rg, gates, h, mod, lam, subln, rnorm, wb, wo, ln)


def _top_rows(s, n, with_rank=False):
    out = []
    cur = s
    rank = jnp.full(s.shape, float(n), F32) if with_rank else None
    for r in range(n):
        mx = jnp.max(cur, axis=0, keepdims=True)
        out.append(mx)
        if with_rank:
            rank = jnp.where(cur == mx, float(r), rank)
        if r + 1 < n:
            cur = jnp.where(cur == mx, -jnp.inf, cur)
    return (out, rank) if with_rank else out


def _peer_route_kernel(h_ref, mod_ref, wh_ref, wl_ref, sk_ref, xt_ref, rk_ref, b1_ref, nn_ref, az_ref,
                       cand_ref):
    m = mod_ref[...]
    u = h_ref[...] * (1.0 + m[4:5, :]) + m[3:4, :]
    xt_ref[...] = u.T.astype(BF16)
    uh, ul = _split_bf16(u)
    d = functools.partial(jnp.dot, preferred_element_type=F32)
    nk = PEER_NK
    nt = (((1,), (1,)), ((), ()))
    k1 = PEER_TOPK + 1
    for hd in range(PEER_HEADS):
        c0 = 2 * hd * PEER_DQ
        wh = wh_ref[:, c0:c0 + 2 * PEER_DQ]
        wl = wl_ref[:, c0:c0 + 2 * PEER_DQ]
        q = d(uh, wh) + (d(ul, wh) + d(uh, wl))
        st = [_dot3(sk_ref[hd, p], q[:, p * PEER_DQ:(p + 1) * PEER_DQ], nt) for p in range(2)]
        top0 = _top_rows(st[0], k1)
        top1, rank1 = _top_rows(st[1], k1, with_rank=True)
        r = 0
        for p0 in range(k1):
            for p1 in range(k1 // (p0 + 1)):
                cand_ref[r:r + 1, :] = top0[p0] + top1[p1]
                r += 1
        cand_ref[r:, :] = jnp.full((cand_ref.shape[0] - r, cand_ref.shape[1]), -jnp.inf, F32)
        cand = cand_ref[...]
        ctop = _top_rows(cand, k1)
        tau = 0.5 * (ctop[PEER_TOPK - 1] + ctop[PEER_TOPK])
        mx = top0[0] + top1[0]
        z = jnp.sum(jnp.where(cand >= tau, jnp.exp(cand - mx), 0.0), axis=0, keepdims=True)
        th = tau - st[0]
        nn = jnp.zeros_like(th)
        for q in range(PEER_TOPK):
            nn = nn + jnp.where(top1[q] >= th, 1.0, 0.0)
        rk_ref[hd] = rank1.astype(BF16)
        b1_ref[hd] = jnp.exp(st[1] - top1[0]).astype(BF16)
        nn_ref[hd] = nn
        az_ref[hd] = jnp.exp(st[0] - top0[0]) / z


def _peer_route(h, mod, group_of_block, wq_hi, wq_lo, subkeys):
    t, d = h.shape
    tb = TOKEN_BLOCK
    hh, nk = PEER_HEADS, PEER_NK
    st_shape = lambda dt: jax.ShapeDtypeStruct((hh, nk, t), dt)
    st_spec = pl.BlockSpec((hh, nk, tb), lambda i: (0, 0, i))
    nq = wq_hi.shape[1]
    return pl.pallas_call(
        _peer_route_kernel,
        out_shape=(jax.ShapeDtypeStruct((d, t), BF16), st_shape(BF16), st_shape(BF16), st_shape(F32),
                   st_shape(F32)),
        grid=(t // tb,),
        in_specs=[pl.BlockSpec((tb, d), lambda i: (i, 0)),
                  pl.BlockSpec((None, 6, d), lambda i: (group_of_block(i), 0, 0)),
                  pl.BlockSpec((d, nq), lambda i: (0, 0)),
                  pl.BlockSpec((d, nq), lambda i: (0, 0)),
                  pl.BlockSpec((hh, 2, nk, PEER_DQ), lambda i: (0, 0, 0, 0))],
        out_specs=(pl.BlockSpec((d, tb), lambda i: (0, i)), st_spec, st_spec, st_spec, st_spec),
        scratch_shapes=[pltpu.VMEM((PEER_CAND_ROWS, tb), F32)],
        compiler_params=_cparams(("parallel",)),
    )(h, mod, wq_hi, wq_lo, subkeys)


GELU_K1 = -2.0 * math.sqrt(2.0 / math.pi) * LOG2E
GELU_K2 = GELU_K1 * 0.044715


def _gelu_tanh(x):
    return x / (1.0 + jnp.exp2(x * (GELU_K1 + GELU_K2 * (x * x))))


def _peer_dense_kernel(xt_ref, u_ref, vt_ref, rk_ref, b1_ref, nn_ref, az_ref, yt_ref, g_ref):
    c = pl.program_id(1)
    nk = PEER_NK
    tp = xt_ref.shape[1]
    rows_per_step = u_ref.shape[0] // nk
    n_chunks = pl.num_programs(1) - 1

    @pl.when(c == 0)
    def _():
        yt_ref[...] = jnp.zeros_like(yt_ref)
        g_ref[...] = jnp.zeros_like(g_ref)

    xt = xt_ref[...]
    row0 = jnp.minimum(c, n_chunks - 1) * rows_per_step
    pre = lambda ii: jnp.dot(u_ref[ii * nk:(ii + 1) * nk, :], xt, preferred_element_type=F32)
    act_next = pre(0)
    for ii in range(rows_per_step):
        if ii % 2 == 0:
            pair = slice(ii * nk, (ii + 2) * nk)
            yt_ref[...] += jnp.dot(vt_ref[:, pair], g_ref[pair, :], preferred_element_type=F32)
        act = act_next
        if ii + 1 < rows_per_step:
            act_next = pre(ii + 1)
        i = row0 + ii
        w = None
        tile = (nk // BF16_ROWS, BF16_ROWS, tp)
        row = lambda ref, hd: jnp.broadcast_to(ref[hd, pl.ds(i, 1), :], (BF16_ROWS, tp)).astype(BF16)[None]
        for hd in range(PEER_HEADS):
            t = jnp.where(rk_ref[hd].reshape(tile) < row(nn_ref, hd), b1_ref[hd].reshape(tile),
                          jnp.zeros((), BF16)) * row(az_ref, hd)
            w = t if w is None else w + t
        g_ref[ii * nk:(ii + 1) * nk, :] = w.reshape(nk, tp) * _gelu_tanh(act.astype(BF16))


def _peer_dense(xt, u, vt, rk, b1, nn, az):
    d, t = xt.shape
    n = u.shape[0]
    tp = PEER_TOKENS
    ec = PEER_EXPERTS
    nc = n // ec
    hh, nk = PEER_HEADS, PEER_NK
    st_spec = pl.BlockSpec((hh, nk, tp), lambda i, c: (0, 0, i))
    return pl.pallas_call(
        _peer_dense_kernel,
        out_shape=jax.ShapeDtypeStruct((d, t), F32),
        grid=(t // tp, nc + 1),
        in_specs=[pl.BlockSpec((d, tp), lambda i, c: (0, i)),
                  pl.BlockSpec((ec, d), lambda i, c: (jnp.minimum(c, nc - 1), 0)),
                  pl.BlockSpec((d, ec), lambda i, c: (0, jnp.maximum(c - 1, 0))),
                  st_spec, st_spec, st_spec, st_spec],
        out_specs=pl.BlockSpec((d, tp), lambda i, c: (0, i)),
        scratch_shapes=[pltpu.VMEM((ec, tp), BF16)],
        compiler_params=_cparams(("parallel", "arbitrary")),
    )(xt, u, vt, rk, b1, nn, az)


def _resid_ln_kernel(h_ref, yt_ref, mod_ref, ln_ref, o_ref, *, alpha, gate_row):
    z = alpha * h_ref[...] + mod_ref[gate_row:gate_row + 1, :] * yt_ref[...].T
    o_ref[...] = _layer_norm_rows(z, ln_ref[...])


def _resid_ln(h, yt, mod, group_of_block, ln, alpha, gate_row):
    t, d = h.shape
    tb = TOKEN_BLOCK
    row = pl.BlockSpec((tb, d), lambda i: (i, 0))
    return pl.pallas_call(
        functools.partial(_resid_ln_kernel, alpha=alpha, gate_row=gate_row),
        out_shape=jax.ShapeDtypeStruct((t, d), F32),
        grid=(t // tb,),
        in_specs=[row, pl.BlockSpec((d, tb), lambda i: (0, i)),
                  pl.BlockSpec((None, 6, d), lambda i: (group_of_block(i), 0, 0)),
                  pl.BlockSpec((2, d), lambda i: (0, 0))],
        out_specs=row,
        compiler_params=_cparams(("parallel",)),
    )(h, yt, mod, ln)


def _axial_tables(s, n_ctx, d):
    rows = s // GRID_W
    row = jnp.broadcast_to(jnp.arange(rows, dtype=F32)[:, None], (rows, GRID_W)).reshape(-1)
    col = jnp.broadcast_to(jnp.arange(GRID_W, dtype=F32)[None, :], (rows, GRID_W)).reshape(-1)
    quarter = d // 4
    inv = ROPE_THETA ** (-jnp.arange(quarter, dtype=F32) / quarter)
    ar, ac = row[:, None] * inv, col[:, None] * inv
    cos = jnp.concatenate([jnp.cos(ar), jnp.cos(ar), jnp.cos(ac), jnp.cos(ac)], axis=-1)
    sin = jnp.concatenate([-jnp.sin(ar), jnp.sin(ar), -jnp.sin(ac), jnp.sin(ac)], axis=-1)
    cos = jnp.concatenate([jnp.ones((n_ctx, d), F32), cos], axis=0)
    sin = jnp.concatenate([jnp.zeros((n_ctx, d), F32), sin], axis=0)
    return cos, sin


def _rope1d_tables(lt, d):
    half = d // 2
    inv = ROPE_THETA ** (-jnp.arange(half, dtype=F32) / half)
    ang = jnp.arange(lt, dtype=F32)[:, None] * inv
    return (jnp.concatenate([jnp.cos(ang), jnp.cos(ang)], axis=-1),
            jnp.concatenate([-jnp.sin(ang), jnp.sin(ang)], axis=-1))


def _swap_perm():
    cols = []
    for piece, flip in ((0, 16), (1, 16), (3, 8), (4, 8), (6, 16), (7, 16), (9, 32), (10, 32)):
        base = np.arange(PIECE_OFF[piece], PIECE_OFF[piece + 1])
        cols.append(PIECE_OFF[piece] + ((base - PIECE_OFF[piece]) ^ flip))
    return np.concatenate(cols)


SWAP_COLS = _swap_perm()
SWAP_PIECES = (0, 1, 3, 4, 6, 7, 9, 10)
SWAP_OFF = dict(zip(SWAP_PIECES, np.cumsum([0] + [PIECES[p] for p in SWAP_PIECES])[:-1] + MIX_COLS))


def _rms_heads(x, xs, gain, gain_s):
    r = lax.rsqrt(jnp.mean(x * x, axis=-1, keepdims=True) + RMS_EPS)
    return x * r * gain, xs * r * gain_s


def kernel(x, c, ctx, c_ctx, w_mod, b_mod, w_in, qk_gain, diff_lambda, diff_subln, win_sink, ret_decay,
           ret_norm, w_branch, w_out, ln_attn, ln_ffn, peer_wq, peer_subkeys, peer_u, peer_v):
    b, s, d = x.shape
    n_ctx = ctx.shape[1]
    depth = w_mod.shape[0]
    lt = n_ctx + s
    t = b * lt
    tb = TOKEN_BLOCK
    assert n_ctx % tb == 0 and s % tb == 0 and t % PEER_TOKENS == 0
    alpha = (2 * depth) ** 0.25
    blocks_per_batch = lt // tb
    ctx_blocks = n_ctx // tb

    def group_of_block(i):
        return jnp.where(i % blocks_per_batch < ctx_blocks, b, i // blocks_per_batch)

    cos64, sin64 = _axial_tables(s, n_ctx, HEAD_DIM)
    cos32, sin32 = _axial_tables(s, n_ctx, DIFF_DIM)
    cos1d, sin1d = _rope1d_tables(lt, RET_DK)
    sc_a = HEAD_DIM ** -0.5 * LOG2E
    sc_d = DIFF_DIM ** -0.5 * LOG2E
    rep = lambda a, n: jnp.tile(a, (1, n))
    tok_part = lambda t64, t32, t1d: [rep(t64, 2), rep(t32, 8), rep(t64, 4) * sc_a, rep(t64, 2), rep(t1d, 4)]
    feat_part = lambda t64, t32, t1d: [rep(t64, 4) * sc_a, rep(t32, 8) * sc_d, rep(t1d, 4) * RET_DK ** -0.5]
    ttok = jnp.concatenate(tok_part(cos64, cos32, cos1d) + tok_part(sin64, sin32, sin1d), axis=1)
    tfeat = jnp.concatenate(feat_part(cos64, cos32, cos1d) + feat_part(sin64, sin32, sin1d), axis=1).T
    swap64 = np.arange(HEAD_DIM) ^ 16
    head_ids = np.arange(2 * HEAD_DIM) // HEAD_DIM
    avg = jnp.asarray((head_ids[:, None] == head_ids[None, :]) / HEAD_DIM, BF16)
    cond8 = jnp.zeros((8, d), F32).at[:b].set(jax.nn.silu(c)).at[b].set(jax.nn.silu(c_ctx))

    h = jnp.concatenate([ctx, x], axis=1).reshape(t, d)
    nqb = lt // ATT_TQ
    nwb = lt // WINDOW
    n_groups = A_KV_HEADS + DIFF_HEADS

    for l in range(depth):
        mod = _modulation(cond8, w_mod[l], b_mod[l]).reshape(8, 6, d)
        w_mix = w_in[l, :, :MIX_COLS]
        w_gate = w_in[l, :, MIX_COLS:].astype(BF16)
        g0, g1 = qk_gain[l, 0].astype(F32), qk_gain[l, 1].astype(F32)
        gtok = jnp.stack([jnp.tile(g1, 2), jnp.tile(g1[swap64], 2)])
        gfeat = jnp.broadcast_to(jnp.concatenate([jnp.tile(g0, A_HEADS), jnp.tile(g0[swap64], A_HEADS)])[:, None],
                                 (2 * A_HEADS * HEAD_DIM, tb))
        kall, qt, vt, wq, wk, wv, rq, rkt, rv, rg = _mixer_in(
            h, mod, group_of_block, w_mix[:, TOK_COLS].astype(BF16), w_mix[:, FEAT_COLS].T.astype(BF16),
            ttok, tfeat, gtok, gfeat, avg, b, lt)
        gates = _inproj(h, mod, group_of_block, w_gate, sigmoid=True, out_dtype=BF16)

        ot = _flash(qt, kall, vt, n_ctx)
        lam_init = 0.8 - 0.6 * math.exp(-0.3 * l)
        lp = diff_lambda[l].astype(F32)
        lam = (jnp.exp(jnp.sum(lp[0] * lp[1])) - jnp.exp(jnp.sum(lp[2] * lp[3])) + lam_init).reshape(1, 1)
        subln = jnp.broadcast_to((jnp.tile(diff_subln[l].astype(F32), DIFF_HEADS) * (1.0 - lam_init))[:, None],
                                 (DIFF_HEADS * HEAD_DIM, tb))

        sink = jnp.repeat(win_sink[l].astype(F32) * LOG2E, WINDOW).reshape(WIN_KV_HEADS, 2 * WINDOW, 1)
        ow = _window(wq, wk, wv, sink, n_ctx)

        lg =jax.nn.log_sigmoid(ret_decay[l].astype(F32))
        idx = jnp.arange(RET_CHUNK, dtype=F32)
        diff = idx[:, None] - idx[None, :]
        lg3 = lg[:, :, None, None]
        dm_f = jnp.exp(jnp.where(diff >= 0, diff * lg3[0], -jnp.inf))
        dm_b = jnp.exp(jnp.where(diff <= 0, -diff * lg3[1], -jnp.inf))
        dmat = jnp.stack([dm_f, dm_b])
        xi = jnp.stack([jnp.exp((idx + 1.0) * lg[0][:, None]), jnp.exp((RET_CHUNK - idx) * lg[1][:, None])])
        zeta = jnp.stack([jnp.exp((RET_CHUNK - 1.0 - idx) * lg[0][:, None]), jnp.exp(idx * lg[1][:, None])])
        gch = jnp.exp(RET_CHUNK * lg)
        o_ret = _retention_call(rq, rkt, rv, dmat, xi[..., None], zeta[:, :, None, :],
                                gch[:, :, None, None], n_ctx // RET_CHUNK)

        h = _merge(ot, ow, o_ret, rg, gates, h, mod, group_of_block, lam, subln, ret_norm[l].astype(F32),
                   w_branch[l].astype(BF16), w_out[l].astype(BF16), ln_attn[l], alpha, b, lt)

        wq_hi, wq_lo = _split_bf16(peer_wq[l])
        xt, rk, b1, nn, az = _peer_route(h, mod, group_of_block, wq_hi, wq_lo, peer_subkeys[l])
        yt = _peer_dense(xt, peer_u[l].astype(BF16), peer_v[l].T.astype(BF16), rk, b1, nn, az)
        h = _resid_ln(h, yt, mod, group_of_block, ln_ffn[l], alpha, 5)

    return h.reshape(b, lt, d)[:, n_ctx:, :]
```

```python
import functools
import math

import numpy as np
import jax
import jax.numpy as jnp
from jax import lax
from jax.experimental import pallas as pl
from jax.experimental.pallas import tpu as pltpu

GRID_W = 64
HEAD_DIM = 64
ROPE_THETA = 10000.0
A_HEADS = 4
A_KV_HEADS = 2
DIFF_HEADS = 4
DIFF_DIM = 32
WIN_HEADS = 4
WIN_KV_HEADS = 2
WINDOW = 128
RET_HEADS = 4
RET_DK = 64
RET_DV = 64
RET_CHUNK = 128
N_BRANCH = 4
BRANCH_W = 256
PIECES = (
    A_HEADS * HEAD_DIM, A_KV_HEADS * HEAD_DIM, A_KV_HEADS * HEAD_DIM,
    2 * DIFF_HEADS * DIFF_DIM, 2 * DIFF_HEADS * DIFF_DIM, DIFF_HEADS * 2 * DIFF_DIM,
    WIN_HEADS * HEAD_DIM, WIN_KV_HEADS * HEAD_DIM, WIN_KV_HEADS * HEAD_DIM,
    RET_HEADS * RET_DK, RET_HEADS * RET_DK, RET_HEADS * RET_DV, RET_HEADS * RET_DV,
)
MIX_COLS = sum(PIECES)
PIECE_OFF = tuple(int(v) for v in np.cumsum((0,) + PIECES))
PEER_HEADS = 8
PEER_NK = 128
PEER_TOPK = 16
PEER_DQ = 128
PEER_CAND_ROWS = -(-sum((PEER_TOPK + 1) // (p + 1) for p in range(PEER_TOPK + 1)) // 8) * 8
LN_EPS = 1e-5
RMS_EPS = 1e-6
LOG2E = 1.4426950408889634

F32 = jnp.float32
BF16 = jnp.bfloat16

TOKEN_BLOCK = 256
ATT_TQ = 256
ATT_TK = 1024
ATT_VPAD = 80
PEER_TOKENS = 512
PEER_EXPERTS = 1024
VMEM_LIMIT = 56 * 1024 * 1024
BF16_ROWS = 16


def _cparams(sem):
    return pltpu.CompilerParams(dimension_semantics=sem, vmem_limit_bytes=VMEM_LIMIT)


def _split_bf16(a):
    hi = a.astype(BF16)
    lo = (a - hi.astype(F32)).astype(BF16)
    return hi, lo


def _dot3(a, b, dims=(((1,), (0,)), ((), ()))):
    ah, al = _split_bf16(a)
    bh, bl = _split_bf16(b)
    d = functools.partial(lax.dot_general, dimension_numbers=dims, preferred_element_type=F32)
    return d(ah, bh) + (d(al, bh) + d(ah, bl))


def _mod_kernel(c_ref, w_ref, b_ref, o_ref):
    o_ref[...] = _dot3(c_ref[...], w_ref[...]) + b_ref[...]


def _modulation(cond8, w, b):
    d, n = w.shape
    tn = 1536
    return pl.pallas_call(
        _mod_kernel,
        out_shape=jax.ShapeDtypeStruct((8, n), F32),
        grid=(n // tn,),
        in_specs=[pl.BlockSpec((8, d), lambda j: (0, 0)),
                  pl.BlockSpec((d, tn), lambda j: (0, j)),
                  pl.BlockSpec((1, tn), lambda j: (0, j))],
        out_specs=pl.BlockSpec((8, tn), lambda j: (0, j)),
        compiler_params=_cparams(("arbitrary",)),
    )(cond8, w, b.reshape(1, n))


def _inproj_kernel(x_ref, mod_ref, w_ref, o_ref, *, sigmoid, chunk):
    m = mod_ref[...]
    xm = (x_ref[...] * (1.0 + m[1:2, :]) + m[0:1, :]).astype(BF16)
    n = w_ref.shape[1]
    for j in range(n // chunk):
        acc = jnp.dot(xm, w_ref[:, j * chunk:(j + 1) * chunk], preferred_element_type=F32)
        if sigmoid:
            acc = jax.nn.sigmoid(acc)
        o_ref[:, j * chunk:(j + 1) * chunk] = acc.astype(o_ref.dtype)


def _inproj(h, mod, group_of_block, w, *, sigmoid, out_dtype):
    t, d = h.shape
    n = w.shape[1]
    tb = TOKEN_BLOCK
    return pl.pallas_call(
        functools.partial(_inproj_kernel, sigmoid=sigmoid, chunk=512),
        out_shape=jax.ShapeDtypeStruct((t, n), out_dtype),
        grid=(t // tb,),
        in_specs=[pl.BlockSpec((tb, d), lambda i: (i, 0)),
                  pl.BlockSpec((None, 6, d), lambda i: (group_of_block(i), 0, 0)),
                  pl.BlockSpec((d, n), lambda i: (0, 0))],
        out_specs=pl.BlockSpec((tb, n), lambda i: (i, 0)),
        compiler_params=_cparams(("parallel",)),
    )(h, mod, w)


TOK_PIECES = ((1, True), (4, True), (6, True), (7, True), (8, False), (9, True), (11, False), (12, False))
FEAT_PIECES = ((0, True), (3, True), (2, False), (5, False), (10, True))


def _piece_cols(pieces):
    flips = {0: 16, 1: 16, 3: 8, 4: 8, 6: 16, 7: 16, 9: 32, 10: 32}
    cols, offs = [], {}
    n = 0
    for p, rotary in pieces:
        base = np.arange(PIECE_OFF[p], PIECE_OFF[p + 1])
        offs[p] = n
        cols.append(base)
        n += len(base)
        if rotary:
            cols.append(PIECE_OFF[p] + ((base - PIECE_OFF[p]) ^ flips[p]))
            n += len(base)
    return np.concatenate(cols), offs


TOK_COLS, TOK_OFF = _piece_cols(TOK_PIECES)
FEAT_COLS, FEAT_OFF = _piece_cols(FEAT_PIECES)
TTOK_OFF = {1: 0, 4: 128, 6: 384, 7: 640, 9: 768}
TTOK_W = 1024
TFEAT_OFF = {0: 0, 3: 256, 10: 512}
TFEAT_W = 768


def _mixer_in_kernel(x_ref, mod_ref, wt_ref, wf_ref, ttok_ref, tfeat_ref, gtok_ref, gfeat_ref, avg_ref,
                     kall_ref, qt_ref, vt_ref, wq_ref, wk_ref, wv_ref, rq_ref, rkt_ref, rv_ref, rg_ref):
    m = mod_ref[...]
    u = x_ref[...] * (1.0 + m[1:2, :]) + m[0:1, :]
    xm = u.astype(BF16)
    xmt = u.T.astype(BF16)
    tb = xm.shape[0]

    def tok(p, swapped=False):
        a = TOK_OFF[p] + (PIECES[p] if swapped else 0)
        return jnp.dot(xm, wt_ref[:, a:a + PIECES[p]], preferred_element_type=F32)

    def feat(p, swapped=False):
        a = FEAT_OFF[p] + (PIECES[p] if swapped else 0)
        return jnp.dot(wf_ref[a:a + PIECES[p], :], xmt, preferred_element_type=F32)

    def rope_tok(p, x, xs):
        a = TTOK_OFF[p]
        return x * ttok_ref[:, a:a + PIECES[p]] + xs * ttok_ref[:, TTOK_W + a:TTOK_W + a + PIECES[p]]

    def rope_feat(p, x, xs):
        a = TFEAT_OFF[p]
        return x * tfeat_ref[a:a + PIECES[p], :] + xs * tfeat_ref[TFEAT_W + a:TFEAT_W + a + PIECES[p], :]

    x, xs = tok(1), tok(1, True)
    sq_hi, sq_lo = _split_bf16(x * x)
    avg = avg_ref[...]
    ms = jnp.dot(sq_hi, avg, preferred_element_type=F32) + jnp.dot(sq_lo, avg, preferred_element_type=F32)
    r = lax.rsqrt(ms + RMS_EPS)
    ka = r * rope_tok(1, x * gtok_ref[0:1, :], xs * gtok_ref[1:2, :])
    kd = rope_tok(4, tok(4), tok(4, True))
    kall_ref[:, :PIECES[1]] = ka.astype(BF16)
    kall_ref[:, PIECES[1]:] = kd.astype(BF16)

    def split_heads(ref, val):
        for hd in range(ref.shape[0]):
            ref[hd] = val[:, hd * HEAD_DIM:(hd + 1) * HEAD_DIM].astype(ref.dtype)

    split_heads(wq_ref, rope_tok(6, tok(6), tok(6, True)))
    split_heads(wk_ref, rope_tok(7, tok(7), tok(7, True)))
    split_heads(wv_ref, tok(8))
    split_heads(rq_ref, rope_tok(9, tok(9), tok(9, True)))
    split_heads(rv_ref, tok(11))
    rg_ref[...] = tok(12)
    rkt = rope_feat(10, feat(10), feat(10, True))
    for hd in range(RET_HEADS):
        rkt_ref[hd] = rkt[hd * RET_DK:(hd + 1) * RET_DK]

    zero64 = jnp.zeros((HEAD_DIM, tb), F32)
    zero32 = jnp.zeros((DIFF_DIM, tb), F32)

    def place(q, upper):
        return jnp.concatenate([zero64, q] if upper else [q, zero64], axis=0).astype(BF16)

    xq, xqs = feat(0), feat(0, True)
    for hd in range(A_HEADS):
        rows = slice(hd * HEAD_DIM, (hd + 1) * HEAD_DIM)
        xh = xq[rows]
        rh = lax.rsqrt(jnp.mean(xh * xh, axis=0, keepdims=True) + RMS_EPS)
        swapped_rows = slice(A_HEADS * HEAD_DIM + hd * HEAD_DIM, A_HEADS * HEAD_DIM + (hd + 1) * HEAD_DIM)
        qh = rh * (xh * gfeat_ref[rows, :] * tfeat_ref[rows, :]
                   + xqs[rows] * gfeat_ref[swapped_rows, :]
                   * tfeat_ref[TFEAT_W + hd * HEAD_DIM:TFEAT_W + (hd + 1) * HEAD_DIM, :])
        qt_ref[hd // 2, hd % 2] = place(qh, hd // 2 == 1)
    qd = rope_feat(3, feat(3), feat(3, True))
    for hd in range(DIFF_HEADS):
        qh = qd[hd * HEAD_DIM:(hd + 1) * HEAD_DIM]
        q1 = jnp.concatenate([qh[:DIFF_DIM], zero32], axis=0)
        q2 = jnp.concatenate([zero32, qh[DIFF_DIM:]], axis=0)
        qt_ref[A_KV_HEADS + hd, 0] = place(q1, hd % 2 == 1)
        qt_ref[A_KV_HEADS + hd, 1] = place(q2, hd % 2 == 1)

    pad_rows = vt_ref.shape[1] - HEAD_DIM
    tail = (lax.broadcasted_iota(jnp.int32, (pad_rows, tb), 0) == 0).astype(F32)
    va, vd = feat(2), feat(5)
    for g in range(A_KV_HEADS + DIFF_HEADS):
        v = va[g * HEAD_DIM:(g + 1) * HEAD_DIM] if g < A_KV_HEADS else \
            vd[(g - A_KV_HEADS) * HEAD_DIM:(g - A_KV_HEADS + 1) * HEAD_DIM]
        vt_ref[g] = jnp.concatenate([v, tail], axis=0).astype(BF16)


def _mixer_in(h, mod, group_of_block, wt, wf, ttok, tfeat, gtok, gfeat, avg, b, lt):
    t, d = h.shape
    tb = TOKEN_BLOCK
    bpb = lt // tb
    n_groups = A_KV_HEADS + DIFF_HEADS
    kw = PIECES[1] + PIECES[4]
    row = lambda w: pl.BlockSpec((tb, w), lambda i: (i, 0))
    const = lambda a: pl.BlockSpec(a.shape, lambda i: (0,) * a.ndim)
    heads_shape = lambda n, dt: jax.ShapeDtypeStruct((b, n, lt, HEAD_DIM), dt)
    heads_spec = lambda n: pl.BlockSpec((None, n, tb, HEAD_DIM), lambda i: (i // bpb, 0, i % bpb, 0))
    return pl.pallas_call(
        _mixer_in_kernel,
        out_shape=(jax.ShapeDtypeStruct((b, lt, kw), BF16),
                   jax.ShapeDtypeStruct((b, n_groups, bpb, 2, 2 * HEAD_DIM, tb), BF16),
                   jax.ShapeDtypeStruct((b, n_groups, bpb, ATT_VPAD, tb), BF16),
                   heads_shape(WIN_HEADS, BF16), heads_shape(WIN_KV_HEADS, BF16), heads_shape(WIN_KV_HEADS, BF16),
                   heads_shape(RET_HEADS, F32), jax.ShapeDtypeStruct((b, RET_HEADS, RET_DK, lt), F32),
                   heads_shape(RET_HEADS, F32), jax.ShapeDtypeStruct((t, PIECES[12]), F32)),
        grid=(t // tb,),
        in_specs=[row(d),
                  pl.BlockSpec((None, 6, d), lambda i: (group_of_block(i), 0, 0)),
                  const(wt), const(wf),
                  pl.BlockSpec((tb, 2 * TTOK_W), lambda i: (i % bpb, 0)),
                  pl.BlockSpec((2 * TFEAT_W, tb), lambda i: (0, i % bpb)),
                  const(gtok), const(gfeat), const(avg)],
        out_specs=(pl.BlockSpec((None, tb, kw), lambda i: (i // bpb, i % bpb, 0)),
                   pl.BlockSpec((None, n_groups, None, 2, 2 * HEAD_DIM, tb),
                                lambda i: (i // bpb, 0, i % bpb, 0, 0, 0)),
                   pl.BlockSpec((None, n_groups, None, ATT_VPAD, tb), lambda i: (i // bpb, 0, i % bpb, 0, 0)),
                   heads_spec(WIN_HEADS), heads_spec(WIN_KV_HEADS), heads_spec(WIN_KV_HEADS),
                   heads_spec(RET_HEADS),
                   pl.BlockSpec((None, RET_HEADS, RET_DK, tb), lambda i: (i // bpb, 0, 0, i % bpb)),
                   heads_spec(RET_HEADS), row(PIECES[12])),
        compiler_params=_cparams(("parallel",)),
    )(h, mod, wt, wf, ttok, tfeat, gtok, gfeat, avg)


def _flash_kernel(qa_ref, qb_ref, k_ref, vt_ref, o_ref, s_ref, *, n_ctx, tk):
    i = pl.program_id(2)
    tq = qa_ref.shape[2]
    qt = jnp.concatenate([qa_ref[0], qb_ref[0], qa_ref[1], qb_ref[1]], axis=1)
    w = qt.shape[1]
    tile = vt_ref.shape[2]
    nlc = (k_ref.shape[0] - n_ctx) // tk

    def scores(row0, rows):
        return jnp.dot(k_ref[pl.ds(row0, rows), :], qt, preferred_element_type=F32)

    def absorb(s, tile0, carry):
        m, acc = carry
        m_new = jnp.maximum(m, jnp.max(s, axis=0, keepdims=True))
        p = jnp.exp2(s - m_new).astype(BF16)
        acc = acc * jnp.exp2(m - m_new)
        for j in range(s.shape[0] // tile):
            acc = acc + jnp.dot(vt_ref[tile0 + j], p[j * tile:(j + 1) * tile], preferred_element_type=F32)
        return m_new, acc

    carry = (jnp.full((1, w), -jnp.inf, F32), jnp.zeros((vt_ref.shape[1], w), F32))
    carry = absorb(scores(0, n_ctx), 0, carry)
    lat_row = lambda c: pl.multiple_of(n_ctx + c * tk, tile)
    lat_tile = lambda c: (n_ctx + c * tk) // tile
    s_ref[0] = scores(lat_row(0), tk)

    def pair(j, carry):
        c0 = 2 * j
        s_ref[1] = scores(lat_row(c0 + 1), tk)
        carry = absorb(s_ref[0], lat_tile(c0), carry)
        s_ref[0] = scores(lat_row(jnp.minimum(c0 + 2, nlc - 1)), tk)
        return absorb(s_ref[1], lat_tile(c0 + 1), carry)

    n_pairs = jnp.where(i == 0, 0, nlc // 2)
    _, acc = lax.fori_loop(0, n_pairs, pair, carry)
    o = acc[:HEAD_DIM] * (1.0 / acc[HEAD_DIM:HEAD_DIM + 1])
    o_ref[0] = jnp.concatenate([o[:, :tq], o[:, 2 * tq:3 * tq]], axis=1)
    o_ref[1] = jnp.concatenate([o[:, tq:2 * tq], o[:, 3 * tq:]], axis=1)


def _flash(qt, k_all, vt, n_ctx):
    b, g, tiles, _, dk, tq = qt.shape
    lt = k_all.shape[1]
    s = lt - n_ctx
    tk = next(c for c in (ATT_TK, 512, 256) if s % (2 * c) == 0)
    assert n_ctx == tq and tk % tq == 0 and (tiles - 1) % 2 == 0
    steps = 1 + (tiles - 1) // 2
    key_block = lambda gi: jnp.where(gi < A_KV_HEADS, 0, 1 + (gi - A_KV_HEADS) // 2)
    qspec = lambda second: pl.BlockSpec(
        (None, None, None, 2, dk, tq),
        lambda bi, gi, i: (bi, gi, jnp.where(i == 0, 0, 2 * i - 1 + second), 0, 0, 0))
    return pl.pallas_call(
        functools.partial(_flash_kernel, n_ctx=n_ctx, tk=tk),
        out_shape=jax.ShapeDtypeStruct((b, g, steps, 2, HEAD_DIM, 2 * tq), F32),
        grid=(b, g, steps),
        in_specs=[qspec(0), qspec(1),
                  pl.BlockSpec((None, lt, dk), lambda bi, gi, i: (bi, 0, key_block(gi))),
                  pl.BlockSpec((None, None) + vt.shape[2:], lambda bi, gi, i: (bi, gi, 0, 0, 0))],
        out_specs=pl.BlockSpec((None, None, None, 2, HEAD_DIM, 2 * tq), lambda bi, gi, i: (bi, gi, i, 0, 0, 0)),
        scratch_shapes=[pltpu.VMEM((2, tk, 4 * tq), F32)],
        compiler_params=_cparams(("parallel", "parallel", "arbitrary")),
    )(qt, qt, k_all, vt)


def _window_kernel(q_ref, kp_ref, kc_ref, kn_ref, vp_ref, vc_ref, vn_ref, kx_ref, vx_ref, sink_ref,
                   o_ref, *, n_ctx_blocks, n_blocks):
    qb = pl.program_id(2)
    q = q_ref[...].reshape(-1, HEAD_DIM)
    rows = q.shape[0]
    nt = (((1,), (1,)), ((), ()))
    sdot = functools.partial(lax.dot_general, dimension_numbers=nt, preferred_element_type=F32)
    qi = lax.broadcasted_iota(jnp.int32, (rows, WINDOW), 0) % WINDOW
    kj = lax.broadcasted_iota(jnp.int32, (rows, WINDOW), 1)
    neg = -jnp.inf
    off_p = jnp.where(qb >= n_ctx_blocks + 1, 0, 2 * WINDOW)
    off_c = jnp.where(qb >= n_ctx_blocks, 0, 2 * WINDOW)
    off_n = jnp.where(jnp.logical_and(qb >= n_ctx_blocks, qb <= n_blocks - 2), 0, 2 * WINDOW)
    s_p = jnp.where(kj >= qi + off_p, sdot(q, kp_ref[...]), neg)
    s_c = jnp.where(kj >= off_c, sdot(q, kc_ref[...]), neg)
    s_n = jnp.where(kj <= qi - off_n, sdot(q, kn_ref[...]), neg)
    kx = kx_ref[...]
    vx = vx_ref[...]
    s_x = sdot(q, kx)
    sink = sink_ref[...]
    m = jnp.maximum(jnp.maximum(jnp.max(s_p, axis=1, keepdims=True), jnp.max(s_c, axis=1, keepdims=True)),
                    jnp.maximum(jnp.max(s_n, axis=1, keepdims=True), jnp.max(s_x, axis=1, keepdims=True)))
    m = jnp.maximum(m, sink)
    e_p, e_c, e_n, e_x = (jnp.exp2(s - m) for s in (s_p, s_c, s_n, s_x))
    den = (jnp.sum(e_p, axis=1, keepdims=True) + jnp.sum(e_c, axis=1, keepdims=True)
           + jnp.sum(e_n, axis=1, keepdims=True) + jnp.sum(e_x, axis=1, keepdims=True)
           + jnp.exp2(sink - m))
    pv = functools.partial(jnp.dot, preferred_element_type=F32)
    o = (pv(e_p.astype(BF16), vp_ref[...]) + pv(e_c.astype(BF16), vc_ref[...])
         + pv(e_n.astype(BF16), vn_ref[...]) + pv(e_x.astype(BF16), vx))
    o_ref[...] = o / den


def _window(q, k, v, sink, n_ctx):
    b, hq, lt, dh = q.shape
    g = k.shape[1]
    blk = WINDOW
    nb = lt // blk
    rows = (hq // g) * blk
    n_ctx_blocks = n_ctx // blk
    lo, hi = n_ctx_blocks, nb - 1

    def nbr(delta):
        return lambda bi, gi, i: (bi, gi, jnp.clip(i + delta, lo, hi), 0)

    kv_spec = lambda d: pl.BlockSpec((None, None, blk, dh), nbr(d))
    ctx_spec = pl.BlockSpec((None, None, n_ctx, dh), lambda bi, gi, i: (bi, gi, 0, 0))
    return pl.pallas_call(
        functools.partial(_window_kernel, n_ctx_blocks=n_ctx_blocks, n_blocks=nb),
        out_shape=jax.ShapeDtypeStruct((b, g, nb, rows, dh), F32),
        grid=(b, g, nb),
        in_specs=[pl.BlockSpec((None, hq // g, blk, dh), lambda bi, gi, i: (bi, gi, i, 0)),
                  kv_spec(-1), kv_spec(0), kv_spec(1), kv_spec(-1), kv_spec(0), kv_spec(1),
                  ctx_spec, ctx_spec,
                  pl.BlockSpec((None, rows, 1), lambda bi, gi, i: (gi, 0, 0))],
        out_specs=pl.BlockSpec((None, None, None, rows, dh), lambda bi, gi, i: (bi, gi, i, 0, 0)),
        compiler_params=_cparams(("parallel", "parallel", "arbitrary")),
    )(q, k, k, k, v, v, v, k, v, sink)


def _retention_kernel(q_ref, kt_ref, v_ref, dmat_ref, xi_ref, zeta_ref, gch_ref, o_ref, st_ref):
    t = pl.program_id(2)

    @pl.when(t == 0)
    def _():
        st_ref[...] = jnp.zeros_like(st_ref)

    for hd in range(RET_HEADS):
        q = q_ref[hd]
        kt = kt_ref[hd]
        v = v_ref[hd]
        st = st_ref[hd]
        inner = _dot3(q, kt) * dmat_ref[hd]
        o_ref[hd] = _dot3(inner, v) + _dot3(q, st) * xi_ref[hd]
        st_ref[hd] = st * gch_ref[hd] + _dot3(kt * zeta_ref[hd], v)


def _retention_call(q, kt, v, dmat, xi, zeta, gch, n_ctx_chunks):
    b, hh, lt, dk = q.shape
    dv = v.shape[-1]
    c = RET_CHUNK
    nch = lt // c

    def blk(d, ti):
        back = jnp.where(ti < n_ctx_chunks, n_ctx_chunks - 1 - ti, nch - 1 - (ti - n_ctx_chunks))
        return jnp.where(d == 0, ti, back)

    tab = lambda shape: pl.BlockSpec((None,) + shape, lambda d, bi, ti: (d,) + (0,) * len(shape))
    return pl.pallas_call(
        _retention_kernel,
        out_shape=jax.ShapeDtypeStruct((2, b, hh, lt, dv), F32),
        grid=(2, b, nch),
        in_specs=[pl.BlockSpec((None, hh, c, dk), lambda d, bi, ti: (bi, 0, blk(d, ti), 0)),
                  pl.BlockSpec((None, hh, dk, c), lambda d, bi, ti: (bi, 0, 0, blk(d, ti))),
                  pl.BlockSpec((None, hh, c, dv), lambda d, bi, ti: (bi, 0, blk(d, ti), 0)),
                  tab((hh, c, c)), tab((hh, c, 1)), tab((hh, 1, c)), tab((hh, 1, 1))],
        out_specs=pl.BlockSpec((None, None, hh, c, dv), lambda d, bi, ti: (d, bi, 0, blk(d, ti), 0)),
        scratch_shapes=[pltpu.VMEM((hh, dk, dv), F32)],
        compiler_params=_cparams(("parallel", "parallel", "arbitrary")),
    )(q, kt, v, dmat, xi, zeta, gch)


def _layer_norm_rows(z, ln):
    mu = jnp.mean(z, axis=-1, keepdims=True)
    zc = z - mu
    var = jnp.mean(zc * zc, axis=-1, keepdims=True)
    return zc * lax.rsqrt(var + LN_EPS) * ln[0:1, :] + ln[1:2, :]


def _merge_kernel(fa_ref, fd0_ref, fd1_ref, win_ref, ret_ref, rg_ref, g_ref, h_ref, mod_ref, lam_ref,
                  subln_ref, rnorm_ref, wb_ref, wo_ref, ln_ref, o_ref, *, alpha):
    d = h_ref.shape[1]
    tb = h_ref.shape[0]
    hd = HEAD_DIM
    proj = functools.partial(jnp.dot, preferred_element_type=F32)
    gate = lambda i: g_ref[:, i * d:(i + 1) * d].astype(F32)

    oat = jnp.concatenate([fa_ref[g][:, st * tb:(st + 1) * tb] for g in range(A_KV_HEADS) for st in range(2)],
                          axis=0)
    m = gate(0) * proj(oat.T.astype(BF16), wb_ref[0])

    lam = lam_ref[...]
    heads = []
    for h4 in range(DIFF_HEADS):
        f = (fd0_ref if h4 < 2 else fd1_ref)[h4 % 2]
        o = f[:, :tb] - lam * f[:, tb:]
        heads.append(o * lax.rsqrt(jnp.mean(o * o, axis=0, keepdims=True) + RMS_EPS))
    obt = jnp.concatenate(heads, axis=0) * subln_ref[...]
    m = m + gate(1) * proj(obt.T.astype(BF16), wb_ref[1])

    half = tb // 2
    acc = None
    for h4 in range(WIN_HEADS):
        g, st = h4 // 2, h4 % 2
        o = jnp.concatenate([win_ref[g, 0][st * half:(st + 1) * half], win_ref[g, 1][st * half:(st + 1) * half]],
                            axis=0)
        t = proj(o.astype(BF16), wb_ref[2, h4 * hd:(h4 + 1) * hd, :])
        acc = t if acc is None else acc + t
    m = m + gate(2) * acc

    acc = None
    for h4 in range(RET_HEADS):
        cols = slice(h4 * RET_DV, (h4 + 1) * RET_DV)
        o = ret_ref[0, h4] + ret_ref[1, h4]
        mu = jnp.mean(o, axis=-1, keepdims=True)
        oc = o - mu
        var = jnp.mean(oc * oc, axis=-1, keepdims=True)
        on = oc * lax.rsqrt(var + LN_EPS) * rnorm_ref[0:1, cols] + rnorm_ref[1:2, cols]
        gt = rg_ref[:, cols]
        t = proj((on * (gt / (1.0 + jnp.exp(-gt)))).astype(BF16), wb_ref[3, cols, :])
        acc = t if acc is None else acc + t
    m = m + gate(3) * acc

    y = proj(m.astype(BF16), wo_ref[...])
    z = alpha * h_ref[...] + mod_ref[2:3, :] * y
    o_ref[...] = _layer_norm_rows(z, ln_ref[...])


def _merge(flash_out, win_out, ret_out, rg, gates, h, mod, group_of_block, lam, subln, rnorm, wb, wo, ln,
           alpha, b, lt):
    t, d = h.shape
    tb = TOKEN_BLOCK
    bpb = lt // tb
    row = lambda n: pl.BlockSpec((tb, n), lambda i: (i, 0))
    const = lambda a: pl.BlockSpec(a.shape, lambda i: (0,) * a.ndim)
    fspec = lambda gb: pl.BlockSpec(
        (None, 2, None, None) + flash_out.shape[4:],
        lambda i: (i // bpb, gb, (i % bpb + 1) // 2, (i % bpb + 1) % 2, 0, 0))
    return pl.pallas_call(
        functools.partial(_merge_kernel, alpha=alpha),
        out_shape=jax.ShapeDtypeStruct((t, d), F32),
        grid=(t // tb,),
        in_specs=[fspec(0), fspec(1), fspec(2),
                  pl.BlockSpec((None, WIN_KV_HEADS, 2) + win_out.shape[3:], lambda i: (i // bpb, 0, i % bpb, 0, 0)),
                  pl.BlockSpec((2, None, RET_HEADS, tb, RET_DV), lambda i: (0, i // bpb, 0, i % bpb, 0)),
                  row(rg.shape[1]), row(N_BRANCH * d), row(d),
                  pl.BlockSpec((None, 6, d), lambda i: (group_of_block(i), 0, 0)),
                  const(lam), const(subln), const(rnorm), const(wb), const(wo), const(ln)],
        out_specs=row(d),
        compiler_params=_cparams(("parallel",)),
    )(flash_out, flash_out, flash_out, win_out, ret_out, rg, gates, h, mod, lam, subln, rnorm, wb, wo, ln)


def _top_rows(s, n, with_rank=False):
    out = []
    cur = s
    rank = jnp.full(s.shape, float(n), F32) if with_rank else None
    for r in range(n):
        mx = jnp.max(cur, axis=0, keepdims=True)
        out.append(mx)
        if with_rank:
            rank = jnp.where(cur == mx, float(r), rank)
        if r + 1 < n:
            cur = jnp.where(cur == mx, -jnp.inf, cur)
    return (out, rank) if with_rank else out


def _peer_route_kernel(h_ref, mod_ref, wh_ref, wl_ref, sk_ref, xt_ref, rk_ref, b1_ref, nn_ref, az_ref,
                       cand_ref):
    m = mod_ref[...]
    u = h_ref[...] * (1.0 + m[4:5, :]) + m[3:4, :]
    xt_ref[...] = u.T.astype(BF16)
    uh, ul = _split_bf16(u)
    d = functools.partial(jnp.dot, preferred_element_type=F32)
    nk = PEER_NK
    nt = (((1,), (1,)), ((), ()))
    k1 = PEER_TOPK + 1
    for hd in range(PEER_HEADS):
        c0 = 2 * hd * PEER_DQ
        wh = wh_ref[:, c0:c0 + 2 * PEER_DQ]
        wl = wl_ref[:, c0:c0 + 2 * PEER_DQ]
        q = d(uh, wh) + (d(ul, wh) + d(uh, wl))
        st = [_dot3(sk_ref[hd, p], q[:, p * PEER_DQ:(p + 1) * PEER_DQ], nt) for p in range(2)]
        top0 = _top_rows(st[0], k1)
        top1, rank1 = _top_rows(st[1], k1, with_rank=True)
        r = 0
        for p0 in range(k1):
            for p1 in range(k1 // (p0 + 1)):
                cand_ref[r:r + 1, :] = top0[p0] + top1[p1]
                r += 1
        cand_ref[r:, :] = jnp.full((cand_ref.shape[0] - r, cand_ref.shape[1]), -jnp.inf, F32)
        cand = cand_ref[...]
        ctop = _top_rows(cand, k1)
        tau = 0.5 * (ctop[PEER_TOPK - 1] + ctop[PEER_TOPK])
        mx = top0[0] + top1[0]
        z = jnp.sum(jnp.where(cand >= tau, jnp.exp(cand - mx), 0.0), axis=0, keepdims=True)
        th = tau - st[0]
        nn = jnp.zeros_like(th)
        for q in range(PEER_TOPK):
            nn = nn + jnp.where(top1[q] >= th, 1.0, 0.0)
        rk_ref[hd] = rank1.astype(BF16)
        b1_ref[hd] = jnp.exp(st[1] - top1[0]).astype(BF16)
        nn_ref[hd] = nn
        az_ref[hd] = jnp.exp(st[0] - top0[0]) / z


def _peer_route(h, mod, group_of_block, wq_hi, wq_lo, subkeys):
    t, d = h.shape
    tb = TOKEN_BLOCK
    hh, nk = PEER_HEADS, PEER_NK
    st_shape = lambda dt: jax.ShapeDtypeStruct((hh, nk, t), dt)
    st_spec = pl.BlockSpec((hh, nk, tb), lambda i: (0, 0, i))
    nq = wq_hi.shape[1]
    return pl.pallas_call(
        _peer_route_kernel,
        out_shape=(jax.ShapeDtypeStruct((d, t), BF16), st_shape(BF16), st_shape(BF16), st_shape(F32),
                   st_shape(F32)),
        grid=(t // tb,),
        in_specs=[pl.BlockSpec((tb, d), lambda i: (i, 0)),
                  pl.BlockSpec((None, 6, d), lambda i: (group_of_block(i), 0, 0)),
                  pl.BlockSpec((d, nq), lambda i: (0, 0)),
                  pl.BlockSpec((d, nq), lambda i: (0, 0)),
                  pl.BlockSpec((hh, 2, nk, PEER_DQ), lambda i: (0, 0, 0, 0))],
        out_specs=(pl.BlockSpec((d, tb), lambda i: (0, i)), st_spec, st_spec, st_spec, st_spec),
        scratch_shapes=[pltpu.VMEM((PEER_CAND_ROWS, tb), F32)],
        compiler_params=_cparams(("parallel",)),
    )(h, mod, wq_hi, wq_lo, subkeys)


GELU_K1 = -2.0 * math.sqrt(2.0 / math.pi) * LOG2E
GELU_K2 = GELU_K1 * 0.044715


def _gelu_tanh(x):
    return x / (1.0 + jnp.exp2(x * (GELU_K1 + GELU_K2 * (x * x))))


def _peer_dense_kernel(xt_ref, u_ref, vt_ref, rk_ref, b1_ref, nn_ref, az_ref, yt_ref, g_ref):
    c = pl.program_id(1)
    nk = PEER_NK
    tp = xt_ref.shape[1]
    rows_per_step = u_ref.shape[0] // nk
    n_chunks = pl.num_programs(1) - 1

    @pl.when(c == 0)
    def _():
        yt_ref[...] = jnp.zeros_like(yt_ref)
        g_ref[...] = jnp.zeros_like(g_ref)

    xt = xt_ref[...]
    row0 = jnp.minimum(c, n_chunks - 1) * rows_per_step
    pre = lambda ii: jnp.dot(u_ref[ii * nk:(ii + 1) * nk, :], xt, preferred_element_type=F32)
    act_next = pre(0)
    for ii in range(rows_per_step):
        if ii % 2 == 0:
            pair = slice(ii * nk, (ii + 2) * nk)
            yt_ref[...] += jnp.dot(vt_ref[:, pair], g_ref[pair, :], preferred_element_type=F32)
        act = act_next
        if ii + 1 < rows_per_step:
            act_next = pre(ii + 1)
        i = row0 + ii
        w = None
        tile = (nk // BF16_ROWS, BF16_ROWS, tp)
        row = lambda ref, hd: jnp.broadcast_to(ref[hd, pl.ds(i, 1), :], (BF16_ROWS, tp)).astype(BF16)[None]
        for hd in range(PEER_HEADS):
            t = jnp.where(rk_ref[hd].reshape(tile) < row(nn_ref, hd), b1_ref[hd].reshape(tile),
                          jnp.zeros((), BF16)) * row(az_ref, hd)
            w = t if w is None else w + t
        g_ref[ii * nk:(ii + 1) * nk, :] = w.reshape(nk, tp) * _gelu_tanh(act.astype(BF16))


def _peer_dense(xt, u, vt, rk, b1, nn, az):
    d, t = xt.shape
    n = u.shape[0]
    tp = PEER_TOKENS
    ec = PEER_EXPERTS
    nc = n // ec
    hh, nk = PEER_HEADS, PEER_NK
    st_spec = pl.BlockSpec((hh, nk, tp), lambda i, c: (0, 0, i))
    return pl.pallas_call(
        _peer_dense_kernel,
        out_shape=jax.ShapeDtypeStruct((d, t), F32),
        grid=(t // tp, nc + 1),
        in_specs=[pl.BlockSpec((d, tp), lambda i, c: (0, i)),
                  pl.BlockSpec((ec, d), lambda i, c: (jnp.minimum(c, nc - 1), 0)),
                  pl.BlockSpec((d, ec), lambda i, c: (0, jnp.maximum(c - 1, 0))),
                  st_spec, st_spec, st_spec, st_spec],
        out_specs=pl.BlockSpec((d, tp), lambda i, c: (0, i)),
        scratch_shapes=[pltpu.VMEM((ec, tp), BF16)],
        compiler_params=_cparams(("parallel", "arbitrary")),
    )(xt, u, vt, rk, b1, nn, az)


def _resid_ln_kernel(h_ref, yt_ref, mod_ref, ln_ref, o_ref, *, alpha, gate_row):
    z = alpha * h_ref[...] + mod_ref[gate_row:gate_row + 1, :] * yt_ref[...].T
    o_ref[...] = _layer_norm_rows(z, ln_ref[...])


def _resid_ln(h, yt, mod, group_of_block, ln, alpha, gate_row):
    t, d = h.shape
    tb = TOKEN_BLOCK
    row = pl.BlockSpec((tb, d), lambda i: (i, 0))
    return pl.pallas_call(
        functools.partial(_resid_ln_kernel, alpha=alpha, gate_row=gate_row),
        out_shape=jax.ShapeDtypeStruct((t, d), F32),
        grid=(t // tb,),
        in_specs=[row, pl.BlockSpec((d, tb), lambda i: (0, i)),
                  pl.BlockSpec((None, 6, d), lambda i: (group_of_block(i), 0, 0)),
                  pl.BlockSpec((2, d), lambda i: (0, 0))],
        out_specs=row,
        compiler_params=_cparams(("parallel",)),
    )(h, yt, mod, ln)


def _axial_tables(s, n_ctx, d):
    rows = s // GRID_W
    row = jnp.broadcast_to(jnp.arange(rows, dtype=F32)[:, None], (rows, GRID_W)).reshape(-1)
    col = jnp.broadcast_to(jnp.arange(GRID_W, dtype=F32)[None, :], (rows, GRID_W)).reshape(-1)
    quarter = d // 4
    inv = ROPE_THETA ** (-jnp.arange(quarter, dtype=F32) / quarter)
    ar, ac = row[:, None] * inv, col[:, None] * inv
    cos = jnp.concatenate([jnp.cos(ar), jnp.cos(ar), jnp.cos(ac), jnp.cos(ac)], axis=-1)
    sin = jnp.concatenate([-jnp.sin(ar), jnp.sin(ar), -jnp.sin(ac), jnp.sin(ac)], axis=-1)
    cos = jnp.concatenate([jnp.ones((n_ctx, d), F32), cos], axis=0)
    sin = jnp.concatenate([jnp.zeros((n_ctx, d), F32), sin], axis=0)
    return cos, sin


def _rope1d_tables(lt, d):
    half = d // 2
    inv = ROPE_THETA ** (-jnp.arange(half, dtype=F32) / half)
    ang = jnp.arange(lt, dtype=F32)[:, None] * inv
    return (jnp.concatenate([jnp.cos(ang), jnp.cos(ang)], axis=-1),
            jnp.concatenate([-jnp.sin(ang), jnp.sin(ang)], axis=-1))


def _swap_perm():
    cols = []
    for piece, flip in ((0, 16), (1, 16), (3, 8), (4, 8), (6, 16), (7, 16), (9, 32), (10, 32)):
        base = np.arange(PIECE_OFF[piece], PIECE_OFF[piece + 1])
        cols.append(PIECE_OFF[piece] + ((base - PIECE_OFF[piece]) ^ flip))
    return np.concatenate(cols)


SWAP_COLS = _swap_perm()
SWAP_PIECES = (0, 1, 3, 4, 6, 7, 9, 10)
SWAP_OFF = dict(zip(SWAP_PIECES, np.cumsum([0] + [PIECES[p] for p in SWAP_PIECES])[:-1] + MIX_COLS))


def _rms_heads(x, xs, gain, gain_s):
    r = lax.rsqrt(jnp.mean(x * x, axis=-1, keepdims=True) + RMS_EPS)
    return x * r * gain, xs * r * gain_s


def kernel(x, c, ctx, c_ctx, w_mod, b_mod, w_in, qk_gain, diff_lambda, diff_subln, win_sink, ret_decay,
           ret_norm, w_branch, w_out, ln_attn, ln_ffn, peer_wq, peer_subkeys, peer_u, peer_v):
    b, s, d = x.shape
    n_ctx = ctx.shape[1]
    depth = w_mod.shape[0]
    lt = n_ctx + s
    t = b * lt
    tb = TOKEN_BLOCK
    assert n_ctx % tb == 0 and s % tb == 0 and t % PEER_TOKENS == 0
    alpha = (2 * depth) ** 0.25
    blocks_per_batch = lt // tb
    ctx_blocks = n_ctx // tb

    def group_of_block(i):
        return jnp.where(i % blocks_per_batch < ctx_blocks, b, i // blocks_per_batch)

    cos64, sin64 = _axial_tables(s, n_ctx, HEAD_DIM)
    cos32, sin32 = _axial_tables(s, n_ctx, DIFF_DIM)
    cos1d, sin1d = _rope1d_tables(lt, RET_DK)
    sc_a = HEAD_DIM ** -0.5 * LOG2E
    sc_d = DIFF_DIM ** -0.5 * LOG2E
    rep = lambda a, n: jnp.tile(a, (1, n))
    tok_part = lambda t64, t32, t1d: [rep(t64, 2), rep(t32, 8), rep(t64, 4) * sc_a, rep(t64, 2), rep(t1d, 4)]
    feat_part = lambda t64, t32, t1d: [rep(t64, 4) * sc_a, rep(t32, 8) * sc_d, rep(t1d, 4) * RET_DK ** -0.5]
    ttok = jnp.concatenate(tok_part(cos64, cos32, cos1d) + tok_part(sin64, sin32, sin1d), axis=1)
    tfeat = jnp.concatenate(feat_part(cos64, cos32, cos1d) + feat_part(sin64, sin32, sin1d), axis=1).T
    swap64 = np.arange(HEAD_DIM) ^ 16
    head_ids = np.arange(2 * HEAD_DIM) // HEAD_DIM
    avg = jnp.asarray((head_ids[:, None] == head_ids[None, :]) / HEAD_DIM, BF16)
    cond8 = jnp.zeros((8, d), F32).at[:b].set(jax.nn.silu(c)).at[b].set(jax.nn.silu(c_ctx))

    h = jnp.concatenate([ctx, x], axis=1).reshape(t, d)
    nqb = lt // ATT_TQ
    nwb = lt // WINDOW
    n_groups = A_KV_HEADS + DIFF_HEADS

    for l in range(depth):
        mod = _modulation(cond8, w_mod[l], b_mod[l]).reshape(8, 6, d)
        w_mix = w_in[l, :, :MIX_COLS]
        w_gate = w_in[l, :, MIX_COLS:].astype(BF16)
        g0, g1 = qk_gain[l, 0].astype(F32), qk_gain[l, 1].astype(F32)
        gtok = jnp.stack([jnp.tile(g1, 2), jnp.tile(g1[swap64], 2)])
        gfeat = jnp.broadcast_to(jnp.concatenate([jnp.tile(g0, A_HEADS), jnp.tile(g0[swap64], A_HEADS)])[:, None],
                                 (2 * A_HEADS * HEAD_DIM, tb))
        kall, qt, vt, wq, wk, wv, rq, rkt, rv, rg = _mixer_in(
            h, mod, group_of_block, w_mix[:, TOK_COLS].astype(BF16), w_mix[:, FEAT_COLS].T.astype(BF16),
            ttok, tfeat, gtok, gfeat, avg, b, lt)
        gates = _inproj(h, mod, group_of_block, w_gate, sigmoid=True, out_dtype=BF16)

        ot = _flash(qt, kall, vt, n_ctx)
        lam_init = 0.8 - 0.6 * math.exp(-0.3 * l)
        lp = diff_lambda[l].astype(F32)
        lam = (jnp.exp(jnp.sum(lp[0] * lp[1])) - jnp.exp(jnp.sum(lp[2] * lp[3])) + lam_init).reshape(1, 1)
        subln = jnp.broadcast_to((jnp.tile(diff_subln[l].astype(F32), DIFF_HEADS) * (1.0 - lam_init))[:, None],
                                 (DIFF_HEADS * HEAD_DIM, tb))

        sink = jnp.repeat(win_sink[l].astype(F32) * LOG2E, WINDOW).reshape(WIN_KV_HEADS, 2 * WINDOW, 1)
        ow = _window(wq, wk, wv, sink, n_ctx)

        lg =jax.nn.log_sigmoid(ret_decay[l].astype(F32))
        idx = jnp.arange(RET_CHUNK, dtype=F32)
        diff = idx[:, None] - idx[None, :]
        lg3 = lg[:, :, None, None]
        dm_f = jnp.exp(jnp.where(diff >= 0, diff * lg3[0], -jnp.inf))
        dm_b = jnp.exp(jnp.where(diff <= 0, -diff * lg3[1], -jnp.inf))
        dmat = jnp.stack([dm_f, dm_b])
        xi = jnp.stack([jnp.exp((idx + 1.0) * lg[0][:, None]), jnp.exp((RET_CHUNK - idx) * lg[1][:, None])])
        zeta = jnp.stack([jnp.exp((RET_CHUNK - 1.0 - idx) * lg[0][:, None]), jnp.exp(idx * lg[1][:, None])])
        gch = jnp.exp(RET_CHUNK * lg)
        o_ret = _retention_call(rq, rkt, rv, dmat, xi[..., None], zeta[:, :, None, :],
                                gch[:, :, None, None], n_ctx // RET_CHUNK)

        h = _merge(ot, ow, o_ret, rg, gates, h, mod, group_of_block, lam, subln, ret_norm[l].astype(F32),
                   w_branch[l].astype(BF16), w_out[l].astype(BF16), ln_attn[l], alpha, b, lt)

        wq_hi, wq_lo = _split_bf16(peer_wq[l])
        xt, rk, b1, nn, az = _peer_route(h, mod, group_of_block, wq_hi, wq_lo, peer_subkeys[l])
        yt = _peer_dense(xt, peer_u[l].astype(BF16), peer_v[l].T.astype(BF16), rk, b1, nn, az)
        h = _resid_ln(h, yt, mod, group_of_block, ln_ffn[l], alpha, 5)

    return h.reshape(b, lt, d)[:, n_ctx:, :]
```

```python
import functools
import math

import numpy as np
import jax
import jax.numpy as jnp
from jax import lax
from jax.experimental import pallas as pl
from jax.experimental.pallas import tpu as pltpu

GRID_W = 64
HEAD_DIM = 64
ROPE_THETA = 10000.0
A_HEADS = 4
A_KV_HEADS = 2
DIFF_HEADS = 4
DIFF_DIM = 32
WIN_HEADS = 4
WIN_KV_HEADS = 2
WINDOW = 128
RET_HEADS = 4
RET_DK = 64
RET_DV = 64
RET_CHUNK = 128
N_BRANCH = 4
BRANCH_W = 256
PIECES = (
    A_HEADS * HEAD_DIM, A_KV_HEADS * HEAD_DIM, A_KV_HEADS * HEAD_DIM,
    2 * DIFF_HEADS * DIFF_DIM, 2 * DIFF_HEADS * DIFF_DIM, DIFF_HEADS * 2 * DIFF_DIM,
    WIN_HEADS * HEAD_DIM, WIN_KV_HEADS * HEAD_DIM, WIN_KV_HEADS * HEAD_DIM,
    RET_HEADS * RET_DK, RET_HEADS * RET_DK, RET_HEADS * RET_DV, RET_HEADS * RET_DV,
)
MIX_COLS = sum(PIECES)
PIECE_OFF = tuple(int(v) for v in np.cumsum((0,) + PIECES))
PEER_HEADS = 8
PEER_NK = 128
PEER_TOPK = 16
PEER_DQ = 128
PEER_CAND_ROWS = -(-sum((PEER_TOPK + 1) // (p + 1) for p in range(PEER_TOPK + 1)) // 8) * 8
LN_EPS = 1e-5
RMS_EPS = 1e-6
LOG2E = 1.4426950408889634

F32 = jnp.float32
BF16 = jnp.bfloat16

TOKEN_BLOCK = 256
ATT_TQ = 256
ATT_TK = 1024
ATT_QTILES = 4
ATT_VPAD = 80
PEER_TOKENS = 512
PEER_EXPERTS = 1024
VMEM_LIMIT = 56 * 1024 * 1024
BF16_ROWS = 16


def _cparams(sem):
    return pltpu.CompilerParams(dimension_semantics=sem, vmem_limit_bytes=VMEM_LIMIT)


def _split_bf16(a):
    hi = a.astype(BF16)
    lo = (a - hi.astype(F32)).astype(BF16)
    return hi, lo


def _dot3(a, b, dims=(((1,), (0,)), ((), ()))):
    ah, al = _split_bf16(a)
    bh, bl = _split_bf16(b)
    d = functools.partial(lax.dot_general, dimension_numbers=dims, preferred_element_type=F32)
    return d(ah, bh) + (d(al, bh) + d(ah, bl))


def _mod_kernel(c_ref, w_ref, b_ref, o_ref):
    o_ref[...] = _dot3(c_ref[...], w_ref[...]) + b_ref[...]


def _modulation(cond8, w, b):
    d, n = w.shape
    tn = 1536
    return pl.pallas_call(
        _mod_kernel,
        out_shape=jax.ShapeDtypeStruct((8, n), F32),
        grid=(n // tn,),
        in_specs=[pl.BlockSpec((8, d), lambda j: (0, 0)),
                  pl.BlockSpec((d, tn), lambda j: (0, j)),
                  pl.BlockSpec((1, tn), lambda j: (0, j))],
        out_specs=pl.BlockSpec((8, tn), lambda j: (0, j)),
        compiler_params=_cparams(("arbitrary",)),
    )(cond8, w, b.reshape(1, n))


def _inproj_kernel(x_ref, mod_ref, w_ref, o_ref, *, sigmoid, chunk):
    m = mod_ref[...]
    xm = (x_ref[...] * (1.0 + m[1:2, :]) + m[0:1, :]).astype(BF16)
    n = w_ref.shape[1]
    for j in range(n // chunk):
        acc = jnp.dot(xm, w_ref[:, j * chunk:(j + 1) * chunk], preferred_element_type=F32)
        if sigmoid:
            acc = jax.nn.sigmoid(acc)
        o_ref[:, j * chunk:(j + 1) * chunk] = acc.astype(o_ref.dtype)


def _inproj(h, mod, group_of_block, w, *, sigmoid, out_dtype):
    t, d = h.shape
    n = w.shape[1]
    tb = TOKEN_BLOCK
    return pl.pallas_call(
        functools.partial(_inproj_kernel, sigmoid=sigmoid, chunk=512),
        out_shape=jax.ShapeDtypeStruct((t, n), out_dtype),
        grid=(t // tb,),
        in_specs=[pl.BlockSpec((tb, d), lambda i: (i, 0)),
                  pl.BlockSpec((None, 6, d), lambda i: (group_of_block(i), 0, 0)),
                  pl.BlockSpec((d, n), lambda i: (0, 0))],
        out_specs=pl.BlockSpec((tb, n), lambda i: (i, 0)),
        compiler_params=_cparams(("parallel",)),
    )(h, mod, w)


TOK_PIECES = ((1, True), (4, True), (6, True), (7, True), (8, False), (9, True), (11, False), (12, False))
FEAT_PIECES = ((0, True), (3, True), (2, False), (5, False), (10, True))


def _piece_cols(pieces):
    flips = {0: 16, 1: 16, 3: 8, 4: 8, 6: 16, 7: 16, 9: 32, 10: 32}
    cols, offs = [], {}
    n = 0
    for p, rotary in pieces:
        base = np.arange(PIECE_OFF[p], PIECE_OFF[p + 1])
        offs[p] = n
        cols.append(base)
        n += len(base)
        if rotary:
            cols.append(PIECE_OFF[p] + ((base - PIECE_OFF[p]) ^ flips[p]))
            n += len(base)
    return np.concatenate(cols), offs


TOK_COLS, TOK_OFF = _piece_cols(TOK_PIECES)
FEAT_COLS, FEAT_OFF = _piece_cols(FEAT_PIECES)
TTOK_OFF = {1: 0, 4: 128, 6: 384, 7: 640, 9: 768}
TTOK_W = 1024
TFEAT_OFF = {0: 0, 3: 256, 10: 512}
TFEAT_W = 768


def _mixer_in_kernel(x_ref, mod_ref, wt_ref, wf_ref, ttok_ref, tfeat_ref, gtok_ref, gfeat_ref, avg_ref,
                     kall_ref, qt_ref, vt_ref, wq_ref, wk_ref, wv_ref, rq_ref, rkt_ref, rv_ref, rg_ref):
    m = mod_ref[...]
    u = x_ref[...] * (1.0 + m[1:2, :]) + m[0:1, :]
    xm = u.astype(BF16)
    xmt = u.T.astype(BF16)
    tb = xm.shape[0]

    def tok(p, swapped=False):
        a = TOK_OFF[p] + (PIECES[p] if swapped else 0)
        return jnp.dot(xm, wt_ref[:, a:a + PIECES[p]], preferred_element_type=F32)

    def feat(p, swapped=False):
        a = FEAT_OFF[p] + (PIECES[p] if swapped else 0)
        return jnp.dot(wf_ref[a:a + PIECES[p], :], xmt, preferred_element_type=F32)

    def rope_tok(p, x, xs):
        a = TTOK_OFF[p]
        return x * ttok_ref[:, a:a + PIECES[p]] + xs * ttok_ref[:, TTOK_W + a:TTOK_W + a + PIECES[p]]

    def rope_feat(p, x, xs):
        a = TFEAT_OFF[p]
        return x * tfeat_ref[a:a + PIECES[p], :] + xs * tfeat_ref[TFEAT_W + a:TFEAT_W + a + PIECES[p], :]

    x, xs = tok(1), tok(1, True)
    sq_hi, sq_lo = _split_bf16(x * x)
    avg = avg_ref[...]
    ms = jnp.dot(sq_hi, avg, preferred_element_type=F32) + jnp.dot(sq_lo, avg, preferred_element_type=F32)
    r = lax.rsqrt(ms + RMS_EPS)
    ka = r * rope_tok(1, x * gtok_ref[0:1, :], xs * gtok_ref[1:2, :])
    kd = rope_tok(4, tok(4), tok(4, True))
    kall_ref[:, :PIECES[1]] = ka.astype(BF16)
    kall_ref[:, PIECES[1]:] = kd.astype(BF16)

    def split_heads(ref, val):
        for hd in range(ref.shape[0]):
            ref[hd] = val[:, hd * HEAD_DIM:(hd + 1) * HEAD_DIM].astype(ref.dtype)

    split_heads(wq_ref, rope_tok(6, tok(6), tok(6, True)))
    split_heads(wk_ref, rope_tok(7, tok(7), tok(7, True)))
    split_heads(wv_ref, tok(8))
    split_heads(rq_ref, rope_tok(9, tok(9), tok(9, True)))
    split_heads(rv_ref, tok(11))
    rg_ref[...] = tok(12)
    rkt = rope_feat(10, feat(10), feat(10, True))
    for hd in range(RET_HEADS):
        rkt_ref[hd] = rkt[hd * RET_DK:(hd + 1) * RET_DK]

    zero64 = jnp.zeros((HEAD_DIM, tb), F32)
    zero32 = jnp.zeros((DIFF_DIM, tb), F32)

    def place(q, upper):
        return jnp.concatenate([zero64, q] if upper else [q, zero64], axis=0).astype(BF16)

    xq, xqs = feat(0), feat(0, True)
    for hd in range(A_HEADS):
        rows = slice(hd * HEAD_DIM, (hd + 1) * HEAD_DIM)
        xh = xq[rows]
        rh = lax.rsqrt(jnp.mean(xh * xh, axis=0, keepdims=True) + RMS_EPS)
        swapped_rows = slice(A_HEADS * HEAD_DIM + hd * HEAD_DIM, A_HEADS * HEAD_DIM + (hd + 1) * HEAD_DIM)
        qh = rh * (xh * gfeat_ref[rows, :] * tfeat_ref[rows, :]
                   + xqs[rows] * gfeat_ref[swapped_rows, :]
                   * tfeat_ref[TFEAT_W + hd * HEAD_DIM:TFEAT_W + (hd + 1) * HEAD_DIM, :])
        qt_ref[hd // 2, hd % 2] = place(qh, hd // 2 == 1)
    qd = rope_feat(3, feat(3), feat(3, True))
    for hd in range(DIFF_HEADS):
        qh = qd[hd * HEAD_DIM:(hd + 1) * HEAD_DIM]
        q1 = jnp.concatenate([qh[:DIFF_DIM], zero32], axis=0)
        q2 = jnp.concatenate([zero32, qh[DIFF_DIM:]], axis=0)
        qt_ref[A_KV_HEADS + hd, 0] = place(q1, hd % 2 == 1)
        qt_ref[A_KV_HEADS + hd, 1] = place(q2, hd % 2 == 1)

    pad_rows = vt_ref.shape[1] - HEAD_DIM
    tail = (lax.broadcasted_iota(jnp.int32, (pad_rows, tb), 0) == 0).astype(F32)
    va, vd = feat(2), feat(5)
    for g in range(A_KV_HEADS + DIFF_HEADS):
        v = va[g * HEAD_DIM:(g + 1) * HEAD_DIM] if g < A_KV_HEADS else \
            vd[(g - A_KV_HEADS) * HEAD_DIM:(g - A_KV_HEADS + 1) * HEAD_DIM]
        vt_ref[g] = jnp.concatenate([v, tail], axis=0).astype(BF16)


def _mixer_in(h, mod, group_of_block, wt, wf, ttok, tfeat, gtok, gfeat, avg, b, lt):
    t, d = h.shape
    tb = TOKEN_BLOCK
    bpb = lt // tb
    n_groups = A_KV_HEADS + DIFF_HEADS
    kw = PIECES[1] + PIECES[4]
    row = lambda w: pl.BlockSpec((tb, w), lambda i: (i, 0))
    const = lambda a: pl.BlockSpec(a.shape, lambda i: (0,) * a.ndim)
    heads_shape = lambda n, dt: jax.ShapeDtypeStruct((b, n, lt, HEAD_DIM), dt)
    heads_spec = lambda n: pl.BlockSpec((None, n, tb, HEAD_DIM), lambda i: (i // bpb, 0, i % bpb, 0))
    return pl.pallas_call(
        _mixer_in_kernel,
        out_shape=(jax.ShapeDtypeStruct((b, lt, kw), BF16),
                   jax.ShapeDtypeStruct((b, n_groups, bpb, 2, 2 * HEAD_DIM, tb), BF16),
                   jax.ShapeDtypeStruct((b, n_groups, bpb, ATT_VPAD, tb), BF16),
                   heads_shape(WIN_HEADS, BF16), heads_shape(WIN_KV_HEADS, BF16), heads_shape(WIN_KV_HEADS, BF16),
                   heads_shape(RET_HEADS, F32), jax.ShapeDtypeStruct((b, RET_HEADS, RET_DK, lt), F32),
                   heads_shape(RET_HEADS, F32), jax.ShapeDtypeStruct((t, PIECES[12]), F32)),
        grid=(t // tb,),
        in_specs=[row(d),
                  pl.BlockSpec((None, 6, d), lambda i: (group_of_block(i), 0, 0)),
                  const(wt), const(wf),
                  pl.BlockSpec((tb, 2 * TTOK_W), lambda i: (i % bpb, 0)),
                  pl.BlockSpec((2 * TFEAT_W, tb), lambda i: (0, i % bpb)),
                  const(gtok), const(gfeat), const(avg)],
        out_specs=(pl.BlockSpec((None, tb, kw), lambda i: (i // bpb, i % bpb, 0)),
                   pl.BlockSpec((None, n_groups, None, 2, 2 * HEAD_DIM, tb),
                                lambda i: (i // bpb, 0, i % bpb, 0, 0, 0)),
                   pl.BlockSpec((None, n_groups, None, ATT_VPAD, tb), lambda i: (i // bpb, 0, i % bpb, 0, 0)),
                   heads_spec(WIN_HEADS), heads_spec(WIN_KV_HEADS), heads_spec(WIN_KV_HEADS),
                   heads_spec(RET_HEADS),
                   pl.BlockSpec((None, RET_HEADS, RET_DK, tb), lambda i: (i // bpb, 0, 0, i % bpb)),
                   heads_spec(RET_HEADS), row(PIECES[12])),
        compiler_params=_cparams(("parallel",)),
    )(h, mod, wt, wf, ttok, tfeat, gtok, gfeat, avg)


def _flash_kernel(*refs, n_ctx, tk, nt):
    q_refs, (k_ref, vt_ref, o_ref, s_ref) = refs[:nt], refs[nt:]
    i = pl.program_id(2)
    tq = q_refs[0].shape[2]
    qt = jnp.concatenate([q[st] for st in range(2) for q in q_refs], axis=1)
    w = qt.shape[1]
    tile = vt_ref.shape[2]
    nlc = (k_ref.shape[0] - n_ctx) // tk

    def scores(row0, rows):
        return jnp.dot(k_ref[pl.ds(row0, rows), :], qt, preferred_element_type=F32)

    def absorb(s, tile0, carry):
        m, acc = carry
        m_new = jnp.maximum(m, jnp.max(s, axis=0, keepdims=True))
        p = jnp.exp2(s - m_new).astype(BF16)
        acc = acc * jnp.exp2(m - m_new)
        for j in range(s.shape[0] // tile):
            acc = acc + jnp.dot(vt_ref[tile0 + j], p[j * tile:(j + 1) * tile], preferred_element_type=F32)
        return m_new, acc

    carry = (jnp.full((1, w), -jnp.inf, F32), jnp.zeros((vt_ref.shape[1], w), F32))
    carry = absorb(scores(0, n_ctx), 0, carry)
    lat_row = lambda c: pl.multiple_of(n_ctx + c * tk, tile)
    lat_tile = lambda c: (n_ctx + c * tk) // tile
    s_ref[0] = scores(lat_row(0), tk)

    def pair(j, carry):
        c0 = 2 * j
        s_ref[1] = scores(lat_row(c0 + 1), tk)
        carry = absorb(s_ref[0], lat_tile(c0), carry)
        s_ref[0] = scores(lat_row(jnp.minimum(c0 + 2, nlc - 1)), tk)
        return absorb(s_ref[1], lat_tile(c0 + 1), carry)

    n_pairs = jnp.where(i == 0, 0, nlc // 2)
    _, acc = lax.fori_loop(0, n_pairs, pair, carry)
    o = acc[:HEAD_DIM] * (1.0 / acc[HEAD_DIM:HEAD_DIM + 1])
    for j in range(nt):
        o_ref[j] = jnp.concatenate([o[:, j * tq:(j + 1) * tq], o[:, (nt + j) * tq:(nt + j + 1) * tq]], axis=1)


def _flash(qt, k_all, vt, n_ctx):
    b, g, tiles, _, dk, tq = qt.shape
    nt = ATT_QTILES
    lt = k_all.shape[1]
    s = lt - n_ctx
    tk = next(c for c in (ATT_TK, 512, 256) if s % (2 * c) == 0)
    assert n_ctx == tq and tk % tq == 0 and (tiles - 1) % nt == 0
    steps = 1 + (tiles - 1) // nt
    key_block = lambda gi: jnp.where(gi < A_KV_HEADS, 0, 1 + (gi - A_KV_HEADS) // 2)
    qspec = lambda j: pl.BlockSpec(
        (None, None, None, 2, dk, tq),
        lambda bi, gi, i: (bi, gi, jnp.where(i == 0, 0, nt * i - (nt - 1) + j), 0, 0, 0))
    return pl.pallas_call(
        functools.partial(_flash_kernel, n_ctx=n_ctx, tk=tk, nt=nt),
        out_shape=jax.ShapeDtypeStruct((b, g, steps, nt, HEAD_DIM, 2 * tq), F32),
        grid=(b, g, steps),
        in_specs=[qspec(j) for j in range(nt)]
        + [pl.BlockSpec((None, lt, dk), lambda bi, gi, i: (bi, 0, key_block(gi))),
           pl.BlockSpec((None, None) + vt.shape[2:], lambda bi, gi, i: (bi, gi, 0, 0, 0))],
        out_specs=pl.BlockSpec((None, None, None, nt, HEAD_DIM, 2 * tq), lambda bi, gi, i: (bi, gi, i, 0, 0, 0)),
        scratch_shapes=[pltpu.VMEM((2, tk, 2 * nt * tq), F32)],
        compiler_params=_cparams(("parallel", "parallel", "arbitrary")),
    )(*([qt] * nt), k_all, vt)


def _window_kernel(q_ref, kp_ref, kc_ref, kn_ref, vp_ref, vc_ref, vn_ref, kx_ref, vx_ref, sink_ref,
                   o_ref, *, n_ctx_blocks, n_blocks):
    step = pl.program_id(1)
    n_kv = kc_ref.shape[0]
    group = q_ref.shape[0] // n_kv
    rows = group * WINDOW
    nt = (((1,), (1,)), ((), ()))
    sdot = functools.partial(lax.dot_general, dimension_numbers=nt, preferred_element_type=F32)
    pv = functools.partial(jnp.dot, preferred_element_type=F32)
    qi = lax.broadcasted_iota(jnp.int32, (rows, WINDOW), 0) % WINDOW
    kj = lax.broadcasted_iota(jnp.int32, (rows, WINDOW), 1)
    neg = -jnp.inf
    for g in range(n_kv):
        sink = sink_ref[g]
        kx, vx = kx_ref[g], vx_ref[g]
        for j in range(2):
            qb = 2 * step + j
            cur = slice(j * WINDOW, (j + 1) * WINDOW)
            q = q_ref[g * group:(g + 1) * group, cur, :].reshape(rows, HEAD_DIM)
            if j == 0:
                kp, vp, kn, vn = kp_ref[g], vp_ref[g], kc_ref[g, WINDOW:, :], vc_ref[g, WINDOW:, :]
            else:
                kp, vp, kn, vn = kc_ref[g, :WINDOW, :], vc_ref[g, :WINDOW, :], kn_ref[g], vn_ref[g]
            off_p = jnp.where(qb >= n_ctx_blocks + 1, 0, 2 * WINDOW)
            off_c = jnp.where(qb >= n_ctx_blocks, 0, 2 * WINDOW)
            off_n = jnp.where(jnp.logical_and(qb >= n_ctx_blocks, qb <= n_blocks - 2), 0, 2 * WINDOW)
            s_p = jnp.where(kj >= qi + off_p, sdot(q, kp), neg)
            s_c = jnp.where(kj >= off_c, sdot(q, kc_ref[g, cur, :]), neg)
            s_n = jnp.where(kj <= qi - off_n, sdot(q, kn), neg)
            s_x = sdot(q, kx)
            rmax = lambda s: jnp.max(s, axis=1, keepdims=True)
            m = jnp.maximum(jnp.maximum(jnp.maximum(rmax(s_p), rmax(s_c)), jnp.maximum(rmax(s_n), rmax(s_x))), sink)
            e_p, e_c, e_n, e_x = (jnp.exp2(s - m) for s in (s_p, s_c, s_n, s_x))
            rsum = lambda e: jnp.sum(e, axis=1, keepdims=True)
            den = rsum(e_p) + rsum(e_c) + rsum(e_n) + rsum(e_x) + jnp.exp2(sink - m)
            o = (pv(e_p.astype(BF16), vp) + pv(e_c.astype(BF16), vc_ref[g, cur, :])
                 + pv(e_n.astype(BF16), vn) + pv(e_x.astype(BF16), vx))
            o_ref[g, j] = o / den


def _window(q, k, v, sink, n_ctx):
    b, hq, lt, dh = q.shape
    g = k.shape[1]
    blk = WINDOW
    nb = lt // blk
    rows = (hq // g) * blk
    n_ctx_blocks = n_ctx // blk
    assert n_ctx % (2 * blk) == 0 and nb % 2 == 0
    lo, hi = n_ctx_blocks, nb - 1
    pair_spec = lambda heads: pl.BlockSpec((None, heads, 2 * blk, dh), lambda bi, i: (bi, 0, i, 0))
    side_spec = lambda delta: pl.BlockSpec((None, g, blk, dh),
                                           lambda bi, i: (bi, 0, jnp.clip(2 * i + delta, lo, hi), 0))
    ctx_spec = pl.BlockSpec((None, g, n_ctx, dh), lambda bi, i: (bi, 0, 0, 0))
    return pl.pallas_call(
        functools.partial(_window_kernel, n_ctx_blocks=n_ctx_blocks, n_blocks=nb),
        out_shape=jax.ShapeDtypeStruct((b, g, nb, rows, dh), F32),
        grid=(b, nb // 2),
        in_specs=[pair_spec(hq), side_spec(-1), pair_spec(g), side_spec(2), side_spec(-1), pair_spec(g),
                  side_spec(2), ctx_spec, ctx_spec, pl.BlockSpec((g, rows, 1), lambda bi, i: (0, 0, 0))],
        out_specs=pl.BlockSpec((None, g, 2, rows, dh), lambda bi, i: (bi, 0, i, 0, 0)),
        compiler_params=_cparams(("parallel", "arbitrary")),
    )(q, k, k, k, v, v, v, k, v, sink)


def _retention_kernel(qf_ref, ktf_ref, vf_ref, qb_ref, ktb_ref, vb_ref, dmat_ref, xi_ref, zeta_ref, gch_ref,
                      of_ref, ob_ref, st_ref):
    t = pl.program_id(0)

    @pl.when(t == 0)
    def _():
        st_ref[...] = jnp.zeros_like(st_ref)

    for d, (q_ref, kt_ref, v_ref, o_ref) in enumerate(((qf_ref, ktf_ref, vf_ref, of_ref),
                                                       (qb_ref, ktb_ref, vb_ref, ob_ref))):
        for bi in range(q_ref.shape[0]):
            for hd in range(RET_HEADS):
                q = q_ref[bi, hd]
                kt = kt_ref[bi, hd]
                v = v_ref[bi, hd]
                st = st_ref[d, bi, hd]
                inner = _dot3(q, kt) * dmat_ref[d, hd]
                o_ref[bi, hd] = _dot3(inner, v) + _dot3(q, st) * xi_ref[d, hd]
                st_ref[d, bi, hd] = st * gch_ref[d, hd] + _dot3(kt * zeta_ref[d, hd], v)


def _retention_call(q, kt, v, dmat, xi, zeta, gch, n_ctx_chunks):
    b, hh, lt, dk = q.shape
    dv = v.shape[-1]
    c = RET_CHUNK
    nch = lt // c

    def back(ti):
        return jnp.where(ti < n_ctx_chunks, n_ctx_chunks - 1 - ti, nch - 1 - (ti - n_ctx_chunks))

    fwd = lambda ti: ti
    rows = lambda blk, w: pl.BlockSpec((b, hh, c, w), lambda ti: (0, 0, blk(ti), 0))
    cols = lambda blk: pl.BlockSpec((b, hh, dk, c), lambda ti: (0, 0, 0, blk(ti)))
    tab = lambda a: pl.BlockSpec(a.shape, lambda ti: (0,) * a.ndim)
    out = jax.ShapeDtypeStruct((b, hh, lt, dv), F32)
    return pl.pallas_call(
        _retention_kernel,
        out_shape=(out, out),
        grid=(nch,),
        in_specs=[rows(fwd, dk), cols(fwd), rows(fwd, dv), rows(back, dk), cols(back), rows(back, dv),
                  tab(dmat), tab(xi), tab(zeta), tab(gch)],
        out_specs=(rows(fwd, dv), rows(back, dv)),
        scratch_shapes=[pltpu.VMEM((2, b, hh, dk, dv), F32)],
        compiler_params=_cparams(("arbitrary",)),
    )(q, kt, v, q, kt, v, dmat, xi, zeta, gch)


def _layer_norm_rows(z, ln):
    mu = jnp.mean(z, axis=-1, keepdims=True)
    zc = z - mu
    var = jnp.mean(zc * zc, axis=-1, keepdims=True)
    return zc * lax.rsqrt(var + LN_EPS) * ln[0:1, :] + ln[1:2, :]


def _merge_kernel(fa_ref, fd0_ref, fd1_ref, win_ref, retf_ref, retb_ref, rg_ref, g_ref, h_ref, mod_ref, lam_ref,
                  subln_ref, rnorm_ref, wb_ref, wo_ref, ln_ref, o_ref, *, alpha):
    d = h_ref.shape[1]
    tb = h_ref.shape[0]
    hd = HEAD_DIM
    proj = functools.partial(jnp.dot, preferred_element_type=F32)
    gate = lambda i: g_ref[:, i * d:(i + 1) * d].astype(F32)

    oat = jnp.concatenate([fa_ref[g][:, st * tb:(st + 1) * tb] for g in range(A_KV_HEADS) for st in range(2)],
                          axis=0)
    m = gate(0) * proj(oat.T.astype(BF16), wb_ref[0])

    lam = lam_ref[...]
    heads = []
    for h4 in range(DIFF_HEADS):
        f = (fd0_ref if h4 < 2 else fd1_ref)[h4 % 2]
        o = f[:, :tb] - lam * f[:, tb:]
        heads.append(o * lax.rsqrt(jnp.mean(o * o, axis=0, keepdims=True) + RMS_EPS))
    obt = jnp.concatenate(heads, axis=0) * subln_ref[...]
    m = m + gate(1) * proj(obt.T.astype(BF16), wb_ref[1])

    half = tb // 2
    acc = None
    for h4 in range(WIN_HEADS):
        g, st = h4 // 2, h4 % 2
        o = jnp.concatenate([win_ref[g, 0][st * half:(st + 1) * half], win_ref[g, 1][st * half:(st + 1) * half]],
                            axis=0)
        t = proj(o.astype(BF16), wb_ref[2, h4 * hd:(h4 + 1) * hd, :])
        acc = t if acc is None else acc + t
    m = m + gate(2) * acc

    acc = None
    for h4 in range(RET_HEADS):
        cols = slice(h4 * RET_DV, (h4 + 1) * RET_DV)
        o = retf_ref[h4] + retb_ref[h4]
        mu = jnp.mean(o, axis=-1, keepdims=True)
        oc = o - mu
        var = jnp.mean(oc * oc, axis=-1, keepdims=True)
        on = oc * lax.rsqrt(var + LN_EPS) * rnorm_ref[0:1, cols] + rnorm_ref[1:2, cols]
        gt = rg_ref[:, cols]
        t = proj((on * (gt / (1.0 + jnp.exp(-gt)))).astype(BF16), wb_ref[3, cols, :])
        acc = t if acc is None else acc + t
    m = m + gate(3) * acc

    y = proj(m.astype(BF16), wo_ref[...])
    z = alpha * h_ref[...] + mod_ref[2:3, :] * y
    o_ref[...] = _layer_norm_rows(z, ln_ref[...])


def _merge(flash_out, win_out, ret_out, rg, gates, h, mod, group_of_block, lam, subln, rnorm, wb, wo, ln,
           alpha, b, lt):
    t, d = h.shape
    tb = TOKEN_BLOCK
    bpb = lt // tb
    row = lambda n: pl.BlockSpec((tb, n), lambda i: (i, 0))
    const = lambda a: pl.BlockSpec(a.shape, lambda i: (0,) * a.ndim)
    fspec = lambda gb: pl.BlockSpec(
        (None, 2, None, None) + flash_out.shape[4:],
        lambda i: (i // bpb, gb, (i % bpb + ATT_QTILES - 1) // ATT_QTILES, (i % bpb + ATT_QTILES - 1) % ATT_QTILES,
                   0, 0))
    ret_spec = pl.BlockSpec((None, RET_HEADS, tb, RET_DV), lambda i: (i // bpb, 0, i % bpb, 0))
    return pl.pallas_call(
        functools.partial(_merge_kernel, alpha=alpha),
        out_shape=jax.ShapeDtypeStruct((t, d), F32),
        grid=(t // tb,),
        in_specs=[fspec(0), fspec(1), fspec(2),
                  pl.BlockSpec((None, WIN_KV_HEADS, 2) + win_out.shape[3:], lambda i: (i // bpb, 0, i % bpb, 0, 0)),
                  ret_spec, ret_spec, row(rg.shape[1]), row(N_BRANCH * d), row(d),
                  pl.BlockSpec((None, 6, d), lambda i: (group_of_block(i), 0, 0)),
                  const(lam), const(subln), const(rnorm), const(wb), const(wo), const(ln)],
        out_specs=row(d),
        compiler_params=_cparams(("parallel",)),
    )(flash_out, flash_out, flash_out, win_out, *ret_out, rg, gates, h, mod, lam, subln, rnorm, wb, wo, ln)


def _top_rows(s, n, with_rank=False):
    out = []
    cur = s
    rank = jnp.full(s.shape, float(n), F32) if with_rank else None
    for r in range(n):
        mx = jnp.max(cur, axis=0, keepdims=True)
        out.append(mx)
        if with_rank:
            rank = jnp.where(cur == mx, float(r), rank)
        if r + 1 < n:
            cur = jnp.where(cur == mx, -jnp.inf, cur)
    return (out, rank) if with_rank else out


def _peer_route_kernel(h_ref, mod_ref, wh_ref, wl_ref, sk_ref, xt_ref, rk_ref, b1_ref, nn_ref, az_ref,
                       cand_ref):
    m = mod_ref[...]
    u = h_ref[...] * (1.0 + m[4:5, :]) + m[3:4, :]
    xt_ref[...] = u.T.astype(BF16)
    uh, ul = _split_bf16(u)
    d = functools.partial(jnp.dot, preferred_element_type=F32)
    nk = PEER_NK
    nt = (((1,), (1,)), ((), ()))
    k1 = PEER_TOPK + 1
    for hd in range(PEER_HEADS):
        c0 = 2 * hd * PEER_DQ
        wh = wh_ref[:, c0:c0 + 2 * PEER_DQ]
        wl = wl_ref[:, c0:c0 + 2 * PEER_DQ]
        q = d(uh, wh) + (d(ul, wh) + d(uh, wl))
        st = [_dot3(sk_ref[hd, p], q[:, p * PEER_DQ:(p + 1) * PEER_DQ], nt) for p in range(2)]
        top0 = _top_rows(st[0], k1)
        top1, rank1 = _top_rows(st[1], k1, with_rank=True)
        r = 0
        for p0 in range(k1):
            for p1 in range(k1 // (p0 + 1)):
                cand_ref[r:r + 1, :] = top0[p0] + top1[p1]
                r += 1
        cand_ref[r:, :] = jnp.full((cand_ref.shape[0] - r, cand_ref.shape[1]), -jnp.inf, F32)
        cand = cand_ref[...]
        ctop = _top_rows(cand, k1)
        tau = 0.5 * (ctop[PEER_TOPK - 1] + ctop[PEER_TOPK])
        mx = top0[0] + top1[0]
        z = jnp.sum(jnp.where(cand >= tau, jnp.exp(cand - mx), 0.0), axis=0, keepdims=True)
        th = tau - st[0]
        nn = jnp.zeros_like(th)
        for q in range(PEER_TOPK):
            nn = nn + jnp.where(top1[q] >= th, 1.0, 0.0)
        rk_ref[hd] = rank1.astype(BF16)
        b1_ref[hd] = jnp.exp(st[1] - top1[0]).astype(BF16)
        nn_ref[hd] = nn
        az_ref[hd] = jnp.exp(st[0] - top0[0]) / z


def _peer_route(h, mod, group_of_block, wq_hi, wq_lo, subkeys):
    t, d = h.shape
    tb = TOKEN_BLOCK
    hh, nk = PEER_HEADS, PEER_NK
    st_shape = lambda dt: jax.ShapeDtypeStruct((hh, nk, t), dt)
    st_spec = pl.BlockSpec((hh, nk, tb), lambda i: (0, 0, i))
    nq = wq_hi.shape[1]
    return pl.pallas_call(
        _peer_route_kernel,
        out_shape=(jax.ShapeDtypeStruct((d, t), BF16), st_shape(BF16), st_shape(BF16), st_shape(F32),
                   st_shape(F32)),
        grid=(t // tb,),
        in_specs=[pl.BlockSpec((tb, d), lambda i: (i, 0)),
                  pl.BlockSpec((None, 6, d), lambda i: (group_of_block(i), 0, 0)),
                  pl.BlockSpec((d, nq), lambda i: (0, 0)),
                  pl.BlockSpec((d, nq), lambda i: (0, 0)),
                  pl.BlockSpec((hh, 2, nk, PEER_DQ), lambda i: (0, 0, 0, 0))],
        out_specs=(pl.BlockSpec((d, tb), lambda i: (0, i)), st_spec, st_spec, st_spec, st_spec),
        scratch_shapes=[pltpu.VMEM((PEER_CAND_ROWS, tb), F32)],
        compiler_params=_cparams(("parallel",)),
    )(h, mod, wq_hi, wq_lo, subkeys)


GELU_K1 = -2.0 * math.sqrt(2.0 / math.pi) * LOG2E
GELU_K2 = GELU_K1 * 0.044715


def _gelu_tanh(x):
    return x / (1.0 + jnp.exp2(x * (GELU_K1 + GELU_K2 * (x * x))))


def _peer_dense_kernel(xt_ref, u_ref, vt_ref, rk_ref, b1_ref, nn_ref, az_ref, yt_ref, g_ref):
    c = pl.program_id(1)
    nk = PEER_NK
    tp = xt_ref.shape[1]
    rows_per_step = u_ref.shape[0] // nk
    n_chunks = pl.num_programs(1) - 1

    @pl.when(c == 0)
    def _():
        yt_ref[...] = jnp.zeros_like(yt_ref)
        g_ref[...] = jnp.zeros_like(g_ref)

    @pl.when(c == n_chunks)
    def _():
        yt_ref[...] += jnp.dot(vt_ref[...], g_ref[...], preferred_element_type=F32)

    @pl.when(c < n_chunks)
    def _():
        xt = xt_ref[...]
        row0 = c * rows_per_step
        pre = lambda ii: jnp.dot(u_ref[ii * nk:(ii + 1) * nk, :], xt, preferred_element_type=F32)
        act_next = pre(0)
        for ii in range(rows_per_step):
            if ii % 2 == 0:
                pair = slice(ii * nk, (ii + 2) * nk)
                yt_ref[...] += jnp.dot(vt_ref[:, pair], g_ref[pair, :], preferred_element_type=F32)
            act = act_next
            if ii + 1 < rows_per_step:
                act_next = pre(ii + 1)
            i = row0 + ii
            w = None
            tile = (nk // BF16_ROWS, BF16_ROWS, tp)
            row = lambda ref, hd: jnp.broadcast_to(ref[hd, pl.ds(i, 1), :], (BF16_ROWS, tp)).astype(BF16)[None]
            for hd in range(PEER_HEADS):
                t = jnp.where(rk_ref[hd].reshape(tile) < row(nn_ref, hd), b1_ref[hd].reshape(tile),
                              jnp.zeros((), BF16)) * row(az_ref, hd)
                w = t if w is None else w + t
            g_ref[ii * nk:(ii + 1) * nk, :] = w.reshape(nk, tp) * _gelu_tanh(act.astype(BF16))


def _peer_dense(xt, u, vt, rk, b1, nn, az):
    d, t = xt.shape
    n = u.shape[0]
    tp = PEER_TOKENS
    ec = PEER_EXPERTS
    nc = n // ec
    hh, nk = PEER_HEADS, PEER_NK
    st_spec = pl.BlockSpec((hh, nk, tp), lambda i, c: (0, 0, i))
    return pl.pallas_call(
        _peer_dense_kernel,
        out_shape=jax.ShapeDtypeStruct((d, t), F32),
        grid=(t // tp, nc + 1),
        in_specs=[pl.BlockSpec((d, tp), lambda i, c: (0, i)),
                  pl.BlockSpec((ec, d), lambda i, c: (jnp.minimum(c, nc - 1), 0)),
                  pl.BlockSpec((d, ec), lambda i, c: (0, jnp.maximum(c - 1, 0))),
                  st_spec, st_spec, st_spec, st_spec],
        out_specs=pl.BlockSpec((d, tp), lambda i, c: (0, i)),
        scratch_shapes=[pltpu.VMEM((ec, tp), BF16)],
        compiler_params=_cparams(("parallel", "arbitrary")),
    )(xt, u, vt, rk, b1, nn, az)


def _resid_ln_kernel(h_ref, yt_ref, mod_ref, ln_ref, o_ref, *, alpha, gate_row):
    z = alpha * h_ref[...] + mod_ref[gate_row:gate_row + 1, :] * yt_ref[...].T
    o_ref[...] = _layer_norm_rows(z, ln_ref[...])


def _resid_ln(h, yt, mod, group_of_block, ln, alpha, gate_row):
    t, d = h.shape
    tb = TOKEN_BLOCK
    row = pl.BlockSpec((tb, d), lambda i: (i, 0))
    return pl.pallas_call(
        functools.partial(_resid_ln_kernel, alpha=alpha, gate_row=gate_row),
        out_shape=jax.ShapeDtypeStruct((t, d), F32),
        grid=(t // tb,),
        in_specs=[row, pl.BlockSpec((d, tb), lambda i: (0, i)),
                  pl.BlockSpec((None, 6, d), lambda i: (group_of_block(i), 0, 0)),
                  pl.BlockSpec((2, d), lambda i: (0, 0))],
        out_specs=row,
        compiler_params=_cparams(("parallel",)),
    )(h, yt, mod, ln)


def _axial_tables(s, n_ctx, d):
    rows = s // GRID_W
    row = jnp.broadcast_to(jnp.arange(rows, dtype=F32)[:, None], (rows, GRID_W)).reshape(-1)
    col = jnp.broadcast_to(jnp.arange(GRID_W, dtype=F32)[None, :], (rows, GRID_W)).reshape(-1)
    quarter = d // 4
    inv = ROPE_THETA ** (-jnp.arange(quarter, dtype=F32) / quarter)
    ar, ac = row[:, None] * inv, col[:, None] * inv
    cos = jnp.concatenate([jnp.cos(ar), jnp.cos(ar), jnp.cos(ac), jnp.cos(ac)], axis=-1)
    sin = jnp.concatenate([-jnp.sin(ar), jnp.sin(ar), -jnp.sin(ac), jnp.sin(ac)], axis=-1)
    cos = jnp.concatenate([jnp.ones((n_ctx, d), F32), cos], axis=0)
    sin = jnp.concatenate([jnp.zeros((n_ctx, d), F32), sin], axis=0)
    return cos, sin


def _rope1d_tables(lt, d):
    half = d // 2
    inv = ROPE_THETA ** (-jnp.arange(half, dtype=F32) / half)
    ang = jnp.arange(lt, dtype=F32)[:, None] * inv
    return (jnp.concatenate([jnp.cos(ang), jnp.cos(ang)], axis=-1),
            jnp.concatenate([-jnp.sin(ang), jnp.sin(ang)], axis=-1))


def _swap_perm():
    cols = []
    for piece, flip in ((0, 16), (1, 16), (3, 8), (4, 8), (6, 16), (7, 16), (9, 32), (10, 32)):
        base = np.arange(PIECE_OFF[piece], PIECE_OFF[piece + 1])
        cols.append(PIECE_OFF[piece] + ((base - PIECE_OFF[piece]) ^ flip))
    return np.concatenate(cols)


SWAP_COLS = _swap_perm()
SWAP_PIECES = (0, 1, 3, 4, 6, 7, 9, 10)
SWAP_OFF = dict(zip(SWAP_PIECES, np.cumsum([0] + [PIECES[p] for p in SWAP_PIECES])[:-1] + MIX_COLS))


def _rms_heads(x, xs, gain, gain_s):
    r = lax.rsqrt(jnp.mean(x * x, axis=-1, keepdims=True) + RMS_EPS)
    return x * r * gain, xs * r * gain_s


def kernel(x, c, ctx, c_ctx, w_mod, b_mod, w_in, qk_gain, diff_lambda, diff_subln, win_sink, ret_decay,
           ret_norm, w_branch, w_out, ln_attn, ln_ffn, peer_wq, peer_subkeys, peer_u, peer_v):
    b, s, d = x.shape
    n_ctx = ctx.shape[1]
    depth = w_mod.shape[0]
    lt = n_ctx + s
    t = b * lt
    tb = TOKEN_BLOCK
    assert n_ctx % tb == 0 and s % tb == 0 and t % PEER_TOKENS == 0
    alpha = (2 * depth) ** 0.25
    blocks_per_batch = lt // tb
    ctx_blocks = n_ctx // tb

    def group_of_block(i):
        return jnp.where(i % blocks_per_batch < ctx_blocks, b, i // blocks_per_batch)

    cos64, sin64 = _axial_tables(s, n_ctx, HEAD_DIM)
    cos32, sin32 = _axial_tables(s, n_ctx, DIFF_DIM)
    cos1d, sin1d = _rope1d_tables(lt, RET_DK)
    sc_a = HEAD_DIM ** -0.5 * LOG2E
    sc_d = DIFF_DIM ** -0.5 * LOG2E
    rep = lambda a, n: jnp.tile(a, (1, n))
    tok_part = lambda t64, t32, t1d: [rep(t64, 2), rep(t32, 8), rep(t64, 4) * sc_a, rep(t64, 2), rep(t1d, 4)]
    feat_part = lambda t64, t32, t1d: [rep(t64, 4) * sc_a, rep(t32, 8) * sc_d, rep(t1d, 4) * RET_DK ** -0.5]
    ttok = jnp.concatenate(tok_part(cos64, cos32, cos1d) + tok_part(sin64, sin32, sin1d), axis=1)
    tfeat = jnp.concatenate(feat_part(cos64, cos32, cos1d) + feat_part(sin64, sin32, sin1d), axis=1).T
    swap64 = np.arange(HEAD_DIM) ^ 16
    head_ids = np.arange(2 * HEAD_DIM) // HEAD_DIM
    avg = jnp.asarray((head_ids[:, None] == head_ids[None, :]) / HEAD_DIM, BF16)
    cond8 = jnp.zeros((8, d), F32).at[:b].set(jax.nn.silu(c)).at[b].set(jax.nn.silu(c_ctx))

    h = jnp.concatenate([ctx, x], axis=1).reshape(t, d)
    nqb = lt // ATT_TQ
    nwb = lt // WINDOW
    n_groups = A_KV_HEADS + DIFF_HEADS

    for l in range(depth):
        mod = _modulation(cond8, w_mod[l], b_mod[l]).reshape(8, 6, d)
        w_mix = w_in[l, :, :MIX_COLS]
        w_gate = w_in[l, :, MIX_COLS:].astype(BF16)
        g0, g1 = qk_gain[l, 0].astype(F32), qk_gain[l, 1].astype(F32)
        gtok = jnp.stack([jnp.tile(g1, 2), jnp.tile(g1[swap64], 2)])
        gfeat = jnp.broadcast_to(jnp.concatenate([jnp.tile(g0, A_HEADS), jnp.tile(g0[swap64], A_HEADS)])[:, None],
                                 (2 * A_HEADS * HEAD_DIM, tb))
        kall, qt, vt, wq, wk, wv, rq, rkt, rv, rg = _mixer_in(
            h, mod, group_of_block, w_mix[:, TOK_COLS].astype(BF16), w_mix[:, FEAT_COLS].T.astype(BF16),
            ttok, tfeat, gtok, gfeat, avg, b, lt)
        gates = _inproj(h, mod, group_of_block, w_gate, sigmoid=True, out_dtype=BF16)

        ot = _flash(qt, kall, vt, n_ctx)
        lam_init = 0.8 - 0.6 * math.exp(-0.3 * l)
        lp = diff_lambda[l].astype(F32)
        lam = (jnp.exp(jnp.sum(lp[0] * lp[1])) - jnp.exp(jnp.sum(lp[2] * lp[3])) + lam_init).reshape(1, 1)
        subln = jnp.broadcast_to((jnp.tile(diff_subln[l].astype(F32), DIFF_HEADS) * (1.0 - lam_init))[:, None],
                                 (DIFF_HEADS * HEAD_DIM, tb))

        sink = jnp.repeat(win_sink[l].astype(F32) * LOG2E, WINDOW).reshape(WIN_KV_HEADS, 2 * WINDOW, 1)
        ow = _window(wq, wk, wv, sink, n_ctx)

        lg =jax.nn.log_sigmoid(ret_decay[l].astype(F32))
        idx = jnp.arange(RET_CHUNK, dtype=F32)
        diff = idx[:, None] - idx[None, :]
        lg3 = lg[:, :, None, None]
        dm_f = jnp.exp(jnp.where(diff >= 0, diff * lg3[0], -jnp.inf))
        dm_b = jnp.exp(jnp.where(diff <= 0, -diff * lg3[1], -jnp.inf))
        dmat = jnp.stack([dm_f, dm_b])
        xi = jnp.stack([jnp.exp((idx + 1.0) * lg[0][:, None]), jnp.exp((RET_CHUNK - idx) * lg[1][:, None])])
        zeta = jnp.stack([jnp.exp((RET_CHUNK - 1.0 - idx) * lg[0][:, None]), jnp.exp(idx * lg[1][:, None])])
        gch = jnp.exp(RET_CHUNK * lg)
        o_ret = _retention_call(rq, rkt, rv, dmat, xi[..., None], zeta[:, :, None, :],
                                gch[:, :, None, None], n_ctx // RET_CHUNK)

        h = _merge(ot, ow, o_ret, rg, gates, h, mod, group_of_block, lam, subln, ret_norm[l].astype(F32),
                   w_branch[l].astype(BF16), w_out[l].astype(BF16), ln_attn[l], alpha, b, lt)

        wq_hi, wq_lo = _split_bf16(peer_wq[l])
        xt, rk, b1, nn, az = _peer_route(h, mod, group_of_block, wq_hi, wq_lo, peer_subkeys[l])
        yt = _peer_dense(xt, peer_u[l].astype(BF16), peer_v[l].T.astype(BF16), rk, b1, nn, az)
        h = _resid_ln(h, yt, mod, group_of_block, ln_ffn[l], alpha, 5)

    return h.reshape(b, lt, d)[:, n_ctx:, :]
```

```python
import functools
import math

import numpy as np
import jax
import jax.numpy as jnp
from jax import lax
from jax.experimental import pallas as pl
from jax.experimental.pallas import tpu as pltpu

GRID_W = 64
HEAD_DIM = 64
ROPE_THETA = 10000.0
A_HEADS = 4
A_KV_HEADS = 2
DIFF_HEADS = 4
DIFF_DIM = 32
WIN_HEADS = 4
WIN_KV_HEADS = 2
WINDOW = 128
RET_HEADS = 4
RET_DK = 64
RET_DV = 64
RET_CHUNK = 128
N_BRANCH = 4
BRANCH_W = 256
PIECES = (
    A_HEADS * HEAD_DIM, A_KV_HEADS * HEAD_DIM, A_KV_HEADS * HEAD_DIM,
    2 * DIFF_HEADS * DIFF_DIM, 2 * DIFF_HEADS * DIFF_DIM, DIFF_HEADS * 2 * DIFF_DIM,
    WIN_HEADS * HEAD_DIM, WIN_KV_HEADS * HEAD_DIM, WIN_KV_HEADS * HEAD_DIM,
    RET_HEADS * RET_DK, RET_HEADS * RET_DK, RET_HEADS * RET_DV, RET_HEADS * RET_DV,
)
MIX_COLS = sum(PIECES)
PIECE_OFF = tuple(int(v) for v in np.cumsum((0,) + PIECES))
PEER_HEADS = 8
PEER_NK = 128
PEER_TOPK = 16
PEER_DQ = 128
PEER_CAND_ROWS = -(-sum((PEER_TOPK + 1) // (p + 1) for p in range(PEER_TOPK + 1)) // 8) * 8
LN_EPS = 1e-5
RMS_EPS = 1e-6
LOG2E = 1.4426950408889634

F32 = jnp.float32
BF16 = jnp.bfloat16

TOKEN_BLOCK = 256
ATT_TQ = 256
ATT_TK = 1024
ATT_QTILES = 4
ATT_VPAD = 80
PEER_TOKENS = 512
PEER_EXPERTS = 1024
VMEM_LIMIT = 56 * 1024 * 1024
BF16_ROWS = 16


def _cparams(sem):
    return pltpu.CompilerParams(dimension_semantics=sem, vmem_limit_bytes=VMEM_LIMIT)


def _split_bf16(a):
    hi = a.astype(BF16)
    lo = (a - hi.astype(F32)).astype(BF16)
    return hi, lo


def _dot3(a, b, dims=(((1,), (0,)), ((), ()))):
    ah, al = _split_bf16(a)
    bh, bl = _split_bf16(b)
    d = functools.partial(lax.dot_general, dimension_numbers=dims, preferred_element_type=F32)
    return d(ah, bh) + (d(al, bh) + d(ah, bl))


def _mod_kernel(c_ref, w_ref, b_ref, o_ref):
    o_ref[...] = _dot3(c_ref[...], w_ref[...]) + b_ref[...]


def _modulation(cond8, w, b):
    d, n = w.shape
    tn = 1536
    return pl.pallas_call(
        _mod_kernel,
        out_shape=jax.ShapeDtypeStruct((8, n), F32),
        grid=(n // tn,),
        in_specs=[pl.BlockSpec((8, d), lambda j: (0, 0)),
                  pl.BlockSpec((d, tn), lambda j: (0, j)),
                  pl.BlockSpec((1, tn), lambda j: (0, j))],
        out_specs=pl.BlockSpec((8, tn), lambda j: (0, j)),
        compiler_params=_cparams(("arbitrary",)),
    )(cond8, w, b.reshape(1, n))


def _inproj_kernel(x_ref, mod_ref, w_ref, o_ref, *, sigmoid, chunk):
    m = mod_ref[...]
    xm = (x_ref[...] * (1.0 + m[1:2, :]) + m[0:1, :]).astype(BF16)
    n = w_ref.shape[1]
    for j in range(n // chunk):
        acc = jnp.dot(xm, w_ref[:, j * chunk:(j + 1) * chunk], preferred_element_type=F32)
        if sigmoid:
            acc = jax.nn.sigmoid(acc)
        o_ref[:, j * chunk:(j + 1) * chunk] = acc.astype(o_ref.dtype)


def _inproj(h, mod, group_of_block, w, *, sigmoid, out_dtype):
    t, d = h.shape
    n = w.shape[1]
    tb = TOKEN_BLOCK
    return pl.pallas_call(
        functools.partial(_inproj_kernel, sigmoid=sigmoid, chunk=512),
        out_shape=jax.ShapeDtypeStruct((t, n), out_dtype),
        grid=(t // tb,),
        in_specs=[pl.BlockSpec((tb, d), lambda i: (i, 0)),
                  pl.BlockSpec((None, 6, d), lambda i: (group_of_block(i), 0, 0)),
                  pl.BlockSpec((d, n), lambda i: (0, 0))],
        out_specs=pl.BlockSpec((tb, n), lambda i: (i, 0)),
        compiler_params=_cparams(("parallel",)),
    )(h, mod, w)


TOK_PIECES = ((1, True), (4, True), (6, True), (7, True), (8, False), (9, True), (11, False), (12, False))
FEAT_PIECES = ((0, True), (3, True), (2, False), (5, False), (10, True))


def _piece_cols(pieces):
    flips = {0: 16, 1: 16, 3: 8, 4: 8, 6: 16, 7: 16, 9: 32, 10: 32}
    cols, offs = [], {}
    n = 0
    for p, rotary in pieces:
        base = np.arange(PIECE_OFF[p], PIECE_OFF[p + 1])
        offs[p] = n
        cols.append(base)
        n += len(base)
        if rotary:
            cols.append(PIECE_OFF[p] + ((base - PIECE_OFF[p]) ^ flips[p]))
            n += len(base)
    return np.concatenate(cols), offs


TOK_COLS, TOK_OFF = _piece_cols(TOK_PIECES)
FEAT_COLS, FEAT_OFF = _piece_cols(FEAT_PIECES)
TTOK_OFF = {1: 0, 4: 128, 6: 384, 7: 640, 9: 768}
TTOK_W = 1024
TFEAT_OFF = {0: 0, 3: 256, 10: 512}
TFEAT_W = 768


def _mixer_in_kernel(x_ref, mod_ref, wt_ref, wf_ref, ttok_ref, tfeat_ref, gtok_ref, gfeat_ref, avg_ref,
                     kall_ref, qt_ref, vt_ref, wq_ref, wk_ref, wv_ref, rq_ref, rkt_ref, rv_ref, rg_ref):
    m = mod_ref[...]
    u = x_ref[...] * (1.0 + m[1:2, :]) + m[0:1, :]
    xm = u.astype(BF16)
    xmt = u.T.astype(BF16)
    tb = xm.shape[0]

    def tok(p, swapped=False):
        a = TOK_OFF[p] + (PIECES[p] if swapped else 0)
        return jnp.dot(xm, wt_ref[:, a:a + PIECES[p]], preferred_element_type=F32)

    def feat(p, swapped=False):
        a = FEAT_OFF[p] + (PIECES[p] if swapped else 0)
        return jnp.dot(wf_ref[a:a + PIECES[p], :], xmt, preferred_element_type=F32)

    def rope_tok(p, x, xs):
        a = TTOK_OFF[p]
        return x * ttok_ref[:, a:a + PIECES[p]] + xs * ttok_ref[:, TTOK_W + a:TTOK_W + a + PIECES[p]]

    def rope_feat(p, x, xs):
        a = TFEAT_OFF[p]
        return x * tfeat_ref[a:a + PIECES[p], :] + xs * tfeat_ref[TFEAT_W + a:TFEAT_W + a + PIECES[p], :]

    x, xs = tok(1), tok(1, True)
    sq_hi, sq_lo = _split_bf16(x * x)
    avg = avg_ref[...]
    ms = jnp.dot(sq_hi, avg, preferred_element_type=F32) + jnp.dot(sq_lo, avg, preferred_element_type=F32)
    r = lax.rsqrt(ms + RMS_EPS)
    ka = r * rope_tok(1, x * gtok_ref[0:1, :], xs * gtok_ref[1:2, :])
    kd = rope_tok(4, tok(4), tok(4, True))
    kall_ref[:, :PIECES[1]] = ka.astype(BF16)
    kall_ref[:, PIECES[1]:] = kd.astype(BF16)

    def split_heads(ref, val):
        for hd in range(ref.shape[0]):
            ref[hd] = val[:, hd * HEAD_DIM:(hd + 1) * HEAD_DIM].astype(ref.dtype)

    split_heads(wq_ref, rope_tok(6, tok(6), tok(6, True)))
    split_heads(wk_ref, rope_tok(7, tok(7), tok(7, True)))
    split_heads(wv_ref, tok(8))
    split_heads(rq_ref, rope_tok(9, tok(9), tok(9, True)))
    split_heads(rv_ref, tok(11))
    rg_ref[...] = tok(12)
    rkt = rope_feat(10, feat(10), feat(10, True))
    for hd in range(RET_HEADS):
        rkt_ref[hd] = rkt[hd * RET_DK:(hd + 1) * RET_DK]

    zero64 = jnp.zeros((HEAD_DIM, tb), F32)
    zero32 = jnp.zeros((DIFF_DIM, tb), F32)

    def place(q, upper):
        return jnp.concatenate([zero64, q] if upper else [q, zero64], axis=0).astype(BF16)

    xq, xqs = feat(0), feat(0, True)
    for hd in range(A_HEADS):
        rows = slice(hd * HEAD_DIM, (hd + 1) * HEAD_DIM)
        xh = xq[rows]
        rh = lax.rsqrt(jnp.mean(xh * xh, axis=0, keepdims=True) + RMS_EPS)
        swapped_rows = slice(A_HEADS * HEAD_DIM + hd * HEAD_DIM, A_HEADS * HEAD_DIM + (hd + 1) * HEAD_DIM)
        qh = rh * (xh * gfeat_ref[rows, :] * tfeat_ref[rows, :]
                   + xqs[rows] * gfeat_ref[swapped_rows, :]
                   * tfeat_ref[TFEAT_W + hd * HEAD_DIM:TFEAT_W + (hd + 1) * HEAD_DIM, :])
        qt_ref[hd // 2, hd % 2] = place(qh, hd // 2 == 1)
    qd = rope_feat(3, feat(3), feat(3, True))
    for hd in range(DIFF_HEADS):
        qh = qd[hd * HEAD_DIM:(hd + 1) * HEAD_DIM]
        q1 = jnp.concatenate([qh[:DIFF_DIM], zero32], axis=0)
        q2 = jnp.concatenate([zero32, qh[DIFF_DIM:]], axis=0)
        qt_ref[A_KV_HEADS + hd, 0] = place(q1, hd % 2 == 1)
        qt_ref[A_KV_HEADS + hd, 1] = place(q2, hd % 2 == 1)

    pad_rows = vt_ref.shape[1] - HEAD_DIM
    tail = (lax.broadcasted_iota(jnp.int32, (pad_rows, tb), 0) == 0).astype(F32)
    va, vd = feat(2), feat(5)
    for g in range(A_KV_HEADS + DIFF_HEADS):
        v = va[g * HEAD_DIM:(g + 1) * HEAD_DIM] if g < A_KV_HEADS else \
            vd[(g - A_KV_HEADS) * HEAD_DIM:(g - A_KV_HEADS + 1) * HEAD_DIM]
        vt_ref[g] = jnp.concatenate([v, tail], axis=0).astype(BF16)


def _mixer_in(h, mod, group_of_block, wt, wf, ttok, tfeat, gtok, gfeat, avg, b, lt):
    t, d = h.shape
    tb = TOKEN_BLOCK
    bpb = lt // tb
    n_groups = A_KV_HEADS + DIFF_HEADS
    kw = PIECES[1] + PIECES[4]
    row = lambda w: pl.BlockSpec((tb, w), lambda i: (i, 0))
    const = lambda a: pl.BlockSpec(a.shape, lambda i: (0,) * a.ndim)
    heads_shape = lambda n, dt: jax.ShapeDtypeStruct((b, n, lt, HEAD_DIM), dt)
    heads_spec = lambda n: pl.BlockSpec((None, n, tb, HEAD_DIM), lambda i: (i // bpb, 0, i % bpb, 0))
    return pl.pallas_call(
        _mixer_in_kernel,
        out_shape=(jax.ShapeDtypeStruct((b, lt, kw), BF16),
                   jax.ShapeDtypeStruct((b, n_groups, bpb, 2, 2 * HEAD_DIM, tb), BF16),
                   jax.ShapeDtypeStruct((b, n_groups, bpb, ATT_VPAD, tb), BF16),
                   heads_shape(WIN_HEADS, BF16), heads_shape(WIN_KV_HEADS, BF16), heads_shape(WIN_KV_HEADS, BF16),
                   heads_shape(RET_HEADS, F32), jax.ShapeDtypeStruct((b, RET_HEADS, RET_DK, lt), F32),
                   heads_shape(RET_HEADS, F32), jax.ShapeDtypeStruct((t, PIECES[12]), F32)),
        grid=(t // tb,),
        in_specs=[row(d),
                  pl.BlockSpec((None, 6, d), lambda i: (group_of_block(i), 0, 0)),
                  const(wt), const(wf),
                  pl.BlockSpec((tb, 2 * TTOK_W), lambda i: (i % bpb, 0)),
                  pl.BlockSpec((2 * TFEAT_W, tb), lambda i: (0, i % bpb)),
                  const(gtok), const(gfeat), const(avg)],
        out_specs=(pl.BlockSpec((None, tb, kw), lambda i: (i // bpb, i % bpb, 0)),
                   pl.BlockSpec((None, n_groups, None, 2, 2 * HEAD_DIM, tb),
                                lambda i: (i // bpb, 0, i % bpb, 0, 0, 0)),
                   pl.BlockSpec((None, n_groups, None, ATT_VPAD, tb), lambda i: (i // bpb, 0, i % bpb, 0, 0)),
                   heads_spec(WIN_HEADS), heads_spec(WIN_KV_HEADS), heads_spec(WIN_KV_HEADS),
                   heads_spec(RET_HEADS),
                   pl.BlockSpec((None, RET_HEADS, RET_DK, tb), lambda i: (i // bpb, 0, 0, i % bpb)),
                   heads_spec(RET_HEADS), row(PIECES[12])),
        compiler_params=_cparams(("parallel",)),
    )(h, mod, wt, wf, ttok, tfeat, gtok, gfeat, avg)


def _flash_kernel(*refs, n_ctx, tk, nt):
    q_refs, (k_ref, vt_ref, o_ref, s_ref, smax_ref, m_ref, acc_ref) = refs[:nt], refs[nt:]
    i = pl.program_id(2)
    tq = q_refs[0].shape[2]
    qt = jnp.concatenate([q[st] for st in range(2) for q in q_refs], axis=1)
    tile = vt_ref.shape[2]
    nlc = (k_ref.shape[0] - n_ctx) // tk

    def scores(row0, rows):
        return jnp.dot(k_ref[pl.ds(row0, rows), :], qt, preferred_element_type=F32)

    def stage(slot, chunk):
        s = scores(lat_row(chunk), tk)
        s_ref[slot] = s
        smax_ref[slot] = jnp.max(s, axis=0, keepdims=True)

    def absorb(s, smax, tile0):
        m = m_ref[...]
        m_new = jnp.maximum(m, smax)
        p = jnp.exp2(s - m_new).astype(BF16)
        acc = acc_ref[...] * jnp.exp2(m - m_new)
        for j in range(s.shape[0] // tile):
            acc = acc + jnp.dot(vt_ref[tile0 + j], p[j * tile:(j + 1) * tile], preferred_element_type=F32)
        m_ref[...] = m_new
        acc_ref[...] = acc

    m_ref[...] = jnp.full(m_ref.shape, -jnp.inf, F32)
    acc_ref[...] = jnp.zeros(acc_ref.shape, F32)
    s_ctx = scores(0, n_ctx)
    absorb(s_ctx, jnp.max(s_ctx, axis=0, keepdims=True), 0)
    lat_row = lambda c: pl.multiple_of(n_ctx + c * tk, tile)
    lat_tile = lambda c: (n_ctx + c * tk) // tile

    @pl.when(i > 0)
    def _():
        stage(0, 0)

        def pair(j, carry):
            c0 = 2 * j
            stage(1, c0 + 1)
            absorb(s_ref[0], smax_ref[0], lat_tile(c0))
            stage(0, c0 + 2)
            absorb(s_ref[1], smax_ref[1], lat_tile(c0 + 1))
            return carry

        lax.fori_loop(0, nlc // 2 - 1, pair, 0)
        stage(1, nlc - 1)
        absorb(s_ref[0], smax_ref[0], lat_tile(nlc - 2))
        absorb(s_ref[1], smax_ref[1], lat_tile(nlc - 1))

    acc = acc_ref[...]
    o = acc[:HEAD_DIM] * (1.0 / acc[HEAD_DIM:HEAD_DIM + 1])
    for j in range(nt):
        o_ref[j] = jnp.concatenate([o[:, j * tq:(j + 1) * tq], o[:, (nt + j) * tq:(nt + j + 1) * tq]], axis=1)


def _flash(qt, k_all, vt, n_ctx):
    b, g, tiles, _, dk, tq = qt.shape
    nt = ATT_QTILES
    lt = k_all.shape[1]
    s = lt - n_ctx
    tk = next(c for c in (ATT_TK, 512, 256) if s % (2 * c) == 0)
    assert n_ctx == tq and tk % tq == 0 and (tiles - 1) % nt == 0
    steps = 1 + (tiles - 1) // nt
    key_block = lambda gi: jnp.where(gi < A_KV_HEADS, 0, 1 + (gi - A_KV_HEADS) // 2)
    qspec = lambda j: pl.BlockSpec(
        (None, None, None, 2, dk, tq),
        lambda bi, gi, i: (bi, gi, jnp.where(i == 0, 0, nt * i - (nt - 1) + j), 0, 0, 0))
    return pl.pallas_call(
        functools.partial(_flash_kernel, n_ctx=n_ctx, tk=tk, nt=nt),
        out_shape=jax.ShapeDtypeStruct((b, g, steps, nt, HEAD_DIM, 2 * tq), F32),
        grid=(b, g, steps),
        in_specs=[qspec(j) for j in range(nt)]
        + [pl.BlockSpec((None, lt, dk), lambda bi, gi, i: (bi, 0, key_block(gi))),
           pl.BlockSpec((None, None) + vt.shape[2:], lambda bi, gi, i: (bi, gi, 0, 0, 0))],
        out_specs=pl.BlockSpec((None, None, None, nt, HEAD_DIM, 2 * tq), lambda bi, gi, i: (bi, gi, i, 0, 0, 0)),
        scratch_shapes=[pltpu.VMEM((2, tk, 2 * nt * tq), F32), pltpu.VMEM((2, 1, 2 * nt * tq), F32),
                        pltpu.VMEM((1, 2 * nt * tq), F32), pltpu.VMEM((vt.shape[3], 2 * nt * tq), F32)],
        compiler_params=_cparams(("parallel", "parallel", "arbitrary")),
    )(*([qt] * nt), k_all, vt)


def _window_kernel(q_ref, kp_ref, kc_ref, kn_ref, vp_ref, vc_ref, vn_ref, kx_ref, vx_ref, sink_ref,
                   o_ref, *, n_ctx_blocks, n_blocks):
    step = pl.program_id(1)
    n_kv = kc_ref.shape[0]
    group = q_ref.shape[0] // n_kv
    rows = group * WINDOW
    nt = (((1,), (1,)), ((), ()))
    sdot = functools.partial(lax.dot_general, dimension_numbers=nt, preferred_element_type=F32)
    pv = functools.partial(jnp.dot, preferred_element_type=F32)
    qi = lax.broadcasted_iota(jnp.int32, (rows, WINDOW), 0) % WINDOW
    kj = lax.broadcasted_iota(jnp.int32, (rows, WINDOW), 1)
    neg = -jnp.inf
    for g in range(n_kv):
        sink = sink_ref[g]
        kx, vx = kx_ref[g], vx_ref[g]
        for j in range(2):
            qb = 2 * step + j
            cur = slice(j * WINDOW, (j + 1) * WINDOW)
            q = q_ref[g * group:(g + 1) * group, cur, :].reshape(rows, HEAD_DIM)
            if j == 0:
                kp, vp, kn, vn = kp_ref[g], vp_ref[g], kc_ref[g, WINDOW:, :], vc_ref[g, WINDOW:, :]
            else:
                kp, vp, kn, vn = kc_ref[g, :WINDOW, :], vc_ref[g, :WINDOW, :], kn_ref[g], vn_ref[g]
            off_p = jnp.where(qb >= n_ctx_blocks + 1, 0, 2 * WINDOW)
            off_c = jnp.where(qb >= n_ctx_blocks, 0, 2 * WINDOW)
            off_n = jnp.where(jnp.logical_and(qb >= n_ctx_blocks, qb <= n_blocks - 2), 0, 2 * WINDOW)
            s_p = jnp.where(kj >= qi + off_p, sdot(q, kp), neg)
            s_c = jnp.where(kj >= off_c, sdot(q, kc_ref[g, cur, :]), neg)
            s_n = jnp.where(kj <= qi - off_n, sdot(q, kn), neg)
            s_x = sdot(q, kx)
            rmax = lambda s: jnp.max(s, axis=1, keepdims=True)
            m = jnp.maximum(jnp.maximum(jnp.maximum(rmax(s_p), rmax(s_c)), jnp.maximum(rmax(s_n), rmax(s_x))), sink)
            e_p, e_c, e_n, e_x = (jnp.exp2(s - m) for s in (s_p, s_c, s_n, s_x))
            rsum = lambda e: jnp.sum(e, axis=1, keepdims=True)
            den = rsum(e_p) + rsum(e_c) + rsum(e_n) + rsum(e_x) + jnp.exp2(sink - m)
            o = (pv(e_p.astype(BF16), vp) + pv(e_c.astype(BF16), vc_ref[g, cur, :])
                 + pv(e_n.astype(BF16), vn) + pv(e_x.astype(BF16), vx))
            o_ref[g, j] = o / den


def _window(q, k, v, sink, n_ctx):
    b, hq, lt, dh = q.shape
    g = k.shape[1]
    blk = WINDOW
    nb = lt // blk
    rows = (hq // g) * blk
    n_ctx_blocks = n_ctx // blk
    assert n_ctx % (2 * blk) == 0 and nb % 2 == 0
    lo, hi = n_ctx_blocks, nb - 1
    pair_spec = lambda heads: pl.BlockSpec((None, heads, 2 * blk, dh), lambda bi, i: (bi, 0, i, 0))
    side_spec = lambda delta: pl.BlockSpec((None, g, blk, dh),
                                           lambda bi, i: (bi, 0, jnp.clip(2 * i + delta, lo, hi), 0))
    ctx_spec = pl.BlockSpec((None, g, n_ctx, dh), lambda bi, i: (bi, 0, 0, 0))
    return pl.pallas_call(
        functools.partial(_window_kernel, n_ctx_blocks=n_ctx_blocks, n_blocks=nb),
        out_shape=jax.ShapeDtypeStruct((b, g, nb, rows, dh), F32),
        grid=(b, nb // 2),
        in_specs=[pair_spec(hq), side_spec(-1), pair_spec(g), side_spec(2), side_spec(-1), pair_spec(g),
                  side_spec(2), ctx_spec, ctx_spec, pl.BlockSpec((g, rows, 1), lambda bi, i: (0, 0, 0))],
        out_specs=pl.BlockSpec((None, g, 2, rows, dh), lambda bi, i: (bi, 0, i, 0, 0)),
        compiler_params=_cparams(("parallel", "arbitrary")),
    )(q, k, k, k, v, v, v, k, v, sink)


def _retention_kernel(qf_ref, ktf_ref, vf_ref, qb_ref, ktb_ref, vb_ref, dmat_ref, xi_ref, zeta_ref, gch_ref,
                      of_ref, ob_ref, st_ref):
    t = pl.program_id(0)

    @pl.when(t == 0)
    def _():
        st_ref[...] = jnp.zeros_like(st_ref)

    for d, (q_ref, kt_ref, v_ref, o_ref) in enumerate(((qf_ref, ktf_ref, vf_ref, of_ref),
                                                       (qb_ref, ktb_ref, vb_ref, ob_ref))):
        for bi in range(q_ref.shape[0]):
            for hd in range(RET_HEADS):
                q = q_ref[bi, hd]
                kt = kt_ref[bi, hd]
                v = v_ref[bi, hd]
                st = st_ref[d, bi, hd]
                inner = _dot3(q, kt) * dmat_ref[d, hd]
                o_ref[bi, hd] = _dot3(inner, v) + _dot3(q, st) * xi_ref[d, hd]
                st_ref[d, bi, hd] = st * gch_ref[d, hd] + _dot3(kt * zeta_ref[d, hd], v)


def _retention_call(q, kt, v, dmat, xi, zeta, gch, n_ctx_chunks):
    b, hh, lt, dk = q.shape
    dv = v.shape[-1]
    c = RET_CHUNK
    nch = lt // c

    def back(ti):
        return jnp.where(ti < n_ctx_chunks, n_ctx_chunks - 1 - ti, nch - 1 - (ti - n_ctx_chunks))

    fwd = lambda ti: ti
    rows = lambda blk, w: pl.BlockSpec((b, hh, c, w), lambda ti: (0, 0, blk(ti), 0))
    cols = lambda blk: pl.BlockSpec((b, hh, dk, c), lambda ti: (0, 0, 0, blk(ti)))
    tab = lambda a: pl.BlockSpec(a.shape, lambda ti: (0,) * a.ndim)
    out = jax.ShapeDtypeStruct((b, hh, lt, dv), F32)
    return pl.pallas_call(
        _retention_kernel,
        out_shape=(out, out),
        grid=(nch,),
        in_specs=[rows(fwd, dk), cols(fwd), rows(fwd, dv), rows(back, dk), cols(back), rows(back, dv),
                  tab(dmat), tab(xi), tab(zeta), tab(gch)],
        out_specs=(rows(fwd, dv), rows(back, dv)),
        scratch_shapes=[pltpu.VMEM((2, b, hh, dk, dv), F32)],
        compiler_params=_cparams(("arbitrary",)),
    )(q, kt, v, q, kt, v, dmat, xi, zeta, gch)


def _layer_norm_rows(z, ln):
    mu = jnp.mean(z, axis=-1, keepdims=True)
    zc = z - mu
    var = jnp.mean(zc * zc, axis=-1, keepdims=True)
    return zc * lax.rsqrt(var + LN_EPS) * ln[0:1, :] + ln[1:2, :]


def _merge_kernel(fa_ref, fd0_ref, fd1_ref, win_ref, retf_ref, retb_ref, rg_ref, g_ref, h_ref, mod_ref, lam_ref,
                  subln_ref, rnorm_ref, wb_ref, wo_ref, ln_ref, o_ref, *, alpha):
    d = h_ref.shape[1]
    tb = h_ref.shape[0]
    hd = HEAD_DIM
    proj = functools.partial(jnp.dot, preferred_element_type=F32)
    gate = lambda i: g_ref[:, i * d:(i + 1) * d].astype(F32)

    oat = jnp.concatenate([fa_ref[g][:, st * tb:(st + 1) * tb] for g in range(A_KV_HEADS) for st in range(2)],
                          axis=0)
    m = gate(0) * proj(oat.T.astype(BF16), wb_ref[0])

    lam = lam_ref[...]
    heads = []
    for h4 in range(DIFF_HEADS):
        f = (fd0_ref if h4 < 2 else fd1_ref)[h4 % 2]
        o = f[:, :tb] - lam * f[:, tb:]
        heads.append(o * lax.rsqrt(jnp.mean(o * o, axis=0, keepdims=True) + RMS_EPS))
    obt = jnp.concatenate(heads, axis=0) * subln_ref[...]
    m = m + gate(1) * proj(obt.T.astype(BF16), wb_ref[1])

    half = tb // 2
    acc = None
    for h4 in range(WIN_HEADS):
        g, st = h4 // 2, h4 % 2
        o = jnp.concatenate([win_ref[g, 0][st * half:(st + 1) * half], win_ref[g, 1][st * half:(st + 1) * half]],
                            axis=0)
        t = proj(o.astype(BF16), wb_ref[2, h4 * hd:(h4 + 1) * hd, :])
        acc = t if acc is None else acc + t
    m = m + gate(2) * acc

    acc = None
    for h4 in range(RET_HEADS):
        cols = slice(h4 * RET_DV, (h4 + 1) * RET_DV)
        o = retf_ref[h4] + retb_ref[h4]
        mu = jnp.mean(o, axis=-1, keepdims=True)
        oc = o - mu
        var = jnp.mean(oc * oc, axis=-1, keepdims=True)
        on = oc * lax.rsqrt(var + LN_EPS) * rnorm_ref[0:1, cols] + rnorm_ref[1:2, cols]
        gt = rg_ref[:, cols]
        t = proj((on * (gt / (1.0 + jnp.exp(-gt)))).astype(BF16), wb_ref[3, cols, :])
        acc = t if acc is None else acc + t
    m = m + gate(3) * acc

    y = proj(m.astype(BF16), wo_ref[...])
    z = alpha * h_ref[...] + mod_ref[2:3, :] * y
    o_ref[...] = _layer_norm_rows(z, ln_ref[...])


def _merge(flash_out, win_out, ret_out, rg, gates, h, mod, group_of_block, lam, subln, rnorm, wb, wo, ln,
           alpha, b, lt):
    t, d = h.shape
    tb = TOKEN_BLOCK
    bpb = lt // tb
    row = lambda n: pl.BlockSpec((tb, n), lambda i: (i, 0))
    const = lambda a: pl.BlockSpec(a.shape, lambda i: (0,) * a.ndim)
    fspec = lambda gb: pl.BlockSpec(
        (None, 2, None, None) + flash_out.shape[4:],
        lambda i: (i // bpb, gb, (i % bpb + ATT_QTILES - 1) // ATT_QTILES, (i % bpb + ATT_QTILES - 1) % ATT_QTILES,
                   0, 0))
    ret_spec = pl.BlockSpec((None, RET_HEADS, tb, RET_DV), lambda i: (i // bpb, 0, i % bpb, 0))
    return pl.pallas_call(
        functools.partial(_merge_kernel, alpha=alpha),
        out_shape=jax.ShapeDtypeStruct((t, d), F32),
        grid=(t // tb,),
        in_specs=[fspec(0), fspec(1), fspec(2),
                  pl.BlockSpec((None, WIN_KV_HEADS, 2) + win_out.shape[3:], lambda i: (i // bpb, 0, i % bpb, 0, 0)),
                  ret_spec, ret_spec, row(rg.shape[1]), row(N_BRANCH * d), row(d),
                  pl.BlockSpec((None, 6, d), lambda i: (group_of_block(i), 0, 0)),
                  const(lam), const(subln), const(rnorm), const(wb), const(wo), const(ln)],
        out_specs=row(d),
        compiler_params=_cparams(("parallel",)),
    )(flash_out, flash_out, flash_out, win_out, *ret_out, rg, gates, h, mod, lam, subln, rnorm, wb, wo, ln)


def _top_rows(s, n, with_rank=False):
    out = []
    cur = s
    rank = jnp.full(s.shape, float(n), F32) if with_rank else None
    for r in range(n):
        mx = jnp.max(cur, axis=0, keepdims=True)
        out.append(mx)
        if with_rank:
            rank = jnp.where(cur == mx, float(r), rank)
        if r + 1 < n:
            cur = jnp.where(cur == mx, -jnp.inf, cur)
    return (out, rank) if with_rank else out


def _peer_route_kernel(h_ref, mod_ref, wh_ref, wl_ref, sk_ref, xt_ref, rk_ref, b1_ref, nn_ref, az_ref,
                       cand_ref):
    m = mod_ref[...]
    u = h_ref[...] * (1.0 + m[4:5, :]) + m[3:4, :]
    xt_ref[...] = u.T.astype(BF16)
    uh, ul = _split_bf16(u)
    d = functools.partial(jnp.dot, preferred_element_type=F32)
    nk = PEER_NK
    nt = (((1,), (1,)), ((), ()))
    k1 = PEER_TOPK + 1
    for hd in range(PEER_HEADS):
        c0 = 2 * hd * PEER_DQ
        wh = wh_ref[:, c0:c0 + 2 * PEER_DQ]
        wl = wl_ref[:, c0:c0 + 2 * PEER_DQ]
        q = d(uh, wh) + (d(ul, wh) + d(uh, wl))
        st = [_dot3(sk_ref[hd, p], q[:, p * PEER_DQ:(p + 1) * PEER_DQ], nt) for p in range(2)]
        top0 = _top_rows(st[0], k1)
        top1, rank1 = _top_rows(st[1], k1, with_rank=True)
        r = 0
        for p0 in range(k1):
            for p1 in range(k1 // (p0 + 1)):
                cand_ref[r:r + 1, :] = top0[p0] + top1[p1]
                r += 1
        cand_ref[r:, :] = jnp.full((cand_ref.shape[0] - r, cand_ref.shape[1]), -jnp.inf, F32)
        cand = cand_ref[...]
        ctop = _top_rows(cand, k1)
        tau = 0.5 * (ctop[PEER_TOPK - 1] + ctop[PEER_TOPK])
        mx = top0[0] + top1[0]
        z = jnp.sum(jnp.where(cand >= tau, jnp.exp(cand - mx), 0.0), axis=0, keepdims=True)
        th = tau - st[0]
        nn = jnp.zeros_like(th)
        for q in range(PEER_TOPK):
            nn = nn + jnp.where(top1[q] >= th, 1.0, 0.0)
        rk_ref[hd] = rank1.astype(BF16)
        b1_ref[hd] = jnp.exp(st[1] - top1[0]).astype(BF16)
        nn_ref[hd] = nn
        az_ref[hd] = jnp.exp(st[0] - top0[0]) / z


def _peer_route(h, mod, group_of_block, wq_hi, wq_lo, subkeys):
    t, d = h.shape
    tb = TOKEN_BLOCK
    hh, nk = PEER_HEADS, PEER_NK
    st_shape = lambda dt: jax.ShapeDtypeStruct((hh, nk, t), dt)
    st_spec = pl.BlockSpec((hh, nk, tb), lambda i: (0, 0, i))
    nq = wq_hi.shape[1]
    return pl.pallas_call(
        _peer_route_kernel,
        out_shape=(jax.ShapeDtypeStruct((d, t), BF16), st_shape(BF16), st_shape(BF16), st_shape(F32),
                   st_shape(F32)),
        grid=(t // tb,),
        in_specs=[pl.BlockSpec((tb, d), lambda i: (i, 0)),
                  pl.BlockSpec((None, 6, d), lambda i: (group_of_block(i), 0, 0)),
                  pl.BlockSpec((d, nq), lambda i: (0, 0)),
                  pl.BlockSpec((d, nq), lambda i: (0, 0)),
                  pl.BlockSpec((hh, 2, nk, PEER_DQ), lambda i: (0, 0, 0, 0))],
        out_specs=(pl.BlockSpec((d, tb), lambda i: (0, i)), st_spec, st_spec, st_spec, st_spec),
        scratch_shapes=[pltpu.VMEM((PEER_CAND_ROWS, tb), F32)],
        compiler_params=_cparams(("parallel",)),
    )(h, mod, wq_hi, wq_lo, subkeys)


GELU_K1 = -2.0 * math.sqrt(2.0 / math.pi) * LOG2E
GELU_K2 = GELU_K1 * 0.044715


def _gelu_tanh(x):
    return x / (1.0 + jnp.exp2(x * (GELU_K1 + GELU_K2 * (x * x))))


def _peer_dense_kernel(xt_ref, u_ref, vt_ref, rk_ref, b1_ref, nn_ref, az_ref, yt_ref, g_ref):
    c = pl.program_id(1)
    nk = PEER_NK
    tp = xt_ref.shape[1]
    rows_per_step = u_ref.shape[0] // nk
    n_chunks = pl.num_programs(1) - 1

    @pl.when(c == 0)
    def _():
        yt_ref[...] = jnp.zeros_like(yt_ref)
        g_ref[...] = jnp.zeros_like(g_ref)

    xt = xt_ref[...]
    pre = lambda ii: jnp.dot(u_ref[ii * nk:(ii + 1) * nk, :], xt, preferred_element_type=F32)
    act_next = pre(0)
    for ii in range(rows_per_step):
        if ii % 2 == 0:
            pair = slice(ii * nk, (ii + 2) * nk)
            yt_ref[...] += jnp.dot(vt_ref[:, pair], g_ref[pair, :], preferred_element_type=F32)
        act = act_next
        if ii + 1 < rows_per_step:
            act_next = pre(ii + 1)
        w = None
        tile = (nk // BF16_ROWS, BF16_ROWS, tp)
        row = lambda ref, hd: jnp.broadcast_to(ref[hd, ii:ii + 1, :], (BF16_ROWS, tp)).astype(BF16)[None]
        for hd in range(PEER_HEADS):
            t = jnp.where(rk_ref[hd].reshape(tile) < row(nn_ref, hd), b1_ref[hd].reshape(tile),
                          jnp.zeros((), BF16)) * row(az_ref, hd)
            w = t if w is None else w + t
        g_ref[ii * nk:(ii + 1) * nk, :] = w.reshape(nk, tp) * _gelu_tanh(act.astype(BF16))


def _peer_dense(xt, u, vt, rk, b1, nn, az):
    d, t = xt.shape
    n = u.shape[0]
    tp = PEER_TOKENS
    ec = PEER_EXPERTS
    nc = n // ec
    hh, nk = PEER_HEADS, PEER_NK
    st_spec = pl.BlockSpec((hh, nk, tp), lambda i, c: (0, 0, i))
    row_spec = pl.BlockSpec((hh, ec // nk, tp), lambda i, c: (0, jnp.minimum(c, nc - 1), i))
    return pl.pallas_call(
        _peer_dense_kernel,
        out_shape=jax.ShapeDtypeStruct((d, t), F32),
        grid=(t // tp, nc + 1),
        in_specs=[pl.BlockSpec((d, tp), lambda i, c: (0, i)),
                  pl.BlockSpec((ec, d), lambda i, c: (jnp.minimum(c, nc - 1), 0)),
                  pl.BlockSpec((d, ec), lambda i, c: (0, jnp.maximum(c - 1, 0))),
                  st_spec, st_spec, row_spec, row_spec],
        out_specs=pl.BlockSpec((d, tp), lambda i, c: (0, i)),
        scratch_shapes=[pltpu.VMEM((ec, tp), BF16)],
        compiler_params=_cparams(("parallel", "arbitrary")),
    )(xt, u, vt, rk, b1, nn, az)


def _resid_ln_kernel(h_ref, yt_ref, mod_ref, ln_ref, o_ref, *, alpha, gate_row):
    z = alpha * h_ref[...] + mod_ref[gate_row:gate_row + 1, :] * yt_ref[...].T
    o_ref[...] = _layer_norm_rows(z, ln_ref[...])


def _resid_ln(h, yt, mod, group_of_block, ln, alpha, gate_row):
    t, d = h.shape
    tb = TOKEN_BLOCK
    row = pl.BlockSpec((tb, d), lambda i: (i, 0))
    return pl.pallas_call(
        functools.partial(_resid_ln_kernel, alpha=alpha, gate_row=gate_row),
        out_shape=jax.ShapeDtypeStruct((t, d), F32),
        grid=(t // tb,),
        in_specs=[row, pl.BlockSpec((d, tb), lambda i: (0, i)),
                  pl.BlockSpec((None, 6, d), lambda i: (group_of_block(i), 0, 0)),
                  pl.BlockSpec((2, d), lambda i: (0, 0))],
        out_specs=row,
        compiler_params=_cparams(("parallel",)),
    )(h, yt, mod, ln)


def _axial_tables(s, n_ctx, d):
    rows = s // GRID_W
    row = jnp.broadcast_to(jnp.arange(rows, dtype=F32)[:, None], (rows, GRID_W)).reshape(-1)
    col = jnp.broadcast_to(jnp.arange(GRID_W, dtype=F32)[None, :], (rows, GRID_W)).reshape(-1)
    quarter = d // 4
    inv = ROPE_THETA ** (-jnp.arange(quarter, dtype=F32) / quarter)
    ar, ac = row[:, None] * inv, col[:, None] * inv
    cos = jnp.concatenate([jnp.cos(ar), jnp.cos(ar), jnp.cos(ac), jnp.cos(ac)], axis=-1)
    sin = jnp.concatenate([-jnp.sin(ar), jnp.sin(ar), -jnp.sin(ac), jnp.sin(ac)], axis=-1)
    cos = jnp.concatenate([jnp.ones((n_ctx, d), F32), cos], axis=0)
    sin = jnp.concatenate([jnp.zeros((n_ctx, d), F32), sin], axis=0)
    return cos, sin


def _rope1d_tables(lt, d):
    half = d // 2
    inv = ROPE_THETA ** (-jnp.arange(half, dtype=F32) / half)
    ang = jnp.arange(lt, dtype=F32)[:, None] * inv
    return (jnp.concatenate([jnp.cos(ang), jnp.cos(ang)], axis=-1),
            jnp.concatenate([-jnp.sin(ang), jnp.sin(ang)], axis=-1))


def _swap_perm():
    cols = []
    for piece, flip in ((0, 16), (1, 16), (3, 8), (4, 8), (6, 16), (7, 16), (9, 32), (10, 32)):
        base = np.arange(PIECE_OFF[piece], PIECE_OFF[piece + 1])
        cols.append(PIECE_OFF[piece] + ((base - PIECE_OFF[piece]) ^ flip))
    return np.concatenate(cols)


SWAP_COLS = _swap_perm()
SWAP_PIECES = (0, 1, 3, 4, 6, 7, 9, 10)
SWAP_OFF = dict(zip(SWAP_PIECES, np.cumsum([0] + [PIECES[p] for p in SWAP_PIECES])[:-1] + MIX_COLS))


def _rms_heads(x, xs, gain, gain_s):
    r = lax.rsqrt(jnp.mean(x * x, axis=-1, keepdims=True) + RMS_EPS)
    return x * r * gain, xs * r * gain_s


def kernel(x, c, ctx, c_ctx, w_mod, b_mod, w_in, qk_gain, diff_lambda, diff_subln, win_sink, ret_decay,
           ret_norm, w_branch, w_out, ln_attn, ln_ffn, peer_wq, peer_subkeys, peer_u, peer_v):
    b, s, d = x.shape
    n_ctx = ctx.shape[1]
    depth = w_mod.shape[0]
    lt = n_ctx + s
    t = b * lt
    tb = TOKEN_BLOCK
    assert n_ctx % tb == 0 and s % tb == 0 and t % PEER_TOKENS == 0
    alpha = (2 * depth) ** 0.25
    blocks_per_batch = lt // tb
    ctx_blocks = n_ctx // tb

    def group_of_block(i):
        return jnp.where(i % blocks_per_batch < ctx_blocks, b, i // blocks_per_batch)

    cos64, sin64 = _axial_tables(s, n_ctx, HEAD_DIM)
    cos32, sin32 = _axial_tables(s, n_ctx, DIFF_DIM)
    cos1d, sin1d = _rope1d_tables(lt, RET_DK)
    sc_a = HEAD_DIM ** -0.5 * LOG2E
    sc_d = DIFF_DIM ** -0.5 * LOG2E
    rep = lambda a, n: jnp.tile(a, (1, n))
    tok_part = lambda t64, t32, t1d: [rep(t64, 2), rep(t32, 8), rep(t64, 4) * sc_a, rep(t64, 2), rep(t1d, 4)]
    feat_part = lambda t64, t32, t1d: [rep(t64, 4) * sc_a, rep(t32, 8) * sc_d, rep(t1d, 4) * RET_DK ** -0.5]
    ttok = jnp.concatenate(tok_part(cos64, cos32, cos1d) + tok_part(sin64, sin32, sin1d), axis=1)
    tfeat = jnp.concatenate(feat_part(cos64, cos32, cos1d) + feat_part(sin64, sin32, sin1d), axis=1).T
    swap64 = np.arange(HEAD_DIM) ^ 16
    head_ids = np.arange(2 * HEAD_DIM) // HEAD_DIM
    avg = jnp.asarray((head_ids[:, None] == head_ids[None, :]) / HEAD_DIM, BF16)
    cond8 = jnp.zeros((8, d), F32).at[:b].set(jax.nn.silu(c)).at[b].set(jax.nn.silu(c_ctx))

    h = jnp.concatenate([ctx, x], axis=1).reshape(t, d)
    nqb = lt // ATT_TQ
    nwb = lt // WINDOW
    n_groups = A_KV_HEADS + DIFF_HEADS

    for l in range(depth):
        mod = _modulation(cond8, w_mod[l], b_mod[l]).reshape(8, 6, d)
        w_mix = w_in[l, :, :MIX_COLS]
        w_gate = w_in[l, :, MIX_COLS:].astype(BF16)
        g0, g1 = qk_gain[l, 0].astype(F32), qk_gain[l, 1].astype(F32)
        gtok = jnp.stack([jnp.tile(g1, 2), jnp.tile(g1[swap64], 2)])
        gfeat = jnp.broadcast_to(jnp.concatenate([jnp.tile(g0, A_HEADS), jnp.tile(g0[swap64], A_HEADS)])[:, None],
                                 (2 * A_HEADS * HEAD_DIM, tb))
        kall, qt, vt, wq, wk, wv, rq, rkt, rv, rg = _mixer_in(
            h, mod, group_of_block, w_mix[:, TOK_COLS].astype(BF16), w_mix[:, FEAT_COLS].T.astype(BF16),
            ttok, tfeat, gtok, gfeat, avg, b, lt)
        gates = _inproj(h, mod, group_of_block, w_gate, sigmoid=True, out_dtype=BF16)

        ot = _flash(qt, kall, vt, n_ctx)
        lam_init = 0.8 - 0.6 * math.exp(-0.3 * l)
        lp = diff_lambda[l].astype(F32)
        lam = (jnp.exp(jnp.sum(lp[0] * lp[1])) - jnp.exp(jnp.sum(lp[2] * lp[3])) + lam_init).reshape(1, 1)
        subln = jnp.broadcast_to((jnp.tile(diff_subln[l].astype(F32), DIFF_HEADS) * (1.0 - lam_init))[:, None],
                                 (DIFF_HEADS * HEAD_DIM, tb))

        sink = jnp.repeat(win_sink[l].astype(F32) * LOG2E, WINDOW).reshape(WIN_KV_HEADS, 2 * WINDOW, 1)
        ow = _window(wq, wk, wv, sink, n_ctx)

        lg =jax.nn.log_sigmoid(ret_decay[l].astype(F32))
        idx = jnp.arange(RET_CHUNK, dtype=F32)
        diff = idx[:, None] - idx[None, :]
        lg3 = lg[:, :, None, None]
        dm_f = jnp.exp(jnp.where(diff >= 0, diff * lg3[0], -jnp.inf))
        dm_b = jnp.exp(jnp.where(diff <= 0, -diff * lg3[1], -jnp.inf))
        dmat = jnp.stack([dm_f, dm_b])
        xi = jnp.stack([jnp.exp((idx + 1.0) * lg[0][:, None]), jnp.exp((RET_CHUNK - idx) * lg[1][:, None])])
        zeta = jnp.stack([jnp.exp((RET_CHUNK - 1.0 - idx) * lg[0][:, None]), jnp.exp(idx * lg[1][:, None])])
        gch = jnp.exp(RET_CHUNK * lg)
        o_ret = _retention_call(rq, rkt, rv, dmat, xi[..., None], zeta[:, :, None, :],
                                gch[:, :, None, None], n_ctx // RET_CHUNK)

        h = _merge(ot, ow, o_ret, rg, gates, h, mod, group_of_block, lam, subln, ret_norm[l].astype(F32),
                   w_branch[l].astype(BF16), w_out[l].astype(BF16), ln_attn[l], alpha, b, lt)

        wq_hi, wq_lo = _split_bf16(peer_wq[l])
        xt, rk, b1, nn, az = _peer_route(h, mod, group_of_block, wq_hi, wq_lo, peer_subkeys[l])
        yt = _peer_dense(xt, peer_u[l].astype(BF16), peer_v[l].T.astype(BF16), rk, b1, nn, az)
        h = _resid_ln(h, yt, mod, group_of_block, ln_ffn[l], alpha, 5)

    return h.reshape(b, lt, d)[:, n_ctx:, :]
```

```python
import functools
import math

import numpy as np
import jax
import jax.numpy as jnp
from jax import lax
from jax.experimental import pallas as pl
from jax.experimental.pallas import tpu as pltpu

GRID_W = 64
HEAD_DIM = 64
ROPE_THETA = 10000.0
A_HEADS = 4
A_KV_HEADS = 2
DIFF_HEADS = 4
DIFF_DIM = 32
WIN_HEADS = 4
WIN_KV_HEADS = 2
WINDOW = 128
RET_HEADS = 4
RET_DK = 64
RET_DV = 64
RET_CHUNK = 128
N_BRANCH = 4
PIECES = (
    A_HEADS * HEAD_DIM, A_KV_HEADS * HEAD_DIM, A_KV_HEADS * HEAD_DIM,
    2 * DIFF_HEADS * DIFF_DIM, 2 * DIFF_HEADS * DIFF_DIM, DIFF_HEADS * 2 * DIFF_DIM,
    WIN_HEADS * HEAD_DIM, WIN_KV_HEADS * HEAD_DIM, WIN_KV_HEADS * HEAD_DIM,
    RET_HEADS * RET_DK, RET_HEADS * RET_DK, RET_HEADS * RET_DV, RET_HEADS * RET_DV,
)
MIX_COLS = sum(PIECES)
PIECE_OFF = tuple(int(v) for v in np.cumsum((0,) + PIECES))
PEER_HEADS = 8
PEER_NK = 128
PEER_TOPK = 16
PEER_DQ = 128
PEER_CAND_ROWS = -(-sum((PEER_TOPK + 1) // (p + 1) for p in range(PEER_TOPK + 1)) // 8) * 8
LN_EPS = 1e-5
RMS_EPS = 1e-6
LOG2E = 1.4426950408889634

F32 = jnp.float32
BF16 = jnp.bfloat16

TOKEN_BLOCK = 256
ATT_TK = 1024
ATT_QTILES = 4
ATT_VPAD = 80
PEER_TOKENS = 512
PEER_EXPERTS = 1024
VMEM_LIMIT = 56 * 1024 * 1024
BF16_ROWS = 16


def _cparams(sem):
    return pltpu.CompilerParams(dimension_semantics=sem, vmem_limit_bytes=VMEM_LIMIT)


def _split_bf16(a):
    hi = a.astype(BF16)
    lo = (a - hi.astype(F32)).astype(BF16)
    return hi, lo


def _dot3(a, b, dims=(((1,), (0,)), ((), ()))):
    ah, al = _split_bf16(a)
    bh, bl = _split_bf16(b)
    d = functools.partial(lax.dot_general, dimension_numbers=dims, preferred_element_type=F32)
    return d(ah, bh) + (d(al, bh) + d(ah, bl))


def _mod_kernel(c_ref, w_ref, b_ref, o_ref):
    o_ref[...] = _dot3(c_ref[...], w_ref[...]) + b_ref[...]


def _modulation(cond8, w, b):
    d, n = w.shape
    tn = 1536
    return pl.pallas_call(
        _mod_kernel,
        out_shape=jax.ShapeDtypeStruct((8, n), F32),
        grid=(n // tn,),
        in_specs=[pl.BlockSpec((8, d), lambda j: (0, 0)),
                  pl.BlockSpec((d, tn), lambda j: (0, j)),
                  pl.BlockSpec((1, tn), lambda j: (0, j))],
        out_specs=pl.BlockSpec((8, tn), lambda j: (0, j)),
        compiler_params=_cparams(("arbitrary",)),
    )(cond8, w, b.reshape(1, n))


def _gates_kernel(x_ref, mod_ref, w_ref, o_ref, *, chunk):
    m = mod_ref[...]
    xm = (x_ref[...] * (1.0 + m[1:2, :]) + m[0:1, :]).astype(BF16)
    n = w_ref.shape[1]
    for j in range(n // chunk):
        acc = jnp.dot(xm, w_ref[:, j * chunk:(j + 1) * chunk], preferred_element_type=F32)
        o_ref[:, j * chunk:(j + 1) * chunk] = jax.nn.sigmoid(acc).astype(o_ref.dtype)


def _gates(h, mod, group_of_block, w):
    t, d = h.shape
    n = w.shape[1]
    tb = TOKEN_BLOCK
    return pl.pallas_call(
        functools.partial(_gates_kernel, chunk=512),
        out_shape=jax.ShapeDtypeStruct((t, n), BF16),
        grid=(t // tb,),
        in_specs=[pl.BlockSpec((tb, d), lambda i: (i, 0)),
                  pl.BlockSpec((None, 6, d), lambda i: (group_of_block(i), 0, 0)),
                  pl.BlockSpec((d, n), lambda i: (0, 0))],
        out_specs=pl.BlockSpec((tb, n), lambda i: (i, 0)),
        compiler_params=_cparams(("parallel",)),
    )(h, mod, w)


TOK_PIECES = ((1, True), (4, True), (6, True), (7, True), (8, False), (9, True), (11, False), (12, False))
FEAT_PIECES = ((0, True), (3, True), (2, False), (5, False), (10, True))


def _piece_cols(pieces):
    flips = {0: 16, 1: 16, 3: 8, 4: 8, 6: 16, 7: 16, 9: 32, 10: 32}
    cols, offs = [], {}
    n = 0
    for p, rotary in pieces:
        base = np.arange(PIECE_OFF[p], PIECE_OFF[p + 1])
        offs[p] = n
        cols.append(base)
        n += len(base)
        if rotary:
            cols.append(PIECE_OFF[p] + ((base - PIECE_OFF[p]) ^ flips[p]))
            n += len(base)
    return np.concatenate(cols), offs


TOK_COLS, TOK_OFF = _piece_cols(TOK_PIECES)
FEAT_COLS, FEAT_OFF = _piece_cols(FEAT_PIECES)
TTOK_OFF = {1: 0, 4: 128, 6: 384, 7: 640, 9: 768}
TTOK_W = 1024
TFEAT_OFF = {0: 0, 3: 256, 10: 512}
TFEAT_W = 768


def _mixer_in_kernel(x_ref, mod_ref, wt_ref, wf_ref, ttok_ref, tfeat_ref, gtok_ref, gfeat_ref, avg_ref,
                     kall_ref, qt_ref, vt_ref, wq_ref, wk_ref, wv_ref, rq_ref, rkt_ref, rv_ref, rg_ref):
    m = mod_ref[...]
    u = x_ref[...] * (1.0 + m[1:2, :]) + m[0:1, :]
    xm = u.astype(BF16)
    xmt = u.T.astype(BF16)
    tb = xm.shape[0]

    def tok(p, swapped=False):
        a = TOK_OFF[p] + (PIECES[p] if swapped else 0)
        return jnp.dot(xm, wt_ref[:, a:a + PIECES[p]], preferred_element_type=F32)

    def feat(p, swapped=False):
        a = FEAT_OFF[p] + (PIECES[p] if swapped else 0)
        return jnp.dot(wf_ref[a:a + PIECES[p], :], xmt, preferred_element_type=F32)

    def rope_tok(p, x, xs):
        a = TTOK_OFF[p]
        return x * ttok_ref[:, a:a + PIECES[p]] + xs * ttok_ref[:, TTOK_W + a:TTOK_W + a + PIECES[p]]

    def rope_feat(p, x, xs):
        a = TFEAT_OFF[p]
        return x * tfeat_ref[a:a + PIECES[p], :] + xs * tfeat_ref[TFEAT_W + a:TFEAT_W + a + PIECES[p], :]

    x, xs = tok(1), tok(1, True)
    sq_hi, sq_lo = _split_bf16(x * x)
    avg = avg_ref[...]
    ms = jnp.dot(sq_hi, avg, preferred_element_type=F32) + jnp.dot(sq_lo, avg, preferred_element_type=F32)
    r = lax.rsqrt(ms + RMS_EPS)
    ka = r * rope_tok(1, x * gtok_ref[0:1, :], xs * gtok_ref[1:2, :])
    kd = rope_tok(4, tok(4), tok(4, True))
    kall_ref[:, :PIECES[1]] = ka.astype(BF16)
    kall_ref[:, PIECES[1]:] = kd.astype(BF16)

    def split_heads(ref, val):
        for hd in range(ref.shape[0]):
            ref[hd] = val[:, hd * HEAD_DIM:(hd + 1) * HEAD_DIM].astype(ref.dtype)

    split_heads(wq_ref, rope_tok(6, tok(6), tok(6, True)))
    split_heads(wk_ref, rope_tok(7, tok(7), tok(7, True)))
    split_heads(wv_ref, tok(8))
    split_heads(rq_ref, rope_tok(9, tok(9), tok(9, True)))
    split_heads(rv_ref, tok(11))
    rg_ref[...] = tok(12)
    rkt = rope_feat(10, feat(10), feat(10, True))
    for hd in range(RET_HEADS):
        rkt_ref[hd] = rkt[hd * RET_DK:(hd + 1) * RET_DK]

    zero64 = jnp.zeros((HEAD_DIM, tb), F32)
    zero32 = jnp.zeros((DIFF_DIM, tb), F32)

    def place(q, upper):
        return jnp.concatenate([zero64, q] if upper else [q, zero64], axis=0).astype(BF16)

    xq, xqs = feat(0), feat(0, True)
    for hd in range(A_HEADS):
        rows = slice(hd * HEAD_DIM, (hd + 1) * HEAD_DIM)
        xh = xq[rows]
        rh = lax.rsqrt(jnp.mean(xh * xh, axis=0, keepdims=True) + RMS_EPS)
        swapped_rows = slice(A_HEADS * HEAD_DIM + hd * HEAD_DIM, A_HEADS * HEAD_DIM + (hd + 1) * HEAD_DIM)
        qh = rh * (xh * gfeat_ref[rows, :] * tfeat_ref[rows, :]
                   + xqs[rows] * gfeat_ref[swapped_rows, :]
                   * tfeat_ref[TFEAT_W + hd * HEAD_DIM:TFEAT_W + (hd + 1) * HEAD_DIM, :])
        qt_ref[hd // 2, hd % 2] = place(qh, hd // 2 == 1)
    qd = rope_feat(3, feat(3), feat(3, True))
    for hd in range(DIFF_HEADS):
        qh = qd[hd * HEAD_DIM:(hd + 1) * HEAD_DIM]
        q1 = jnp.concatenate([qh[:DIFF_DIM], zero32], axis=0)
        q2 = jnp.concatenate([zero32, qh[DIFF_DIM:]], axis=0)
        qt_ref[A_KV_HEADS + hd, 0] = place(q1, hd % 2 == 1)
        qt_ref[A_KV_HEADS + hd, 1] = place(q2, hd % 2 == 1)

    pad_rows = vt_ref.shape[1] - HEAD_DIM
    tail = (lax.broadcasted_iota(jnp.int32, (pad_rows, tb), 0) == 0).astype(F32)
    va, vd = feat(2), feat(5)
    for g in range(A_KV_HEADS + DIFF_HEADS):
        v = va[g * HEAD_DIM:(g + 1) * HEAD_DIM] if g < A_KV_HEADS else \
            vd[(g - A_KV_HEADS) * HEAD_DIM:(g - A_KV_HEADS + 1) * HEAD_DIM]
        vt_ref[g] = jnp.concatenate([v, tail], axis=0).astype(BF16)


def _mixer_in(h, mod, group_of_block, wt, wf, ttok, tfeat, gtok, gfeat, avg, b, lt):
    t, d = h.shape
    tb = TOKEN_BLOCK
    bpb = lt // tb
    n_groups = A_KV_HEADS + DIFF_HEADS
    kw = PIECES[1] + PIECES[4]
    row = lambda w: pl.BlockSpec((tb, w), lambda i: (i, 0))
    const = lambda a: pl.BlockSpec(a.shape, lambda i: (0,) * a.ndim)
    heads_shape = lambda n, dt: jax.ShapeDtypeStruct((b, n, lt, HEAD_DIM), dt)
    heads_spec = lambda n: pl.BlockSpec((None, n, tb, HEAD_DIM), lambda i: (i // bpb, 0, i % bpb, 0))
    return pl.pallas_call(
        _mixer_in_kernel,
        out_shape=(jax.ShapeDtypeStruct((b, lt, kw), BF16),
                   jax.ShapeDtypeStruct((b, n_groups, bpb, 2, 2 * HEAD_DIM, tb), BF16),
                   jax.ShapeDtypeStruct((b, n_groups, bpb, ATT_VPAD, tb), BF16),
                   heads_shape(WIN_HEADS, BF16), heads_shape(WIN_KV_HEADS, BF16), heads_shape(WIN_KV_HEADS, BF16),
                   heads_shape(RET_HEADS, F32), jax.ShapeDtypeStruct((b, RET_HEADS, RET_DK, lt), F32),
                   heads_shape(RET_HEADS, F32), jax.ShapeDtypeStruct((t, PIECES[12]), F32)),
        grid=(t // tb,),
        in_specs=[row(d),
                  pl.BlockSpec((None, 6, d), lambda i: (group_of_block(i), 0, 0)),
                  const(wt), const(wf),
                  pl.BlockSpec((tb, 2 * TTOK_W), lambda i: (i % bpb, 0)),
                  pl.BlockSpec((2 * TFEAT_W, tb), lambda i: (0, i % bpb)),
                  const(gtok), const(gfeat), const(avg)],
        out_specs=(pl.BlockSpec((None, tb, kw), lambda i: (i // bpb, i % bpb, 0)),
                   pl.BlockSpec((None, n_groups, None, 2, 2 * HEAD_DIM, tb),
                                lambda i: (i // bpb, 0, i % bpb, 0, 0, 0)),
                   pl.BlockSpec((None, n_groups, None, ATT_VPAD, tb), lambda i: (i // bpb, 0, i % bpb, 0, 0)),
                   heads_spec(WIN_HEADS), heads_spec(WIN_KV_HEADS), heads_spec(WIN_KV_HEADS),
                   heads_spec(RET_HEADS),
                   pl.BlockSpec((None, RET_HEADS, RET_DK, tb), lambda i: (i // bpb, 0, 0, i % bpb)),
                   heads_spec(RET_HEADS), row(PIECES[12])),
        compiler_params=_cparams(("parallel",)),
    )(h, mod, wt, wf, ttok, tfeat, gtok, gfeat, avg)


def _flash_kernel(*refs, n_ctx, tk, nt):
    q_refs, (k_ref, vt_ref, o_ref, s_ref, smax_ref, m_ref, acc_ref) = refs[:nt], refs[nt:]
    i = pl.program_id(2)
    tq = q_refs[0].shape[2]
    qt = jnp.concatenate([q[st] for st in range(2) for q in q_refs], axis=1)
    tile = vt_ref.shape[2]
    nlc = (k_ref.shape[0] - n_ctx) // tk

    def scores(row0, rows):
        return jnp.dot(k_ref[pl.ds(row0, rows), :], qt, preferred_element_type=F32)

    def stage(slot, chunk):
        s = scores(lat_row(chunk), tk)
        s_ref[slot] = s
        smax_ref[slot] = jnp.max(s, axis=0, keepdims=True)

    def absorb(s, smax, tile0):
        m = m_ref[...]
        m_new = jnp.maximum(m, smax)
        p = jnp.exp2(s - m_new).astype(BF16)
        acc = acc_ref[...] * jnp.exp2(m - m_new)
        for j in range(s.shape[0] // tile):
            acc = acc + jnp.dot(vt_ref[tile0 + j], p[j * tile:(j + 1) * tile], preferred_element_type=F32)
        m_ref[...] = m_new
        acc_ref[...] = acc

    m_ref[...] = jnp.full(m_ref.shape, -jnp.inf, F32)
    acc_ref[...] = jnp.zeros(acc_ref.shape, F32)
    s_ctx = scores(0, n_ctx)
    absorb(s_ctx, jnp.max(s_ctx, axis=0, keepdims=True), 0)
    lat_row = lambda c: pl.multiple_of(n_ctx + c * tk, tile)
    lat_tile = lambda c: (n_ctx + c * tk) // tile

    @pl.when(i > 0)
    def _():
        stage(0, 0)

        def pair(j, carry):
            c0 = 2 * j
            stage(1, c0 + 1)
            absorb(s_ref[0], smax_ref[0], lat_tile(c0))
            stage(0, c0 + 2)
            absorb(s_ref[1], smax_ref[1], lat_tile(c0 + 1))
            return carry

        lax.fori_loop(0, nlc // 2 - 1, pair, 0)
        stage(1, nlc - 1)
        absorb(s_ref[0], smax_ref[0], lat_tile(nlc - 2))
        absorb(s_ref[1], smax_ref[1], lat_tile(nlc - 1))

    acc = acc_ref[...]
    o = acc[:HEAD_DIM] * (1.0 / acc[HEAD_DIM:HEAD_DIM + 1])
    for j in range(nt):
        o_ref[j] = jnp.concatenate([o[:, j * tq:(j + 1) * tq], o[:, (nt + j) * tq:(nt + j + 1) * tq]], axis=1)


def _flash(qt, k_all, vt, n_ctx):
    b, g, tiles, _, dk, tq = qt.shape
    nt = ATT_QTILES
    lt = k_all.shape[1]
    s = lt - n_ctx
    tk = next(c for c in (ATT_TK, 512, 256) if s % (2 * c) == 0)
    assert n_ctx == tq and tk % tq == 0 and (tiles - 1) % nt == 0
    steps = 1 + (tiles - 1) // nt
    key_block = lambda gi: jnp.where(gi < A_KV_HEADS, 0, 1 + (gi - A_KV_HEADS) // 2)
    qspec = lambda j: pl.BlockSpec(
        (None, None, None, 2, dk, tq),
        lambda bi, gi, i: (bi, gi, jnp.where(i == 0, 0, nt * i - (nt - 1) + j), 0, 0, 0))
    return pl.pallas_call(
        functools.partial(_flash_kernel, n_ctx=n_ctx, tk=tk, nt=nt),
        out_shape=jax.ShapeDtypeStruct((b, g, steps, nt, HEAD_DIM, 2 * tq), F32),
        grid=(b, g, steps),
        in_specs=[qspec(j) for j in range(nt)]
        + [pl.BlockSpec((None, lt, dk), lambda bi, gi, i: (bi, 0, key_block(gi))),
           pl.BlockSpec((None, None) + vt.shape[2:], lambda bi, gi, i: (bi, gi, 0, 0, 0))],
        out_specs=pl.BlockSpec((None, None, None, nt, HEAD_DIM, 2 * tq), lambda bi, gi, i: (bi, gi, i, 0, 0, 0)),
        scratch_shapes=[pltpu.VMEM((2, tk, 2 * nt * tq), F32), pltpu.VMEM((2, 1, 2 * nt * tq), F32),
                        pltpu.VMEM((1, 2 * nt * tq), F32), pltpu.VMEM((vt.shape[3], 2 * nt * tq), F32)],
        compiler_params=_cparams(("parallel", "parallel", "arbitrary")),
    )(*([qt] * nt), k_all, vt)


def _window_kernel(q_ref, kp_ref, kc_ref, kn_ref, vp_ref, vc_ref, vn_ref, kx_ref, vx_ref, sink_ref,
                   o_ref, *, n_ctx_blocks, n_blocks):
    step = pl.program_id(1)
    n_kv = kc_ref.shape[0]
    group = q_ref.shape[0] // n_kv
    rows = group * WINDOW
    nt = (((1,), (1,)), ((), ()))
    sdot = functools.partial(lax.dot_general, dimension_numbers=nt, preferred_element_type=F32)
    pv = functools.partial(jnp.dot, preferred_element_type=F32)
    qi = lax.broadcasted_iota(jnp.int32, (rows, WINDOW), 0) % WINDOW
    kj = lax.broadcasted_iota(jnp.int32, (rows, WINDOW), 1)
    neg = -jnp.inf
    for g in range(n_kv):
        sink = sink_ref[g]
        kx, vx = kx_ref[g], vx_ref[g]
        for j in range(2):
            qb = 2 * step + j
            cur = slice(j * WINDOW, (j + 1) * WINDOW)
            q = q_ref[g * group:(g + 1) * group, cur, :].reshape(rows, HEAD_DIM)
            if j == 0:
                kp, vp, kn, vn = kp_ref[g], vp_ref[g], kc_ref[g, WINDOW:, :], vc_ref[g, WINDOW:, :]
            else:
                kp, vp, kn, vn = kc_ref[g, :WINDOW, :], vc_ref[g, :WINDOW, :], kn_ref[g], vn_ref[g]
            off_p = jnp.where(qb >= n_ctx_blocks + 1, 0, 2 * WINDOW)
            off_c = jnp.where(qb >= n_ctx_blocks, 0, 2 * WINDOW)
            off_n = jnp.where(jnp.logical_and(qb >= n_ctx_blocks, qb <= n_blocks - 2), 0, 2 * WINDOW)
            s_p = jnp.where(kj >= qi + off_p, sdot(q, kp), neg)
            s_c = jnp.where(kj >= off_c, sdot(q, kc_ref[g, cur, :]), neg)
            s_n = jnp.where(kj <= qi - off_n, sdot(q, kn), neg)
            s_x = sdot(q, kx)
            rmax = lambda s: jnp.max(s, axis=1, keepdims=True)
            m = jnp.maximum(jnp.maximum(jnp.maximum(rmax(s_p), rmax(s_c)), jnp.maximum(rmax(s_n), rmax(s_x))), sink)
            e_p, e_c, e_n, e_x = (jnp.exp2(s - m) for s in (s_p, s_c, s_n, s_x))
            rsum = lambda e: jnp.sum(e, axis=1, keepdims=True)
            den = rsum(e_p) + rsum(e_c) + rsum(e_n) + rsum(e_x) + jnp.exp2(sink - m)
            o = (pv(e_p.astype(BF16), vp) + pv(e_c.astype(BF16), vc_ref[g, cur, :])
                 + pv(e_n.astype(BF16), vn) + pv(e_x.astype(BF16), vx))
            o_ref[g, j] = o / den


def _window(q, k, v, sink, n_ctx):
    b, hq, lt, dh = q.shape
    g = k.shape[1]
    blk = WINDOW
    nb = lt // blk
    rows = (hq // g) * blk
    n_ctx_blocks = n_ctx // blk
    assert n_ctx % (2 * blk) == 0 and nb % 2 == 0
    lo, hi = n_ctx_blocks, nb - 1
    pair_spec = lambda heads: pl.BlockSpec((None, heads, 2 * blk, dh), lambda bi, i: (bi, 0, i, 0))
    side_spec = lambda delta: pl.BlockSpec((None, g, blk, dh),
                                           lambda bi, i: (bi, 0, jnp.clip(2 * i + delta, lo, hi), 0))
    ctx_spec = pl.BlockSpec((None, g, n_ctx, dh), lambda bi, i: (bi, 0, 0, 0))
    return pl.pallas_call(
        functools.partial(_window_kernel, n_ctx_blocks=n_ctx_blocks, n_blocks=nb),
        out_shape=jax.ShapeDtypeStruct((b, g, nb, rows, dh), F32),
        grid=(b, nb // 2),
        in_specs=[pair_spec(hq), side_spec(-1), pair_spec(g), side_spec(2), side_spec(-1), pair_spec(g),
                  side_spec(2), ctx_spec, ctx_spec, pl.BlockSpec((g, rows, 1), lambda bi, i: (0, 0, 0))],
        out_specs=pl.BlockSpec((None, g, 2, rows, dh), lambda bi, i: (bi, 0, i, 0, 0)),
        compiler_params=_cparams(("parallel", "arbitrary")),
    )(q, k, k, k, v, v, v, k, v, sink)


def _retention_kernel(qf_ref, ktf_ref, vf_ref, qb_ref, ktb_ref, vb_ref, dmat_ref, xi_ref, zeta_ref, gch_ref,
                      of_ref, ob_ref, st_ref):
    t = pl.program_id(0)

    @pl.when(t == 0)
    def _():
        st_ref[...] = jnp.zeros_like(st_ref)

    for d, (q_ref, kt_ref, v_ref, o_ref) in enumerate(((qf_ref, ktf_ref, vf_ref, of_ref),
                                                       (qb_ref, ktb_ref, vb_ref, ob_ref))):
        for bi in range(q_ref.shape[0]):
            for hd in range(RET_HEADS):
                q = q_ref[bi, hd]
                kt = kt_ref[bi, hd]
                v = v_ref[bi, hd]
                st = st_ref[d, bi, hd]
                inner = _dot3(q, kt) * dmat_ref[d, hd]
                o_ref[bi, hd] = _dot3(inner, v) + _dot3(q, st) * xi_ref[d, hd]
                st_ref[d, bi, hd] = st * gch_ref[d, hd] + _dot3(kt * zeta_ref[d, hd], v)


def _retention_call(q, kt, v, dmat, xi, zeta, gch, n_ctx_chunks):
    b, hh, lt, dk = q.shape
    dv = v.shape[-1]
    c = RET_CHUNK
    nch = lt // c

    def back(ti):
        return jnp.where(ti < n_ctx_chunks, n_ctx_chunks - 1 - ti, nch - 1 - (ti - n_ctx_chunks))

    fwd = lambda ti: ti
    rows = lambda blk, w: pl.BlockSpec((b, hh, c, w), lambda ti: (0, 0, blk(ti), 0))
    cols = lambda blk: pl.BlockSpec((b, hh, dk, c), lambda ti: (0, 0, 0, blk(ti)))
    tab = lambda a: pl.BlockSpec(a.shape, lambda ti: (0,) * a.ndim)
    out = jax.ShapeDtypeStruct((b, hh, lt, dv), F32)
    return pl.pallas_call(
        _retention_kernel,
        out_shape=(out, out),
        grid=(nch,),
        in_specs=[rows(fwd, dk), cols(fwd), rows(fwd, dv), rows(back, dk), cols(back), rows(back, dv),
                  tab(dmat), tab(xi), tab(zeta), tab(gch)],
        out_specs=(rows(fwd, dv), rows(back, dv)),
        scratch_shapes=[pltpu.VMEM((2, b, hh, dk, dv), F32)],
        compiler_params=_cparams(("arbitrary",)),
    )(q, kt, v, q, kt, v, dmat, xi, zeta, gch)


def _layer_norm_rows(z, ln):
    mu = jnp.mean(z, axis=-1, keepdims=True)
    zc = z - mu
    var = jnp.mean(zc * zc, axis=-1, keepdims=True)
    return zc * lax.rsqrt(var + LN_EPS) * ln[0:1, :] + ln[1:2, :]


def _merge_kernel(fa_ref, fd0_ref, fd1_ref, win_ref, retf_ref, retb_ref, rg_ref, g_ref, h_ref, mod_ref, lam_ref,
                  subln_ref, rnorm_ref, wb_ref, wo_ref, ln_ref, o_ref, *, alpha):
    d = h_ref.shape[1]
    tb = h_ref.shape[0]
    hd = HEAD_DIM
    proj = functools.partial(jnp.dot, preferred_element_type=F32)
    gate = lambda i: g_ref[:, i * d:(i + 1) * d].astype(F32)

    oat = jnp.concatenate([fa_ref[g][:, st * tb:(st + 1) * tb] for g in range(A_KV_HEADS) for st in range(2)],
                          axis=0)
    m = gate(0) * proj(oat.T.astype(BF16), wb_ref[0])

    lam = lam_ref[...]
    heads = []
    for h4 in range(DIFF_HEADS):
        f = (fd0_ref if h4 < 2 else fd1_ref)[h4 % 2]
        o = f[:, :tb] - lam * f[:, tb:]
        heads.append(o * lax.rsqrt(jnp.mean(o * o, axis=0, keepdims=True) + RMS_EPS))
    obt = jnp.concatenate(heads, axis=0) * subln_ref[...]
    m = m + gate(1) * proj(obt.T.astype(BF16), wb_ref[1])

    half = tb // 2
    acc = None
    for h4 in range(WIN_HEADS):
        g, st = h4 // 2, h4 % 2
        o = jnp.concatenate([win_ref[g, 0][st * half:(st + 1) * half], win_ref[g, 1][st * half:(st + 1) * half]],
                            axis=0)
        t = proj(o.astype(BF16), wb_ref[2, h4 * hd:(h4 + 1) * hd, :])
        acc = t if acc is None else acc + t
    m = m + gate(2) * acc

    acc = None
    for h4 in range(RET_HEADS):
        cols = slice(h4 * RET_DV, (h4 + 1) * RET_DV)
        o = retf_ref[h4] + retb_ref[h4]
        mu = jnp.mean(o, axis=-1, keepdims=True)
        oc = o - mu
        var = jnp.mean(oc * oc, axis=-1, keepdims=True)
        on = oc * lax.rsqrt(var + LN_EPS) * rnorm_ref[0:1, cols] + rnorm_ref[1:2, cols]
        gt = rg_ref[:, cols]
        t = proj((on * (gt / (1.0 + jnp.exp(-gt)))).astype(BF16), wb_ref[3, cols, :])
        acc = t if acc is None else acc + t
    m = m + gate(3) * acc

    y = proj(m.astype(BF16), wo_ref[...])
    z = alpha * h_ref[...] + mod_ref[2:3, :] * y
    o_ref[...] = _layer_norm_rows(z, ln_ref[...])


def _merge(flash_out, win_out, ret_out, rg, gates, h, mod, group_of_block, lam, subln, rnorm, wb, wo, ln,
           alpha, b, lt):
    t, d = h.shape
    tb = TOKEN_BLOCK
    bpb = lt // tb
    row = lambda n: pl.BlockSpec((tb, n), lambda i: (i, 0))
    const = lambda a: pl.BlockSpec(a.shape, lambda i: (0,) * a.ndim)
    fspec = lambda gb: pl.BlockSpec(
        (None, 2, None, None) + flash_out.shape[4:],
        lambda i: (i // bpb, gb, (i % bpb + ATT_QTILES - 1) // ATT_QTILES, (i % bpb + ATT_QTILES - 1) % ATT_QTILES,
                   0, 0))
    ret_spec = pl.BlockSpec((None, RET_HEADS, tb, RET_DV), lambda i: (i // bpb, 0, i % bpb, 0))
    return pl.pallas_call(
        functools.partial(_merge_kernel, alpha=alpha),
        out_shape=jax.ShapeDtypeStruct((t, d), F32),
        grid=(t // tb,),
        in_specs=[fspec(0), fspec(1), fspec(2),
                  pl.BlockSpec((None, WIN_KV_HEADS, 2) + win_out.shape[3:], lambda i: (i // bpb, 0, i % bpb, 0, 0)),
                  ret_spec, ret_spec, row(rg.shape[1]), row(N_BRANCH * d), row(d),
                  pl.BlockSpec((None, 6, d), lambda i: (group_of_block(i), 0, 0)),
                  const(lam), const(subln), const(rnorm), const(wb), const(wo), const(ln)],
        out_specs=row(d),
        compiler_params=_cparams(("parallel",)),
    )(flash_out, flash_out, flash_out, win_out, *ret_out, rg, gates, h, mod, lam, subln, rnorm, wb, wo, ln)


def _top_rows(s, n, with_rank=False):
    out = []
    cur = s
    rank = jnp.full(s.shape, float(n), F32) if with_rank else None
    for r in range(n):
        mx = jnp.max(cur, axis=0, keepdims=True)
        out.append(mx)
        if with_rank:
            rank = jnp.where(cur == mx, float(r), rank)
        if r + 1 < n:
            cur = jnp.where(cur == mx, -jnp.inf, cur)
    return (out, rank) if with_rank else out


def _peer_route_kernel(h_ref, mod_ref, wh_ref, wl_ref, sk_ref, xt_ref, rk_ref, b1_ref, nn_ref, az_ref,
                       cand_ref):
    m = mod_ref[...]
    u = h_ref[...] * (1.0 + m[4:5, :]) + m[3:4, :]
    xt_ref[...] = u.T.astype(BF16)
    uh, ul = _split_bf16(u)
    d = functools.partial(jnp.dot, preferred_element_type=F32)
    nk = PEER_NK
    nt = (((1,), (1,)), ((), ()))
    k1 = PEER_TOPK + 1
    for hd in range(PEER_HEADS):
        c0 = 2 * hd * PEER_DQ
        wh = wh_ref[:, c0:c0 + 2 * PEER_DQ]
        wl = wl_ref[:, c0:c0 + 2 * PEER_DQ]
        q = d(uh, wh) + (d(ul, wh) + d(uh, wl))
        st = [_dot3(sk_ref[hd, p], q[:, p * PEER_DQ:(p + 1) * PEER_DQ], nt) for p in range(2)]
        top0 = _top_rows(st[0], k1)
        top1, rank1 = _top_rows(st[1], k1, with_rank=True)
        r = 0
        for p0 in range(k1):
            for p1 in range(k1 // (p0 + 1)):
                cand_ref[r:r + 1, :] = top0[p0] + top1[p1]
                r += 1
        cand_ref[r:, :] = jnp.full((cand_ref.shape[0] - r, cand_ref.shape[1]), -jnp.inf, F32)
        cand = cand_ref[...]
        ctop = _top_rows(cand, k1)
        tau = 0.5 * (ctop[PEER_TOPK - 1] + ctop[PEER_TOPK])
        mx = top0[0] + top1[0]
        z = jnp.sum(jnp.where(cand >= tau, jnp.exp(cand - mx), 0.0), axis=0, keepdims=True)
        th = tau - st[0]
        nn = jnp.zeros_like(th)
        for q in range(PEER_TOPK):
            nn = nn + jnp.where(top1[q] >= th, 1.0, 0.0)
        rk_ref[hd] = rank1.astype(BF16)
        b1_ref[hd] = jnp.exp(st[1] - top1[0]).astype(BF16)
        nn_ref[hd] = nn
        az_ref[hd] = jnp.exp(st[0] - top0[0]) / z


def _peer_route(h, mod, group_of_block, wq_hi, wq_lo, subkeys):
    t, d = h.shape
    tb = TOKEN_BLOCK
    hh, nk = PEER_HEADS, PEER_NK
    st_shape = lambda dt: jax.ShapeDtypeStruct((hh, nk, t), dt)
    st_spec = pl.BlockSpec((hh, nk, tb), lambda i: (0, 0, i))
    nq = wq_hi.shape[1]
    return pl.pallas_call(
        _peer_route_kernel,
        out_shape=(jax.ShapeDtypeStruct((d, t), BF16), st_shape(BF16), st_shape(BF16), st_shape(F32),
                   st_shape(F32)),
        grid=(t // tb,),
        in_specs=[pl.BlockSpec((tb, d), lambda i: (i, 0)),
                  pl.BlockSpec((None, 6, d), lambda i: (group_of_block(i), 0, 0)),
                  pl.BlockSpec((d, nq), lambda i: (0, 0)),
                  pl.BlockSpec((d, nq), lambda i: (0, 0)),
                  pl.BlockSpec((hh, 2, nk, PEER_DQ), lambda i: (0, 0, 0, 0))],
        out_specs=(pl.BlockSpec((d, tb), lambda i: (0, i)), st_spec, st_spec, st_spec, st_spec),
        scratch_shapes=[pltpu.VMEM((PEER_CAND_ROWS, tb), F32)],
        compiler_params=_cparams(("parallel",)),
    )(h, mod, wq_hi, wq_lo, subkeys)


GELU_K1 = -2.0 * math.sqrt(2.0 / math.pi) * LOG2E
GELU_K2 = GELU_K1 * 0.044715


def _gelu_tanh(x):
    return x / (1.0 + jnp.exp2(x * (GELU_K1 + GELU_K2 * (x * x))))


def _peer_dense_kernel(xt_ref, u_ref, vt_ref, rk_ref, b1_ref, nn_ref, az_ref, yt_ref, g_ref):
    c = pl.program_id(1)
    nk = PEER_NK
    tp = xt_ref.shape[1]
    rows_per_step = u_ref.shape[0] // nk
    n_chunks = pl.num_programs(1) - 1

    @pl.when(c == 0)
    def _():
        yt_ref[...] = jnp.zeros_like(yt_ref)
        g_ref[...] = jnp.zeros_like(g_ref)

    xt = xt_ref[...]
    pre = lambda ii: jnp.dot(u_ref[ii * nk:(ii + 1) * nk, :], xt, preferred_element_type=F32)
    yt_ref[...] += jnp.dot(vt_ref[...], g_ref[...], preferred_element_type=F32)
    act_next = pre(0)
    for ii in range(rows_per_step):
        act = act_next
        if ii + 1 < rows_per_step:
            act_next = pre(ii + 1)
        w = None
        tile = (nk // BF16_ROWS, BF16_ROWS, tp)
        row = lambda ref, hd: jnp.broadcast_to(ref[hd, ii:ii + 1, :], (BF16_ROWS, tp)).astype(BF16)[None]
        for hd in range(PEER_HEADS):
            t = jnp.where(rk_ref[hd].reshape(tile) < row(nn_ref, hd), b1_ref[hd].reshape(tile),
                          jnp.zeros((), BF16)) * row(az_ref, hd)
            w = t if w is None else w + t
        g_ref[ii * nk:(ii + 1) * nk, :] = w.reshape(nk, tp) * _gelu_tanh(act.astype(BF16))


def _peer_dense(xt, u, vt, rk, b1, nn, az):
    d, t = xt.shape
    n = u.shape[0]
    tp = PEER_TOKENS
    ec = PEER_EXPERTS
    nc = n // ec
    hh, nk = PEER_HEADS, PEER_NK
    st_spec = pl.BlockSpec((hh, nk, tp), lambda i, c: (0, 0, i))
    row_spec = pl.BlockSpec((hh, ec // nk, tp), lambda i, c: (0, jnp.minimum(c, nc - 1), i))
    return pl.pallas_call(
        _peer_dense_kernel,
        out_shape=jax.ShapeDtypeStruct((d, t), F32),
        grid=(t // tp, nc + 1),
        in_specs=[pl.BlockSpec((d, tp), lambda i, c: (0, i)),
                  pl.BlockSpec((ec, d), lambda i, c: (jnp.minimum(c, nc - 1), 0)),
                  pl.BlockSpec((d, ec), lambda i, c: (0, jnp.maximum(c - 1, 0))),
                  st_spec, st_spec, row_spec, row_spec],
        out_specs=pl.BlockSpec((d, tp), lambda i, c: (0, i)),
        scratch_shapes=[pltpu.VMEM((ec, tp), BF16)],
        compiler_params=_cparams(("parallel", "arbitrary")),
    )(xt, u, vt, rk, b1, nn, az)


def _resid_ln_kernel(h_ref, yt_ref, mod_ref, ln_ref, o_ref, *, alpha, gate_row):
    z = alpha * h_ref[...] + mod_ref[gate_row:gate_row + 1, :] * yt_ref[...].T
    o_ref[...] = _layer_norm_rows(z, ln_ref[...])


def _resid_ln(h, yt, mod, group_of_block, ln, alpha, gate_row):
    t, d = h.shape
    tb = TOKEN_BLOCK
    row = pl.BlockSpec((tb, d), lambda i: (i, 0))
    return pl.pallas_call(
        functools.partial(_resid_ln_kernel, alpha=alpha, gate_row=gate_row),
        out_shape=jax.ShapeDtypeStruct((t, d), F32),
        grid=(t // tb,),
        in_specs=[row, pl.BlockSpec((d, tb), lambda i: (0, i)),
                  pl.BlockSpec((None, 6, d), lambda i: (group_of_block(i), 0, 0)),
                  pl.BlockSpec((2, d), lambda i: (0, 0))],
        out_specs=row,
        compiler_params=_cparams(("parallel",)),
    )(h, yt, mod, ln)


def _axial_tables(s, n_ctx, d):
    rows = s // GRID_W
    row = jnp.broadcast_to(jnp.arange(rows, dtype=F32)[:, None], (rows, GRID_W)).reshape(-1)
    col = jnp.broadcast_to(jnp.arange(GRID_W, dtype=F32)[None, :], (rows, GRID_W)).reshape(-1)
    quarter = d // 4
    inv = ROPE_THETA ** (-jnp.arange(quarter, dtype=F32) / quarter)
    ar, ac = row[:, None] * inv, col[:, None] * inv
    cos = jnp.concatenate([jnp.cos(ar), jnp.cos(ar), jnp.cos(ac), jnp.cos(ac)], axis=-1)
    sin = jnp.concatenate([-jnp.sin(ar), jnp.sin(ar), -jnp.sin(ac), jnp.sin(ac)], axis=-1)
    cos = jnp.concatenate([jnp.ones((n_ctx, d), F32), cos], axis=0)
    sin = jnp.concatenate([jnp.zeros((n_ctx, d), F32), sin], axis=0)
    return cos, sin


def _rope1d_tables(lt, d):
    half = d // 2
    inv = ROPE_THETA ** (-jnp.arange(half, dtype=F32) / half)
    ang = jnp.arange(lt, dtype=F32)[:, None] * inv
    return (jnp.concatenate([jnp.cos(ang), jnp.cos(ang)], axis=-1),
            jnp.concatenate([-jnp.sin(ang), jnp.sin(ang)], axis=-1))


def kernel(x, c, ctx, c_ctx, w_mod, b_mod, w_in, qk_gain, diff_lambda, diff_subln, win_sink, ret_decay,
           ret_norm, w_branch, w_out, ln_attn, ln_ffn, peer_wq, peer_subkeys, peer_u, peer_v):
    b, s, d = x.shape
    n_ctx = ctx.shape[1]
    depth = w_mod.shape[0]
    lt = n_ctx + s
    t = b * lt
    tb = TOKEN_BLOCK
    assert n_ctx % tb == 0 and s % tb == 0 and t % PEER_TOKENS == 0
    alpha = (2 * depth) ** 0.25
    blocks_per_batch = lt // tb
    ctx_blocks = n_ctx // tb

    def group_of_block(i):
        return jnp.where(i % blocks_per_batch < ctx_blocks, b, i // blocks_per_batch)

    cos64, sin64 = _axial_tables(s, n_ctx, HEAD_DIM)
    cos32, sin32 = _axial_tables(s, n_ctx, DIFF_DIM)
    cos1d, sin1d = _rope1d_tables(lt, RET_DK)
    sc_a = HEAD_DIM ** -0.5 * LOG2E
    sc_d = DIFF_DIM ** -0.5 * LOG2E
    rep = lambda a, n: jnp.tile(a, (1, n))
    tok_part = lambda t64, t32, t1d: [rep(t64, 2), rep(t32, 8), rep(t64, 4) * sc_a, rep(t64, 2), rep(t1d, 4)]
    feat_part = lambda t64, t32, t1d: [rep(t64, 4) * sc_a, rep(t32, 8) * sc_d, rep(t1d, 4) * RET_DK ** -0.5]
    ttok = jnp.concatenate(tok_part(cos64, cos32, cos1d) + tok_part(sin64, sin32, sin1d), axis=1)
    tfeat = jnp.concatenate(feat_part(cos64, cos32, cos1d) + feat_part(sin64, sin32, sin1d), axis=1).T
    swap64 = np.arange(HEAD_DIM) ^ 16
    head_ids = np.arange(2 * HEAD_DIM) // HEAD_DIM
    avg = jnp.asarray((head_ids[:, None] == head_ids[None, :]) / HEAD_DIM, BF16)
    cond8 = jnp.zeros((8, d), F32).at[:b].set(jax.nn.silu(c)).at[b].set(jax.nn.silu(c_ctx))

    h = jnp.concatenate([ctx, x], axis=1).reshape(t, d)

    for l in range(depth):
        mod = _modulation(cond8, w_mod[l], b_mod[l]).reshape(8, 6, d)
        w_mix = w_in[l, :, :MIX_COLS]
        w_gate = w_in[l, :, MIX_COLS:].astype(BF16)
        g0, g1 = qk_gain[l, 0].astype(F32), qk_gain[l, 1].astype(F32)
        gtok = jnp.stack([jnp.tile(g1, 2), jnp.tile(g1[swap64], 2)])
        gfeat = jnp.broadcast_to(jnp.concatenate([jnp.tile(g0, A_HEADS), jnp.tile(g0[swap64], A_HEADS)])[:, None],
                                 (2 * A_HEADS * HEAD_DIM, tb))
        kall, qt, vt, wq, wk, wv, rq, rkt, rv, rg = _mixer_in(
            h, mod, group_of_block, w_mix[:, TOK_COLS].astype(BF16), w_mix[:, FEAT_COLS].T.astype(BF16),
            ttok, tfeat, gtok, gfeat, avg, b, lt)
        gates = _gates(h, mod, group_of_block, w_gate)

        ot = _flash(qt, kall, vt, n_ctx)
        lam_init = 0.8 - 0.6 * math.exp(-0.3 * l)
        lp = diff_lambda[l].astype(F32)
        lam = (jnp.exp(jnp.sum(lp[0] * lp[1])) - jnp.exp(jnp.sum(lp[2] * lp[3])) + lam_init).reshape(1, 1)
        subln = jnp.broadcast_to((jnp.tile(diff_subln[l].astype(F32), DIFF_HEADS) * (1.0 - lam_init))[:, None],
                                 (DIFF_HEADS * HEAD_DIM, tb))

        sink = jnp.repeat(win_sink[l].astype(F32) * LOG2E, WINDOW).reshape(WIN_KV_HEADS, 2 * WINDOW, 1)
        ow = _window(wq, wk, wv, sink, n_ctx)

        lg = jax.nn.log_sigmoid(ret_decay[l].astype(F32))
        idx = jnp.arange(RET_CHUNK, dtype=F32)
        diff = idx[:, None] - idx[None, :]
        lg3 = lg[:, :, None, None]
        dm_f = jnp.exp(jnp.where(diff >= 0, diff * lg3[0], -jnp.inf))
        dm_b = jnp.exp(jnp.where(diff <= 0, -diff * lg3[1], -jnp.inf))
        dmat = jnp.stack([dm_f, dm_b])
        xi = jnp.stack([jnp.exp((idx + 1.0) * lg[0][:, None]), jnp.exp((RET_CHUNK - idx) * lg[1][:, None])])
        zeta = jnp.stack([jnp.exp((RET_CHUNK - 1.0 - idx) * lg[0][:, None]), jnp.exp(idx * lg[1][:, None])])
        gch = jnp.exp(RET_CHUNK * lg)
        o_ret = _retention_call(rq, rkt, rv, dmat, xi[..., None], zeta[:, :, None, :],
                                gch[:, :, None, None], n_ctx // RET_CHUNK)

        h = _merge(ot, ow, o_ret, rg, gates, h, mod, group_of_block, lam, subln, ret_norm[l].astype(F32),
                   w_branch[l].astype(BF16), w_out[l].astype(BF16), ln_attn[l], alpha, b, lt)

        wq_hi, wq_lo = _split_bf16(peer_wq[l])
        xt, rk, b1, nn, az = _peer_route(h, mod, group_of_block, wq_hi, wq_lo, peer_subkeys[l])
        yt = _peer_dense(xt, peer_u[l].astype(BF16), peer_v[l].T.astype(BF16), rk, b1, nn, az)
        h = _resid_ln(h, yt, mod, group_of_block, ln_ffn[l], alpha, 5)

    return h.reshape(b, lt, d)[:, n_ctx:, :]
```

```python
import functools
import math

import numpy as np
import jax
import jax.numpy as jnp
from jax import lax
from jax.experimental import pallas as pl
from jax.experimental.pallas import tpu as pltpu

GRID_W = 64
HEAD_DIM = 64
ROPE_THETA = 10000.0
A_HEADS = 4
A_KV_HEADS = 2
DIFF_HEADS = 4
DIFF_DIM = 32
WIN_HEADS = 4
WIN_KV_HEADS = 2
WINDOW = 128
RET_HEADS = 4
RET_DK = 64
RET_DV = 64
RET_CHUNK = 128
N_BRANCH = 4
PIECES = (
    A_HEADS * HEAD_DIM, A_KV_HEADS * HEAD_DIM, A_KV_HEADS * HEAD_DIM,
    2 * DIFF_HEADS * DIFF_DIM, 2 * DIFF_HEADS * DIFF_DIM, DIFF_HEADS * 2 * DIFF_DIM,
    WIN_HEADS * HEAD_DIM, WIN_KV_HEADS * HEAD_DIM, WIN_KV_HEADS * HEAD_DIM,
    RET_HEADS * RET_DK, RET_HEADS * RET_DK, RET_HEADS * RET_DV, RET_HEADS * RET_DV,
)
MIX_COLS = sum(PIECES)
PIECE_OFF = tuple(int(v) for v in np.cumsum((0,) + PIECES))
PEER_HEADS = 8
PEER_NK = 128
PEER_TOPK = 16
PEER_DQ = 128
PEER_CAND_ROWS = -(-sum((PEER_TOPK + 1) // (p + 1) for p in range(PEER_TOPK + 1)) // 8) * 8
LN_EPS = 1e-5
RMS_EPS = 1e-6
LOG2E = 1.4426950408889634

F32 = jnp.float32
BF16 = jnp.bfloat16

TOKEN_BLOCK = 256
ATT_TK = 1024
ATT_QTILES = 4
ATT_VPAD = 80
PEER_TOKENS = 512
PEER_EXPERTS = 1024
VMEM_LIMIT = 56 * 1024 * 1024
BF16_ROWS = 16


def _cparams(sem):
    return pltpu.CompilerParams(dimension_semantics=sem, vmem_limit_bytes=VMEM_LIMIT)


def _split_bf16(a):
    hi = a.astype(BF16)
    lo = (a - hi.astype(F32)).astype(BF16)
    return hi, lo


def _dot3(a, b, dims=(((1,), (0,)), ((), ()))):
    ah, al = _split_bf16(a)
    bh, bl = _split_bf16(b)
    d = functools.partial(lax.dot_general, dimension_numbers=dims, preferred_element_type=F32)
    return d(ah, bh) + (d(al, bh) + d(ah, bl))


def _mod_kernel(c_ref, w_ref, b_ref, o_ref):
    o_ref[...] = _dot3(c_ref[...], w_ref[...]) + b_ref[...]


def _modulation(cond8, w, b):
    d, n = w.shape
    tn = 1536
    return pl.pallas_call(
        _mod_kernel,
        out_shape=jax.ShapeDtypeStruct((8, n), F32),
        grid=(n // tn,),
        in_specs=[pl.BlockSpec((8, d), lambda j: (0, 0)),
                  pl.BlockSpec((d, tn), lambda j: (0, j)),
                  pl.BlockSpec((1, tn), lambda j: (0, j))],
        out_specs=pl.BlockSpec((8, tn), lambda j: (0, j)),
        compiler_params=_cparams(("arbitrary",)),
    )(cond8, w, b.reshape(1, n))


def _gates_kernel(x_ref, mod_ref, w_ref, o_ref, *, chunk):
    m = mod_ref[...]
    xm = (x_ref[...] * (1.0 + m[1:2, :]) + m[0:1, :]).astype(BF16)
    n = w_ref.shape[1]
    for j in range(n // chunk):
        acc = jnp.dot(xm, w_ref[:, j * chunk:(j + 1) * chunk], preferred_element_type=F32)
        o_ref[:, j * chunk:(j + 1) * chunk] = jax.nn.sigmoid(acc).astype(o_ref.dtype)


def _gates(h, mod, group_of_block, w):
    t, d = h.shape
    n = w.shape[1]
    tb = TOKEN_BLOCK
    return pl.pallas_call(
        functools.partial(_gates_kernel, chunk=512),
        out_shape=jax.ShapeDtypeStruct((t, n), BF16),
        grid=(t // tb,),
        in_specs=[pl.BlockSpec((tb, d), lambda i: (i, 0)),
                  pl.BlockSpec((None, 6, d), lambda i: (group_of_block(i), 0, 0)),
                  pl.BlockSpec((d, n), lambda i: (0, 0))],
        out_specs=pl.BlockSpec((tb, n), lambda i: (i, 0)),
        compiler_params=_cparams(("parallel",)),
    )(h, mod, w)


TOK_PIECES = ((1, True), (4, True), (6, True), (7, True), (8, False), (9, True), (11, False), (12, False))
FEAT_PIECES = ((0, True), (3, True), (2, False), (5, False), (10, True))


def _piece_cols(pieces):
    flips = {0: 16, 1: 16, 3: 8, 4: 8, 6: 16, 7: 16, 9: 32, 10: 32}
    cols, offs = [], {}
    n = 0
    for p, rotary in pieces:
        base = np.arange(PIECE_OFF[p], PIECE_OFF[p + 1])
        offs[p] = n
        cols.append(base)
        n += len(base)
        if rotary:
            cols.append(PIECE_OFF[p] + ((base - PIECE_OFF[p]) ^ flips[p]))
            n += len(base)
    return np.concatenate(cols), offs


TOK_COLS, TOK_OFF = _piece_cols(TOK_PIECES)
FEAT_COLS, FEAT_OFF = _piece_cols(FEAT_PIECES)
TTOK_OFF = {1: 0, 4: 128, 6: 384, 7: 640, 9: 768}
TTOK_W = 1024
TFEAT_OFF = {0: 0, 3: 256, 10: 512}
TFEAT_W = 768


def _mixer_in_kernel(x_ref, mod_ref, wt_ref, wf_ref, ttok_ref, tfeat_ref, gtok_ref, gfeat_ref, avg_ref,
                     kall_ref, qt_ref, vt_ref, wq_ref, wk_ref, wv_ref, rq_ref, rkt_ref, rv_ref, rg_ref):
    m = mod_ref[...]
    u = x_ref[...] * (1.0 + m[1:2, :]) + m[0:1, :]
    xm = u.astype(BF16)
    xmt = u.T.astype(BF16)
    tb = xm.shape[0]

    def tok(p, swapped=False):
        a = TOK_OFF[p] + (PIECES[p] if swapped else 0)
        return jnp.dot(xm, wt_ref[:, a:a + PIECES[p]], preferred_element_type=F32)

    def feat(p, swapped=False):
        a = FEAT_OFF[p] + (PIECES[p] if swapped else 0)
        return jnp.dot(wf_ref[a:a + PIECES[p], :], xmt, preferred_element_type=F32)

    def rope_tok(p, x, xs):
        a = TTOK_OFF[p]
        return x * ttok_ref[:, a:a + PIECES[p]] + xs * ttok_ref[:, TTOK_W + a:TTOK_W + a + PIECES[p]]

    def rope_feat(p, x, xs):
        a = TFEAT_OFF[p]
        return x * tfeat_ref[a:a + PIECES[p], :] + xs * tfeat_ref[TFEAT_W + a:TFEAT_W + a + PIECES[p], :]

    x, xs = tok(1), tok(1, True)
    sq_hi, sq_lo = _split_bf16(x * x)
    avg = avg_ref[...]
    ms = jnp.dot(sq_hi, avg, preferred_element_type=F32) + jnp.dot(sq_lo, avg, preferred_element_type=F32)
    r = lax.rsqrt(ms + RMS_EPS)
    ka = r * rope_tok(1, x * gtok_ref[0:1, :], xs * gtok_ref[1:2, :])
    kd = rope_tok(4, tok(4), tok(4, True))
    kall_ref[:, :PIECES[1]] = ka.astype(BF16)
    kall_ref[:, PIECES[1]:] = kd.astype(BF16)

    def split_heads(ref, val):
        for hd in range(ref.shape[0]):
            ref[hd] = val[:, hd * HEAD_DIM:(hd + 1) * HEAD_DIM].astype(ref.dtype)

    split_heads(wq_ref, rope_tok(6, tok(6), tok(6, True)))
    split_heads(wk_ref, rope_tok(7, tok(7), tok(7, True)))
    split_heads(wv_ref, tok(8))
    split_heads(rq_ref, rope_tok(9, tok(9), tok(9, True)))
    split_heads(rv_ref, tok(11))
    rg_ref[...] = tok(12)
    rkt = rope_feat(10, feat(10), feat(10, True))
    for hd in range(RET_HEADS):
        rkt_ref[hd] = rkt[hd * RET_DK:(hd + 1) * RET_DK]

    zero64 = jnp.zeros((HEAD_DIM, tb), F32)
    zero32 = jnp.zeros((DIFF_DIM, tb), F32)

    def place(q, upper):
        return jnp.concatenate([zero64, q] if upper else [q, zero64], axis=0).astype(BF16)

    xq, xqs = feat(0), feat(0, True)
    for hd in range(A_HEADS):
        rows = slice(hd * HEAD_DIM, (hd + 1) * HEAD_DIM)
        xh = xq[rows]
        rh = lax.rsqrt(jnp.mean(xh * xh, axis=0, keepdims=True) + RMS_EPS)
        swapped_rows = slice(A_HEADS * HEAD_DIM + hd * HEAD_DIM, A_HEADS * HEAD_DIM + (hd + 1) * HEAD_DIM)
        qh = rh * (xh * gfeat_ref[rows, :] * tfeat_ref[rows, :]
                   + xqs[rows] * gfeat_ref[swapped_rows, :]
                   * tfeat_ref[TFEAT_W + hd * HEAD_DIM:TFEAT_W + (hd + 1) * HEAD_DIM, :])
        qt_ref[hd // 2, hd % 2] = place(qh, hd // 2 == 1)
    qd = rope_feat(3, feat(3), feat(3, True))
    for hd in range(DIFF_HEADS):
        qh = qd[hd * HEAD_DIM:(hd + 1) * HEAD_DIM]
        q1 = jnp.concatenate([qh[:DIFF_DIM], zero32], axis=0)
        q2 = jnp.concatenate([zero32, qh[DIFF_DIM:]], axis=0)
        qt_ref[A_KV_HEADS + hd, 0] = place(q1, hd % 2 == 1)
        qt_ref[A_KV_HEADS + hd, 1] = place(q2, hd % 2 == 1)

    pad_rows = vt_ref.shape[1] - HEAD_DIM
    tail = (lax.broadcasted_iota(jnp.int32, (pad_rows, tb), 0) == 0).astype(F32)
    va, vd = feat(2), feat(5)
    for g in range(A_KV_HEADS + DIFF_HEADS):
        v = va[g * HEAD_DIM:(g + 1) * HEAD_DIM] if g < A_KV_HEADS else \
            vd[(g - A_KV_HEADS) * HEAD_DIM:(g - A_KV_HEADS + 1) * HEAD_DIM]
        vt_ref[g] = jnp.concatenate([v, tail], axis=0).astype(BF16)


def _mixer_in(h, mod, group_of_block, wt, wf, ttok, tfeat, gtok, gfeat, avg, b, lt):
    t, d = h.shape
    tb = TOKEN_BLOCK
    bpb = lt // tb
    n_groups = A_KV_HEADS + DIFF_HEADS
    kw = PIECES[1] + PIECES[4]
    row = lambda w: pl.BlockSpec((tb, w), lambda i: (i, 0))
    const = lambda a: pl.BlockSpec(a.shape, lambda i: (0,) * a.ndim)
    heads_shape = lambda n, dt: jax.ShapeDtypeStruct((b, n, lt, HEAD_DIM), dt)
    heads_spec = lambda n: pl.BlockSpec((None, n, tb, HEAD_DIM), lambda i: (i // bpb, 0, i % bpb, 0))
    return pl.pallas_call(
        _mixer_in_kernel,
        out_shape=(jax.ShapeDtypeStruct((b, lt, kw), BF16),
                   jax.ShapeDtypeStruct((b, n_groups, bpb, 2, 2 * HEAD_DIM, tb), BF16),
                   jax.ShapeDtypeStruct((b, n_groups, bpb, ATT_VPAD, tb), BF16),
                   heads_shape(WIN_HEADS, BF16), heads_shape(WIN_KV_HEADS, BF16), heads_shape(WIN_KV_HEADS, BF16),
                   heads_shape(RET_HEADS, F32), jax.ShapeDtypeStruct((b, RET_HEADS, RET_DK, lt), F32),
                   heads_shape(RET_HEADS, F32), jax.ShapeDtypeStruct((t, PIECES[12]), F32)),
        grid=(t // tb,),
        in_specs=[row(d),
                  pl.BlockSpec((None, 6, d), lambda i: (group_of_block(i), 0, 0)),
                  const(wt), const(wf),
                  pl.BlockSpec((tb, 2 * TTOK_W), lambda i: (i % bpb, 0)),
                  pl.BlockSpec((2 * TFEAT_W, tb), lambda i: (0, i % bpb)),
                  const(gtok), const(gfeat), const(avg)],
        out_specs=(pl.BlockSpec((None, tb, kw), lambda i: (i // bpb, i % bpb, 0)),
                   pl.BlockSpec((None, n_groups, None, 2, 2 * HEAD_DIM, tb),
                                lambda i: (i // bpb, 0, i % bpb, 0, 0, 0)),
                   pl.BlockSpec((None, n_groups, None, ATT_VPAD, tb), lambda i: (i // bpb, 0, i % bpb, 0, 0)),
                   heads_spec(WIN_HEADS), heads_spec(WIN_KV_HEADS), heads_spec(WIN_KV_HEADS),
                   heads_spec(RET_HEADS),
                   pl.BlockSpec((None, RET_HEADS, RET_DK, tb), lambda i: (i // bpb, 0, 0, i % bpb)),
                   heads_spec(RET_HEADS), row(PIECES[12])),
        compiler_params=_cparams(("parallel",)),
    )(h, mod, wt, wf, ttok, tfeat, gtok, gfeat, avg)


def _flash_kernel(*refs, n_ctx, tk, nt):
    q_refs, (k_ref, vt_ref, o_ref, s_ref, smax_ref, m_ref, acc_ref) = refs[:nt], refs[nt:]
    i = pl.program_id(2)
    tq = q_refs[0].shape[2]
    qt = jnp.concatenate([q[st] for st in range(2) for q in q_refs], axis=1)
    tile = vt_ref.shape[2]
    nlc = (k_ref.shape[0] - n_ctx) // tk

    def scores(row0, rows):
        return jnp.dot(k_ref[pl.ds(row0, rows), :], qt, preferred_element_type=F32)

    def stage(slot, chunk):
        s = scores(lat_row(chunk), tk)
        s_ref[slot] = s
        smax_ref[slot] = jnp.max(s, axis=0, keepdims=True)

    def absorb(s, smax, tile0):
        m = m_ref[...]
        m_new = jnp.maximum(m, smax)
        p = jnp.exp2(s - m_new).astype(BF16)
        acc = acc_ref[...] * jnp.exp2(m - m_new)
        for j in range(s.shape[0] // tile):
            acc = acc + jnp.dot(vt_ref[tile0 + j], p[j * tile:(j + 1) * tile], preferred_element_type=F32)
        m_ref[...] = m_new
        acc_ref[...] = acc

    m_ref[...] = jnp.full(m_ref.shape, -jnp.inf, F32)
    acc_ref[...] = jnp.zeros(acc_ref.shape, F32)
    s_ctx = scores(0, n_ctx)
    absorb(s_ctx, jnp.max(s_ctx, axis=0, keepdims=True), 0)
    lat_row = lambda c: pl.multiple_of(n_ctx + c * tk, tile)
    lat_tile = lambda c: (n_ctx + c * tk) // tile

    @pl.when(i > 0)
    def _():
        stage(0, 0)

        def pair(j, carry):
            c0 = 2 * j
            stage(1, c0 + 1)
            absorb(s_ref[0], smax_ref[0], lat_tile(c0))
            stage(0, c0 + 2)
            absorb(s_ref[1], smax_ref[1], lat_tile(c0 + 1))
            return carry

        lax.fori_loop(0, nlc // 2 - 1, pair, 0)
        stage(1, nlc - 1)
        absorb(s_ref[0], smax_ref[0], lat_tile(nlc - 2))
        absorb(s_ref[1], smax_ref[1], lat_tile(nlc - 1))

    acc = acc_ref[...]
    o = acc[:HEAD_DIM] * (1.0 / acc[HEAD_DIM:HEAD_DIM + 1])
    for j in range(nt):
        o_ref[j] = jnp.concatenate([o[:, j * tq:(j + 1) * tq], o[:, (nt + j) * tq:(nt + j + 1) * tq]], axis=1)


def _flash(qt, k_all, vt, n_ctx):
    b, g, tiles, _, dk, tq = qt.shape
    nt = ATT_QTILES
    lt = k_all.shape[1]
    s = lt - n_ctx
    tk = next(c for c in (ATT_TK, 512, 256) if s % (2 * c) == 0)
    assert n_ctx == tq and tk % tq == 0 and (tiles - 1) % nt == 0
    steps = 1 + (tiles - 1) // nt
    key_block = lambda gi: jnp.where(gi < A_KV_HEADS, 0, 1 + (gi - A_KV_HEADS) // 2)
    qspec = lambda j: pl.BlockSpec(
        (None, None, None, 2, dk, tq),
        lambda bi, gi, i: (bi, gi, jnp.where(i == 0, 0, nt * i - (nt - 1) + j), 0, 0, 0))
    return pl.pallas_call(
        functools.partial(_flash_kernel, n_ctx=n_ctx, tk=tk, nt=nt),
        out_shape=jax.ShapeDtypeStruct((b, g, steps, nt, HEAD_DIM, 2 * tq), F32),
        grid=(b, g, steps),
        in_specs=[qspec(j) for j in range(nt)]
        + [pl.BlockSpec((None, lt, dk), lambda bi, gi, i: (bi, 0, key_block(gi))),
           pl.BlockSpec((None, None) + vt.shape[2:], lambda bi, gi, i: (bi, gi, 0, 0, 0))],
        out_specs=pl.BlockSpec((None, None, None, nt, HEAD_DIM, 2 * tq), lambda bi, gi, i: (bi, gi, i, 0, 0, 0)),
        scratch_shapes=[pltpu.VMEM((2, tk, 2 * nt * tq), F32), pltpu.VMEM((2, 1, 2 * nt * tq), F32),
                        pltpu.VMEM((1, 2 * nt * tq), F32), pltpu.VMEM((vt.shape[3], 2 * nt * tq), F32)],
        compiler_params=_cparams(("parallel", "parallel", "arbitrary")),
    )(*([qt] * nt), k_all, vt)


def _window_kernel(q_ref, kp_ref, kc_ref, kn_ref, vp_ref, vc_ref, vn_ref, kx_ref, vx_ref, sink_ref,
                   o_ref, *, n_ctx_blocks, n_blocks):
    step = pl.program_id(1)
    n_kv = kc_ref.shape[0]
    group = q_ref.shape[0] // n_kv
    rows = group * WINDOW
    nt = (((1,), (1,)), ((), ()))
    sdot = functools.partial(lax.dot_general, dimension_numbers=nt, preferred_element_type=F32)
    pv = functools.partial(jnp.dot, preferred_element_type=F32)
    qi = lax.broadcasted_iota(jnp.int32, (rows, WINDOW), 0) % WINDOW
    kj = lax.broadcasted_iota(jnp.int32, (rows, WINDOW), 1)
    neg = -jnp.inf
    for g in range(n_kv):
        sink = sink_ref[g]
        kx, vx = kx_ref[g], vx_ref[g]
        for j in range(2):
            qb = 2 * step + j
            cur = slice(j * WINDOW, (j + 1) * WINDOW)
            q = q_ref[g * group:(g + 1) * group, cur, :].reshape(rows, HEAD_DIM)
            if j == 0:
                kp, vp, kn, vn = kp_ref[g], vp_ref[g], kc_ref[g, WINDOW:, :], vc_ref[g, WINDOW:, :]
            else:
                kp, vp, kn, vn = kc_ref[g, :WINDOW, :], vc_ref[g, :WINDOW, :], kn_ref[g], vn_ref[g]
            off_p = jnp.where(qb >= n_ctx_blocks + 1, 0, 2 * WINDOW)
            off_c = jnp.where(qb >= n_ctx_blocks, 0, 2 * WINDOW)
            off_n = jnp.where(jnp.logical_and(qb >= n_ctx_blocks, qb <= n_blocks - 2), 0, 2 * WINDOW)
            s_p = jnp.where(kj >= qi + off_p, sdot(q, kp), neg)
            s_c = jnp.where(kj >= off_c, sdot(q, kc_ref[g, cur, :]), neg)
            s_n = jnp.where(kj <= qi - off_n, sdot(q, kn), neg)
            s_x = sdot(q, kx)
            rmax = lambda s: jnp.max(s, axis=1, keepdims=True)
            m = jnp.maximum(jnp.maximum(jnp.maximum(rmax(s_p), rmax(s_c)), jnp.maximum(rmax(s_n), rmax(s_x))), sink)
            e_p, e_c, e_n, e_x = (jnp.exp2(s - m) for s in (s_p, s_c, s_n, s_x))
            rsum = lambda e: jnp.sum(e, axis=1, keepdims=True)
            den = rsum(e_p) + rsum(e_c) + rsum(e_n) + rsum(e_x) + jnp.exp2(sink - m)
            o = (pv(e_p.astype(BF16), vp) + pv(e_c.astype(BF16), vc_ref[g, cur, :])
                 + pv(e_n.astype(BF16), vn) + pv(e_x.astype(BF16), vx))
            o_ref[g, j] = o / den


def _window(q, k, v, sink, n_ctx):
    b, hq, lt, dh = q.shape
    g = k.shape[1]
    blk = WINDOW
    nb = lt // blk
    rows = (hq // g) * blk
    n_ctx_blocks = n_ctx // blk
    assert n_ctx % (2 * blk) == 0 and nb % 2 == 0
    lo, hi = n_ctx_blocks, nb - 1
    pair_spec = lambda heads: pl.BlockSpec((None, heads, 2 * blk, dh), lambda bi, i: (bi, 0, i, 0))
    side_spec = lambda delta: pl.BlockSpec((None, g, blk, dh),
                                           lambda bi, i: (bi, 0, jnp.clip(2 * i + delta, lo, hi), 0))
    ctx_spec = pl.BlockSpec((None, g, n_ctx, dh), lambda bi, i: (bi, 0, 0, 0))
    return pl.pallas_call(
        functools.partial(_window_kernel, n_ctx_blocks=n_ctx_blocks, n_blocks=nb),
        out_shape=jax.ShapeDtypeStruct((b, g, nb, rows, dh), F32),
        grid=(b, nb // 2),
        in_specs=[pair_spec(hq), side_spec(-1), pair_spec(g), side_spec(2), side_spec(-1), pair_spec(g),
                  side_spec(2), ctx_spec, ctx_spec, pl.BlockSpec((g, rows, 1), lambda bi, i: (0, 0, 0))],
        out_specs=pl.BlockSpec((None, g, 2, rows, dh), lambda bi, i: (bi, 0, i, 0, 0)),
        compiler_params=_cparams(("parallel", "arbitrary")),
    )(q, k, k, k, v, v, v, k, v, sink)


def _retention_kernel(qf_ref, ktf_ref, vf_ref, qb_ref, ktb_ref, vb_ref, dmat_ref, xi_ref, zeta_ref, gch_ref,
                      of_ref, ob_ref, st_ref):
    t = pl.program_id(0)

    @pl.when(t == 0)
    def _():
        st_ref[...] = jnp.zeros_like(st_ref)

    for d, (q_ref, kt_ref, v_ref, o_ref) in enumerate(((qf_ref, ktf_ref, vf_ref, of_ref),
                                                       (qb_ref, ktb_ref, vb_ref, ob_ref))):
        for bi in range(q_ref.shape[0]):
            for hd in range(RET_HEADS):
                q = q_ref[bi, hd]
                kt = kt_ref[bi, hd]
                v = v_ref[bi, hd]
                st = st_ref[d, bi, hd]
                inner = _dot3(q, kt) * dmat_ref[d, hd]
                o_ref[bi, hd] = _dot3(inner, v) + _dot3(q, st) * xi_ref[d, hd]
                st_ref[d, bi, hd] = st * gch_ref[d, hd] + _dot3(kt * zeta_ref[d, hd], v)


def _retention_call(q, kt, v, dmat, xi, zeta, gch, n_ctx_chunks):
    b, hh, lt, dk = q.shape
    dv = v.shape[-1]
    c = RET_CHUNK
    nch = lt // c

    def back(ti):
        return jnp.where(ti < n_ctx_chunks, n_ctx_chunks - 1 - ti, nch - 1 - (ti - n_ctx_chunks))

    fwd = lambda ti: ti
    rows = lambda blk, w: pl.BlockSpec((b, hh, c, w), lambda ti: (0, 0, blk(ti), 0))
    cols = lambda blk: pl.BlockSpec((b, hh, dk, c), lambda ti: (0, 0, 0, blk(ti)))
    tab = lambda a: pl.BlockSpec(a.shape, lambda ti: (0,) * a.ndim)
    out = jax.ShapeDtypeStruct((b, hh, lt, dv), F32)
    return pl.pallas_call(
        _retention_kernel,
        out_shape=(out, out),
        grid=(nch,),
        in_specs=[rows(fwd, dk), cols(fwd), rows(fwd, dv), rows(back, dk), cols(back), rows(back, dv),
                  tab(dmat), tab(xi), tab(zeta), tab(gch)],
        out_specs=(rows(fwd, dv), rows(back, dv)),
        scratch_shapes=[pltpu.VMEM((2, b, hh, dk, dv), F32)],
        compiler_params=_cparams(("arbitrary",)),
    )(q, kt, v, q, kt, v, dmat, xi, zeta, gch)


def _layer_norm_rows(z, ln):
    mu = jnp.mean(z, axis=-1, keepdims=True)
    zc = z - mu
    var = jnp.mean(zc * zc, axis=-1, keepdims=True)
    return zc * lax.rsqrt(var + LN_EPS) * ln[0:1, :] + ln[1:2, :]


def _merge_kernel(fa_ref, fd0_ref, fd1_ref, win_ref, retf_ref, retb_ref, rg_ref, g_ref, h_ref, mod_ref, lam_ref,
                  subln_ref, rnorm_ref, wb_ref, wo_ref, ln_ref, o_ref, *, alpha):
    d = h_ref.shape[1]
    tb = h_ref.shape[0]
    hd = HEAD_DIM
    proj = functools.partial(jnp.dot, preferred_element_type=F32)
    gate = lambda i: g_ref[:, i * d:(i + 1) * d].astype(F32)

    oat = jnp.concatenate([fa_ref[g][:, st * tb:(st + 1) * tb] for g in range(A_KV_HEADS) for st in range(2)],
                          axis=0)
    m = gate(0) * proj(oat.T.astype(BF16), wb_ref[0])

    lam = lam_ref[...]
    heads = []
    for h4 in range(DIFF_HEADS):
        f = (fd0_ref if h4 < 2 else fd1_ref)[h4 % 2]
        o = f[:, :tb] - lam * f[:, tb:]
        heads.append(o * lax.rsqrt(jnp.mean(o * o, axis=0, keepdims=True) + RMS_EPS))
    obt = jnp.concatenate(heads, axis=0) * subln_ref[...]
    m = m + gate(1) * proj(obt.T.astype(BF16), wb_ref[1])

    half = tb // 2
    acc = None
    for h4 in range(WIN_HEADS):
        g, st = h4 // 2, h4 % 2
        o = jnp.concatenate([win_ref[g, 0][st * half:(st + 1) * half], win_ref[g, 1][st * half:(st + 1) * half]],
                            axis=0)
        t = proj(o.astype(BF16), wb_ref[2, h4 * hd:(h4 + 1) * hd, :])
        acc = t if acc is None else acc + t
    m = m + gate(2) * acc

    acc = None
    for h4 in range(RET_HEADS):
        cols = slice(h4 * RET_DV, (h4 + 1) * RET_DV)
        o = retf_ref[h4] + retb_ref[h4]
        mu = jnp.mean(o, axis=-1, keepdims=True)
        oc = o - mu
        var = jnp.mean(oc * oc, axis=-1, keepdims=True)
        on = oc * lax.rsqrt(var + LN_EPS) * rnorm_ref[0:1, cols] + rnorm_ref[1:2, cols]
        gt = rg_ref[:, cols]
        t = proj((on * (gt / (1.0 + jnp.exp(-gt)))).astype(BF16), wb_ref[3, cols, :])
        acc = t if acc is None else acc + t
    m = m + gate(3) * acc

    y = proj(m.astype(BF16), wo_ref[...])
    z = alpha * h_ref[...] + mod_ref[2:3, :] * y
    o_ref[...] = _layer_norm_rows(z, ln_ref[...])


def _merge(flash_out, win_out, ret_out, rg, gates, h, mod, group_of_block, lam, subln, rnorm, wb, wo, ln,
           alpha, b, lt):
    t, d = h.shape
    tb = TOKEN_BLOCK
    bpb = lt // tb
    row = lambda n: pl.BlockSpec((tb, n), lambda i: (i, 0))
    const = lambda a: pl.BlockSpec(a.shape, lambda i: (0,) * a.ndim)
    fspec = lambda gb: pl.BlockSpec(
        (None, 2, None, None) + flash_out.shape[4:],
        lambda i: (i // bpb, gb, (i % bpb + ATT_QTILES - 1) // ATT_QTILES, (i % bpb + ATT_QTILES - 1) % ATT_QTILES,
                   0, 0))
    ret_spec = pl.BlockSpec((None, RET_HEADS, tb, RET_DV), lambda i: (i // bpb, 0, i % bpb, 0))
    return pl.pallas_call(
        functools.partial(_merge_kernel, alpha=alpha),
        out_shape=jax.ShapeDtypeStruct((t, d), F32),
        grid=(t // tb,),
        in_specs=[fspec(0), fspec(1), fspec(2),
                  pl.BlockSpec((None, WIN_KV_HEADS, 2) + win_out.shape[3:], lambda i: (i // bpb, 0, i % bpb, 0, 0)),
                  ret_spec, ret_spec, row(rg.shape[1]), row(N_BRANCH * d), row(d),
                  pl.BlockSpec((None, 6, d), lambda i: (group_of_block(i), 0, 0)),
                  const(lam), const(subln), const(rnorm), const(wb), const(wo), const(ln)],
        out_specs=row(d),
        compiler_params=_cparams(("parallel",)),
    )(flash_out, flash_out, flash_out, win_out, *ret_out, rg, gates, h, mod, lam, subln, rnorm, wb, wo, ln)


def _top_rows(s, n, with_rank=False):
    out = []
    cur = s
    rank = jnp.full(s.shape, float(n), F32) if with_rank else None
    for r in range(n):
        mx = jnp.max(cur, axis=0, keepdims=True)
        out.append(mx)
        if with_rank:
            rank = jnp.where(cur == mx, float(r), rank)
        if r + 1 < n:
            cur = jnp.where(cur == mx, -jnp.inf, cur)
    return (out, rank) if with_rank else out


def _peer_route_kernel(h_ref, mod_ref, wh_ref, wl_ref, sk_ref, xt_ref, rk_ref, b1_ref, nn_ref, az_ref,
                       cand_ref):
    m = mod_ref[...]
    u = h_ref[...] * (1.0 + m[4:5, :]) + m[3:4, :]
    xt_ref[...] = u.T.astype(BF16)
    uh, ul = _split_bf16(u)
    d = functools.partial(jnp.dot, preferred_element_type=F32)
    nk = PEER_NK
    nt = (((1,), (1,)), ((), ()))
    k1 = PEER_TOPK + 1
    for hd in range(PEER_HEADS):
        c0 = 2 * hd * PEER_DQ
        wh = wh_ref[:, c0:c0 + 2 * PEER_DQ]
        wl = wl_ref[:, c0:c0 + 2 * PEER_DQ]
        q = d(uh, wh) + (d(ul, wh) + d(uh, wl))
        st = [_dot3(sk_ref[hd, p], q[:, p * PEER_DQ:(p + 1) * PEER_DQ], nt) for p in range(2)]
        top0 = _top_rows(st[0], k1)
        top1, rank1 = _top_rows(st[1], k1, with_rank=True)
        r = 0
        for p0 in range(k1):
            for p1 in range(k1 // (p0 + 1)):
                cand_ref[r:r + 1, :] = top0[p0] + top1[p1]
                r += 1
        cand_ref[r:, :] = jnp.full((cand_ref.shape[0] - r, cand_ref.shape[1]), -jnp.inf, F32)
        cand = cand_ref[...]
        ctop = _top_rows(cand, k1)
        tau = 0.5 * (ctop[PEER_TOPK - 1] + ctop[PEER_TOPK])
        mx = top0[0] + top1[0]
        z = jnp.sum(jnp.where(cand >= tau, jnp.exp(cand - mx), 0.0), axis=0, keepdims=True)
        th = tau - st[0]
        nn = jnp.zeros_like(th)
        for q in range(PEER_TOPK):
            nn = nn + jnp.where(top1[q] >= th, 1.0, 0.0)
        rk_ref[hd] = rank1.astype(BF16)
        b1_ref[hd] = jnp.exp(st[1] - top1[0]).astype(BF16)
        nn_ref[hd] = nn
        az_ref[hd] = jnp.exp(st[0] - top0[0]) / z


def _peer_route(h, mod, group_of_block, wq_hi, wq_lo, subkeys):
    t, d = h.shape
    tb = TOKEN_BLOCK
    hh, nk = PEER_HEADS, PEER_NK
    st_shape = lambda dt: jax.ShapeDtypeStruct((hh, nk, t), dt)
    st_spec = pl.BlockSpec((hh, nk, tb), lambda i: (0, 0, i))
    nq = wq_hi.shape[1]
    return pl.pallas_call(
        _peer_route_kernel,
        out_shape=(jax.ShapeDtypeStruct((d, t), BF16), st_shape(BF16), st_shape(BF16), st_shape(F32),
                   st_shape(F32)),
        grid=(t // tb,),
        in_specs=[pl.BlockSpec((tb, d), lambda i: (i, 0)),
                  pl.BlockSpec((None, 6, d), lambda i: (group_of_block(i), 0, 0)),
                  pl.BlockSpec((d, nq), lambda i: (0, 0)),
                  pl.BlockSpec((d, nq), lambda i: (0, 0)),
                  pl.BlockSpec((hh, 2, nk, PEER_DQ), lambda i: (0, 0, 0, 0))],
        out_specs=(pl.BlockSpec((d, tb), lambda i: (0, i)), st_spec, st_spec, st_spec, st_spec),
        scratch_shapes=[pltpu.VMEM((PEER_CAND_ROWS, tb), F32)],
        compiler_params=_cparams(("parallel",)),
    )(h, mod, wq_hi, wq_lo, subkeys)


GELU_K1 = -2.0 * math.sqrt(2.0 / math.pi) * LOG2E
GELU_K2 = GELU_K1 * 0.044715


def _gelu_tanh(x):
    return x / (1.0 + jnp.exp2(x * (GELU_K1 + GELU_K2 * (x * x))))


def _peer_dense_kernel(xt_ref, u_ref, vt_ref, vtl_ref, rk_ref, b1_ref, nn_ref, az_ref, yt_ref, g_ref):
    c = pl.program_id(1)
    nk = PEER_NK
    tp = xt_ref.shape[1]
    rows_per_step = u_ref.shape[0] // nk

    @pl.when(c == 0)
    def _():
        yt_ref[...] = jnp.zeros_like(yt_ref)
        g_ref[...] = jnp.zeros_like(g_ref)

    xt = xt_ref[...]
    pre = lambda ii: jnp.dot(u_ref[ii * nk:(ii + 1) * nk, :], xt, preferred_element_type=F32)
    yt_ref[...] += jnp.dot(vt_ref[...], g_ref[...], preferred_element_type=F32)
    act_next = pre(0)
    for ii in range(rows_per_step):
        act = act_next
        if ii + 1 < rows_per_step:
            act_next = pre(ii + 1)
        w = None
        tile = (nk // BF16_ROWS, BF16_ROWS, tp)
        row = lambda ref, hd: jnp.broadcast_to(ref[hd, ii:ii + 1, :], (BF16_ROWS, tp)).astype(BF16)[None]
        for hd in range(PEER_HEADS):
            t = jnp.where(rk_ref[hd].reshape(tile) < row(nn_ref, hd), b1_ref[hd].reshape(tile),
                          jnp.zeros((), BF16)) * row(az_ref, hd)
            w = t if w is None else w + t
        g_ref[ii * nk:(ii + 1) * nk, :] = w.reshape(nk, tp) * _gelu_tanh(act.astype(BF16))

    @pl.when(c == pl.num_programs(1) - 1)
    def _():
        yt_ref[...] += jnp.dot(vtl_ref[...], g_ref[...], preferred_element_type=F32)


def _peer_dense(xt, u, vt, rk, b1, nn, az):
    d, t = xt.shape
    n = u.shape[0]
    tp = PEER_TOKENS
    ec = PEER_EXPERTS
    nc = n // ec
    hh, nk = PEER_HEADS, PEER_NK
    st_spec = pl.BlockSpec((hh, nk, tp), lambda i, c: (0, 0, i))
    row_spec = pl.BlockSpec((hh, ec // nk, tp), lambda i, c: (0, c, i))
    return pl.pallas_call(
        _peer_dense_kernel,
        out_shape=jax.ShapeDtypeStruct((d, t), F32),
        grid=(t // tp, nc),
        in_specs=[pl.BlockSpec((d, tp), lambda i, c: (0, i)),
                  pl.BlockSpec((ec, d), lambda i, c: (c, 0)),
                  pl.BlockSpec((d, ec), lambda i, c: (0, jnp.maximum(c - 1, 0))),
                  pl.BlockSpec((d, ec), lambda i, c: (0, nc - 1)),
                  st_spec, st_spec, row_spec, row_spec],
        out_specs=pl.BlockSpec((d, tp), lambda i, c: (0, i)),
        scratch_shapes=[pltpu.VMEM((ec, tp), BF16)],
        compiler_params=_cparams(("parallel", "arbitrary")),
    )(xt, u, vt, vt, rk, b1, nn, az)


def _resid_ln_kernel(h_ref, yt_ref, mod_ref, ln_ref, o_ref, *, alpha, gate_row):
    z = alpha * h_ref[...] + mod_ref[gate_row:gate_row + 1, :] * yt_ref[...].T
    o_ref[...] = _layer_norm_rows(z, ln_ref[...])


def _resid_ln(h, yt, mod, group_of_block, ln, alpha, gate_row):
    t, d = h.shape
    tb = TOKEN_BLOCK
    row = pl.BlockSpec((tb, d), lambda i: (i, 0))
    return pl.pallas_call(
        functools.partial(_resid_ln_kernel, alpha=alpha, gate_row=gate_row),
        out_shape=jax.ShapeDtypeStruct((t, d), F32),
        grid=(t // tb,),
        in_specs=[row, pl.BlockSpec((d, tb), lambda i: (0, i)),
                  pl.BlockSpec((None, 6, d), lambda i: (group_of_block(i), 0, 0)),
                  pl.BlockSpec((2, d), lambda i: (0, 0))],
        out_specs=row,
        compiler_params=_cparams(("parallel",)),
    )(h, yt, mod, ln)


def _axial_tables(s, n_ctx, d):
    rows = s // GRID_W
    row = jnp.broadcast_to(jnp.arange(rows, dtype=F32)[:, None], (rows, GRID_W)).reshape(-1)
    col = jnp.broadcast_to(jnp.arange(GRID_W, dtype=F32)[None, :], (rows, GRID_W)).reshape(-1)
    quarter = d // 4
    inv = ROPE_THETA ** (-jnp.arange(quarter, dtype=F32) / quarter)
    ar, ac = row[:, None] * inv, col[:, None] * inv
    cos = jnp.concatenate([jnp.cos(ar), jnp.cos(ar), jnp.cos(ac), jnp.cos(ac)], axis=-1)
    sin = jnp.concatenate([-jnp.sin(ar), jnp.sin(ar), -jnp.sin(ac), jnp.sin(ac)], axis=-1)
    cos = jnp.concatenate([jnp.ones((n_ctx, d), F32), cos], axis=0)
    sin = jnp.concatenate([jnp.zeros((n_ctx, d), F32), sin], axis=0)
    return cos, sin


def _rope1d_tables(lt, d):
    half = d // 2
    inv = ROPE_THETA ** (-jnp.arange(half, dtype=F32) / half)
    ang = jnp.arange(lt, dtype=F32)[:, None] * inv
    return (jnp.concatenate([jnp.cos(ang), jnp.cos(ang)], axis=-1),
            jnp.concatenate([-jnp.sin(ang), jnp.sin(ang)], axis=-1))


def kernel(x, c, ctx, c_ctx, w_mod, b_mod, w_in, qk_gain, diff_lambda, diff_subln, win_sink, ret_decay,
           ret_norm, w_branch, w_out, ln_attn, ln_ffn, peer_wq, peer_subkeys, peer_u, peer_v):
    b, s, d = x.shape
    n_ctx = ctx.shape[1]
    depth = w_mod.shape[0]
    lt = n_ctx + s
    t = b * lt
    tb = TOKEN_BLOCK
    assert n_ctx % tb == 0 and s % tb == 0 and t % PEER_TOKENS == 0
    alpha = (2 * depth) ** 0.25
    blocks_per_batch = lt // tb
    ctx_blocks = n_ctx // tb

    def group_of_block(i):
        return jnp.where(i % blocks_per_batch < ctx_blocks, b, i // blocks_per_batch)

    cos64, sin64 = _axial_tables(s, n_ctx, HEAD_DIM)
    cos32, sin32 = _axial_tables(s, n_ctx, DIFF_DIM)
    cos1d, sin1d = _rope1d_tables(lt, RET_DK)
    sc_a = HEAD_DIM ** -0.5 * LOG2E
    sc_d = DIFF_DIM ** -0.5 * LOG2E
    rep = lambda a, n: jnp.tile(a, (1, n))
    tok_part = lambda t64, t32, t1d: [rep(t64, 2), rep(t32, 8), rep(t64, 4) * sc_a, rep(t64, 2), rep(t1d, 4)]
    feat_part = lambda t64, t32, t1d: [rep(t64, 4) * sc_a, rep(t32, 8) * sc_d, rep(t1d, 4) * RET_DK ** -0.5]
    ttok = jnp.concatenate(tok_part(cos64, cos32, cos1d) + tok_part(sin64, sin32, sin1d), axis=1)
    tfeat = jnp.concatenate(feat_part(cos64, cos32, cos1d) + feat_part(sin64, sin32, sin1d), axis=1).T
    swap64 = np.arange(HEAD_DIM) ^ 16
    head_ids = np.arange(2 * HEAD_DIM) // HEAD_DIM
    avg = jnp.asarray((head_ids[:, None] == head_ids[None, :]) / HEAD_DIM, BF16)
    cond8 = jnp.zeros((8, d), F32).at[:b].set(jax.nn.silu(c)).at[b].set(jax.nn.silu(c_ctx))

    h = jnp.concatenate([ctx, x], axis=1).reshape(t, d)

    for l in range(depth):
        mod = _modulation(cond8, w_mod[l], b_mod[l]).reshape(8, 6, d)
        w_mix = w_in[l, :, :MIX_COLS]
        w_gate = w_in[l, :, MIX_COLS:].astype(BF16)
        g0, g1 = qk_gain[l, 0].astype(F32), qk_gain[l, 1].astype(F32)
        gtok = jnp.stack([jnp.tile(g1, 2), jnp.tile(g1[swap64], 2)])
        gfeat = jnp.broadcast_to(jnp.concatenate([jnp.tile(g0, A_HEADS), jnp.tile(g0[swap64], A_HEADS)])[:, None],
                                 (2 * A_HEADS * HEAD_DIM, tb))
        kall, qt, vt, wq, wk, wv, rq, rkt, rv, rg = _mixer_in(
            h, mod, group_of_block, w_mix[:, TOK_COLS].astype(BF16), w_mix[:, FEAT_COLS].T.astype(BF16),
            ttok, tfeat, gtok, gfeat, avg, b, lt)
        gates = _gates(h, mod, group_of_block, w_gate)

        ot = _flash(qt, kall, vt, n_ctx)
        lam_init = 0.8 - 0.6 * math.exp(-0.3 * l)
        lp = diff_lambda[l].astype(F32)
        lam = (jnp.exp(jnp.sum(lp[0] * lp[1])) - jnp.exp(jnp.sum(lp[2] * lp[3])) + lam_init).reshape(1, 1)
        subln = jnp.broadcast_to((jnp.tile(diff_subln[l].astype(F32), DIFF_HEADS) * (1.0 - lam_init))[:, None],
                                 (DIFF_HEADS * HEAD_DIM, tb))

        sink = jnp.repeat(win_sink[l].astype(F32) * LOG2E, WINDOW).reshape(WIN_KV_HEADS, 2 * WINDOW, 1)
        ow = _window(wq, wk, wv, sink, n_ctx)

        lg = jax.nn.log_sigmoid(ret_decay[l].astype(F32))
        idx = jnp.arange(RET_CHUNK, dtype=F32)
        diff = idx[:, None] - idx[None, :]
        lg3 = lg[:, :, None, None]
        dm_f = jnp.exp(jnp.where(diff >= 0, diff * lg3[0], -jnp.inf))
        dm_b = jnp.exp(jnp.where(diff <= 0, -diff * lg3[1], -jnp.inf))
        dmat = jnp.stack([dm_f, dm_b])
        xi = jnp.stack([jnp.exp((idx + 1.0) * lg[0][:, None]), jnp.exp((RET_CHUNK - idx) * lg[1][:, None])])
        zeta = jnp.stack([jnp.exp((RET_CHUNK - 1.0 - idx) * lg[0][:, None]), jnp.exp(idx * lg[1][:, None])])
        gch = jnp.exp(RET_CHUNK * lg)
        o_ret = _retention_call(rq, rkt, rv, dmat, xi[..., None], zeta[:, :, None, :],
                                gch[:, :, None, None], n_ctx // RET_CHUNK)

        h = _merge(ot, ow, o_ret, rg, gates, h, mod, group_of_block, lam, subln, ret_norm[l].astype(F32),
                   w_branch[l].astype(BF16), w_out[l].astype(BF16), ln_attn[l], alpha, b, lt)

        wq_hi, wq_lo = _split_bf16(peer_wq[l])
        xt, rk, b1, nn, az = _peer_route(h, mod, group_of_block, wq_hi, wq_lo, peer_subkeys[l])
        yt = _peer_dense(xt, peer_u[l].astype(BF16), peer_v[l].T.astype(BF16), rk, b1, nn, az)
        h = _resid_ln(h, yt, mod, group_of_block, ln_ffn[l], alpha, 5)

    return h.reshape(b, lt, d)[:, n_ctx:, :]
```

```python
import functools
import math

import numpy as np
import jax
import jax.numpy as jnp
from jax import lax
from jax.experimental import pallas as pl
from jax.experimental.pallas import tpu as pltpu

GRID_W = 64
HEAD_DIM = 64
ROPE_THETA = 10000.0
A_HEADS = 4
A_KV_HEADS = 2
DIFF_HEADS = 4
DIFF_DIM = 32
WIN_HEADS = 4
WIN_KV_HEADS = 2
WINDOW = 128
RET_HEADS = 4
RET_DK = 64
RET_DV = 64
RET_CHUNK = 128
N_BRANCH = 4
PIECES = (
    A_HEADS * HEAD_DIM, A_KV_HEADS * HEAD_DIM, A_KV_HEADS * HEAD_DIM,
    2 * DIFF_HEADS * DIFF_DIM, 2 * DIFF_HEADS * DIFF_DIM, DIFF_HEADS * 2 * DIFF_DIM,
    WIN_HEADS * HEAD_DIM, WIN_KV_HEADS * HEAD_DIM, WIN_KV_HEADS * HEAD_DIM,
    RET_HEADS * RET_DK, RET_HEADS * RET_DK, RET_HEADS * RET_DV, RET_HEADS * RET_DV,
)
MIX_COLS = sum(PIECES)
PIECE_OFF = tuple(int(v) for v in np.cumsum((0,) + PIECES))
PEER_HEADS = 8
PEER_NK = 128
PEER_TOPK = 16
PEER_DQ = 128
PEER_CAND_ROWS = -(-sum((PEER_TOPK + 1) // (p + 1) for p in range(PEER_TOPK + 1)) // 8) * 8
LN_EPS = 1e-5
RMS_EPS = 1e-6
LOG2E = 1.4426950408889634

F32 = jnp.float32
BF16 = jnp.bfloat16

TOKEN_BLOCK = 256
ATT_TK = 512
ATT_UNROLL = 4
ATT_QTILES = 4
ATT_VPAD = 80
PEER_TOKENS = 512
PEER_EXPERTS = 1024
VMEM_LIMIT = 56 * 1024 * 1024
BF16_ROWS = 16


def _cparams(sem):
    return pltpu.CompilerParams(dimension_semantics=sem, vmem_limit_bytes=VMEM_LIMIT)


def _split_bf16(a):
    hi = a.astype(BF16)
    lo = (a - hi.astype(F32)).astype(BF16)
    return hi, lo


def _dot3(a, b, dims=(((1,), (0,)), ((), ()))):
    ah, al = _split_bf16(a)
    bh, bl = _split_bf16(b)
    d = functools.partial(lax.dot_general, dimension_numbers=dims, preferred_element_type=F32)
    return d(ah, bh) + (d(al, bh) + d(ah, bl))


def _mod_kernel(c_ref, w_ref, b_ref, o_ref):
    o_ref[...] = _dot3(c_ref[...], w_ref[...]) + b_ref[...]


def _modulation(cond8, w, b):
    d, n = w.shape
    tn = 1536
    return pl.pallas_call(
        _mod_kernel,
        out_shape=jax.ShapeDtypeStruct((8, n), F32),
        grid=(n // tn,),
        in_specs=[pl.BlockSpec((8, d), lambda j: (0, 0)),
                  pl.BlockSpec((d, tn), lambda j: (0, j)),
                  pl.BlockSpec((1, tn), lambda j: (0, j))],
        out_specs=pl.BlockSpec((8, tn), lambda j: (0, j)),
        compiler_params=_cparams(("arbitrary",)),
    )(cond8, w, b.reshape(1, n))


def _gates_kernel(x_ref, mod_ref, w_ref, o_ref, *, chunk):
    m = mod_ref[...]
    xm = (x_ref[...] * (1.0 + m[1:2, :]) + m[0:1, :]).astype(BF16)
    n = w_ref.shape[1]
    for j in range(n // chunk):
        acc = jnp.dot(xm, w_ref[:, j * chunk:(j + 1) * chunk], preferred_element_type=F32)
        o_ref[:, j * chunk:(j + 1) * chunk] = jax.nn.sigmoid(acc).astype(o_ref.dtype)


def _gates(h, mod, group_of_block, w):
    t, d = h.shape
    n = w.shape[1]
    tb = TOKEN_BLOCK
    return pl.pallas_call(
        functools.partial(_gates_kernel, chunk=512),
        out_shape=jax.ShapeDtypeStruct((t, n), BF16),
        grid=(t // tb,),
        in_specs=[pl.BlockSpec((tb, d), lambda i: (i, 0)),
                  pl.BlockSpec((None, 6, d), lambda i: (group_of_block(i), 0, 0)),
                  pl.BlockSpec((d, n), lambda i: (0, 0))],
        out_specs=pl.BlockSpec((tb, n), lambda i: (i, 0)),
        compiler_params=_cparams(("parallel",)),
    )(h, mod, w)


TOK_PIECES = ((1, True), (4, True), (6, True), (7, True), (8, False), (9, True), (11, False), (12, False))
FEAT_PIECES = ((0, True), (3, True), (2, False), (5, False), (10, True))


def _piece_cols(pieces):
    flips = {0: 16, 1: 16, 3: 8, 4: 8, 6: 16, 7: 16, 9: 32, 10: 32}
    cols, offs = [], {}
    n = 0
    for p, rotary in pieces:
        base = np.arange(PIECE_OFF[p], PIECE_OFF[p + 1])
        offs[p] = n
        cols.append(base)
        n += len(base)
        if rotary:
            cols.append(PIECE_OFF[p] + ((base - PIECE_OFF[p]) ^ flips[p]))
            n += len(base)
    return np.concatenate(cols), offs


TOK_COLS, TOK_OFF = _piece_cols(TOK_PIECES)
FEAT_COLS, FEAT_OFF = _piece_cols(FEAT_PIECES)
TTOK_OFF = {1: 0, 4: 128, 6: 384, 7: 640, 9: 768}
TTOK_W = 1024
TFEAT_OFF = {0: 0, 3: 256, 10: 512}
TFEAT_W = 768


def _mixer_in_kernel(x_ref, mod_ref, wt_ref, wf_ref, ttok_ref, tfeat_ref, gtok_ref, gfeat_ref, avg_ref,
                     kall_ref, qt_ref, vt_ref, wq_ref, wk_ref, wv_ref, rq_ref, rkt_ref, rv_ref, rg_ref):
    m = mod_ref[...]
    u = x_ref[...] * (1.0 + m[1:2, :]) + m[0:1, :]
    xm = u.astype(BF16)
    xmt = u.T.astype(BF16)
    tb = xm.shape[0]

    def tok(p, swapped=False):
        a = TOK_OFF[p] + (PIECES[p] if swapped else 0)
        return jnp.dot(xm, wt_ref[:, a:a + PIECES[p]], preferred_element_type=F32)

    def feat(p, swapped=False):
        a = FEAT_OFF[p] + (PIECES[p] if swapped else 0)
        return jnp.dot(wf_ref[a:a + PIECES[p], :], xmt, preferred_element_type=F32)

    def rope_tok(p, x, xs):
        a = TTOK_OFF[p]
        return x * ttok_ref[:, a:a + PIECES[p]] + xs * ttok_ref[:, TTOK_W + a:TTOK_W + a + PIECES[p]]

    def rope_feat(p, x, xs):
        a = TFEAT_OFF[p]
        return x * tfeat_ref[a:a + PIECES[p], :] + xs * tfeat_ref[TFEAT_W + a:TFEAT_W + a + PIECES[p], :]

    x, xs = tok(1), tok(1, True)
    sq_hi, sq_lo = _split_bf16(x * x)
    avg = avg_ref[...]
    ms = jnp.dot(sq_hi, avg, preferred_element_type=F32) + jnp.dot(sq_lo, avg, preferred_element_type=F32)
    r = lax.rsqrt(ms + RMS_EPS)
    ka = r * rope_tok(1, x * gtok_ref[0:1, :], xs * gtok_ref[1:2, :])
    kd = rope_tok(4, tok(4), tok(4, True))
    kall_ref[:, :PIECES[1]] = ka.astype(BF16)
    kall_ref[:, PIECES[1]:] = kd.astype(BF16)

    def split_heads(ref, val):
        for hd in range(ref.shape[0]):
            ref[hd] = val[:, hd * HEAD_DIM:(hd + 1) * HEAD_DIM].astype(ref.dtype)

    split_heads(wq_ref, rope_tok(6, tok(6), tok(6, True)))
    split_heads(wk_ref, rope_tok(7, tok(7), tok(7, True)))
    split_heads(wv_ref, tok(8))
    split_heads(rq_ref, rope_tok(9, tok(9), tok(9, True)))
    split_heads(rv_ref, tok(11))
    rg_ref[...] = tok(12)
    rkt = rope_feat(10, feat(10), feat(10, True))
    for hd in range(RET_HEADS):
        rkt_ref[hd] = rkt[hd * RET_DK:(hd + 1) * RET_DK]

    zero64 = jnp.zeros((HEAD_DIM, tb), F32)
    zero32 = jnp.zeros((DIFF_DIM, tb), F32)

    def place(q, upper):
        return jnp.concatenate([zero64, q] if upper else [q, zero64], axis=0).astype(BF16)

    xq, xqs = feat(0), feat(0, True)
    for hd in range(A_HEADS):
        rows = slice(hd * HEAD_DIM, (hd + 1) * HEAD_DIM)
        xh = xq[rows]
        rh = lax.rsqrt(jnp.mean(xh * xh, axis=0, keepdims=True) + RMS_EPS)
        swapped_rows = slice(A_HEADS * HEAD_DIM + hd * HEAD_DIM, A_HEADS * HEAD_DIM + (hd + 1) * HEAD_DIM)
        qh = rh * (xh * gfeat_ref[rows, :] * tfeat_ref[rows, :]
                   + xqs[rows] * gfeat_ref[swapped_rows, :]
                   * tfeat_ref[TFEAT_W + hd * HEAD_DIM:TFEAT_W + (hd + 1) * HEAD_DIM, :])
        qt_ref[hd // 2, hd % 2] = place(qh, hd // 2 == 1)
    qd = rope_feat(3, feat(3), feat(3, True))
    for hd in range(DIFF_HEADS):
        qh = qd[hd * HEAD_DIM:(hd + 1) * HEAD_DIM]
        q1 = jnp.concatenate([qh[:DIFF_DIM], zero32], axis=0)
        q2 = jnp.concatenate([zero32, qh[DIFF_DIM:]], axis=0)
        qt_ref[A_KV_HEADS + hd, 0] = place(q1, hd % 2 == 1)
        qt_ref[A_KV_HEADS + hd, 1] = place(q2, hd % 2 == 1)

    pad_rows = vt_ref.shape[1] - HEAD_DIM
    tail = (lax.broadcasted_iota(jnp.int32, (pad_rows, tb), 0) == 0).astype(F32)
    va, vd = feat(2), feat(5)
    for g in range(A_KV_HEADS + DIFF_HEADS):
        v = va[g * HEAD_DIM:(g + 1) * HEAD_DIM] if g < A_KV_HEADS else \
            vd[(g - A_KV_HEADS) * HEAD_DIM:(g - A_KV_HEADS + 1) * HEAD_DIM]
        vt_ref[g] = jnp.concatenate([v, tail], axis=0).astype(BF16)


def _mixer_in(h, mod, group_of_block, wt, wf, ttok, tfeat, gtok, gfeat, avg, b, lt):
    t, d = h.shape
    tb = TOKEN_BLOCK
    bpb = lt // tb
    n_groups = A_KV_HEADS + DIFF_HEADS
    kw = PIECES[1] + PIECES[4]
    row = lambda w: pl.BlockSpec((tb, w), lambda i: (i, 0))
    const = lambda a: pl.BlockSpec(a.shape, lambda i: (0,) * a.ndim)
    heads_shape = lambda n, dt: jax.ShapeDtypeStruct((b, n, lt, HEAD_DIM), dt)
    heads_spec = lambda n: pl.BlockSpec((None, n, tb, HEAD_DIM), lambda i: (i // bpb, 0, i % bpb, 0))
    return pl.pallas_call(
        _mixer_in_kernel,
        out_shape=(jax.ShapeDtypeStruct((b, lt, kw), BF16),
                   jax.ShapeDtypeStruct((b, n_groups, bpb, 2, 2 * HEAD_DIM, tb), BF16),
                   jax.ShapeDtypeStruct((b, n_groups, bpb, ATT_VPAD, tb), BF16),
                   heads_shape(WIN_HEADS, BF16), heads_shape(WIN_KV_HEADS, BF16), heads_shape(WIN_KV_HEADS, BF16),
                   heads_shape(RET_HEADS, F32), jax.ShapeDtypeStruct((b, RET_HEADS, RET_DK, lt), F32),
                   heads_shape(RET_HEADS, F32), jax.ShapeDtypeStruct((t, PIECES[12]), F32)),
        grid=(t // tb,),
        in_specs=[row(d),
                  pl.BlockSpec((None, 6, d), lambda i: (group_of_block(i), 0, 0)),
                  const(wt), const(wf),
                  pl.BlockSpec((tb, 2 * TTOK_W), lambda i: (i % bpb, 0)),
                  pl.BlockSpec((2 * TFEAT_W, tb), lambda i: (0, i % bpb)),
                  const(gtok), const(gfeat), const(avg)],
        out_specs=(pl.BlockSpec((None, tb, kw), lambda i: (i // bpb, i % bpb, 0)),
                   pl.BlockSpec((None, n_groups, None, 2, 2 * HEAD_DIM, tb),
                                lambda i: (i // bpb, 0, i % bpb, 0, 0, 0)),
                   pl.BlockSpec((None, n_groups, None, ATT_VPAD, tb), lambda i: (i // bpb, 0, i % bpb, 0, 0)),
                   heads_spec(WIN_HEADS), heads_spec(WIN_KV_HEADS), heads_spec(WIN_KV_HEADS),
                   heads_spec(RET_HEADS),
                   pl.BlockSpec((None, RET_HEADS, RET_DK, tb), lambda i: (i // bpb, 0, 0, i % bpb)),
                   heads_spec(RET_HEADS), row(PIECES[12])),
        compiler_params=_cparams(("parallel",)),
    )(h, mod, wt, wf, ttok, tfeat, gtok, gfeat, avg)


def _flash_kernel(*refs, n_ctx, tk, nt):
    q_refs, (k_ref, vt_ref, o_ref, s_ref, smax_ref, m_ref, acc_ref) = refs[:nt], refs[nt:]
    i = pl.program_id(2)
    tq = q_refs[0].shape[2]
    qt = jnp.concatenate([q[st] for st in range(2) for q in q_refs], axis=1)
    tile = vt_ref.shape[2]
    nlc = (k_ref.shape[0] - n_ctx) // tk

    def scores(row0, rows):
        return jnp.dot(k_ref[pl.ds(row0, rows), :], qt, preferred_element_type=F32)

    def stage(slot, chunk):
        s = scores(lat_row(chunk), tk)
        s_ref[slot] = s
        smax_ref[slot] = jnp.max(s, axis=0, keepdims=True)

    def absorb(s, smax, tile0):
        m = m_ref[...]
        m_new = jnp.maximum(m, smax)
        p = jnp.exp2(s - m_new).astype(BF16)
        acc = acc_ref[...] * jnp.exp2(m - m_new)
        for j in range(s.shape[0] // tile):
            acc = acc + jnp.dot(vt_ref[tile0 + j], p[j * tile:(j + 1) * tile], preferred_element_type=F32)
        m_ref[...] = m_new
        acc_ref[...] = acc

    m_ref[...] = jnp.full(m_ref.shape, -jnp.inf, F32)
    acc_ref[...] = jnp.zeros(acc_ref.shape, F32)
    s_ctx = scores(0, n_ctx)
    absorb(s_ctx, jnp.max(s_ctx, axis=0, keepdims=True), 0)
    lat_row = lambda c: pl.multiple_of(n_ctx + c * tk, tile)
    lat_tile = lambda c: (n_ctx + c * tk) // tile

    @pl.when(i > 0)
    def _():
        stage(0, 0)
        unroll = ATT_UNROLL

        def trip(j, carry):
            for e in range(unroll):
                c = unroll * j + e
                stage((e + 1) % 2, c + 1)
                absorb(s_ref[e % 2], smax_ref[e % 2], lat_tile(c))
            return carry

        lax.fori_loop(0, nlc // unroll - 1, trip, 0)
        for c in range(nlc - unroll, nlc):
            if c + 1 < nlc:
                stage((c + 1) % 2, c + 1)
            absorb(s_ref[c % 2], smax_ref[c % 2], lat_tile(c))

    acc = acc_ref[...]
    o = acc[:HEAD_DIM] * (1.0 / acc[HEAD_DIM:HEAD_DIM + 1])
    for j in range(nt):
        o_ref[j] = jnp.concatenate([o[:, j * tq:(j + 1) * tq], o[:, (nt + j) * tq:(nt + j + 1) * tq]], axis=1)


def _flash(qt, k_all, vt, n_ctx):
    b, g, tiles, _, dk, tq = qt.shape
    nt = ATT_QTILES
    lt = k_all.shape[1]
    s = lt - n_ctx
    tk = next(c for c in (ATT_TK, 512, 256) if s % (ATT_UNROLL * c) == 0)
    assert n_ctx == tq and tk % tq == 0 and (tiles - 1) % nt == 0
    steps = 1 + (tiles - 1) // nt
    key_block = lambda gi: jnp.where(gi < A_KV_HEADS, 0, 1 + (gi - A_KV_HEADS) // 2)
    qspec = lambda j: pl.BlockSpec(
        (None, None, None, 2, dk, tq),
        lambda bi, gi, i: (bi, gi, jnp.where(i == 0, 0, nt * i - (nt - 1) + j), 0, 0, 0))
    return pl.pallas_call(
        functools.partial(_flash_kernel, n_ctx=n_ctx, tk=tk, nt=nt),
        out_shape=jax.ShapeDtypeStruct((b, g, steps, nt, HEAD_DIM, 2 * tq), F32),
        grid=(b, g, steps),
        in_specs=[qspec(j) for j in range(nt)]
        + [pl.BlockSpec((None, lt, dk), lambda bi, gi, i: (bi, 0, key_block(gi))),
           pl.BlockSpec((None, None) + vt.shape[2:], lambda bi, gi, i: (bi, gi, 0, 0, 0))],
        out_specs=pl.BlockSpec((None, None, None, nt, HEAD_DIM, 2 * tq), lambda bi, gi, i: (bi, gi, i, 0, 0, 0)),
        scratch_shapes=[pltpu.VMEM((2, tk, 2 * nt * tq), F32), pltpu.VMEM((2, 1, 2 * nt * tq), F32),
                        pltpu.VMEM((1, 2 * nt * tq), F32), pltpu.VMEM((vt.shape[3], 2 * nt * tq), F32)],
        compiler_params=_cparams(("parallel", "parallel", "arbitrary")),
    )(*([qt] * nt), k_all, vt)


def _window_kernel(q_ref, kp_ref, kc_ref, kn_ref, vp_ref, vc_ref, vn_ref, kx_ref, vx_ref, sink_ref,
                   o_ref, *, n_ctx_blocks, n_blocks):
    step = pl.program_id(1)
    n_kv = kc_ref.shape[0]
    group = q_ref.shape[0] // n_kv
    rows = group * WINDOW
    nt = (((1,), (1,)), ((), ()))
    sdot = functools.partial(lax.dot_general, dimension_numbers=nt, preferred_element_type=F32)
    pv = functools.partial(jnp.dot, preferred_element_type=F32)
    qi = lax.broadcasted_iota(jnp.int32, (rows, WINDOW), 0) % WINDOW
    kj = lax.broadcasted_iota(jnp.int32, (rows, WINDOW), 1)
    neg = -jnp.inf
    for g in range(n_kv):
        sink = sink_ref[g]
        kx, vx = kx_ref[g], vx_ref[g]
        for j in range(2):
            qb = 2 * step + j
            cur = slice(j * WINDOW, (j + 1) * WINDOW)
            q = q_ref[g * group:(g + 1) * group, cur, :].reshape(rows, HEAD_DIM)
            if j == 0:
                kp, vp, kn, vn = kp_ref[g], vp_ref[g], kc_ref[g, WINDOW:, :], vc_ref[g, WINDOW:, :]
            else:
                kp, vp, kn, vn = kc_ref[g, :WINDOW, :], vc_ref[g, :WINDOW, :], kn_ref[g], vn_ref[g]
            off_p = jnp.where(qb >= n_ctx_blocks + 1, 0, 2 * WINDOW)
            off_c = jnp.where(qb >= n_ctx_blocks, 0, 2 * WINDOW)
            off_n = jnp.where(jnp.logical_and(qb >= n_ctx_blocks, qb <= n_blocks - 2), 0, 2 * WINDOW)
            s_p = jnp.where(kj >= qi + off_p, sdot(q, kp), neg)
            s_c = jnp.where(kj >= off_c, sdot(q, kc_ref[g, cur, :]), neg)
            s_n = jnp.where(kj <= qi - off_n, sdot(q, kn), neg)
            s_x = sdot(q, kx)
            rmax = lambda s: jnp.max(s, axis=1, keepdims=True)
            m = jnp.maximum(jnp.maximum(jnp.maximum(rmax(s_p), rmax(s_c)), jnp.maximum(rmax(s_n), rmax(s_x))), sink)
            e_p, e_c, e_n, e_x = (jnp.exp2(s - m) for s in (s_p, s_c, s_n, s_x))
            rsum = lambda e: jnp.sum(e, axis=1, keepdims=True)
            den = rsum(e_p) + rsum(e_c) + rsum(e_n) + rsum(e_x) + jnp.exp2(sink - m)
            o = (pv(e_p.astype(BF16), vp) + pv(e_c.astype(BF16), vc_ref[g, cur, :])
                 + pv(e_n.astype(BF16), vn) + pv(e_x.astype(BF16), vx))
            o_ref[g, j] = o / den


def _window(q, k, v, sink, n_ctx):
    b, hq, lt, dh = q.shape
    g = k.shape[1]
    blk = WINDOW
    nb = lt // blk
    rows = (hq // g) * blk
    n_ctx_blocks = n_ctx // blk
    assert n_ctx % (2 * blk) == 0 and nb % 2 == 0
    lo, hi = n_ctx_blocks, nb - 1
    pair_spec = lambda heads: pl.BlockSpec((None, heads, 2 * blk, dh), lambda bi, i: (bi, 0, i, 0))
    side_spec = lambda delta: pl.BlockSpec((None, g, blk, dh),
                                           lambda bi, i: (bi, 0, jnp.clip(2 * i + delta, lo, hi), 0))
    ctx_spec = pl.BlockSpec((None, g, n_ctx, dh), lambda bi, i: (bi, 0, 0, 0))
    return pl.pallas_call(
        functools.partial(_window_kernel, n_ctx_blocks=n_ctx_blocks, n_blocks=nb),
        out_shape=jax.ShapeDtypeStruct((b, g, nb, rows, dh), F32),
        grid=(b, nb // 2),
        in_specs=[pair_spec(hq), side_spec(-1), pair_spec(g), side_spec(2), side_spec(-1), pair_spec(g),
                  side_spec(2), ctx_spec, ctx_spec, pl.BlockSpec((g, rows, 1), lambda bi, i: (0, 0, 0))],
        out_specs=pl.BlockSpec((None, g, 2, rows, dh), lambda bi, i: (bi, 0, i, 0, 0)),
        compiler_params=_cparams(("parallel", "arbitrary")),
    )(q, k, k, k, v, v, v, k, v, sink)


def _retention_kernel(qf_ref, ktf_ref, vf_ref, qb_ref, ktb_ref, vb_ref, dmat_ref, xi_ref, zeta_ref, gch_ref,
                      of_ref, ob_ref, st_ref):
    t = pl.program_id(0)

    @pl.when(t == 0)
    def _():
        st_ref[...] = jnp.zeros_like(st_ref)

    for d, (q_ref, kt_ref, v_ref, o_ref) in enumerate(((qf_ref, ktf_ref, vf_ref, of_ref),
                                                       (qb_ref, ktb_ref, vb_ref, ob_ref))):
        for bi in range(q_ref.shape[0]):
            for hd in range(RET_HEADS):
                q = q_ref[bi, hd]
                kt = kt_ref[bi, hd]
                v = v_ref[bi, hd]
                st = st_ref[d, bi, hd]
                inner = _dot3(q, kt) * dmat_ref[d, hd]
                o_ref[bi, hd] = _dot3(inner, v) + _dot3(q, st) * xi_ref[d, hd]
                st_ref[d, bi, hd] = st * gch_ref[d, hd] + _dot3(kt * zeta_ref[d, hd], v)


def _retention_call(q, kt, v, dmat, xi, zeta, gch, n_ctx_chunks):
    b, hh, lt, dk = q.shape
    dv = v.shape[-1]
    c = RET_CHUNK
    nch = lt // c

    def back(ti):
        return jnp.where(ti < n_ctx_chunks, n_ctx_chunks - 1 - ti, nch - 1 - (ti - n_ctx_chunks))

    fwd = lambda ti: ti
    rows = lambda blk, w: pl.BlockSpec((b, hh, c, w), lambda ti: (0, 0, blk(ti), 0))
    cols = lambda blk: pl.BlockSpec((b, hh, dk, c), lambda ti: (0, 0, 0, blk(ti)))
    tab = lambda a: pl.BlockSpec(a.shape, lambda ti: (0,) * a.ndim)
    out = jax.ShapeDtypeStruct((b, hh, lt, dv), F32)
    return pl.pallas_call(
        _retention_kernel,
        out_shape=(out, out),
        grid=(nch,),
        in_specs=[rows(fwd, dk), cols(fwd), rows(fwd, dv), rows(back, dk), cols(back), rows(back, dv),
                  tab(dmat), tab(xi), tab(zeta), tab(gch)],
        out_specs=(rows(fwd, dv), rows(back, dv)),
        scratch_shapes=[pltpu.VMEM((2, b, hh, dk, dv), F32)],
        compiler_params=_cparams(("arbitrary",)),
    )(q, kt, v, q, kt, v, dmat, xi, zeta, gch)


def _layer_norm_rows(z, ln):
    mu = jnp.mean(z, axis=-1, keepdims=True)
    zc = z - mu
    var = jnp.mean(zc * zc, axis=-1, keepdims=True)
    return zc * lax.rsqrt(var + LN_EPS) * ln[0:1, :] + ln[1:2, :]


def _merge_kernel(fa_ref, fd0_ref, fd1_ref, win_ref, retf_ref, retb_ref, rg_ref, g_ref, h_ref, mod_ref, lam_ref,
                  subln_ref, rnorm_ref, wb_ref, wo_ref, ln_ref, o_ref, *, alpha):
    d = h_ref.shape[1]
    tb = h_ref.shape[0]
    hd = HEAD_DIM
    proj = functools.partial(jnp.dot, preferred_element_type=F32)
    gate = lambda i: g_ref[:, i * d:(i + 1) * d].astype(F32)

    oat = jnp.concatenate([fa_ref[g][:, st * tb:(st + 1) * tb] for g in range(A_KV_HEADS) for st in range(2)],
                          axis=0)
    m = gate(0) * proj(oat.T.astype(BF16), wb_ref[0])

    lam = lam_ref[...]
    heads = []
    for h4 in range(DIFF_HEADS):
        f = (fd0_ref if h4 < 2 else fd1_ref)[h4 % 2]
        o = f[:, :tb] - lam * f[:, tb:]
        heads.append(o * lax.rsqrt(jnp.mean(o * o, axis=0, keepdims=True) + RMS_EPS))
    obt = jnp.concatenate(heads, axis=0) * subln_ref[...]
    m = m + gate(1) * proj(obt.T.astype(BF16), wb_ref[1])

    half = tb // 2
    acc = None
    for h4 in range(WIN_HEADS):
        g, st = h4 // 2, h4 % 2
        o = jnp.concatenate([win_ref[g, 0][st * half:(st + 1) * half], win_ref[g, 1][st * half:(st + 1) * half]],
                            axis=0)
        t = proj(o.astype(BF16), wb_ref[2, h4 * hd:(h4 + 1) * hd, :])
        acc = t if acc is None else acc + t
    m = m + gate(2) * acc

    acc = None
    for h4 in range(RET_HEADS):
        cols = slice(h4 * RET_DV, (h4 + 1) * RET_DV)
        o = retf_ref[h4] + retb_ref[h4]
        mu = jnp.mean(o, axis=-1, keepdims=True)
        oc = o - mu
        var = jnp.mean(oc * oc, axis=-1, keepdims=True)
        on = oc * lax.rsqrt(var + LN_EPS) * rnorm_ref[0:1, cols] + rnorm_ref[1:2, cols]
        gt = rg_ref[:, cols]
        t = proj((on * (gt / (1.0 + jnp.exp(-gt)))).astype(BF16), wb_ref[3, cols, :])
        acc = t if acc is None else acc + t
    m = m + gate(3) * acc

    y = proj(m.astype(BF16), wo_ref[...])
    z = alpha * h_ref[...] + mod_ref[2:3, :] * y
    o_ref[...] = _layer_norm_rows(z, ln_ref[...])


def _merge(flash_out, win_out, ret_out, rg, gates, h, mod, group_of_block, lam, subln, rnorm, wb, wo, ln,
           alpha, b, lt):
    t, d = h.shape
    tb = TOKEN_BLOCK
    bpb = lt // tb
    row = lambda n: pl.BlockSpec((tb, n), lambda i: (i, 0))
    const = lambda a: pl.BlockSpec(a.shape, lambda i: (0,) * a.ndim)
    fspec = lambda gb: pl.BlockSpec(
        (None, 2, None, None) + flash_out.shape[4:],
        lambda i: (i // bpb, gb, (i % bpb + ATT_QTILES - 1) // ATT_QTILES, (i % bpb + ATT_QTILES - 1) % ATT_QTILES,
                   0, 0))
    ret_spec = pl.BlockSpec((None, RET_HEADS, tb, RET_DV), lambda i: (i // bpb, 0, i % bpb, 0))
    return pl.pallas_call(
        functools.partial(_merge_kernel, alpha=alpha),
        out_shape=jax.ShapeDtypeStruct((t, d), F32),
        grid=(t // tb,),
        in_specs=[fspec(0), fspec(1), fspec(2),
                  pl.BlockSpec((None, WIN_KV_HEADS, 2) + win_out.shape[3:], lambda i: (i // bpb, 0, i % bpb, 0, 0)),
                  ret_spec, ret_spec, row(rg.shape[1]), row(N_BRANCH * d), row(d),
                  pl.BlockSpec((None, 6, d), lambda i: (group_of_block(i), 0, 0)),
                  const(lam), const(subln), const(rnorm), const(wb), const(wo), const(ln)],
        out_specs=row(d),
        compiler_params=_cparams(("parallel",)),
    )(flash_out, flash_out, flash_out, win_out, *ret_out, rg, gates, h, mod, lam, subln, rnorm, wb, wo, ln)


def _top_rows(s, n, with_rank=False):
    out = []
    cur = s
    rank = jnp.full(s.shape, float(n), F32) if with_rank else None
    for r in range(n):
        mx = jnp.max(cur, axis=0, keepdims=True)
        out.append(mx)
        if with_rank:
            rank = jnp.where(cur == mx, float(r), rank)
        if r + 1 < n:
            cur = jnp.where(cur == mx, -jnp.inf, cur)
    return (out, rank) if with_rank else out


def _peer_route_kernel(h_ref, mod_ref, wh_ref, wl_ref, sk_ref, xt_ref, rk_ref, b1_ref, nn_ref, az_ref,
                       cand_ref):
    m = mod_ref[...]
    u = h_ref[...] * (1.0 + m[4:5, :]) + m[3:4, :]
    xt_ref[...] = u.T.astype(BF16)
    uh, ul = _split_bf16(u)
    d = functools.partial(jnp.dot, preferred_element_type=F32)
    nk = PEER_NK
    nt = (((1,), (1,)), ((), ()))
    k1 = PEER_TOPK + 1
    for hd in range(PEER_HEADS):
        c0 = 2 * hd * PEER_DQ
        wh = wh_ref[:, c0:c0 + 2 * PEER_DQ]
        wl = wl_ref[:, c0:c0 + 2 * PEER_DQ]
        q = d(uh, wh) + (d(ul, wh) + d(uh, wl))
        st = [_dot3(sk_ref[hd, p], q[:, p * PEER_DQ:(p + 1) * PEER_DQ], nt) for p in range(2)]
        top0 = _top_rows(st[0], k1)
        top1, rank1 = _top_rows(st[1], k1, with_rank=True)
        r = 0
        for p0 in range(k1):
            for p1 in range(k1 // (p0 + 1)):
                cand_ref[r:r + 1, :] = top0[p0] + top1[p1]
                r += 1
        cand_ref[r:, :] = jnp.full((cand_ref.shape[0] - r, cand_ref.shape[1]), -jnp.inf, F32)
        cand = cand_ref[...]
        ctop = _top_rows(cand, k1)
        tau = 0.5 * (ctop[PEER_TOPK - 1] + ctop[PEER_TOPK])
        mx = top0[0] + top1[0]
        z = jnp.sum(jnp.where(cand >= tau, jnp.exp(cand - mx), 0.0), axis=0, keepdims=True)
        th = tau - st[0]
        nn = jnp.zeros_like(th)
        for q in range(PEER_TOPK):
            nn = nn + jnp.where(top1[q] >= th, 1.0, 0.0)
        rk_ref[hd] = rank1.astype(BF16)
        b1_ref[hd] = jnp.exp(st[1] - top1[0]).astype(BF16)
        nn_ref[hd] = nn
        az_ref[hd] = jnp.exp(st[0] - top0[0]) / z


def _peer_route(h, mod, group_of_block, wq_hi, wq_lo, subkeys):
    t, d = h.shape
    tb = TOKEN_BLOCK
    hh, nk = PEER_HEADS, PEER_NK
    st_shape = lambda dt: jax.ShapeDtypeStruct((hh, nk, t), dt)
    st_spec = pl.BlockSpec((hh, nk, tb), lambda i: (0, 0, i))
    nq = wq_hi.shape[1]
    return pl.pallas_call(
        _peer_route_kernel,
        out_shape=(jax.ShapeDtypeStruct((d, t), BF16), st_shape(BF16), st_shape(BF16), st_shape(F32),
                   st_shape(F32)),
        grid=(t // tb,),
        in_specs=[pl.BlockSpec((tb, d), lambda i: (i, 0)),
                  pl.BlockSpec((None, 6, d), lambda i: (group_of_block(i), 0, 0)),
                  pl.BlockSpec((d, nq), lambda i: (0, 0)),
                  pl.BlockSpec((d, nq), lambda i: (0, 0)),
                  pl.BlockSpec((hh, 2, nk, PEER_DQ), lambda i: (0, 0, 0, 0))],
        out_specs=(pl.BlockSpec((d, tb), lambda i: (0, i)), st_spec, st_spec, st_spec, st_spec),
        scratch_shapes=[pltpu.VMEM((PEER_CAND_ROWS, tb), F32)],
        compiler_params=_cparams(("parallel",)),
    )(h, mod, wq_hi, wq_lo, subkeys)


GELU_K1 = -2.0 * math.sqrt(2.0 / math.pi) * LOG2E
GELU_K2 = GELU_K1 * 0.044715


def _gelu_tanh(x):
    return x / (1.0 + jnp.exp2(x * (GELU_K1 + GELU_K2 * (x * x))))


def _peer_dense_kernel(xt_ref, u_ref, vt_ref, vtl_ref, rk_ref, b1_ref, nn_ref, az_ref, yt_ref, g_ref):
    c = pl.program_id(1)
    nk = PEER_NK
    tp = xt_ref.shape[1]
    rows_per_step = u_ref.shape[0] // nk

    @pl.when(c == 0)
    def _():
        yt_ref[...] = jnp.zeros_like(yt_ref)
        g_ref[...] = jnp.zeros_like(g_ref)

    xt = xt_ref[...]
    pre = lambda ii: jnp.dot(u_ref[ii * nk:(ii + 1) * nk, :], xt, preferred_element_type=F32)
    yt_ref[...] += jnp.dot(vt_ref[...], g_ref[...], preferred_element_type=F32)
    act_next = pre(0)
    for ii in range(rows_per_step):
        act = act_next
        if ii + 1 < rows_per_step:
            act_next = pre(ii + 1)
        w = None
        tile = (nk // BF16_ROWS, BF16_ROWS, tp)
        row = lambda ref, hd: jnp.broadcast_to(ref[hd, ii:ii + 1, :], (BF16_ROWS, tp)).astype(BF16)[None]
        for hd in range(PEER_HEADS):
            t = jnp.where(rk_ref[hd].reshape(tile) < row(nn_ref, hd), b1_ref[hd].reshape(tile),
                          jnp.zeros((), BF16)) * row(az_ref, hd)
            w = t if w is None else w + t
        g_ref[ii * nk:(ii + 1) * nk, :] = w.reshape(nk, tp) * _gelu_tanh(act.astype(BF16))

    @pl.when(c == pl.num_programs(1) - 1)
    def _():
        yt_ref[...] += jnp.dot(vtl_ref[...], g_ref[...], preferred_element_type=F32)


def _peer_dense(xt, u, vt, rk, b1, nn, az):
    d, t = xt.shape
    n = u.shape[0]
    tp = PEER_TOKENS
    ec = PEER_EXPERTS
    nc = n // ec
    hh, nk = PEER_HEADS, PEER_NK
    st_spec = pl.BlockSpec((hh, nk, tp), lambda i, c: (0, 0, i))
    row_spec = pl.BlockSpec((hh, ec // nk, tp), lambda i, c: (0, c, i))
    return pl.pallas_call(
        _peer_dense_kernel,
        out_shape=jax.ShapeDtypeStruct((d, t), F32),
        grid=(t // tp, nc),
        in_specs=[pl.BlockSpec((d, tp), lambda i, c: (0, i)),
                  pl.BlockSpec((ec, d), lambda i, c: (c, 0)),
                  pl.BlockSpec((d, ec), lambda i, c: (0, jnp.maximum(c - 1, 0))),
                  pl.BlockSpec((d, ec), lambda i, c: (0, nc - 1)),
                  st_spec, st_spec, row_spec, row_spec],
        out_specs=pl.BlockSpec((d, tp), lambda i, c: (0, i)),
        scratch_shapes=[pltpu.VMEM((ec, tp), BF16)],
        compiler_params=_cparams(("parallel", "arbitrary")),
    )(xt, u, vt, vt, rk, b1, nn, az)


def _resid_ln_kernel(h_ref, yt_ref, mod_ref, ln_ref, o_ref, *, alpha, gate_row):
    z = alpha * h_ref[...] + mod_ref[gate_row:gate_row + 1, :] * yt_ref[...].T
    o_ref[...] = _layer_norm_rows(z, ln_ref[...])


def _resid_ln(h, yt, mod, group_of_block, ln, alpha, gate_row):
    t, d = h.shape
    tb = TOKEN_BLOCK
    row = pl.BlockSpec((tb, d), lambda i: (i, 0))
    return pl.pallas_call(
        functools.partial(_resid_ln_kernel, alpha=alpha, gate_row=gate_row),
        out_shape=jax.ShapeDtypeStruct((t, d), F32),
        grid=(t // tb,),
        in_specs=[row, pl.BlockSpec((d, tb), lambda i: (0, i)),
                  pl.BlockSpec((None, 6, d), lambda i: (group_of_block(i), 0, 0)),
                  pl.BlockSpec((2, d), lambda i: (0, 0))],
        out_specs=row,
        compiler_params=_cparams(("parallel",)),
    )(h, yt, mod, ln)


def _axial_tables(s, n_ctx, d):
    rows = s // GRID_W
    row = jnp.broadcast_to(jnp.arange(rows, dtype=F32)[:, None], (rows, GRID_W)).reshape(-1)
    col = jnp.broadcast_to(jnp.arange(GRID_W, dtype=F32)[None, :], (rows, GRID_W)).reshape(-1)
    quarter = d // 4
    inv = ROPE_THETA ** (-jnp.arange(quarter, dtype=F32) / quarter)
    ar, ac = row[:, None] * inv, col[:, None] * inv
    cos = jnp.concatenate([jnp.cos(ar), jnp.cos(ar), jnp.cos(ac), jnp.cos(ac)], axis=-1)
    sin = jnp.concatenate([-jnp.sin(ar), jnp.sin(ar), -jnp.sin(ac), jnp.sin(ac)], axis=-1)
    cos = jnp.concatenate([jnp.ones((n_ctx, d), F32), cos], axis=0)
    sin = jnp.concatenate([jnp.zeros((n_ctx, d), F32), sin], axis=0)
    return cos, sin


def _rope1d_tables(lt, d):
    half = d // 2
    inv = ROPE_THETA ** (-jnp.arange(half, dtype=F32) / half)
    ang = jnp.arange(lt, dtype=F32)[:, None] * inv
    return (jnp.concatenate([jnp.cos(ang), jnp.cos(ang)], axis=-1),
            jnp.concatenate([-jnp.sin(ang), jnp.sin(ang)], axis=-1))


def kernel(x, c, ctx, c_ctx, w_mod, b_mod, w_in, qk_gain, diff_lambda, diff_subln, win_sink, ret_decay,
           ret_norm, w_branch, w_out, ln_attn, ln_ffn, peer_wq, peer_subkeys, peer_u, peer_v):
    b, s, d = x.shape
    n_ctx = ctx.shape[1]
    depth = w_mod.shape[0]
    lt = n_ctx + s
    t = b * lt
    tb = TOKEN_BLOCK
    assert n_ctx % tb == 0 and s % tb == 0 and t % PEER_TOKENS == 0
    alpha = (2 * depth) ** 0.25
    blocks_per_batch = lt // tb
    ctx_blocks = n_ctx // tb

    def group_of_block(i):
        return jnp.where(i % blocks_per_batch < ctx_blocks, b, i // blocks_per_batch)

    cos64, sin64 = _axial_tables(s, n_ctx, HEAD_DIM)
    cos32, sin32 = _axial_tables(s, n_ctx, DIFF_DIM)
    cos1d, sin1d = _rope1d_tables(lt, RET_DK)
    sc_a = HEAD_DIM ** -0.5 * LOG2E
    sc_d = DIFF_DIM ** -0.5 * LOG2E
    rep = lambda a, n: jnp.tile(a, (1, n))
    tok_part = lambda t64, t32, t1d: [rep(t64, 2), rep(t32, 8), rep(t64, 4) * sc_a, rep(t64, 2), rep(t1d, 4)]
    feat_part = lambda t64, t32, t1d: [rep(t64, 4) * sc_a, rep(t32, 8) * sc_d, rep(t1d, 4) * RET_DK ** -0.5]
    ttok = jnp.concatenate(tok_part(cos64, cos32, cos1d) + tok_part(sin64, sin32, sin1d), axis=1)
    tfeat = jnp.concatenate(feat_part(cos64, cos32, cos1d) + feat_part(sin64, sin32, sin1d), axis=1).T
    swap64 = np.arange(HEAD_DIM) ^ 16
    head_ids = np.arange(2 * HEAD_DIM) // HEAD_DIM
    avg = jnp.asarray((head_ids[:, None] == head_ids[None, :]) / HEAD_DIM, BF16)
    cond8 = jnp.zeros((8, d), F32).at[:b].set(jax.nn.silu(c)).at[b].set(jax.nn.silu(c_ctx))

    h = jnp.concatenate([ctx, x], axis=1).reshape(t, d)

    for l in range(depth):
        mod = _modulation(cond8, w_mod[l], b_mod[l]).reshape(8, 6, d)
        w_mix = w_in[l, :, :MIX_COLS]
        w_gate = w_in[l, :, MIX_COLS:].astype(BF16)
        g0, g1 = qk_gain[l, 0].astype(F32), qk_gain[l, 1].astype(F32)
        gtok = jnp.stack([jnp.tile(g1, 2), jnp.tile(g1[swap64], 2)])
        gfeat = jnp.broadcast_to(jnp.concatenate([jnp.tile(g0, A_HEADS), jnp.tile(g0[swap64], A_HEADS)])[:, None],
                                 (2 * A_HEADS * HEAD_DIM, tb))
        kall, qt, vt, wq, wk, wv, rq, rkt, rv, rg = _mixer_in(
            h, mod, group_of_block, w_mix[:, TOK_COLS].astype(BF16), w_mix[:, FEAT_COLS].T.astype(BF16),
            ttok, tfeat, gtok, gfeat, avg, b, lt)
        gates = _gates(h, mod, group_of_block, w_gate)

        ot = _flash(qt, kall, vt, n_ctx)
        lam_init = 0.8 - 0.6 * math.exp(-0.3 * l)
        lp = diff_lambda[l].astype(F32)
        lam = (jnp.exp(jnp.sum(lp[0] * lp[1])) - jnp.exp(jnp.sum(lp[2] * lp[3])) + lam_init).reshape(1, 1)
        subln = jnp.broadcast_to((jnp.tile(diff_subln[l].astype(F32), DIFF_HEADS) * (1.0 - lam_init))[:, None],
                                 (DIFF_HEADS * HEAD_DIM, tb))

        sink = jnp.repeat(win_sink[l].astype(F32) * LOG2E, WINDOW).reshape(WIN_KV_HEADS, 2 * WINDOW, 1)
        ow = _window(wq, wk, wv, sink, n_ctx)

        lg = jax.nn.log_sigmoid(ret_decay[l].astype(F32))
        idx = jnp.arange(RET_CHUNK, dtype=F32)
        diff = idx[:, None] - idx[None, :]
        lg3 = lg[:, :, None, None]
        dm_f = jnp.exp(jnp.where(diff >= 0, diff * lg3[0], -jnp.inf))
        dm_b = jnp.exp(jnp.where(diff <= 0, -diff * lg3[1], -jnp.inf))
        dmat = jnp.stack([dm_f, dm_b])
        xi = jnp.stack([jnp.exp((idx + 1.0) * lg[0][:, None]), jnp.exp((RET_CHUNK - idx) * lg[1][:, None])])
        zeta = jnp.stack([jnp.exp((RET_CHUNK - 1.0 - idx) * lg[0][:, None]), jnp.exp(idx * lg[1][:, None])])
        gch = jnp.exp(RET_CHUNK * lg)
        o_ret = _retention_call(rq, rkt, rv, dmat, xi[..., None], zeta[:, :, None, :],
                                gch[:, :, None, None], n_ctx // RET_CHUNK)

        h = _merge(ot, ow, o_ret, rg, gates, h, mod, group_of_block, lam, subln, ret_norm[l].astype(F32),
                   w_branch[l].astype(BF16), w_out[l].astype(BF16), ln_attn[l], alpha, b, lt)

        wq_hi, wq_lo = _split_bf16(peer_wq[l])
        xt, rk, b1, nn, az = _peer_route(h, mod, group_of_block, wq_hi, wq_lo, peer_subkeys[l])
        yt = _peer_dense(xt, peer_u[l].astype(BF16), peer_v[l].T.astype(BF16), rk, b1, nn, az)
        h = _resid_ln(h, yt, mod, group_of_block, ln_ffn[l], alpha, 5)

    return h.reshape(b, lt, d)[:, n_ctx:, :]
```

```python
import functools
import math

import numpy as np
import jax
import jax.numpy as jnp
from jax import lax
from jax.experimental import pallas as pl
from jax.experimental.pallas import tpu as pltpu

GRID_W = 64
HEAD_DIM = 64
ROPE_THETA = 10000.0
A_HEADS = 4
A_KV_HEADS = 2
DIFF_HEADS = 4
DIFF_DIM = 32
WIN_HEADS = 4
WIN_KV_HEADS = 2
WINDOW = 128
RET_HEADS = 4
RET_DK = 64
RET_DV = 64
RET_CHUNK = 128
N_BRANCH = 4
PIECES = (
    A_HEADS * HEAD_DIM, A_KV_HEADS * HEAD_DIM, A_KV_HEADS * HEAD_DIM,
    2 * DIFF_HEADS * DIFF_DIM, 2 * DIFF_HEADS * DIFF_DIM, DIFF_HEADS * 2 * DIFF_DIM,
    WIN_HEADS * HEAD_DIM, WIN_KV_HEADS * HEAD_DIM, WIN_KV_HEADS * HEAD_DIM,
    RET_HEADS * RET_DK, RET_HEADS * RET_DK, RET_HEADS * RET_DV, RET_HEADS * RET_DV,
)
MIX_COLS = sum(PIECES)
PIECE_OFF = tuple(int(v) for v in np.cumsum((0,) + PIECES))
PEER_HEADS = 8
PEER_NK = 128
PEER_TOPK = 16
PEER_DQ = 128
PEER_CAND_ROWS = -(-sum((PEER_TOPK + 1) // (p + 1) for p in range(PEER_TOPK + 1)) // 8) * 8
LN_EPS = 1e-5
RMS_EPS = 1e-6
LOG2E = 1.4426950408889634

F32 = jnp.float32
BF16 = jnp.bfloat16

TOKEN_BLOCK = 256
ATT_TK = 512
ATT_UNROLL = 4
ATT_QTILES = 4
ATT_VPAD = 80
PEER_TOKENS = 512
PEER_EXPERTS = 2048
VMEM_LIMIT = 56 * 1024 * 1024
BF16_ROWS = 16


def _cparams(sem):
    return pltpu.CompilerParams(dimension_semantics=sem, vmem_limit_bytes=VMEM_LIMIT)


def _split_bf16(a):
    hi = a.astype(BF16)
    lo = (a - hi.astype(F32)).astype(BF16)
    return hi, lo


def _dot3(a, b, dims=(((1,), (0,)), ((), ()))):
    ah, al = _split_bf16(a)
    bh, bl = _split_bf16(b)
    d = functools.partial(lax.dot_general, dimension_numbers=dims, preferred_element_type=F32)
    return d(ah, bh) + (d(al, bh) + d(ah, bl))


def _mod_kernel(c_ref, w_ref, b_ref, o_ref):
    o_ref[...] = _dot3(c_ref[...], w_ref[...]) + b_ref[...]


def _modulation(cond8, w, b):
    d, n = w.shape
    tn = 1536
    return pl.pallas_call(
        _mod_kernel,
        out_shape=jax.ShapeDtypeStruct((8, n), F32),
        grid=(n // tn,),
        in_specs=[pl.BlockSpec((8, d), lambda j: (0, 0)),
                  pl.BlockSpec((d, tn), lambda j: (0, j)),
                  pl.BlockSpec((1, tn), lambda j: (0, j))],
        out_specs=pl.BlockSpec((8, tn), lambda j: (0, j)),
        compiler_params=_cparams(("arbitrary",)),
    )(cond8, w, b.reshape(1, n))


def _gates_kernel(x_ref, mod_ref, w_ref, o_ref, *, chunk):
    m = mod_ref[...]
    xm = (x_ref[...] * (1.0 + m[1:2, :]) + m[0:1, :]).astype(BF16)
    n = w_ref.shape[1]
    for j in range(n // chunk):
        acc = jnp.dot(xm, w_ref[:, j * chunk:(j + 1) * chunk], preferred_element_type=F32)
        o_ref[:, j * chunk:(j + 1) * chunk] = jax.nn.sigmoid(acc).astype(o_ref.dtype)


def _gates(h, mod, group_of_block, w):
    t, d = h.shape
    n = w.shape[1]
    tb = TOKEN_BLOCK
    return pl.pallas_call(
        functools.partial(_gates_kernel, chunk=512),
        out_shape=jax.ShapeDtypeStruct((t, n), BF16),
        grid=(t // tb,),
        in_specs=[pl.BlockSpec((tb, d), lambda i: (i, 0)),
                  pl.BlockSpec((None, 6, d), lambda i: (group_of_block(i), 0, 0)),
                  pl.BlockSpec((d, n), lambda i: (0, 0))],
        out_specs=pl.BlockSpec((tb, n), lambda i: (i, 0)),
        compiler_params=_cparams(("parallel",)),
    )(h, mod, w)


TOK_PIECES = ((1, True), (4, True), (6, True), (7, True), (8, False), (9, True), (11, False), (12, False))
FEAT_PIECES = ((0, True), (3, True), (2, False), (5, False), (10, True))


def _piece_cols(pieces):
    flips = {0: 16, 1: 16, 3: 8, 4: 8, 6: 16, 7: 16, 9: 32, 10: 32}
    cols, offs = [], {}
    n = 0
    for p, rotary in pieces:
        base = np.arange(PIECE_OFF[p], PIECE_OFF[p + 1])
        offs[p] = n
        cols.append(base)
        n += len(base)
        if rotary:
            cols.append(PIECE_OFF[p] + ((base - PIECE_OFF[p]) ^ flips[p]))
            n += len(base)
    return np.concatenate(cols), offs


TOK_COLS, TOK_OFF = _piece_cols(TOK_PIECES)
FEAT_COLS, FEAT_OFF = _piece_cols(FEAT_PIECES)
TTOK_OFF = {1: 0, 4: 128, 6: 384, 7: 640, 9: 768}
TTOK_W = 1024
TFEAT_OFF = {0: 0, 3: 256, 10: 512}
TFEAT_W = 768


def _mixer_in_kernel(x_ref, mod_ref, wt_ref, wf_ref, ttok_ref, tfeat_ref, gtok_ref, gfeat_ref, avg_ref,
                     kall_ref, qt_ref, vt_ref, wq_ref, wk_ref, wv_ref, rq_ref, rkt_ref, rv_ref, rg_ref):
    m = mod_ref[...]
    u = x_ref[...] * (1.0 + m[1:2, :]) + m[0:1, :]
    xm = u.astype(BF16)
    xmt = u.T.astype(BF16)
    tb = xm.shape[0]

    def tok(p, swapped=False):
        a = TOK_OFF[p] + (PIECES[p] if swapped else 0)
        return jnp.dot(xm, wt_ref[:, a:a + PIECES[p]], preferred_element_type=F32)

    def feat(p, swapped=False):
        a = FEAT_OFF[p] + (PIECES[p] if swapped else 0)
        return jnp.dot(wf_ref[a:a + PIECES[p], :], xmt, preferred_element_type=F32)

    def rope_tok(p, x, xs):
        a = TTOK_OFF[p]
        return x * ttok_ref[:, a:a + PIECES[p]] + xs * ttok_ref[:, TTOK_W + a:TTOK_W + a + PIECES[p]]

    def rope_feat(p, x, xs):
        a = TFEAT_OFF[p]
        return x * tfeat_ref[a:a + PIECES[p], :] + xs * tfeat_ref[TFEAT_W + a:TFEAT_W + a + PIECES[p], :]

    x, xs = tok(1), tok(1, True)
    sq_hi, sq_lo = _split_bf16(x * x)
    avg = avg_ref[...]
    ms = jnp.dot(sq_hi, avg, preferred_element_type=F32) + jnp.dot(sq_lo, avg, preferred_element_type=F32)
    r = lax.rsqrt(ms + RMS_EPS)
    ka = r * rope_tok(1, x * gtok_ref[0:1, :], xs * gtok_ref[1:2, :])
    kd = rope_tok(4, tok(4), tok(4, True))
    kall_ref[:, :PIECES[1]] = ka.astype(BF16)
    kall_ref[:, PIECES[1]:] = kd.astype(BF16)

    def split_heads(ref, val):
        for hd in range(ref.shape[0]):
            ref[hd] = val[:, hd * HEAD_DIM:(hd + 1) * HEAD_DIM].astype(ref.dtype)

    split_heads(wq_ref, rope_tok(6, tok(6), tok(6, True)))
    split_heads(wk_ref, rope_tok(7, tok(7), tok(7, True)))
    split_heads(wv_ref, tok(8))
    split_heads(rq_ref, rope_tok(9, tok(9), tok(9, True)))
    split_heads(rv_ref, tok(11))
    rg_ref[...] = tok(12)
    rkt = rope_feat(10, feat(10), feat(10, True))
    for hd in range(RET_HEADS):
        rkt_ref[hd] = rkt[hd * RET_DK:(hd + 1) * RET_DK]

    zero64 = jnp.zeros((HEAD_DIM, tb), F32)
    zero32 = jnp.zeros((DIFF_DIM, tb), F32)

    def place(q, upper):
        return jnp.concatenate([zero64, q] if upper else [q, zero64], axis=0).astype(BF16)

    xq, xqs = feat(0), feat(0, True)
    for hd in range(A_HEADS):
        rows = slice(hd * HEAD_DIM, (hd + 1) * HEAD_DIM)
        xh = xq[rows]
        rh = lax.rsqrt(jnp.mean(xh * xh, axis=0, keepdims=True) + RMS_EPS)
        swapped_rows = slice(A_HEADS * HEAD_DIM + hd * HEAD_DIM, A_HEADS * HEAD_DIM + (hd + 1) * HEAD_DIM)
        qh = rh * (xh * gfeat_ref[rows, :] * tfeat_ref[rows, :]
                   + xqs[rows] * gfeat_ref[swapped_rows, :]
                   * tfeat_ref[TFEAT_W + hd * HEAD_DIM:TFEAT_W + (hd + 1) * HEAD_DIM, :])
        qt_ref[hd // 2, hd % 2] = place(qh, hd // 2 == 1)
    qd = rope_feat(3, feat(3), feat(3, True))
    for hd in range(DIFF_HEADS):
        qh = qd[hd * HEAD_DIM:(hd + 1) * HEAD_DIM]
        q1 = jnp.concatenate([qh[:DIFF_DIM], zero32], axis=0)
        q2 = jnp.concatenate([zero32, qh[DIFF_DIM:]], axis=0)
        qt_ref[A_KV_HEADS + hd, 0] = place(q1, hd % 2 == 1)
        qt_ref[A_KV_HEADS + hd, 1] = place(q2, hd % 2 == 1)

    pad_rows = vt_ref.shape[1] - HEAD_DIM
    tail = (lax.broadcasted_iota(jnp.int32, (pad_rows, tb), 0) == 0).astype(F32)
    va, vd = feat(2), feat(5)
    for g in range(A_KV_HEADS + DIFF_HEADS):
        v = va[g * HEAD_DIM:(g + 1) * HEAD_DIM] if g < A_KV_HEADS else \
            vd[(g - A_KV_HEADS) * HEAD_DIM:(g - A_KV_HEADS + 1) * HEAD_DIM]
        vt_ref[g] = jnp.concatenate([v, tail], axis=0).astype(BF16)


def _mixer_in(h, mod, group_of_block, wt, wf, ttok, tfeat, gtok, gfeat, avg, b, lt):
    t, d = h.shape
    tb = TOKEN_BLOCK
    bpb = lt // tb
    n_groups = A_KV_HEADS + DIFF_HEADS
    kw = PIECES[1] + PIECES[4]
    row = lambda w: pl.BlockSpec((tb, w), lambda i: (i, 0))
    const = lambda a: pl.BlockSpec(a.shape, lambda i: (0,) * a.ndim)
    heads_shape = lambda n, dt: jax.ShapeDtypeStruct((b, n, lt, HEAD_DIM), dt)
    heads_spec = lambda n: pl.BlockSpec((None, n, tb, HEAD_DIM), lambda i: (i // bpb, 0, i % bpb, 0))
    return pl.pallas_call(
        _mixer_in_kernel,
        out_shape=(jax.ShapeDtypeStruct((b, lt, kw), BF16),
                   jax.ShapeDtypeStruct((b, n_groups, bpb, 2, 2 * HEAD_DIM, tb), BF16),
                   jax.ShapeDtypeStruct((b, n_groups, bpb, ATT_VPAD, tb), BF16),
                   heads_shape(WIN_HEADS, BF16), heads_shape(WIN_KV_HEADS, BF16), heads_shape(WIN_KV_HEADS, BF16),
                   heads_shape(RET_HEADS, F32), jax.ShapeDtypeStruct((b, RET_HEADS, RET_DK, lt), F32),
                   heads_shape(RET_HEADS, F32), jax.ShapeDtypeStruct((t, PIECES[12]), F32)),
        grid=(t // tb,),
        in_specs=[row(d),
                  pl.BlockSpec((None, 6, d), lambda i: (group_of_block(i), 0, 0)),
                  const(wt), const(wf),
                  pl.BlockSpec((tb, 2 * TTOK_W), lambda i: (i % bpb, 0)),
                  pl.BlockSpec((2 * TFEAT_W, tb), lambda i: (0, i % bpb)),
                  const(gtok), const(gfeat), const(avg)],
        out_specs=(pl.BlockSpec((None, tb, kw), lambda i: (i // bpb, i % bpb, 0)),
                   pl.BlockSpec((None, n_groups, None, 2, 2 * HEAD_DIM, tb),
                                lambda i: (i // bpb, 0, i % bpb, 0, 0, 0)),
                   pl.BlockSpec((None, n_groups, None, ATT_VPAD, tb), lambda i: (i // bpb, 0, i % bpb, 0, 0)),
                   heads_spec(WIN_HEADS), heads_spec(WIN_KV_HEADS), heads_spec(WIN_KV_HEADS),
                   heads_spec(RET_HEADS),
                   pl.BlockSpec((None, RET_HEADS, RET_DK, tb), lambda i: (i // bpb, 0, 0, i % bpb)),
                   heads_spec(RET_HEADS), row(PIECES[12])),
        compiler_params=_cparams(("parallel",)),
    )(h, mod, wt, wf, ttok, tfeat, gtok, gfeat, avg)


def _flash_kernel(*refs, n_ctx, tk, nt):
    q_refs, (k_ref, vt_ref, o_ref, s_ref, smax_ref, m_ref, acc_ref) = refs[:nt], refs[nt:]
    i = pl.program_id(2)
    tq = q_refs[0].shape[2]
    qt = jnp.concatenate([q[st] for st in range(2) for q in q_refs], axis=1)
    tile = vt_ref.shape[2]
    nlc = (k_ref.shape[0] - n_ctx) // tk

    def scores(row0, rows):
        return jnp.dot(k_ref[pl.ds(row0, rows), :], qt, preferred_element_type=F32)

    def stage(slot, chunk):
        s = scores(lat_row(chunk), tk)
        s_ref[slot] = s
        smax_ref[slot] = jnp.max(s, axis=0, keepdims=True)

    def absorb(s, smax, tile0):
        m = m_ref[...]
        m_new = jnp.maximum(m, smax)
        p = jnp.exp2(s - m_new).astype(BF16)
        acc = acc_ref[...] * jnp.exp2(m - m_new)
        for j in range(s.shape[0] // tile):
            acc = acc + jnp.dot(vt_ref[tile0 + j], p[j * tile:(j + 1) * tile], preferred_element_type=F32)
        m_ref[...] = m_new
        acc_ref[...] = acc

    m_ref[...] = jnp.full(m_ref.shape, -jnp.inf, F32)
    acc_ref[...] = jnp.zeros(acc_ref.shape, F32)
    s_ctx = scores(0, n_ctx)
    absorb(s_ctx, jnp.max(s_ctx, axis=0, keepdims=True), 0)
    lat_row = lambda c: pl.multiple_of(n_ctx + c * tk, tile)
    lat_tile = lambda c: (n_ctx + c * tk) // tile

    @pl.when(i > 0)
    def _():
        stage(0, 0)
        unroll = ATT_UNROLL

        def trip(j, carry):
            for e in range(unroll):
                c = unroll * j + e
                stage((e + 1) % 2, c + 1)
                absorb(s_ref[e % 2], smax_ref[e % 2], lat_tile(c))
            return carry

        lax.fori_loop(0, nlc // unroll - 1, trip, 0)
        for c in range(nlc - unroll, nlc):
            if c + 1 < nlc:
                stage((c + 1) % 2, c + 1)
            absorb(s_ref[c % 2], smax_ref[c % 2], lat_tile(c))

    acc = acc_ref[...]
    o = acc[:HEAD_DIM] * (1.0 / acc[HEAD_DIM:HEAD_DIM + 1])
    for j in range(nt):
        o_ref[j] = jnp.concatenate([o[:, j * tq:(j + 1) * tq], o[:, (nt + j) * tq:(nt + j + 1) * tq]], axis=1)


def _flash(qt, k_all, vt, n_ctx):
    b, g, tiles, _, dk, tq = qt.shape
    nt = ATT_QTILES
    lt = k_all.shape[1]
    s = lt - n_ctx
    tk = next(c for c in (ATT_TK, 512, 256) if s % (ATT_UNROLL * c) == 0)
    assert n_ctx == tq and tk % tq == 0 and (tiles - 1) % nt == 0
    steps = 1 + (tiles - 1) // nt
    key_block = lambda gi: jnp.where(gi < A_KV_HEADS, 0, 1 + (gi - A_KV_HEADS) // 2)
    qspec = lambda j: pl.BlockSpec(
        (None, None, None, 2, dk, tq),
        lambda bi, gi, i: (bi, gi, jnp.where(i == 0, 0, nt * i - (nt - 1) + j), 0, 0, 0))
    return pl.pallas_call(
        functools.partial(_flash_kernel, n_ctx=n_ctx, tk=tk, nt=nt),
        out_shape=jax.ShapeDtypeStruct((b, g, steps, nt, HEAD_DIM, 2 * tq), F32),
        grid=(b, g, steps),
        in_specs=[qspec(j) for j in range(nt)]
        + [pl.BlockSpec((None, lt, dk), lambda bi, gi, i: (bi, 0, key_block(gi))),
           pl.BlockSpec((None, None) + vt.shape[2:], lambda bi, gi, i: (bi, gi, 0, 0, 0))],
        out_specs=pl.BlockSpec((None, None, None, nt, HEAD_DIM, 2 * tq), lambda bi, gi, i: (bi, gi, i, 0, 0, 0)),
        scratch_shapes=[pltpu.VMEM((2, tk, 2 * nt * tq), F32), pltpu.VMEM((2, 1, 2 * nt * tq), F32),
                        pltpu.VMEM((1, 2 * nt * tq), F32), pltpu.VMEM((vt.shape[3], 2 * nt * tq), F32)],
        compiler_params=_cparams(("parallel", "parallel", "arbitrary")),
    )(*([qt] * nt), k_all, vt)


def _window_kernel(q_ref, kp_ref, kc_ref, kn_ref, vp_ref, vc_ref, vn_ref, kx_ref, vx_ref, sink_ref,
                   o_ref, *, n_ctx_blocks, n_blocks):
    step = pl.program_id(1)
    n_kv = kc_ref.shape[0]
    group = q_ref.shape[0] // n_kv
    rows = group * WINDOW
    nt = (((1,), (1,)), ((), ()))
    sdot = functools.partial(lax.dot_general, dimension_numbers=nt, preferred_element_type=F32)
    pv = functools.partial(jnp.dot, preferred_element_type=F32)
    qi = lax.broadcasted_iota(jnp.int32, (rows, WINDOW), 0) % WINDOW
    kj = lax.broadcasted_iota(jnp.int32, (rows, WINDOW), 1)
    neg = -jnp.inf
    for g in range(n_kv):
        sink = sink_ref[g]
        kx, vx = kx_ref[g], vx_ref[g]
        for j in range(2):
            qb = 2 * step + j
            cur = slice(j * WINDOW, (j + 1) * WINDOW)
            q = q_ref[g * group:(g + 1) * group, cur, :].reshape(rows, HEAD_DIM)
            if j == 0:
                kp, vp, kn, vn = kp_ref[g], vp_ref[g], kc_ref[g, WINDOW:, :], vc_ref[g, WINDOW:, :]
            else:
                kp, vp, kn, vn = kc_ref[g, :WINDOW, :], vc_ref[g, :WINDOW, :], kn_ref[g], vn_ref[g]
            off_p = jnp.where(qb >= n_ctx_blocks + 1, 0, 2 * WINDOW)
            off_c = jnp.where(qb >= n_ctx_blocks, 0, 2 * WINDOW)
            off_n = jnp.where(jnp.logical_and(qb >= n_ctx_blocks, qb <= n_blocks - 2), 0, 2 * WINDOW)
            s_p = jnp.where(kj >= qi + off_p, sdot(q, kp), neg)
            s_c = jnp.where(kj >= off_c, sdot(q, kc_ref[g, cur, :]), neg)
            s_n = jnp.where(kj <= qi - off_n, sdot(q, kn), neg)
            s_x = sdot(q, kx)
            rmax = lambda s: jnp.max(s, axis=1, keepdims=True)
            m = jnp.maximum(jnp.maximum(jnp.maximum(rmax(s_p), rmax(s_c)), jnp.maximum(rmax(s_n), rmax(s_x))), sink)
            e_p, e_c, e_n, e_x = (jnp.exp2(s - m) for s in (s_p, s_c, s_n, s_x))
            rsum = lambda e: jnp.sum(e, axis=1, keepdims=True)
            den = rsum(e_p) + rsum(e_c) + rsum(e_n) + rsum(e_x) + jnp.exp2(sink - m)
            o = (pv(e_p.astype(BF16), vp) + pv(e_c.astype(BF16), vc_ref[g, cur, :])
                 + pv(e_n.astype(BF16), vn) + pv(e_x.astype(BF16), vx))
            o_ref[g, j] = o / den


def _window(q, k, v, sink, n_ctx):
    b, hq, lt, dh = q.shape
    g = k.shape[1]
    blk = WINDOW
    nb = lt // blk
    rows = (hq // g) * blk
    n_ctx_blocks = n_ctx // blk
    assert n_ctx % (2 * blk) == 0 and nb % 2 == 0
    lo, hi = n_ctx_blocks, nb - 1
    pair_spec = lambda heads: pl.BlockSpec((None, heads, 2 * blk, dh), lambda bi, i: (bi, 0, i, 0))
    side_spec = lambda delta: pl.BlockSpec((None, g, blk, dh),
                                           lambda bi, i: (bi, 0, jnp.clip(2 * i + delta, lo, hi), 0))
    ctx_spec = pl.BlockSpec((None, g, n_ctx, dh), lambda bi, i: (bi, 0, 0, 0))
    return pl.pallas_call(
        functools.partial(_window_kernel, n_ctx_blocks=n_ctx_blocks, n_blocks=nb),
        out_shape=jax.ShapeDtypeStruct((b, g, nb, rows, dh), F32),
        grid=(b, nb // 2),
        in_specs=[pair_spec(hq), side_spec(-1), pair_spec(g), side_spec(2), side_spec(-1), pair_spec(g),
                  side_spec(2), ctx_spec, ctx_spec, pl.BlockSpec((g, rows, 1), lambda bi, i: (0, 0, 0))],
        out_specs=pl.BlockSpec((None, g, 2, rows, dh), lambda bi, i: (bi, 0, i, 0, 0)),
        compiler_params=_cparams(("parallel", "arbitrary")),
    )(q, k, k, k, v, v, v, k, v, sink)


def _retention_kernel(qf_ref, ktf_ref, vf_ref, qb_ref, ktb_ref, vb_ref, dmat_ref, xi_ref, zeta_ref, gch_ref,
                      of_ref, ob_ref, st_ref):
    t = pl.program_id(0)

    @pl.when(t == 0)
    def _():
        st_ref[...] = jnp.zeros_like(st_ref)

    for d, (q_ref, kt_ref, v_ref, o_ref) in enumerate(((qf_ref, ktf_ref, vf_ref, of_ref),
                                                       (qb_ref, ktb_ref, vb_ref, ob_ref))):
        for bi in range(q_ref.shape[0]):
            for hd in range(RET_HEADS):
                q = q_ref[bi, hd]
                kt = kt_ref[bi, hd]
                v = v_ref[bi, hd]
                st = st_ref[d, bi, hd]
                inner = _dot3(q, kt) * dmat_ref[d, hd]
                o_ref[bi, hd] = _dot3(inner, v) + _dot3(q, st) * xi_ref[d, hd]
                st_ref[d, bi, hd] = st * gch_ref[d, hd] + _dot3(kt * zeta_ref[d, hd], v)


def _retention_call(q, kt, v, dmat, xi, zeta, gch, n_ctx_chunks):
    b, hh, lt, dk = q.shape
    dv = v.shape[-1]
    c = RET_CHUNK
    nch = lt // c

    def back(ti):
        return jnp.where(ti < n_ctx_chunks, n_ctx_chunks - 1 - ti, nch - 1 - (ti - n_ctx_chunks))

    fwd = lambda ti: ti
    rows = lambda blk, w: pl.BlockSpec((b, hh, c, w), lambda ti: (0, 0, blk(ti), 0))
    cols = lambda blk: pl.BlockSpec((b, hh, dk, c), lambda ti: (0, 0, 0, blk(ti)))
    tab = lambda a: pl.BlockSpec(a.shape, lambda ti: (0,) * a.ndim)
    out = jax.ShapeDtypeStruct((b, hh, lt, dv), F32)
    return pl.pallas_call(
        _retention_kernel,
        out_shape=(out, out),
        grid=(nch,),
        in_specs=[rows(fwd, dk), cols(fwd), rows(fwd, dv), rows(back, dk), cols(back), rows(back, dv),
                  tab(dmat), tab(xi), tab(zeta), tab(gch)],
        out_specs=(rows(fwd, dv), rows(back, dv)),
        scratch_shapes=[pltpu.VMEM((2, b, hh, dk, dv), F32)],
        compiler_params=_cparams(("arbitrary",)),
    )(q, kt, v, q, kt, v, dmat, xi, zeta, gch)


def _layer_norm_rows(z, ln):
    mu = jnp.mean(z, axis=-1, keepdims=True)
    zc = z - mu
    var = jnp.mean(zc * zc, axis=-1, keepdims=True)
    return zc * lax.rsqrt(var + LN_EPS) * ln[0:1, :] + ln[1:2, :]


def _merge_kernel(fa_ref, fd0_ref, fd1_ref, win_ref, retf_ref, retb_ref, rg_ref, g_ref, h_ref, mod_ref, lam_ref,
                  subln_ref, rnorm_ref, wb_ref, wo_ref, ln_ref, o_ref, *, alpha):
    d = h_ref.shape[1]
    tb = h_ref.shape[0]
    hd = HEAD_DIM
    proj = functools.partial(jnp.dot, preferred_element_type=F32)
    gate = lambda i: g_ref[:, i * d:(i + 1) * d].astype(F32)

    oat = jnp.concatenate([fa_ref[g][:, st * tb:(st + 1) * tb] for g in range(A_KV_HEADS) for st in range(2)],
                          axis=0)
    m = gate(0) * proj(oat.T.astype(BF16), wb_ref[0])

    lam = lam_ref[...]
    heads = []
    for h4 in range(DIFF_HEADS):
        f = (fd0_ref if h4 < 2 else fd1_ref)[h4 % 2]
        o = f[:, :tb] - lam * f[:, tb:]
        heads.append(o * lax.rsqrt(jnp.mean(o * o, axis=0, keepdims=True) + RMS_EPS))
    obt = jnp.concatenate(heads, axis=0) * subln_ref[...]
    m = m + gate(1) * proj(obt.T.astype(BF16), wb_ref[1])

    half = tb // 2
    acc = None
    for h4 in range(WIN_HEADS):
        g, st = h4 // 2, h4 % 2
        o = jnp.concatenate([win_ref[g, 0][st * half:(st + 1) * half], win_ref[g, 1][st * half:(st + 1) * half]],
                            axis=0)
        t = proj(o.astype(BF16), wb_ref[2, h4 * hd:(h4 + 1) * hd, :])
        acc = t if acc is None else acc + t
    m = m + gate(2) * acc

    acc = None
    for h4 in range(RET_HEADS):
        cols = slice(h4 * RET_DV, (h4 + 1) * RET_DV)
        o = retf_ref[h4] + retb_ref[h4]
        mu = jnp.mean(o, axis=-1, keepdims=True)
        oc = o - mu
        var = jnp.mean(oc * oc, axis=-1, keepdims=True)
        on = oc * lax.rsqrt(var + LN_EPS) * rnorm_ref[0:1, cols] + rnorm_ref[1:2, cols]
        gt = rg_ref[:, cols]
        t = proj((on * (gt / (1.0 + jnp.exp(-gt)))).astype(BF16), wb_ref[3, cols, :])
        acc = t if acc is None else acc + t
    m = m + gate(3) * acc

    y = proj(m.astype(BF16), wo_ref[...])
    z = alpha * h_ref[...] + mod_ref[2:3, :] * y
    o_ref[...] = _layer_norm_rows(z, ln_ref[...])


def _merge(flash_out, win_out, ret_out, rg, gates, h, mod, group_of_block, lam, subln, rnorm, wb, wo, ln,
           alpha, b, lt):
    t, d = h.shape
    tb = TOKEN_BLOCK
    bpb = lt // tb
    row = lambda n: pl.BlockSpec((tb, n), lambda i: (i, 0))
    const = lambda a: pl.BlockSpec(a.shape, lambda i: (0,) * a.ndim)
    fspec = lambda gb: pl.BlockSpec(
        (None, 2, None, None) + flash_out.shape[4:],
        lambda i: (i // bpb, gb, (i % bpb + ATT_QTILES - 1) // ATT_QTILES, (i % bpb + ATT_QTILES - 1) % ATT_QTILES,
                   0, 0))
    ret_spec = pl.BlockSpec((None, RET_HEADS, tb, RET_DV), lambda i: (i // bpb, 0, i % bpb, 0))
    return pl.pallas_call(
        functools.partial(_merge_kernel, alpha=alpha),
        out_shape=jax.ShapeDtypeStruct((t, d), F32),
        grid=(t // tb,),
        in_specs=[fspec(0), fspec(1), fspec(2),
                  pl.BlockSpec((None, WIN_KV_HEADS, 2) + win_out.shape[3:], lambda i: (i // bpb, 0, i % bpb, 0, 0)),
                  ret_spec, ret_spec, row(rg.shape[1]), row(N_BRANCH * d), row(d),
                  pl.BlockSpec((None, 6, d), lambda i: (group_of_block(i), 0, 0)),
                  const(lam), const(subln), const(rnorm), const(wb), const(wo), const(ln)],
        out_specs=row(d),
        compiler_params=_cparams(("parallel",)),
    )(flash_out, flash_out, flash_out, win_out, *ret_out, rg, gates, h, mod, lam, subln, rnorm, wb, wo, ln)


def _top_rows(s, n, with_rank=False):
    out = []
    cur = s
    rank = jnp.full(s.shape, float(n), F32) if with_rank else None
    for r in range(n):
        mx = jnp.max(cur, axis=0, keepdims=True)
        out.append(mx)
        if with_rank:
            rank = jnp.where(cur == mx, float(r), rank)
        if r + 1 < n:
            cur = jnp.where(cur == mx, -jnp.inf, cur)
    return (out, rank) if with_rank else out


def _peer_route_kernel(h_ref, mod_ref, wh_ref, wl_ref, sk_ref, xt_ref, rk_ref, b1_ref, nn_ref, az_ref,
                       cand_ref):
    m = mod_ref[...]
    u = h_ref[...] * (1.0 + m[4:5, :]) + m[3:4, :]
    xt_ref[...] = u.T.astype(BF16)
    uh, ul = _split_bf16(u)
    d = functools.partial(jnp.dot, preferred_element_type=F32)
    nk = PEER_NK
    nt = (((1,), (1,)), ((), ()))
    k1 = PEER_TOPK + 1
    for hd in range(PEER_HEADS):
        c0 = 2 * hd * PEER_DQ
        wh = wh_ref[:, c0:c0 + 2 * PEER_DQ]
        wl = wl_ref[:, c0:c0 + 2 * PEER_DQ]
        q = d(uh, wh) + (d(ul, wh) + d(uh, wl))
        st = [_dot3(sk_ref[hd, p], q[:, p * PEER_DQ:(p + 1) * PEER_DQ], nt) for p in range(2)]
        top0 = _top_rows(st[0], k1)
        top1, rank1 = _top_rows(st[1], k1, with_rank=True)
        r = 0
        for p0 in range(k1):
            for p1 in range(k1 // (p0 + 1)):
                cand_ref[r:r + 1, :] = top0[p0] + top1[p1]
                r += 1
        cand_ref[r:, :] = jnp.full((cand_ref.shape[0] - r, cand_ref.shape[1]), -jnp.inf, F32)
        cand = cand_ref[...]
        ctop = _top_rows(cand, k1)
        tau = 0.5 * (ctop[PEER_TOPK - 1] + ctop[PEER_TOPK])
        mx = top0[0] + top1[0]
        z = jnp.sum(jnp.where(cand >= tau, jnp.exp(cand - mx), 0.0), axis=0, keepdims=True)
        th = tau - st[0]
        nn = jnp.zeros_like(th)
        for q in range(PEER_TOPK):
            nn = nn + jnp.where(top1[q] >= th, 1.0, 0.0)
        rk_ref[hd] = rank1.astype(BF16)
        b1_ref[hd] = jnp.exp(st[1] - top1[0]).astype(BF16)
        nn_ref[hd] = nn
        az_ref[hd] = jnp.exp(st[0] - top0[0]) / z


def _peer_route(h, mod, group_of_block, wq_hi, wq_lo, subkeys):
    t, d = h.shape
    tb = TOKEN_BLOCK
    hh, nk = PEER_HEADS, PEER_NK
    st_shape = lambda dt: jax.ShapeDtypeStruct((hh, nk, t), dt)
    st_spec = pl.BlockSpec((hh, nk, tb), lambda i: (0, 0, i))
    nq = wq_hi.shape[1]
    return pl.pallas_call(
        _peer_route_kernel,
        out_shape=(jax.ShapeDtypeStruct((d, t), BF16), st_shape(BF16), st_shape(BF16), st_shape(F32),
                   st_shape(F32)),
        grid=(t // tb,),
        in_specs=[pl.BlockSpec((tb, d), lambda i: (i, 0)),
                  pl.BlockSpec((None, 6, d), lambda i: (group_of_block(i), 0, 0)),
                  pl.BlockSpec((d, nq), lambda i: (0, 0)),
                  pl.BlockSpec((d, nq), lambda i: (0, 0)),
                  pl.BlockSpec((hh, 2, nk, PEER_DQ), lambda i: (0, 0, 0, 0))],
        out_specs=(pl.BlockSpec((d, tb), lambda i: (0, i)), st_spec, st_spec, st_spec, st_spec),
        scratch_shapes=[pltpu.VMEM((PEER_CAND_ROWS, tb), F32)],
        compiler_params=_cparams(("parallel",)),
    )(h, mod, wq_hi, wq_lo, subkeys)


GELU_K1 = -2.0 * math.sqrt(2.0 / math.pi) * LOG2E
GELU_K2 = GELU_K1 * 0.044715


def _gelu_tanh(x):
    return x / (1.0 + jnp.exp2(x * (GELU_K1 + GELU_K2 * (x * x))))


def _peer_dense_kernel(xt_ref, u_ref, vt_ref, vtl_ref, rk_ref, b1_ref, nn_ref, az_ref, yt_ref, g_ref):
    c = pl.program_id(1)
    nk = PEER_NK
    tp = xt_ref.shape[1]
    rows_per_step = u_ref.shape[0] // nk

    @pl.when(c == 0)
    def _():
        yt_ref[...] = jnp.zeros_like(yt_ref)
        g_ref[...] = jnp.zeros_like(g_ref)

    xt = xt_ref[...]
    pre = lambda ii: jnp.dot(u_ref[ii * nk:(ii + 1) * nk, :], xt, preferred_element_type=F32)
    yt_ref[...] += jnp.dot(vt_ref[...], g_ref[...], preferred_element_type=F32)
    act_next = pre(0)
    for ii in range(rows_per_step):
        act = act_next
        if ii + 1 < rows_per_step:
            act_next = pre(ii + 1)
        w = None
        tile = (nk // BF16_ROWS, BF16_ROWS, tp)
        row = lambda ref, hd: jnp.broadcast_to(ref[hd, ii:ii + 1, :], (BF16_ROWS, tp)).astype(BF16)[None]
        for hd in range(PEER_HEADS):
            t = jnp.where(rk_ref[hd].reshape(tile) < row(nn_ref, hd), b1_ref[hd].reshape(tile),
                          jnp.zeros((), BF16)) * row(az_ref, hd)
            w = t if w is None else w + t
        g_ref[ii * nk:(ii + 1) * nk, :] = w.reshape(nk, tp) * _gelu_tanh(act.astype(BF16))

    @pl.when(c == pl.num_programs(1) - 1)
    def _():
        yt_ref[...] += jnp.dot(vtl_ref[...], g_ref[...], preferred_element_type=F32)


def _peer_dense(xt, u, vt, rk, b1, nn, az):
    d, t = xt.shape
    n = u.shape[0]
    tp = PEER_TOKENS
    ec = PEER_EXPERTS
    nc = n // ec
    hh, nk = PEER_HEADS, PEER_NK
    st_spec = pl.BlockSpec((hh, nk, tp), lambda i, c: (0, 0, i))
    row_spec = pl.BlockSpec((hh, ec // nk, tp), lambda i, c: (0, c, i))
    return pl.pallas_call(
        _peer_dense_kernel,
        out_shape=jax.ShapeDtypeStruct((d, t), F32),
        grid=(t // tp, nc),
        in_specs=[pl.BlockSpec((d, tp), lambda i, c: (0, i)),
                  pl.BlockSpec((ec, d), lambda i, c: (c, 0)),
                  pl.BlockSpec((d, ec), lambda i, c: (0, jnp.maximum(c - 1, 0))),
                  pl.BlockSpec((d, ec), lambda i, c: (0, nc - 1)),
                  st_spec, st_spec, row_spec, row_spec],
        out_specs=pl.BlockSpec((d, tp), lambda i, c: (0, i)),
        scratch_shapes=[pltpu.VMEM((ec, tp), BF16)],
        compiler_params=_cparams(("parallel", "arbitrary")),
    )(xt, u, vt, vt, rk, b1, nn, az)


def _resid_ln_kernel(h_ref, yt_ref, mod_ref, ln_ref, o_ref, *, alpha, gate_row):
    z = alpha * h_ref[...] + mod_ref[gate_row:gate_row + 1, :] * yt_ref[...].T
    o_ref[...] = _layer_norm_rows(z, ln_ref[...])


def _resid_ln(h, yt, mod, group_of_block, ln, alpha, gate_row):
    t, d = h.shape
    tb = TOKEN_BLOCK
    row = pl.BlockSpec((tb, d), lambda i: (i, 0))
    return pl.pallas_call(
        functools.partial(_resid_ln_kernel, alpha=alpha, gate_row=gate_row),
        out_shape=jax.ShapeDtypeStruct((t, d), F32),
        grid=(t // tb,),
        in_specs=[row, pl.BlockSpec((d, tb), lambda i: (0, i)),
                  pl.BlockSpec((None, 6, d), lambda i: (group_of_block(i), 0, 0)),
                  pl.BlockSpec((2, d), lambda i: (0, 0))],
        out_specs=row,
        compiler_params=_cparams(("parallel",)),
    )(h, yt, mod, ln)


def _axial_tables(s, n_ctx, d):
    rows = s // GRID_W
    row = jnp.broadcast_to(jnp.arange(rows, dtype=F32)[:, None], (rows, GRID_W)).reshape(-1)
    col = jnp.broadcast_to(jnp.arange(GRID_W, dtype=F32)[None, :], (rows, GRID_W)).reshape(-1)
    quarter = d // 4
    inv = ROPE_THETA ** (-jnp.arange(quarter, dtype=F32) / quarter)
    ar, ac = row[:, None] * inv, col[:, None] * inv
    cos = jnp.concatenate([jnp.cos(ar), jnp.cos(ar), jnp.cos(ac), jnp.cos(ac)], axis=-1)
    sin = jnp.concatenate([-jnp.sin(ar), jnp.sin(ar), -jnp.sin(ac), jnp.sin(ac)], axis=-1)
    cos = jnp.concatenate([jnp.ones((n_ctx, d), F32), cos], axis=0)
    sin = jnp.concatenate([jnp.zeros((n_ctx, d), F32), sin], axis=0)
    return cos, sin


def _rope1d_tables(lt, d):
    half = d // 2
    inv = ROPE_THETA ** (-jnp.arange(half, dtype=F32) / half)
    ang = jnp.arange(lt, dtype=F32)[:, None] * inv
    return (jnp.concatenate([jnp.cos(ang), jnp.cos(ang)], axis=-1),
            jnp.concatenate([-jnp.sin(ang), jnp.sin(ang)], axis=-1))


def kernel(x, c, ctx, c_ctx, w_mod, b_mod, w_in, qk_gain, diff_lambda, diff_subln, win_sink, ret_decay,
           ret_norm, w_branch, w_out, ln_attn, ln_ffn, peer_wq, peer_subkeys, peer_u, peer_v):
    b, s, d = x.shape
    n_ctx = ctx.shape[1]
    depth = w_mod.shape[0]
    lt = n_ctx + s
    t = b * lt
    tb = TOKEN_BLOCK
    assert n_ctx % tb == 0 and s % tb == 0 and t % PEER_TOKENS == 0
    alpha = (2 * depth) ** 0.25
    blocks_per_batch = lt // tb
    ctx_blocks = n_ctx // tb

    def group_of_block(i):
        return jnp.where(i % blocks_per_batch < ctx_blocks, b, i // blocks_per_batch)

    cos64, sin64 = _axial_tables(s, n_ctx, HEAD_DIM)
    cos32, sin32 = _axial_tables(s, n_ctx, DIFF_DIM)
    cos1d, sin1d = _rope1d_tables(lt, RET_DK)
    sc_a = HEAD_DIM ** -0.5 * LOG2E
    sc_d = DIFF_DIM ** -0.5 * LOG2E
    rep = lambda a, n: jnp.tile(a, (1, n))
    tok_part = lambda t64, t32, t1d: [rep(t64, 2), rep(t32, 8), rep(t64, 4) * sc_a, rep(t64, 2), rep(t1d, 4)]
    feat_part = lambda t64, t32, t1d: [rep(t64, 4) * sc_a, rep(t32, 8) * sc_d, rep(t1d, 4) * RET_DK ** -0.5]
    ttok = jnp.concatenate(tok_part(cos64, cos32, cos1d) + tok_part(sin64, sin32, sin1d), axis=1)
    tfeat = jnp.concatenate(feat_part(cos64, cos32, cos1d) + feat_part(sin64, sin32, sin1d), axis=1).T
    swap64 = np.arange(HEAD_DIM) ^ 16
    head_ids = np.arange(2 * HEAD_DIM) // HEAD_DIM
    avg = jnp.asarray((head_ids[:, None] == head_ids[None, :]) / HEAD_DIM, BF16)
    cond8 = jnp.zeros((8, d), F32).at[:b].set(jax.nn.silu(c)).at[b].set(jax.nn.silu(c_ctx))

    h = jnp.concatenate([ctx, x], axis=1).reshape(t, d)

    for l in range(depth):
        mod = _modulation(cond8, w_mod[l], b_mod[l]).reshape(8, 6, d)
        w_mix = w_in[l, :, :MIX_COLS]
        w_gate = w_in[l, :, MIX_COLS:].astype(BF16)
        g0, g1 = qk_gain[l, 0].astype(F32), qk_gain[l, 1].astype(F32)
        gtok = jnp.stack([jnp.tile(g1, 2), jnp.tile(g1[swap64], 2)])
        gfeat = jnp.broadcast_to(jnp.concatenate([jnp.tile(g0, A_HEADS), jnp.tile(g0[swap64], A_HEADS)])[:, None],
                                 (2 * A_HEADS * HEAD_DIM, tb))
        kall, qt, vt, wq, wk, wv, rq, rkt, rv, rg = _mixer_in(
            h, mod, group_of_block, w_mix[:, TOK_COLS].astype(BF16), w_mix[:, FEAT_COLS].T.astype(BF16),
            ttok, tfeat, gtok, gfeat, avg, b, lt)
        gates = _gates(h, mod, group_of_block, w_gate)

        ot = _flash(qt, kall, vt, n_ctx)
        lam_init = 0.8 - 0.6 * math.exp(-0.3 * l)
        lp = diff_lambda[l].astype(F32)
        lam = (jnp.exp(jnp.sum(lp[0] * lp[1])) - jnp.exp(jnp.sum(lp[2] * lp[3])) + lam_init).reshape(1, 1)
        subln = jnp.broadcast_to((jnp.tile(diff_subln[l].astype(F32), DIFF_HEADS) * (1.0 - lam_init))[:, None],
                                 (DIFF_HEADS * HEAD_DIM, tb))

        sink = jnp.repeat(win_sink[l].astype(F32) * LOG2E, WINDOW).reshape(WIN_KV_HEADS, 2 * WINDOW, 1)
        ow = _window(wq, wk, wv, sink, n_ctx)

        lg = jax.nn.log_sigmoid(ret_decay[l].astype(F32))
        idx = jnp.arange(RET_CHUNK, dtype=F32)
        diff = idx[:, None] - idx[None, :]
        lg3 = lg[:, :, None, None]
        dm_f = jnp.exp(jnp.where(diff >= 0, diff * lg3[0], -jnp.inf))
        dm_b = jnp.exp(jnp.where(diff <= 0, -diff * lg3[1], -jnp.inf))
        dmat = jnp.stack([dm_f, dm_b])
        xi = jnp.stack([jnp.exp((idx + 1.0) * lg[0][:, None]), jnp.exp((RET_CHUNK - idx) * lg[1][:, None])])
        zeta = jnp.stack([jnp.exp((RET_CHUNK - 1.0 - idx) * lg[0][:, None]), jnp.exp(idx * lg[1][:, None])])
        gch = jnp.exp(RET_CHUNK * lg)
        o_ret = _retention_call(rq, rkt, rv, dmat, xi[..., None], zeta[:, :, None, :],
                                gch[:, :, None, None], n_ctx // RET_CHUNK)

        h = _merge(ot, ow, o_ret, rg, gates, h, mod, group_of_block, lam, subln, ret_norm[l].astype(F32),
                   w_branch[l].astype(BF16), w_out[l].astype(BF16), ln_attn[l], alpha, b, lt)

        wq_hi, wq_lo = _split_bf16(peer_wq[l])
        xt, rk, b1, nn, az = _peer_route(h, mod, group_of_block, wq_hi, wq_lo, peer_subkeys[l])
        yt = _peer_dense(xt, peer_u[l].astype(BF16), peer_v[l].T.astype(BF16), rk, b1, nn, az)
        h = _resid_ln(h, yt, mod, group_of_block, ln_ffn[l], alpha, 5)

    return h.reshape(b, lt, d)[:, n_ctx:, :]
```

```python
import functools
import math

import numpy as np
import jax
import jax.numpy as jnp
from jax import lax
from jax.experimental import pallas as pl
from jax.experimental.pallas import tpu as pltpu

GRID_W = 64
HEAD_DIM = 64
ROPE_THETA = 10000.0
A_HEADS = 4
A_KV_HEADS = 2
DIFF_HEADS = 4
DIFF_DIM = 32
WIN_HEADS = 4
WIN_KV_HEADS = 2
WINDOW = 128
RET_HEADS = 4
RET_DK = 64
RET_DV = 64
RET_CHUNK = 128
N_BRANCH = 4
PIECES = (
    A_HEADS * HEAD_DIM, A_KV_HEADS * HEAD_DIM, A_KV_HEADS * HEAD_DIM,
    2 * DIFF_HEADS * DIFF_DIM, 2 * DIFF_HEADS * DIFF_DIM, DIFF_HEADS * 2 * DIFF_DIM,
    WIN_HEADS * HEAD_DIM, WIN_KV_HEADS * HEAD_DIM, WIN_KV_HEADS * HEAD_DIM,
    RET_HEADS * RET_DK, RET_HEADS * RET_DK, RET_HEADS * RET_DV, RET_HEADS * RET_DV,
)
MIX_COLS = sum(PIECES)
PIECE_OFF = tuple(int(v) for v in np.cumsum((0,) + PIECES))
PEER_HEADS = 8
PEER_NK = 128
PEER_TOPK = 16
PEER_DQ = 128
PEER_CAND_ROWS = -(-sum((PEER_TOPK + 1) // (p + 1) for p in range(PEER_TOPK + 1)) // 8) * 8
LN_EPS = 1e-5
RMS_EPS = 1e-6
LOG2E = 1.4426950408889634

F32 = jnp.float32
BF16 = jnp.bfloat16

TOKEN_BLOCK = 256
ATT_TK = 512
ATT_UNROLL = 4
ATT_QTILES = 4
ATT_VPAD = 80
PEER_TOKENS = 256
PEER_EXPERTS = 1024
VMEM_LIMIT = 56 * 1024 * 1024
BF16_ROWS = 16


def _cparams(sem):
    return pltpu.CompilerParams(dimension_semantics=sem, vmem_limit_bytes=VMEM_LIMIT)


def _split_bf16(a):
    hi = a.astype(BF16)
    lo = (a - hi.astype(F32)).astype(BF16)
    return hi, lo


def _dot3(a, b, dims=(((1,), (0,)), ((), ()))):
    ah, al = _split_bf16(a)
    bh, bl = _split_bf16(b)
    d = functools.partial(lax.dot_general, dimension_numbers=dims, preferred_element_type=F32)
    return d(ah, bh) + (d(al, bh) + d(ah, bl))


def _mod_kernel(c_ref, w_ref, b_ref, o_ref):
    o_ref[...] = _dot3(c_ref[...], w_ref[...]) + b_ref[...]


def _modulation(cond8, w, b):
    d, n = w.shape
    tn = 1536
    return pl.pallas_call(
        _mod_kernel,
        out_shape=jax.ShapeDtypeStruct((8, n), F32),
        grid=(n // tn,),
        in_specs=[pl.BlockSpec((8, d), lambda j: (0, 0)),
                  pl.BlockSpec((d, tn), lambda j: (0, j)),
                  pl.BlockSpec((1, tn), lambda j: (0, j))],
        out_specs=pl.BlockSpec((8, tn), lambda j: (0, j)),
        compiler_params=_cparams(("arbitrary",)),
    )(cond8, w, b.reshape(1, n))


def _gates_kernel(x_ref, mod_ref, w_ref, o_ref, *, chunk):
    m = mod_ref[...]
    xm = (x_ref[...] * (1.0 + m[1:2, :]) + m[0:1, :]).astype(BF16)
    n = w_ref.shape[1]
    for j in range(n // chunk):
        acc = jnp.dot(xm, w_ref[:, j * chunk:(j + 1) * chunk], preferred_element_type=F32)
        o_ref[:, j * chunk:(j + 1) * chunk] = jax.nn.sigmoid(acc).astype(o_ref.dtype)


def _gates(h, mod, group_of_block, w):
    t, d = h.shape
    n = w.shape[1]
    tb = TOKEN_BLOCK
    return pl.pallas_call(
        functools.partial(_gates_kernel, chunk=512),
        out_shape=jax.ShapeDtypeStruct((t, n), BF16),
        grid=(t // tb,),
        in_specs=[pl.BlockSpec((tb, d), lambda i: (i, 0)),
                  pl.BlockSpec((None, 6, d), lambda i: (group_of_block(i), 0, 0)),
                  pl.BlockSpec((d, n), lambda i: (0, 0))],
        out_specs=pl.BlockSpec((tb, n), lambda i: (i, 0)),
        compiler_params=_cparams(("parallel",)),
    )(h, mod, w)


TOK_PIECES = ((1, True), (4, True), (6, True), (7, True), (8, False), (9, True), (11, False), (12, False))
FEAT_PIECES = ((0, True), (3, True), (2, False), (5, False), (10, True))


def _piece_cols(pieces):
    flips = {0: 16, 1: 16, 3: 8, 4: 8, 6: 16, 7: 16, 9: 32, 10: 32}
    cols, offs = [], {}
    n = 0
    for p, rotary in pieces:
        base = np.arange(PIECE_OFF[p], PIECE_OFF[p + 1])
        offs[p] = n
        cols.append(base)
        n += len(base)
        if rotary:
            cols.append(PIECE_OFF[p] + ((base - PIECE_OFF[p]) ^ flips[p]))
            n += len(base)
    return np.concatenate(cols), offs


TOK_COLS, TOK_OFF = _piece_cols(TOK_PIECES)
FEAT_COLS, FEAT_OFF = _piece_cols(FEAT_PIECES)
TTOK_OFF = {1: 0, 4: 128, 6: 384, 7: 640, 9: 768}
TTOK_W = 1024
TFEAT_OFF = {0: 0, 3: 256, 10: 512}
TFEAT_W = 768


def _mixer_in_kernel(x_ref, mod_ref, wt_ref, wf_ref, ttok_ref, tfeat_ref, gtok_ref, gfeat_ref, avg_ref,
                     kall_ref, qt_ref, vt_ref, wq_ref, wk_ref, wv_ref, rq_ref, rkt_ref, rv_ref, rg_ref):
    m = mod_ref[...]
    u = x_ref[...] * (1.0 + m[1:2, :]) + m[0:1, :]
    xm = u.astype(BF16)
    xmt = u.T.astype(BF16)
    tb = xm.shape[0]

    def tok(p, swapped=False):
        a = TOK_OFF[p] + (PIECES[p] if swapped else 0)
        return jnp.dot(xm, wt_ref[:, a:a + PIECES[p]], preferred_element_type=F32)

    def feat(p, swapped=False):
        a = FEAT_OFF[p] + (PIECES[p] if swapped else 0)
        return jnp.dot(wf_ref[a:a + PIECES[p], :], xmt, preferred_element_type=F32)

    def rope_tok(p, x, xs):
        a = TTOK_OFF[p]
        return x * ttok_ref[:, a:a + PIECES[p]] + xs * ttok_ref[:, TTOK_W + a:TTOK_W + a + PIECES[p]]

    def rope_feat(p, x, xs):
        a = TFEAT_OFF[p]
        return x * tfeat_ref[a:a + PIECES[p], :] + xs * tfeat_ref[TFEAT_W + a:TFEAT_W + a + PIECES[p], :]

    x, xs = tok(1), tok(1, True)
    sq_hi, sq_lo = _split_bf16(x * x)
    avg = avg_ref[...]
    ms = jnp.dot(sq_hi, avg, preferred_element_type=F32) + jnp.dot(sq_lo, avg, preferred_element_type=F32)
    r = lax.rsqrt(ms + RMS_EPS)
    ka = r * rope_tok(1, x * gtok_ref[0:1, :], xs * gtok_ref[1:2, :])
    kd = rope_tok(4, tok(4), tok(4, True))
    kall_ref[:, :PIECES[1]] = ka.astype(BF16)
    kall_ref[:, PIECES[1]:] = kd.astype(BF16)

    def split_heads(ref, val):
        for hd in range(ref.shape[0]):
            ref[hd] = val[:, hd * HEAD_DIM:(hd + 1) * HEAD_DIM].astype(ref.dtype)

    split_heads(wq_ref, rope_tok(6, tok(6), tok(6, True)))
    split_heads(wk_ref, rope_tok(7, tok(7), tok(7, True)))
    split_heads(wv_ref, tok(8))
    split_heads(rq_ref, rope_tok(9, tok(9), tok(9, True)))
    split_heads(rv_ref, tok(11))
    rg_ref[...] = tok(12)
    rkt = rope_feat(10, feat(10), feat(10, True))
    for hd in range(RET_HEADS):
        rkt_ref[hd] = rkt[hd * RET_DK:(hd + 1) * RET_DK]

    zero64 = jnp.zeros((HEAD_DIM, tb), F32)
    zero32 = jnp.zeros((DIFF_DIM, tb), F32)

    def place(q, upper):
        return jnp.concatenate([zero64, q] if upper else [q, zero64], axis=0).astype(BF16)

    xq, xqs = feat(0), feat(0, True)
    for hd in range(A_HEADS):
        rows = slice(hd * HEAD_DIM, (hd + 1) * HEAD_DIM)
        xh = xq[rows]
        rh = lax.rsqrt(jnp.mean(xh * xh, axis=0, keepdims=True) + RMS_EPS)
        swapped_rows = slice(A_HEADS * HEAD_DIM + hd * HEAD_DIM, A_HEADS * HEAD_DIM + (hd + 1) * HEAD_DIM)
        qh = rh * (xh * gfeat_ref[rows, :] * tfeat_ref[rows, :]
                   + xqs[rows] * gfeat_ref[swapped_rows, :]
                   * tfeat_ref[TFEAT_W + hd * HEAD_DIM:TFEAT_W + (hd + 1) * HEAD_DIM, :])
        qt_ref[hd // 2, hd % 2] = place(qh, hd // 2 == 1)
    qd = rope_feat(3, feat(3), feat(3, True))
    for hd in range(DIFF_HEADS):
        qh = qd[hd * HEAD_DIM:(hd + 1) * HEAD_DIM]
        q1 = jnp.concatenate([qh[:DIFF_DIM], zero32], axis=0)
        q2 = jnp.concatenate([zero32, qh[DIFF_DIM:]], axis=0)
        qt_ref[A_KV_HEADS + hd, 0] = place(q1, hd % 2 == 1)
        qt_ref[A_KV_HEADS + hd, 1] = place(q2, hd % 2 == 1)

    pad_rows = vt_ref.shape[1] - HEAD_DIM
    tail = (lax.broadcasted_iota(jnp.int32, (pad_rows, tb), 0) == 0).astype(F32)
    va, vd = feat(2), feat(5)
    for g in range(A_KV_HEADS + DIFF_HEADS):
        v = va[g * HEAD_DIM:(g + 1) * HEAD_DIM] if g < A_KV_HEADS else \
            vd[(g - A_KV_HEADS) * HEAD_DIM:(g - A_KV_HEADS + 1) * HEAD_DIM]
        vt_ref[g] = jnp.concatenate([v, tail], axis=0).astype(BF16)


def _mixer_in(h, mod, group_of_block, wt, wf, ttok, tfeat, gtok, gfeat, avg, b, lt):
    t, d = h.shape
    tb = TOKEN_BLOCK
    bpb = lt // tb
    n_groups = A_KV_HEADS + DIFF_HEADS
    kw = PIECES[1] + PIECES[4]
    row = lambda w: pl.BlockSpec((tb, w), lambda i: (i, 0))
    const = lambda a: pl.BlockSpec(a.shape, lambda i: (0,) * a.ndim)
    heads_shape = lambda n, dt: jax.ShapeDtypeStruct((b, n, lt, HEAD_DIM), dt)
    heads_spec = lambda n: pl.BlockSpec((None, n, tb, HEAD_DIM), lambda i: (i // bpb, 0, i % bpb, 0))
    return pl.pallas_call(
        _mixer_in_kernel,
        out_shape=(jax.ShapeDtypeStruct((b, lt, kw), BF16),
                   jax.ShapeDtypeStruct((b, n_groups, bpb, 2, 2 * HEAD_DIM, tb), BF16),
                   jax.ShapeDtypeStruct((b, n_groups, bpb, ATT_VPAD, tb), BF16),
                   heads_shape(WIN_HEADS, BF16), heads_shape(WIN_KV_HEADS, BF16), heads_shape(WIN_KV_HEADS, BF16),
                   heads_shape(RET_HEADS, F32), jax.ShapeDtypeStruct((b, RET_HEADS, RET_DK, lt), F32),
                   heads_shape(RET_HEADS, F32), jax.ShapeDtypeStruct((t, PIECES[12]), F32)),
        grid=(t // tb,),
        in_specs=[row(d),
                  pl.BlockSpec((None, 6, d), lambda i: (group_of_block(i), 0, 0)),
                  const(wt), const(wf),
                  pl.BlockSpec((tb, 2 * TTOK_W), lambda i: (i % bpb, 0)),
                  pl.BlockSpec((2 * TFEAT_W, tb), lambda i: (0, i % bpb)),
                  const(gtok), const(gfeat), const(avg)],
        out_specs=(pl.BlockSpec((None, tb, kw), lambda i: (i // bpb, i % bpb, 0)),
                   pl.BlockSpec((None, n_groups, None, 2, 2 * HEAD_DIM, tb),
                                lambda i: (i // bpb, 0, i % bpb, 0, 0, 0)),
                   pl.BlockSpec((None, n_groups, None, ATT_VPAD, tb), lambda i: (i // bpb, 0, i % bpb, 0, 0)),
                   heads_spec(WIN_HEADS), heads_spec(WIN_KV_HEADS), heads_spec(WIN_KV_HEADS),
                   heads_spec(RET_HEADS),
                   pl.BlockSpec((None, RET_HEADS, RET_DK, tb), lambda i: (i // bpb, 0, 0, i % bpb)),
                   heads_spec(RET_HEADS), row(PIECES[12])),
        compiler_params=_cparams(("parallel",)),
    )(h, mod, wt, wf, ttok, tfeat, gtok, gfeat, avg)


def _flash_kernel(*refs, n_ctx, tk, nt):
    q_refs, (k_ref, vt_ref, o_ref, s_ref, smax_ref, m_ref, acc_ref) = refs[:nt], refs[nt:]
    i = pl.program_id(2)
    tq = q_refs[0].shape[2]
    qt = jnp.concatenate([q[st] for st in range(2) for q in q_refs], axis=1)
    tile = vt_ref.shape[2]
    nlc = (k_ref.shape[0] - n_ctx) // tk

    def scores(row0, rows):
        return jnp.dot(k_ref[pl.ds(row0, rows), :], qt, preferred_element_type=F32)

    def stage(slot, chunk):
        s = scores(lat_row(chunk), tk)
        s_ref[slot] = s
        smax_ref[slot] = jnp.max(s, axis=0, keepdims=True)

    def absorb(s, smax, tile0):
        m = m_ref[...]
        m_new = jnp.maximum(m, smax)
        p = jnp.exp2(s - m_new).astype(BF16)
        acc = acc_ref[...] * jnp.exp2(m - m_new)
        for j in range(s.shape[0] // tile):
            acc = acc + jnp.dot(vt_ref[tile0 + j], p[j * tile:(j + 1) * tile], preferred_element_type=F32)
        m_ref[...] = m_new
        acc_ref[...] = acc

    m_ref[...] = jnp.full(m_ref.shape, -jnp.inf, F32)
    acc_ref[...] = jnp.zeros(acc_ref.shape, F32)
    s_ctx = scores(0, n_ctx)
    absorb(s_ctx, jnp.max(s_ctx, axis=0, keepdims=True), 0)
    lat_row = lambda c: pl.multiple_of(n_ctx + c * tk, tile)
    lat_tile = lambda c: (n_ctx + c * tk) // tile

    @pl.when(i > 0)
    def _():
        stage(0, 0)
        unroll = ATT_UNROLL

        def trip(j, carry):
            for e in range(unroll):
                c = unroll * j + e
                stage((e + 1) % 2, c + 1)
                absorb(s_ref[e % 2], smax_ref[e % 2], lat_tile(c))
            return carry

        lax.fori_loop(0, nlc // unroll - 1, trip, 0)
        for c in range(nlc - unroll, nlc):
            if c + 1 < nlc:
                stage((c + 1) % 2, c + 1)
            absorb(s_ref[c % 2], smax_ref[c % 2], lat_tile(c))

    acc = acc_ref[...]
    o = acc[:HEAD_DIM] * (1.0 / acc[HEAD_DIM:HEAD_DIM + 1])
    for j in range(nt):
        o_ref[j] = jnp.concatenate([o[:, j * tq:(j + 1) * tq], o[:, (nt + j) * tq:(nt + j + 1) * tq]], axis=1)


def _flash(qt, k_all, vt, n_ctx):
    b, g, tiles, _, dk, tq = qt.shape
    nt = ATT_QTILES
    lt = k_all.shape[1]
    s = lt - n_ctx
    tk = next(c for c in (ATT_TK, 512, 256) if s % (ATT_UNROLL * c) == 0)
    assert n_ctx == tq and tk % tq == 0 and (tiles - 1) % nt == 0
    steps = 1 + (tiles - 1) // nt
    key_block = lambda gi: jnp.where(gi < A_KV_HEADS, 0, 1 + (gi - A_KV_HEADS) // 2)
    qspec = lambda j: pl.BlockSpec(
        (None, None, None, 2, dk, tq),
        lambda bi, gi, i: (bi, gi, jnp.where(i == 0, 0, nt * i - (nt - 1) + j), 0, 0, 0))
    return pl.pallas_call(
        functools.partial(_flash_kernel, n_ctx=n_ctx, tk=tk, nt=nt),
        out_shape=jax.ShapeDtypeStruct((b, g, steps, nt, HEAD_DIM, 2 * tq), F32),
        grid=(b, g, steps),
        in_specs=[qspec(j) for j in range(nt)]
        + [pl.BlockSpec((None, lt, dk), lambda bi, gi, i: (bi, 0, key_block(gi))),
           pl.BlockSpec((None, None) + vt.shape[2:], lambda bi, gi, i: (bi, gi, 0, 0, 0))],
        out_specs=pl.BlockSpec((None, None, None, nt, HEAD_DIM, 2 * tq), lambda bi, gi, i: (bi, gi, i, 0, 0, 0)),
        scratch_shapes=[pltpu.VMEM((2, tk, 2 * nt * tq), F32), pltpu.VMEM((2, 1, 2 * nt * tq), F32),
                        pltpu.VMEM((1, 2 * nt * tq), F32), pltpu.VMEM((vt.shape[3], 2 * nt * tq), F32)],
        compiler_params=_cparams(("parallel", "parallel", "arbitrary")),
    )(*([qt] * nt), k_all, vt)


def _window_kernel(q_ref, kp_ref, kc_ref, kn_ref, vp_ref, vc_ref, vn_ref, kx_ref, vx_ref, sink_ref,
                   o_ref, *, n_ctx_blocks, n_blocks):
    step = pl.program_id(1)
    n_kv = kc_ref.shape[0]
    group = q_ref.shape[0] // n_kv
    rows = group * WINDOW
    nt = (((1,), (1,)), ((), ()))
    sdot = functools.partial(lax.dot_general, dimension_numbers=nt, preferred_element_type=F32)
    pv = functools.partial(jnp.dot, preferred_element_type=F32)
    qi = lax.broadcasted_iota(jnp.int32, (rows, WINDOW), 0) % WINDOW
    kj = lax.broadcasted_iota(jnp.int32, (rows, WINDOW), 1)
    neg = -jnp.inf
    for g in range(n_kv):
        sink = sink_ref[g]
        kx, vx = kx_ref[g], vx_ref[g]
        for j in range(2):
            qb = 2 * step + j
            cur = slice(j * WINDOW, (j + 1) * WINDOW)
            q = q_ref[g * group:(g + 1) * group, cur, :].reshape(rows, HEAD_DIM)
            if j == 0:
                kp, vp, kn, vn = kp_ref[g], vp_ref[g], kc_ref[g, WINDOW:, :], vc_ref[g, WINDOW:, :]
            else:
                kp, vp, kn, vn = kc_ref[g, :WINDOW, :], vc_ref[g, :WINDOW, :], kn_ref[g], vn_ref[g]
            off_p = jnp.where(qb >= n_ctx_blocks + 1, 0, 2 * WINDOW)
            off_c = jnp.where(qb >= n_ctx_blocks, 0, 2 * WINDOW)
            off_n = jnp.where(jnp.logical_and(qb >= n_ctx_blocks, qb <= n_blocks - 2), 0, 2 * WINDOW)
            s_p = jnp.where(kj >= qi + off_p, sdot(q, kp), neg)
            s_c = jnp.where(kj >= off_c, sdot(q, kc_ref[g, cur, :]), neg)
            s_n = jnp.where(kj <= qi - off_n, sdot(q, kn), neg)
            s_x = sdot(q, kx)
            rmax = lambda s: jnp.max(s, axis=1, keepdims=True)
            m = jnp.maximum(jnp.maximum(jnp.maximum(rmax(s_p), rmax(s_c)), jnp.maximum(rmax(s_n), rmax(s_x))), sink)
            e_p, e_c, e_n, e_x = (jnp.exp2(s - m) for s in (s_p, s_c, s_n, s_x))
            rsum = lambda e: jnp.sum(e, axis=1, keepdims=True)
            den = rsum(e_p) + rsum(e_c) + rsum(e_n) + rsum(e_x) + jnp.exp2(sink - m)
            o = (pv(e_p.astype(BF16), vp) + pv(e_c.astype(BF16), vc_ref[g, cur, :])
                 + pv(e_n.astype(BF16), vn) + pv(e_x.astype(BF16), vx))
            o_ref[g, j] = o / den


def _window(q, k, v, sink, n_ctx):
    b, hq, lt, dh = q.shape
    g = k.shape[1]
    blk = WINDOW
    nb = lt // blk
    rows = (hq // g) * blk
    n_ctx_blocks = n_ctx // blk
    assert n_ctx % (2 * blk) == 0 and nb % 2 == 0
    lo, hi = n_ctx_blocks, nb - 1
    pair_spec = lambda heads: pl.BlockSpec((None, heads, 2 * blk, dh), lambda bi, i: (bi, 0, i, 0))
    side_spec = lambda delta: pl.BlockSpec((None, g, blk, dh),
                                           lambda bi, i: (bi, 0, jnp.clip(2 * i + delta, lo, hi), 0))
    ctx_spec = pl.BlockSpec((None, g, n_ctx, dh), lambda bi, i: (bi, 0, 0, 0))
    return pl.pallas_call(
        functools.partial(_window_kernel, n_ctx_blocks=n_ctx_blocks, n_blocks=nb),
        out_shape=jax.ShapeDtypeStruct((b, g, nb, rows, dh), F32),
        grid=(b, nb // 2),
        in_specs=[pair_spec(hq), side_spec(-1), pair_spec(g), side_spec(2), side_spec(-1), pair_spec(g),
                  side_spec(2), ctx_spec, ctx_spec, pl.BlockSpec((g, rows, 1), lambda bi, i: (0, 0, 0))],
        out_specs=pl.BlockSpec((None, g, 2, rows, dh), lambda bi, i: (bi, 0, i, 0, 0)),
        compiler_params=_cparams(("parallel", "arbitrary")),
    )(q, k, k, k, v, v, v, k, v, sink)


def _retention_kernel(qf_ref, ktf_ref, vf_ref, qb_ref, ktb_ref, vb_ref, dmat_ref, xi_ref, zeta_ref, gch_ref,
                      of_ref, ob_ref, st_ref):
    t = pl.program_id(0)

    @pl.when(t == 0)
    def _():
        st_ref[...] = jnp.zeros_like(st_ref)

    for d, (q_ref, kt_ref, v_ref, o_ref) in enumerate(((qf_ref, ktf_ref, vf_ref, of_ref),
                                                       (qb_ref, ktb_ref, vb_ref, ob_ref))):
        for bi in range(q_ref.shape[0]):
            for hd in range(RET_HEADS):
                q = q_ref[bi, hd]
                kt = kt_ref[bi, hd]
                v = v_ref[bi, hd]
                st = st_ref[d, bi, hd]
                inner = _dot3(q, kt) * dmat_ref[d, hd]
                o_ref[bi, hd] = _dot3(inner, v) + _dot3(q, st) * xi_ref[d, hd]
                st_ref[d, bi, hd] = st * gch_ref[d, hd] + _dot3(kt * zeta_ref[d, hd], v)


def _retention_call(q, kt, v, dmat, xi, zeta, gch, n_ctx_chunks):
    b, hh, lt, dk = q.shape
    dv = v.shape[-1]
    c = RET_CHUNK
    nch = lt // c

    def back(ti):
        return jnp.where(ti < n_ctx_chunks, n_ctx_chunks - 1 - ti, nch - 1 - (ti - n_ctx_chunks))

    fwd = lambda ti: ti
    rows = lambda blk, w: pl.BlockSpec((b, hh, c, w), lambda ti: (0, 0, blk(ti), 0))
    cols = lambda blk: pl.BlockSpec((b, hh, dk, c), lambda ti: (0, 0, 0, blk(ti)))
    tab = lambda a: pl.BlockSpec(a.shape, lambda ti: (0,) * a.ndim)
    out = jax.ShapeDtypeStruct((b, hh, lt, dv), F32)
    return pl.pallas_call(
        _retention_kernel,
        out_shape=(out, out),
        grid=(nch,),
        in_specs=[rows(fwd, dk), cols(fwd), rows(fwd, dv), rows(back, dk), cols(back), rows(back, dv),
                  tab(dmat), tab(xi), tab(zeta), tab(gch)],
        out_specs=(rows(fwd, dv), rows(back, dv)),
        scratch_shapes=[pltpu.VMEM((2, b, hh, dk, dv), F32)],
        compiler_params=_cparams(("arbitrary",)),
    )(q, kt, v, q, kt, v, dmat, xi, zeta, gch)


def _layer_norm_rows(z, ln):
    mu = jnp.mean(z, axis=-1, keepdims=True)
    zc = z - mu
    var = jnp.mean(zc * zc, axis=-1, keepdims=True)
    return zc * lax.rsqrt(var + LN_EPS) * ln[0:1, :] + ln[1:2, :]


def _merge_kernel(fa_ref, fd0_ref, fd1_ref, win_ref, retf_ref, retb_ref, rg_ref, g_ref, h_ref, mod_ref, lam_ref,
                  subln_ref, rnorm_ref, wb_ref, wo_ref, ln_ref, o_ref, *, alpha):
    d = h_ref.shape[1]
    tb = h_ref.shape[0]
    hd = HEAD_DIM
    proj = functools.partial(jnp.dot, preferred_element_type=F32)
    gate = lambda i: g_ref[:, i * d:(i + 1) * d].astype(F32)

    oat = jnp.concatenate([fa_ref[g][:, st * tb:(st + 1) * tb] for g in range(A_KV_HEADS) for st in range(2)],
                          axis=0)
    m = gate(0) * proj(oat.T.astype(BF16), wb_ref[0])

    lam = lam_ref[...]
    heads = []
    for h4 in range(DIFF_HEADS):
        f = (fd0_ref if h4 < 2 else fd1_ref)[h4 % 2]
        o = f[:, :tb] - lam * f[:, tb:]
        heads.append(o * lax.rsqrt(jnp.mean(o * o, axis=0, keepdims=True) + RMS_EPS))
    obt = jnp.concatenate(heads, axis=0) * subln_ref[...]
    m = m + gate(1) * proj(obt.T.astype(BF16), wb_ref[1])

    half = tb // 2
    acc = None
    for h4 in range(WIN_HEADS):
        g, st = h4 // 2, h4 % 2
        o = jnp.concatenate([win_ref[g, 0][st * half:(st + 1) * half], win_ref[g, 1][st * half:(st + 1) * half]],
                            axis=0)
        t = proj(o.astype(BF16), wb_ref[2, h4 * hd:(h4 + 1) * hd, :])
        acc = t if acc is None else acc + t
    m = m + gate(2) * acc

    acc = None
    for h4 in range(RET_HEADS):
        cols = slice(h4 * RET_DV, (h4 + 1) * RET_DV)
        o = retf_ref[h4] + retb_ref[h4]
        mu = jnp.mean(o, axis=-1, keepdims=True)
        oc = o - mu
        var = jnp.mean(oc * oc, axis=-1, keepdims=True)
        on = oc * lax.rsqrt(var + LN_EPS) * rnorm_ref[0:1, cols] + rnorm_ref[1:2, cols]
        gt = rg_ref[:, cols]
        t = proj((on * (gt / (1.0 + jnp.exp(-gt)))).astype(BF16), wb_ref[3, cols, :])
        acc = t if acc is None else acc + t
    m = m + gate(3) * acc

    y = proj(m.astype(BF16), wo_ref[...])
    z = alpha * h_ref[...] + mod_ref[2:3, :] * y
    o_ref[...] = _layer_norm_rows(z, ln_ref[...])


def _merge(flash_out, win_out, ret_out, rg, gates, h, mod, group_of_block, lam, subln, rnorm, wb, wo, ln,
           alpha, b, lt):
    t, d = h.shape
    tb = TOKEN_BLOCK
    bpb = lt // tb
    row = lambda n: pl.BlockSpec((tb, n), lambda i: (i, 0))
    const = lambda a: pl.BlockSpec(a.shape, lambda i: (0,) * a.ndim)
    fspec = lambda gb: pl.BlockSpec(
        (None, 2, None, None) + flash_out.shape[4:],
        lambda i: (i // bpb, gb, (i % bpb + ATT_QTILES - 1) // ATT_QTILES, (i % bpb + ATT_QTILES - 1) % ATT_QTILES,
                   0, 0))
    ret_spec = pl.BlockSpec((None, RET_HEADS, tb, RET_DV), lambda i: (i // bpb, 0, i % bpb, 0))
    return pl.pallas_call(
        functools.partial(_merge_kernel, alpha=alpha),
        out_shape=jax.ShapeDtypeStruct((t, d), F32),
        grid=(t // tb,),
        in_specs=[fspec(0), fspec(1), fspec(2),
                  pl.BlockSpec((None, WIN_KV_HEADS, 2) + win_out.shape[3:], lambda i: (i // bpb, 0, i % bpb, 0, 0)),
                  ret_spec, ret_spec, row(rg.shape[1]), row(N_BRANCH * d), row(d),
                  pl.BlockSpec((None, 6, d), lambda i: (group_of_block(i), 0, 0)),
                  const(lam), const(subln), const(rnorm), const(wb), const(wo), const(ln)],
        out_specs=row(d),
        compiler_params=_cparams(("parallel",)),
    )(flash_out, flash_out, flash_out, win_out, *ret_out, rg, gates, h, mod, lam, subln, rnorm, wb, wo, ln)


def _top_rows(s, n, with_rank=False):
    out = []
    cur = s
    rank = jnp.full(s.shape, float(n), F32) if with_rank else None
    for r in range(n):
        mx = jnp.max(cur, axis=0, keepdims=True)
        out.append(mx)
        if with_rank:
            rank = jnp.where(cur == mx, float(r), rank)
        if r + 1 < n:
            cur = jnp.where(cur == mx, -jnp.inf, cur)
    return (out, rank) if with_rank else out


def _peer_route_kernel(h_ref, mod_ref, wh_ref, wl_ref, sk_ref, xt_ref, rk_ref, b1_ref, nn_ref, az_ref,
                       cand_ref):
    m = mod_ref[...]
    u = h_ref[...] * (1.0 + m[4:5, :]) + m[3:4, :]
    xt_ref[...] = u.T.astype(BF16)
    uh, ul = _split_bf16(u)
    d = functools.partial(jnp.dot, preferred_element_type=F32)
    nk = PEER_NK
    nt = (((1,), (1,)), ((), ()))
    k1 = PEER_TOPK + 1
    for hd in range(PEER_HEADS):
        c0 = 2 * hd * PEER_DQ
        wh = wh_ref[:, c0:c0 + 2 * PEER_DQ]
        wl = wl_ref[:, c0:c0 + 2 * PEER_DQ]
        q = d(uh, wh) + (d(ul, wh) + d(uh, wl))
        st = [_dot3(sk_ref[hd, p], q[:, p * PEER_DQ:(p + 1) * PEER_DQ], nt) for p in range(2)]
        top0 = _top_rows(st[0], k1)
        top1, rank1 = _top_rows(st[1], k1, with_rank=True)
        r = 0
        for p0 in range(k1):
            for p1 in range(k1 // (p0 + 1)):
                cand_ref[r:r + 1, :] = top0[p0] + top1[p1]
                r += 1
        cand_ref[r:, :] = jnp.full((cand_ref.shape[0] - r, cand_ref.shape[1]), -jnp.inf, F32)
        cand = cand_ref[...]
        ctop = _top_rows(cand, k1)
        tau = 0.5 * (ctop[PEER_TOPK - 1] + ctop[PEER_TOPK])
        mx = top0[0] + top1[0]
        z = jnp.sum(jnp.where(cand >= tau, jnp.exp(cand - mx), 0.0), axis=0, keepdims=True)
        th = tau - st[0]
        nn = jnp.zeros_like(th)
        for q in range(PEER_TOPK):
            nn = nn + jnp.where(top1[q] >= th, 1.0, 0.0)
        rk_ref[hd] = rank1.astype(BF16)
        b1_ref[hd] = jnp.exp(st[1] - top1[0]).astype(BF16)
        nn_ref[hd] = nn
        az_ref[hd] = jnp.exp(st[0] - top0[0]) / z


def _peer_route(h, mod, group_of_block, wq_hi, wq_lo, subkeys):
    t, d = h.shape
    tb = TOKEN_BLOCK
    hh, nk = PEER_HEADS, PEER_NK
    st_shape = lambda dt: jax.ShapeDtypeStruct((hh, nk, t), dt)
    st_spec = pl.BlockSpec((hh, nk, tb), lambda i: (0, 0, i))
    nq = wq_hi.shape[1]
    return pl.pallas_call(
        _peer_route_kernel,
        out_shape=(jax.ShapeDtypeStruct((d, t), BF16), st_shape(BF16), st_shape(BF16), st_shape(F32),
                   st_shape(F32)),
        grid=(t // tb,),
        in_specs=[pl.BlockSpec((tb, d), lambda i: (i, 0)),
                  pl.BlockSpec((None, 6, d), lambda i: (group_of_block(i), 0, 0)),
                  pl.BlockSpec((d, nq), lambda i: (0, 0)),
                  pl.BlockSpec((d, nq), lambda i: (0, 0)),
                  pl.BlockSpec((hh, 2, nk, PEER_DQ), lambda i: (0, 0, 0, 0))],
        out_specs=(pl.BlockSpec((d, tb), lambda i: (0, i)), st_spec, st_spec, st_spec, st_spec),
        scratch_shapes=[pltpu.VMEM((PEER_CAND_ROWS, tb), F32)],
        compiler_params=_cparams(("parallel",)),
    )(h, mod, wq_hi, wq_lo, subkeys)


GELU_K1 = -2.0 * math.sqrt(2.0 / math.pi) * LOG2E
GELU_K2 = GELU_K1 * 0.044715


def _gelu_tanh(x):
    return x / (1.0 + jnp.exp2(x * (GELU_K1 + GELU_K2 * (x * x))))


def _peer_dense_kernel(xt_ref, u_ref, vt_ref, vtl_ref, rk_ref, b1_ref, nn_ref, az_ref, yt_ref, g_ref):
    c = pl.program_id(1)
    nk = PEER_NK
    tp = xt_ref.shape[1]
    rows_per_step = u_ref.shape[0] // nk

    @pl.when(c == 0)
    def _():
        yt_ref[...] = jnp.zeros_like(yt_ref)
        g_ref[...] = jnp.zeros_like(g_ref)

    xt = xt_ref[...]
    pre = lambda ii: jnp.dot(u_ref[ii * nk:(ii + 1) * nk, :], xt, preferred_element_type=F32)
    yt_ref[...] += jnp.dot(vt_ref[...], g_ref[...], preferred_element_type=F32)
    act_next = pre(0)
    for ii in range(rows_per_step):
        act = act_next
        if ii + 1 < rows_per_step:
            act_next = pre(ii + 1)
        w = None
        tile = (nk // BF16_ROWS, BF16_ROWS, tp)
        row = lambda ref, hd: jnp.broadcast_to(ref[hd, ii:ii + 1, :], (BF16_ROWS, tp)).astype(BF16)[None]
        for hd in range(PEER_HEADS):
            t = jnp.where(rk_ref[hd].reshape(tile) < row(nn_ref, hd), b1_ref[hd].reshape(tile),
                          jnp.zeros((), BF16)) * row(az_ref, hd)
            w = t if w is None else w + t
        g_ref[ii * nk:(ii + 1) * nk, :] = w.reshape(nk, tp) * _gelu_tanh(act.astype(BF16))

    @pl.when(c == pl.num_programs(1) - 1)
    def _():
        yt_ref[...] += jnp.dot(vtl_ref[...], g_ref[...], preferred_element_type=F32)


def _peer_dense(xt, u, vt, rk, b1, nn, az):
    d, t = xt.shape
    n = u.shape[0]
    tp = PEER_TOKENS
    ec = PEER_EXPERTS
    nc = n // ec
    hh, nk = PEER_HEADS, PEER_NK
    st_spec = pl.BlockSpec((hh, nk, tp), lambda i, c: (0, 0, i))
    row_spec = pl.BlockSpec((hh, ec // nk, tp), lambda i, c: (0, c, i))
    return pl.pallas_call(
        _peer_dense_kernel,
        out_shape=jax.ShapeDtypeStruct((d, t), F32),
        grid=(t // tp, nc),
        in_specs=[pl.BlockSpec((d, tp), lambda i, c: (0, i)),
                  pl.BlockSpec((ec, d), lambda i, c: (c, 0)),
                  pl.BlockSpec((d, ec), lambda i, c: (0, jnp.maximum(c - 1, 0))),
                  pl.BlockSpec((d, ec), lambda i, c: (0, nc - 1)),
                  st_spec, st_spec, row_spec, row_spec],
        out_specs=pl.BlockSpec((d, tp), lambda i, c: (0, i)),
        scratch_shapes=[pltpu.VMEM((ec, tp), BF16)],
        compiler_params=_cparams(("parallel", "arbitrary")),
    )(xt, u, vt, vt, rk, b1, nn, az)


def _resid_ln_kernel(h_ref, yt_ref, mod_ref, ln_ref, o_ref, *, alpha, gate_row):
    z = alpha * h_ref[...] + mod_ref[gate_row:gate_row + 1, :] * yt_ref[...].T
    o_ref[...] = _layer_norm_rows(z, ln_ref[...])


def _resid_ln(h, yt, mod, group_of_block, ln, alpha, gate_row):
    t, d = h.shape
    tb = TOKEN_BLOCK
    row = pl.BlockSpec((tb, d), lambda i: (i, 0))
    return pl.pallas_call(
        functools.partial(_resid_ln_kernel, alpha=alpha, gate_row=gate_row),
        out_shape=jax.ShapeDtypeStruct((t, d), F32),
        grid=(t // tb,),
        in_specs=[row, pl.BlockSpec((d, tb), lambda i: (0, i)),
                  pl.BlockSpec((None, 6, d), lambda i: (group_of_block(i), 0, 0)),
                  pl.BlockSpec((2, d), lambda i: (0, 0))],
        out_specs=row,
        compiler_params=_cparams(("parallel",)),
    )(h, yt, mod, ln)


def _axial_tables(s, n_ctx, d):
    rows = s // GRID_W
    row = jnp.broadcast_to(jnp.arange(rows, dtype=F32)[:, None], (rows, GRID_W)).reshape(-1)
    col = jnp.broadcast_to(jnp.arange(GRID_W, dtype=F32)[None, :], (rows, GRID_W)).reshape(-1)
    quarter = d // 4
    inv = ROPE_THETA ** (-jnp.arange(quarter, dtype=F32) / quarter)
    ar, ac = row[:, None] * inv, col[:, None] * inv
    cos = jnp.concatenate([jnp.cos(ar), jnp.cos(ar), jnp.cos(ac), jnp.cos(ac)], axis=-1)
    sin = jnp.concatenate([-jnp.sin(ar), jnp.sin(ar), -jnp.sin(ac), jnp.sin(ac)], axis=-1)
    cos = jnp.concatenate([jnp.ones((n_ctx, d), F32), cos], axis=0)
    sin = jnp.concatenate([jnp.zeros((n_ctx, d), F32), sin], axis=0)
    return cos, sin


def _rope1d_tables(lt, d):
    half = d // 2
    inv = ROPE_THETA ** (-jnp.arange(half, dtype=F32) / half)
    ang = jnp.arange(lt, dtype=F32)[:, None] * inv
    return (jnp.concatenate([jnp.cos(ang), jnp.cos(ang)], axis=-1),
            jnp.concatenate([-jnp.sin(ang), jnp.sin(ang)], axis=-1))


def kernel(x, c, ctx, c_ctx, w_mod, b_mod, w_in, qk_gain, diff_lambda, diff_subln, win_sink, ret_decay,
           ret_norm, w_branch, w_out, ln_attn, ln_ffn, peer_wq, peer_subkeys, peer_u, peer_v):
    b, s, d = x.shape
    n_ctx = ctx.shape[1]
    depth = w_mod.shape[0]
    lt = n_ctx + s
    t = b * lt
    tb = TOKEN_BLOCK
    assert n_ctx % tb == 0 and s % tb == 0 and t % PEER_TOKENS == 0
    alpha = (2 * depth) ** 0.25
    blocks_per_batch = lt // tb
    ctx_blocks = n_ctx // tb

    def group_of_block(i):
        return jnp.where(i % blocks_per_batch < ctx_blocks, b, i // blocks_per_batch)

    cos64, sin64 = _axial_tables(s, n_ctx, HEAD_DIM)
    cos32, sin32 = _axial_tables(s, n_ctx, DIFF_DIM)
    cos1d, sin1d = _rope1d_tables(lt, RET_DK)
    sc_a = HEAD_DIM ** -0.5 * LOG2E
    sc_d = DIFF_DIM ** -0.5 * LOG2E
    rep = lambda a, n: jnp.tile(a, (1, n))
    tok_part = lambda t64, t32, t1d: [rep(t64, 2), rep(t32, 8), rep(t64, 4) * sc_a, rep(t64, 2), rep(t1d, 4)]
    feat_part = lambda t64, t32, t1d: [rep(t64, 4) * sc_a, rep(t32, 8) * sc_d, rep(t1d, 4) * RET_DK ** -0.5]
    ttok = jnp.concatenate(tok_part(cos64, cos32, cos1d) + tok_part(sin64, sin32, sin1d), axis=1)
    tfeat = jnp.concatenate(feat_part(cos64, cos32, cos1d) + feat_part(sin64, sin32, sin1d), axis=1).T
    swap64 = np.arange(HEAD_DIM) ^ 16
    head_ids = np.arange(2 * HEAD_DIM) // HEAD_DIM
    avg = jnp.asarray((head_ids[:, None] == head_ids[None, :]) / HEAD_DIM, BF16)
    cond8 = jnp.zeros((8, d), F32).at[:b].set(jax.nn.silu(c)).at[b].set(jax.nn.silu(c_ctx))

    h = jnp.concatenate([ctx, x], axis=1).reshape(t, d)

    for l in range(depth):
        mod = _modulation(cond8, w_mod[l], b_mod[l]).reshape(8, 6, d)
        w_mix = w_in[l, :, :MIX_COLS]
        w_gate = w_in[l, :, MIX_COLS:].astype(BF16)
        g0, g1 = qk_gain[l, 0].astype(F32), qk_gain[l, 1].astype(F32)
        gtok = jnp.stack([jnp.tile(g1, 2), jnp.tile(g1[swap64], 2)])
        gfeat = jnp.broadcast_to(jnp.concatenate([jnp.tile(g0, A_HEADS), jnp.tile(g0[swap64], A_HEADS)])[:, None],
                                 (2 * A_HEADS * HEAD_DIM, tb))
        kall, qt, vt, wq, wk, wv, rq, rkt, rv, rg = _mixer_in(
            h, mod, group_of_block, w_mix[:, TOK_COLS].astype(BF16), w_mix[:, FEAT_COLS].T.astype(BF16),
            ttok, tfeat, gtok, gfeat, avg, b, lt)
        gates = _gates(h, mod, group_of_block, w_gate)

        ot = _flash(qt, kall, vt, n_ctx)
        lam_init = 0.8 - 0.6 * math.exp(-0.3 * l)
        lp = diff_lambda[l].astype(F32)
        lam = (jnp.exp(jnp.sum(lp[0] * lp[1])) - jnp.exp(jnp.sum(lp[2] * lp[3])) + lam_init).reshape(1, 1)
        subln = jnp.broadcast_to((jnp.tile(diff_subln[l].astype(F32), DIFF_HEADS) * (1.0 - lam_init))[:, None],
                                 (DIFF_HEADS * HEAD_DIM, tb))

        sink = jnp.repeat(win_sink[l].astype(F32) * LOG2E, WINDOW).reshape(WIN_KV_HEADS, 2 * WINDOW, 1)
        ow = _window(wq, wk, wv, sink, n_ctx)

        lg = jax.nn.log_sigmoid(ret_decay[l].astype(F32))
        idx = jnp.arange(RET_CHUNK, dtype=F32)
        diff = idx[:, None] - idx[None, :]
        lg3 = lg[:, :, None, None]
        dm_f = jnp.exp(jnp.where(diff >= 0, diff * lg3[0], -jnp.inf))
        dm_b = jnp.exp(jnp.where(diff <= 0, -diff * lg3[1], -jnp.inf))
        dmat = jnp.stack([dm_f, dm_b])
        xi = jnp.stack([jnp.exp((idx + 1.0) * lg[0][:, None]), jnp.exp((RET_CHUNK - idx) * lg[1][:, None])])
        zeta = jnp.stack([jnp.exp((RET_CHUNK - 1.0 - idx) * lg[0][:, None]), jnp.exp(idx * lg[1][:, None])])
        gch = jnp.exp(RET_CHUNK * lg)
        o_ret = _retention_call(rq, rkt, rv, dmat, xi[..., None], zeta[:, :, None, :],
                                gch[:, :, None, None], n_ctx // RET_CHUNK)

        h = _merge(ot, ow, o_ret, rg, gates, h, mod, group_of_block, lam, subln, ret_norm[l].astype(F32),
                   w_branch[l].astype(BF16), w_out[l].astype(BF16), ln_attn[l], alpha, b, lt)

        wq_hi, wq_lo = _split_bf16(peer_wq[l])
        xt, rk, b1, nn, az = _peer_route(h, mod, group_of_block, wq_hi, wq_lo, peer_subkeys[l])
        yt = _peer_dense(xt, peer_u[l].astype(BF16), peer_v[l].T.astype(BF16), rk, b1, nn, az)
        h = _resid_ln(h, yt, mod, group_of_block, ln_ffn[l], alpha, 5)

    return h.reshape(b, lt, d)[:, n_ctx:, :]
```

```python
import functools
import math

import numpy as np
import jax
import jax.numpy as jnp
from jax import lax
from jax.experimental import pallas as pl
from jax.experimental.pallas import tpu as pltpu

GRID_W = 64
HEAD_DIM = 64
ROPE_THETA = 10000.0
A_HEADS = 4
A_KV_HEADS = 2
DIFF_HEADS = 4
DIFF_DIM = 32
WIN_HEADS = 4
WIN_KV_HEADS = 2
WINDOW = 128
RET_HEADS = 4
RET_DK = 64
RET_DV = 64
RET_CHUNK = 128
N_BRANCH = 4
PIECES = (
    A_HEADS * HEAD_DIM, A_KV_HEADS * HEAD_DIM, A_KV_HEADS * HEAD_DIM,
    2 * DIFF_HEADS * DIFF_DIM, 2 * DIFF_HEADS * DIFF_DIM, DIFF_HEADS * 2 * DIFF_DIM,
    WIN_HEADS * HEAD_DIM, WIN_KV_HEADS * HEAD_DIM, WIN_KV_HEADS * HEAD_DIM,
    RET_HEADS * RET_DK, RET_HEADS * RET_DK, RET_HEADS * RET_DV, RET_HEADS * RET_DV,
)
MIX_COLS = sum(PIECES)
PIECE_OFF = tuple(int(v) for v in np.cumsum((0,) + PIECES))
PEER_HEADS = 8
PEER_NK = 128
PEER_TOPK = 16
PEER_DQ = 128
PEER_CAND_ROWS = -(-sum((PEER_TOPK + 1) // (p + 1) for p in range(PEER_TOPK + 1)) // 8) * 8
LN_EPS = 1e-5
RMS_EPS = 1e-6
LOG2E = 1.4426950408889634

F32 = jnp.float32
BF16 = jnp.bfloat16

TOKEN_BLOCK = 256
ATT_TK = 512
ATT_UNROLL = 4
ATT_QTILES = 4
ATT_VPAD = 80
PEER_TOKENS = 512
PEER_EXPERTS = 1024
VMEM_LIMIT = 56 * 1024 * 1024
BF16_ROWS = 16


def _cparams(sem):
    return pltpu.CompilerParams(dimension_semantics=sem, vmem_limit_bytes=VMEM_LIMIT)


def _split_bf16(a):
    hi = a.astype(BF16)
    lo = (a - hi.astype(F32)).astype(BF16)
    return hi, lo


def _dot3(a, b, dims=(((1,), (0,)), ((), ()))):
    ah, al = _split_bf16(a)
    bh, bl = _split_bf16(b)
    d = functools.partial(lax.dot_general, dimension_numbers=dims, preferred_element_type=F32)
    return d(ah, bh) + (d(al, bh) + d(ah, bl))


def _mod_kernel(c_ref, w_ref, b_ref, o_ref):
    o_ref[...] = _dot3(c_ref[...], w_ref[...]) + b_ref[...]


def _modulation(cond8, w, b):
    d, n = w.shape
    tn = 1536
    return pl.pallas_call(
        _mod_kernel,
        out_shape=jax.ShapeDtypeStruct((8, n), F32),
        grid=(n // tn,),
        in_specs=[pl.BlockSpec((8, d), lambda j: (0, 0)),
                  pl.BlockSpec((d, tn), lambda j: (0, j)),
                  pl.BlockSpec((1, tn), lambda j: (0, j))],
        out_specs=pl.BlockSpec((8, tn), lambda j: (0, j)),
        compiler_params=_cparams(("arbitrary",)),
    )(cond8, w, b.reshape(1, n))


def _gates_kernel(x_ref, mod_ref, w_ref, o_ref, *, chunk):
    m = mod_ref[...]
    xm = (x_ref[...] * (1.0 + m[1:2, :]) + m[0:1, :]).astype(BF16)
    n = w_ref.shape[1]
    for j in range(n // chunk):
        acc = jnp.dot(xm, w_ref[:, j * chunk:(j + 1) * chunk], preferred_element_type=F32)
        o_ref[:, j * chunk:(j + 1) * chunk] = jax.nn.sigmoid(acc).astype(o_ref.dtype)


def _gates(h, mod, group_of_block, w):
    t, d = h.shape
    n = w.shape[1]
    tb = TOKEN_BLOCK
    return pl.pallas_call(
        functools.partial(_gates_kernel, chunk=512),
        out_shape=jax.ShapeDtypeStruct((t, n), BF16),
        grid=(t // tb,),
        in_specs=[pl.BlockSpec((tb, d), lambda i: (i, 0)),
                  pl.BlockSpec((None, 6, d), lambda i: (group_of_block(i), 0, 0)),
                  pl.BlockSpec((d, n), lambda i: (0, 0))],
        out_specs=pl.BlockSpec((tb, n), lambda i: (i, 0)),
        compiler_params=_cparams(("parallel",)),
    )(h, mod, w)


TOK_PIECES = ((1, True), (4, True), (6, True), (7, True), (8, False), (9, True), (11, False), (12, False))
FEAT_PIECES = ((0, True), (3, True), (2, False), (5, False), (10, True))


def _piece_cols(pieces):
    flips = {0: 16, 1: 16, 3: 8, 4: 8, 6: 16, 7: 16, 9: 32, 10: 32}
    cols, offs = [], {}
    n = 0
    for p, rotary in pieces:
        base = np.arange(PIECE_OFF[p], PIECE_OFF[p + 1])
        offs[p] = n
        cols.append(base)
        n += len(base)
        if rotary:
            cols.append(PIECE_OFF[p] + ((base - PIECE_OFF[p]) ^ flips[p]))
            n += len(base)
    return np.concatenate(cols), offs


TOK_COLS, TOK_OFF = _piece_cols(TOK_PIECES)
FEAT_COLS, FEAT_OFF = _piece_cols(FEAT_PIECES)
TTOK_OFF = {1: 0, 4: 128, 6: 384, 7: 640, 9: 768}
TTOK_W = 1024
TFEAT_OFF = {0: 0, 3: 256, 10: 512}
TFEAT_W = 768


def _mixer_in_kernel(x_ref, mod_ref, wt_ref, wf_ref, ttok_ref, tfeat_ref, gtok_ref, gfeat_ref, avg_ref,
                     kall_ref, qt_ref, vt_ref, wq_ref, wk_ref, wv_ref, rq_ref, rkt_ref, rv_ref, rg_ref):
    m = mod_ref[...]
    u = x_ref[...] * (1.0 + m[1:2, :]) + m[0:1, :]
    xm = u.astype(BF16)
    xmt = u.T.astype(BF16)
    tb = xm.shape[0]

    def tok(p, swapped=False):
        a = TOK_OFF[p] + (PIECES[p] if swapped else 0)
        return jnp.dot(xm, wt_ref[:, a:a + PIECES[p]], preferred_element_type=F32)

    def feat(p, swapped=False):
        a = FEAT_OFF[p] + (PIECES[p] if swapped else 0)
        return jnp.dot(wf_ref[a:a + PIECES[p], :], xmt, preferred_element_type=F32)

    def rope_tok(p, x, xs):
        a = TTOK_OFF[p]
        return x * ttok_ref[:, a:a + PIECES[p]] + xs * ttok_ref[:, TTOK_W + a:TTOK_W + a + PIECES[p]]

    def rope_feat(p, x, xs):
        a = TFEAT_OFF[p]
        return x * tfeat_ref[a:a + PIECES[p], :] + xs * tfeat_ref[TFEAT_W + a:TFEAT_W + a + PIECES[p], :]

    x, xs = tok(1), tok(1, True)
    sq_hi, sq_lo = _split_bf16(x * x)
    avg = avg_ref[...]
    ms = jnp.dot(sq_hi, avg, preferred_element_type=F32) + jnp.dot(sq_lo, avg, preferred_element_type=F32)
    r = lax.rsqrt(ms + RMS_EPS)
    ka = r * rope_tok(1, x * gtok_ref[0:1, :], xs * gtok_ref[1:2, :])
    kd = rope_tok(4, tok(4), tok(4, True))
    kall_ref[:, :PIECES[1]] = ka.astype(BF16)
    kall_ref[:, PIECES[1]:] = kd.astype(BF16)

    def split_heads(ref, val):
        for hd in range(ref.shape[0]):
            ref[hd] = val[:, hd * HEAD_DIM:(hd + 1) * HEAD_DIM].astype(ref.dtype)

    split_heads(wq_ref, rope_tok(6, tok(6), tok(6, True)))
    split_heads(wk_ref, rope_tok(7, tok(7), tok(7, True)))
    split_heads(wv_ref, tok(8))
    split_heads(rq_ref, rope_tok(9, tok(9), tok(9, True)))
    split_heads(rv_ref, tok(11))
    rg_ref[...] = tok(12)
    rkt = rope_feat(10, feat(10), feat(10, True))
    for hd in range(RET_HEADS):
        rkt_ref[hd] = rkt[hd * RET_DK:(hd + 1) * RET_DK]

    zero64 = jnp.zeros((HEAD_DIM, tb), F32)
    zero32 = jnp.zeros((DIFF_DIM, tb), F32)

    def place(q, upper):
        return jnp.concatenate([zero64, q] if upper else [q, zero64], axis=0).astype(BF16)

    xq, xqs = feat(0), feat(0, True)
    for hd in range(A_HEADS):
        rows = slice(hd * HEAD_DIM, (hd + 1) * HEAD_DIM)
        xh = xq[rows]
        rh = lax.rsqrt(jnp.mean(xh * xh, axis=0, keepdims=True) + RMS_EPS)
        swapped_rows = slice(A_HEADS * HEAD_DIM + hd * HEAD_DIM, A_HEADS * HEAD_DIM + (hd + 1) * HEAD_DIM)
        qh = rh * (xh * gfeat_ref[rows, :] * tfeat_ref[rows, :]
                   + xqs[rows] * gfeat_ref[swapped_rows, :]
                   * tfeat_ref[TFEAT_W + hd * HEAD_DIM:TFEAT_W + (hd + 1) * HEAD_DIM, :])
        qt_ref[hd // 2, hd % 2] = place(qh, hd // 2 == 1)
    qd = rope_feat(3, feat(3), feat(3, True))
    for hd in range(DIFF_HEADS):
        qh = qd[hd * HEAD_DIM:(hd + 1) * HEAD_DIM]
        q1 = jnp.concatenate([qh[:DIFF_DIM], zero32], axis=0)
        q2 = jnp.concatenate([zero32, qh[DIFF_DIM:]], axis=0)
        qt_ref[A_KV_HEADS + hd, 0] = place(q1, hd % 2 == 1)
        qt_ref[A_KV_HEADS + hd, 1] = place(q2, hd % 2 == 1)

    pad_rows = vt_ref.shape[1] - HEAD_DIM
    tail = (lax.broadcasted_iota(jnp.int32, (pad_rows, tb), 0) == 0).astype(F32)
    va, vd = feat(2), feat(5)
    for g in range(A_KV_HEADS + DIFF_HEADS):
        v = va[g * HEAD_DIM:(g + 1) * HEAD_DIM] if g < A_KV_HEADS else \
            vd[(g - A_KV_HEADS) * HEAD_DIM:(g - A_KV_HEADS + 1) * HEAD_DIM]
        vt_ref[g] = jnp.concatenate([v, tail], axis=0).astype(BF16)


def _mixer_in(h, mod, group_of_block, wt, wf, ttok, tfeat, gtok, gfeat, avg, b, lt):
    t, d = h.shape
    tb = TOKEN_BLOCK
    bpb = lt // tb
    n_groups = A_KV_HEADS + DIFF_HEADS
    kw = PIECES[1] + PIECES[4]
    row = lambda w: pl.BlockSpec((tb, w), lambda i: (i, 0))
    const = lambda a: pl.BlockSpec(a.shape, lambda i: (0,) * a.ndim)
    heads_shape = lambda n, dt: jax.ShapeDtypeStruct((b, n, lt, HEAD_DIM), dt)
    heads_spec = lambda n: pl.BlockSpec((None, n, tb, HEAD_DIM), lambda i: (i // bpb, 0, i % bpb, 0))
    return pl.pallas_call(
        _mixer_in_kernel,
        out_shape=(jax.ShapeDtypeStruct((b, lt, kw), BF16),
                   jax.ShapeDtypeStruct((b, n_groups, bpb, 2, 2 * HEAD_DIM, tb), BF16),
                   jax.ShapeDtypeStruct((b, n_groups, bpb, ATT_VPAD, tb), BF16),
                   heads_shape(WIN_HEADS, BF16), heads_shape(WIN_KV_HEADS, BF16), heads_shape(WIN_KV_HEADS, BF16),
                   heads_shape(RET_HEADS, F32), jax.ShapeDtypeStruct((b, RET_HEADS, RET_DK, lt), F32),
                   heads_shape(RET_HEADS, F32), jax.ShapeDtypeStruct((t, PIECES[12]), F32)),
        grid=(t // tb,),
        in_specs=[row(d),
                  pl.BlockSpec((None, 6, d), lambda i: (group_of_block(i), 0, 0)),
                  const(wt), const(wf),
                  pl.BlockSpec((tb, 2 * TTOK_W), lambda i: (i % bpb, 0)),
                  pl.BlockSpec((2 * TFEAT_W, tb), lambda i: (0, i % bpb)),
                  const(gtok), const(gfeat), const(avg)],
        out_specs=(pl.BlockSpec((None, tb, kw), lambda i: (i // bpb, i % bpb, 0)),
                   pl.BlockSpec((None, n_groups, None, 2, 2 * HEAD_DIM, tb),
                                lambda i: (i // bpb, 0, i % bpb, 0, 0, 0)),
                   pl.BlockSpec((None, n_groups, None, ATT_VPAD, tb), lambda i: (i // bpb, 0, i % bpb, 0, 0)),
                   heads_spec(WIN_HEADS), heads_spec(WIN_KV_HEADS), heads_spec(WIN_KV_HEADS),
                   heads_spec(RET_HEADS),
                   pl.BlockSpec((None, RET_HEADS, RET_DK, tb), lambda i: (i // bpb, 0, 0, i % bpb)),
                   heads_spec(RET_HEADS), row(PIECES[12])),
        compiler_params=_cparams(("parallel",)),
    )(h, mod, wt, wf, ttok, tfeat, gtok, gfeat, avg)


def _flash_kernel(*refs, n_ctx, tk, nt):
    q_refs, (k_ref, vt_ref, o_ref, s_ref, smax_ref, m_ref, acc_ref) = refs[:nt], refs[nt:]
    i = pl.program_id(2)
    tq = q_refs[0].shape[2]
    qt = jnp.concatenate([q[st] for st in range(2) for q in q_refs], axis=1)
    tile = vt_ref.shape[2]
    nlc = (k_ref.shape[0] - n_ctx) // tk

    def scores(row0, rows):
        return jnp.dot(k_ref[pl.ds(row0, rows), :], qt, preferred_element_type=F32)

    def stage(slot, chunk):
        s = scores(lat_row(chunk), tk)
        s_ref[slot] = s
        smax_ref[slot] = jnp.max(s, axis=0, keepdims=True)

    def absorb(s, smax, tile0):
        m = m_ref[...]
        m_new = jnp.maximum(m, smax)
        p = jnp.exp2(s - m_new).astype(BF16)
        acc = acc_ref[...] * jnp.exp2(m - m_new)
        for j in range(s.shape[0] // tile):
            acc = acc + jnp.dot(vt_ref[tile0 + j], p[j * tile:(j + 1) * tile], preferred_element_type=F32)
        m_ref[...] = m_new
        acc_ref[...] = acc

    m_ref[...] = jnp.full(m_ref.shape, -jnp.inf, F32)
    acc_ref[...] = jnp.zeros(acc_ref.shape, F32)
    s_ctx = scores(0, n_ctx)
    absorb(s_ctx, jnp.max(s_ctx, axis=0, keepdims=True), 0)
    lat_row = lambda c: pl.multiple_of(n_ctx + c * tk, tile)
    lat_tile = lambda c: (n_ctx + c * tk) // tile

    @pl.when(i > 0)
    def _():
        stage(0, 0)
        unroll = ATT_UNROLL

        def trip(j, carry):
            for e in range(unroll):
                c = unroll * j + e
                stage((e + 1) % 2, c + 1)
                absorb(s_ref[e % 2], smax_ref[e % 2], lat_tile(c))
            return carry

        lax.fori_loop(0, nlc // unroll - 1, trip, 0)
        for c in range(nlc - unroll, nlc):
            if c + 1 < nlc:
                stage((c + 1) % 2, c + 1)
            absorb(s_ref[c % 2], smax_ref[c % 2], lat_tile(c))

    acc = acc_ref[...]
    o = acc[:HEAD_DIM] * (1.0 / acc[HEAD_DIM:HEAD_DIM + 1])
    for j in range(nt):
        o_ref[j] = jnp.concatenate([o[:, j * tq:(j + 1) * tq], o[:, (nt + j) * tq:(nt + j + 1) * tq]], axis=1)


def _flash(qt, k_all, vt, n_ctx):
    b, g, tiles, _, dk, tq = qt.shape
    nt = ATT_QTILES
    lt = k_all.shape[1]
    s = lt - n_ctx
    tk = next(c for c in (ATT_TK, 512, 256) if s % (ATT_UNROLL * c) == 0)
    assert n_ctx == tq and tk % tq == 0 and (tiles - 1) % nt == 0
    steps = 1 + (tiles - 1) // nt
    key_block = lambda gi: jnp.where(gi < A_KV_HEADS, 0, 1 + (gi - A_KV_HEADS) // 2)
    qspec = lambda j: pl.BlockSpec(
        (None, None, None, 2, dk, tq),
        lambda bi, gi, i: (bi, gi, jnp.where(i == 0, 0, nt * i - (nt - 1) + j), 0, 0, 0))
    return pl.pallas_call(
        functools.partial(_flash_kernel, n_ctx=n_ctx, tk=tk, nt=nt),
        out_shape=jax.ShapeDtypeStruct((b, g, steps, nt, HEAD_DIM, 2 * tq), F32),
        grid=(b, g, steps),
        in_specs=[qspec(j) for j in range(nt)]
        + [pl.BlockSpec((None, lt, dk), lambda bi, gi, i: (bi, 0, key_block(gi))),
           pl.BlockSpec((None, None) + vt.shape[2:], lambda bi, gi, i: (bi, gi, 0, 0, 0))],
        out_specs=pl.BlockSpec((None, None, None, nt, HEAD_DIM, 2 * tq), lambda bi, gi, i: (bi, gi, i, 0, 0, 0)),
        scratch_shapes=[pltpu.VMEM((2, tk, 2 * nt * tq), F32), pltpu.VMEM((2, 1, 2 * nt * tq), F32),
                        pltpu.VMEM((1, 2 * nt * tq), F32), pltpu.VMEM((vt.shape[3], 2 * nt * tq), F32)],
        compiler_params=_cparams(("parallel", "parallel", "arbitrary")),
    )(*([qt] * nt), k_all, vt)


def _window_kernel(q_ref, kp_ref, kc_ref, kn_ref, vp_ref, vc_ref, vn_ref, kx_ref, vx_ref, sink_ref,
                   o_ref, *, n_ctx_blocks, n_blocks):
    step = pl.program_id(1)
    n_kv = kc_ref.shape[0]
    group = q_ref.shape[0] // n_kv
    rows = group * WINDOW
    nt = (((1,), (1,)), ((), ()))
    sdot = functools.partial(lax.dot_general, dimension_numbers=nt, preferred_element_type=F32)
    pv = functools.partial(jnp.dot, preferred_element_type=F32)
    qi = lax.broadcasted_iota(jnp.int32, (rows, WINDOW), 0) % WINDOW
    kj = lax.broadcasted_iota(jnp.int32, (rows, WINDOW), 1)
    neg = -jnp.inf
    for g in range(n_kv):
        sink = sink_ref[g]
        kx, vx = kx_ref[g], vx_ref[g]
        for j in range(2):
            qb = 2 * step + j
            cur = slice(j * WINDOW, (j + 1) * WINDOW)
            q = q_ref[g * group:(g + 1) * group, cur, :].reshape(rows, HEAD_DIM)
            if j == 0:
                kp, vp, kn, vn = kp_ref[g], vp_ref[g], kc_ref[g, WINDOW:, :], vc_ref[g, WINDOW:, :]
            else:
                kp, vp, kn, vn = kc_ref[g, :WINDOW, :], vc_ref[g, :WINDOW, :], kn_ref[g], vn_ref[g]
            off_p = jnp.where(qb >= n_ctx_blocks + 1, 0, 2 * WINDOW)
            off_c = jnp.where(qb >= n_ctx_blocks, 0, 2 * WINDOW)
            off_n = jnp.where(jnp.logical_and(qb >= n_ctx_blocks, qb <= n_blocks - 2), 0, 2 * WINDOW)
            s_p = jnp.where(kj >= qi + off_p, sdot(q, kp), neg)
            s_c = jnp.where(kj >= off_c, sdot(q, kc_ref[g, cur, :]), neg)
            s_n = jnp.where(kj <= qi - off_n, sdot(q, kn), neg)
            s_x = sdot(q, kx)
            rmax = lambda s: jnp.max(s, axis=1, keepdims=True)
            m = jnp.maximum(jnp.maximum(jnp.maximum(rmax(s_p), rmax(s_c)), jnp.maximum(rmax(s_n), rmax(s_x))), sink)
            e_p, e_c, e_n, e_x = (jnp.exp2(s - m) for s in (s_p, s_c, s_n, s_x))
            rsum = lambda e: jnp.sum(e, axis=1, keepdims=True)
            den = rsum(e_p) + rsum(e_c) + rsum(e_n) + rsum(e_x) + jnp.exp2(sink - m)
            o = (pv(e_p.astype(BF16), vp) + pv(e_c.astype(BF16), vc_ref[g, cur, :])
                 + pv(e_n.astype(BF16), vn) + pv(e_x.astype(BF16), vx))
            o_ref[g, j] = o / den


def _window(q, k, v, sink, n_ctx):
    b, hq, lt, dh = q.shape
    g = k.shape[1]
    blk = WINDOW
    nb = lt // blk
    rows = (hq // g) * blk
    n_ctx_blocks = n_ctx // blk
    assert n_ctx % (2 * blk) == 0 and nb % 2 == 0
    lo, hi = n_ctx_blocks, nb - 1
    pair_spec = lambda heads: pl.BlockSpec((None, heads, 2 * blk, dh), lambda bi, i: (bi, 0, i, 0))
    side_spec = lambda delta: pl.BlockSpec((None, g, blk, dh),
                                           lambda bi, i: (bi, 0, jnp.clip(2 * i + delta, lo, hi), 0))
    ctx_spec = pl.BlockSpec((None, g, n_ctx, dh), lambda bi, i: (bi, 0, 0, 0))
    return pl.pallas_call(
        functools.partial(_window_kernel, n_ctx_blocks=n_ctx_blocks, n_blocks=nb),
        out_shape=jax.ShapeDtypeStruct((b, g, nb, rows, dh), F32),
        grid=(b, nb // 2),
        in_specs=[pair_spec(hq), side_spec(-1), pair_spec(g), side_spec(2), side_spec(-1), pair_spec(g),
                  side_spec(2), ctx_spec, ctx_spec, pl.BlockSpec((g, rows, 1), lambda bi, i: (0, 0, 0))],
        out_specs=pl.BlockSpec((None, g, 2, rows, dh), lambda bi, i: (bi, 0, i, 0, 0)),
        compiler_params=_cparams(("parallel", "arbitrary")),
    )(q, k, k, k, v, v, v, k, v, sink)


def _retention_kernel(qf_ref, ktf_ref, vf_ref, qb_ref, ktb_ref, vb_ref, dmat_ref, xi_ref, zeta_ref, gch_ref,
                      of_ref, ob_ref, st_ref):
    t = pl.program_id(0)

    @pl.when(t == 0)
    def _():
        st_ref[...] = jnp.zeros_like(st_ref)

    for d, (q_ref, kt_ref, v_ref, o_ref) in enumerate(((qf_ref, ktf_ref, vf_ref, of_ref),
                                                       (qb_ref, ktb_ref, vb_ref, ob_ref))):
        for bi in range(q_ref.shape[0]):
            for hd in range(RET_HEADS):
                q = q_ref[bi, hd]
                kt = kt_ref[bi, hd]
                v = v_ref[bi, hd]
                st = st_ref[d, bi, hd]
                inner = _dot3(q, kt) * dmat_ref[d, hd]
                o_ref[bi, hd] = _dot3(inner, v) + _dot3(q, st) * xi_ref[d, hd]
                st_ref[d, bi, hd] = st * gch_ref[d, hd] + _dot3(kt * zeta_ref[d, hd], v)


def _retention_call(q, kt, v, dmat, xi, zeta, gch, n_ctx_chunks):
    b, hh, lt, dk = q.shape
    dv = v.shape[-1]
    c = RET_CHUNK
    nch = lt // c

    def back(ti):
        return jnp.where(ti < n_ctx_chunks, n_ctx_chunks - 1 - ti, nch - 1 - (ti - n_ctx_chunks))

    fwd = lambda ti: ti
    rows = lambda blk, w: pl.BlockSpec((b, hh, c, w), lambda ti: (0, 0, blk(ti), 0))
    cols = lambda blk: pl.BlockSpec((b, hh, dk, c), lambda ti: (0, 0, 0, blk(ti)))
    tab = lambda a: pl.BlockSpec(a.shape, lambda ti: (0,) * a.ndim)
    out = jax.ShapeDtypeStruct((b, hh, lt, dv), F32)
    return pl.pallas_call(
        _retention_kernel,
        out_shape=(out, out),
        grid=(nch,),
        in_specs=[rows(fwd, dk), cols(fwd), rows(fwd, dv), rows(back, dk), cols(back), rows(back, dv),
                  tab(dmat), tab(xi), tab(zeta), tab(gch)],
        out_specs=(rows(fwd, dv), rows(back, dv)),
        scratch_shapes=[pltpu.VMEM((2, b, hh, dk, dv), F32)],
        compiler_params=_cparams(("arbitrary",)),
    )(q, kt, v, q, kt, v, dmat, xi, zeta, gch)


def _layer_norm_rows(z, ln):
    mu = jnp.mean(z, axis=-1, keepdims=True)
    zc = z - mu
    var = jnp.mean(zc * zc, axis=-1, keepdims=True)
    return zc * lax.rsqrt(var + LN_EPS) * ln[0:1, :] + ln[1:2, :]


def _merge_kernel(fa_ref, fd0_ref, fd1_ref, win_ref, retf_ref, retb_ref, rg_ref, g_ref, h_ref, mod_ref, lam_ref,
                  subln_ref, rnorm_ref, wb_ref, wo_ref, ln_ref, o_ref, *, alpha):
    d = h_ref.shape[1]
    tb = h_ref.shape[0]
    hd = HEAD_DIM
    proj = functools.partial(jnp.dot, preferred_element_type=F32)
    gate = lambda i: g_ref[:, i * d:(i + 1) * d].astype(F32)

    oat = jnp.concatenate([fa_ref[g][:, st * tb:(st + 1) * tb] for g in range(A_KV_HEADS) for st in range(2)],
                          axis=0)
    m = gate(0) * proj(oat.T.astype(BF16), wb_ref[0])

    lam = lam_ref[...]
    heads = []
    for h4 in range(DIFF_HEADS):
        f = (fd0_ref if h4 < 2 else fd1_ref)[h4 % 2]
        o = f[:, :tb] - lam * f[:, tb:]
        heads.append(o * lax.rsqrt(jnp.mean(o * o, axis=0, keepdims=True) + RMS_EPS))
    obt = jnp.concatenate(heads, axis=0) * subln_ref[...]
    m = m + gate(1) * proj(obt.T.astype(BF16), wb_ref[1])

    half = tb // 2
    acc = None
    for h4 in range(WIN_HEADS):
        g, st = h4 // 2, h4 % 2
        o = jnp.concatenate([win_ref[g, 0][st * half:(st + 1) * half], win_ref[g, 1][st * half:(st + 1) * half]],
                            axis=0)
        t = proj(o.astype(BF16), wb_ref[2, h4 * hd:(h4 + 1) * hd, :])
        acc = t if acc is None else acc + t
    m = m + gate(2) * acc

    acc = None
    for h4 in range(RET_HEADS):
        cols = slice(h4 * RET_DV, (h4 + 1) * RET_DV)
        o = retf_ref[h4] + retb_ref[h4]
        mu = jnp.mean(o, axis=-1, keepdims=True)
        oc = o - mu
        var = jnp.mean(oc * oc, axis=-1, keepdims=True)
        on = oc * lax.rsqrt(var + LN_EPS) * rnorm_ref[0:1, cols] + rnorm_ref[1:2, cols]
        gt = rg_ref[:, cols]
        t = proj((on * (gt / (1.0 + jnp.exp(-gt)))).astype(BF16), wb_ref[3, cols, :])
        acc = t if acc is None else acc + t
    m = m + gate(3) * acc

    y = proj(m.astype(BF16), wo_ref[...])
    z = alpha * h_ref[...] + mod_ref[2:3, :] * y
    o_ref[...] = _layer_norm_rows(z, ln_ref[...])


def _merge(flash_out, win_out, ret_out, rg, gates, h, mod, group_of_block, lam, subln, rnorm, wb, wo, ln,
           alpha, b, lt):
    t, d = h.shape
    tb = TOKEN_BLOCK
    bpb = lt // tb
    row = lambda n: pl.BlockSpec((tb, n), lambda i: (i, 0))
    const = lambda a: pl.BlockSpec(a.shape, lambda i: (0,) * a.ndim)
    fspec = lambda gb: pl.BlockSpec(
        (None, 2, None, None) + flash_out.shape[4:],
        lambda i: (i // bpb, gb, (i % bpb + ATT_QTILES - 1) // ATT_QTILES, (i % bpb + ATT_QTILES - 1) % ATT_QTILES,
                   0, 0))
    ret_spec = pl.BlockSpec((None, RET_HEADS, tb, RET_DV), lambda i: (i // bpb, 0, i % bpb, 0))
    return pl.pallas_call(
        functools.partial(_merge_kernel, alpha=alpha),
        out_shape=jax.ShapeDtypeStruct((t, d), F32),
        grid=(t // tb,),
        in_specs=[fspec(0), fspec(1), fspec(2),
                  pl.BlockSpec((None, WIN_KV_HEADS, 2) + win_out.shape[3:], lambda i: (i // bpb, 0, i % bpb, 0, 0)),
                  ret_spec, ret_spec, row(rg.shape[1]), row(N_BRANCH * d), row(d),
                  pl.BlockSpec((None, 6, d), lambda i: (group_of_block(i), 0, 0)),
                  const(lam), const(subln), const(rnorm), const(wb), const(wo), const(ln)],
        out_specs=row(d),
        compiler_params=_cparams(("parallel",)),
    )(flash_out, flash_out, flash_out, win_out, *ret_out, rg, gates, h, mod, lam, subln, rnorm, wb, wo, ln)


def _top_rows(s, n, with_rank=False):
    out = []
    cur = s
    rank = jnp.full(s.shape, float(n), F32) if with_rank else None
    for r in range(n):
        mx = jnp.max(cur, axis=0, keepdims=True)
        out.append(mx)
        if with_rank:
            rank = jnp.where(cur == mx, float(r), rank)
        if r + 1 < n:
            cur = jnp.where(cur == mx, -jnp.inf, cur)
    return (out, rank) if with_rank else out


def _peer_route_kernel(h_ref, mod_ref, wh_ref, wl_ref, sk_ref, xt_ref, rk_ref, b1_ref, nn_ref, az_ref,
                       cand_ref):
    m = mod_ref[...]
    u = h_ref[...] * (1.0 + m[4:5, :]) + m[3:4, :]
    xt_ref[...] = u.T.astype(BF16)
    uh, ul = _split_bf16(u)
    d = functools.partial(jnp.dot, preferred_element_type=F32)
    nk = PEER_NK
    nt = (((1,), (1,)), ((), ()))
    k1 = PEER_TOPK + 1
    for hd in range(PEER_HEADS):
        c0 = 2 * hd * PEER_DQ
        wh = wh_ref[:, c0:c0 + 2 * PEER_DQ]
        wl = wl_ref[:, c0:c0 + 2 * PEER_DQ]
        q = d(uh, wh) + (d(ul, wh) + d(uh, wl))
        st = [_dot3(sk_ref[hd, p], q[:, p * PEER_DQ:(p + 1) * PEER_DQ], nt) for p in range(2)]
        top0 = _top_rows(st[0], k1)
        top1, rank1 = _top_rows(st[1], k1, with_rank=True)
        r = 0
        for p0 in range(k1):
            for p1 in range(k1 // (p0 + 1)):
                cand_ref[r:r + 1, :] = top0[p0] + top1[p1]
                r += 1
        cand_ref[r:, :] = jnp.full((cand_ref.shape[0] - r, cand_ref.shape[1]), -jnp.inf, F32)
        cand = cand_ref[...]
        ctop = _top_rows(cand, k1)
        tau = 0.5 * (ctop[PEER_TOPK - 1] + ctop[PEER_TOPK])
        mx = top0[0] + top1[0]
        z = jnp.sum(jnp.where(cand >= tau, jnp.exp(cand - mx), 0.0), axis=0, keepdims=True)
        th = tau - st[0]
        nn = jnp.zeros_like(th)
        for q in range(PEER_TOPK):
            nn = nn + jnp.where(top1[q] >= th, 1.0, 0.0)
        rk_ref[hd] = rank1.astype(BF16)
        b1_ref[hd] = jnp.exp(st[1] - top1[0]).astype(BF16)
        nn_ref[hd] = nn
        az_ref[hd] = jnp.exp(st[0] - top0[0]) / z


def _peer_route(h, mod, group_of_block, wq_hi, wq_lo, subkeys):
    t, d = h.shape
    tb = TOKEN_BLOCK
    hh, nk = PEER_HEADS, PEER_NK
    st_shape = lambda dt: jax.ShapeDtypeStruct((t // tb, hh, nk, tb), dt)
    st_spec = pl.BlockSpec((None, hh, nk, tb), lambda i: (i, 0, 0, 0))
    nq = wq_hi.shape[1]
    return pl.pallas_call(
        _peer_route_kernel,
        out_shape=(jax.ShapeDtypeStruct((t // tb, d, tb), BF16), st_shape(BF16), st_shape(BF16), st_shape(F32),
                   st_shape(F32)),
        grid=(t // tb,),
        in_specs=[pl.BlockSpec((tb, d), lambda i: (i, 0)),
                  pl.BlockSpec((None, 6, d), lambda i: (group_of_block(i), 0, 0)),
                  pl.BlockSpec((d, nq), lambda i: (0, 0)),
                  pl.BlockSpec((d, nq), lambda i: (0, 0)),
                  pl.BlockSpec((hh, 2, nk, PEER_DQ), lambda i: (0, 0, 0, 0))],
        out_specs=(pl.BlockSpec((None, d, tb), lambda i: (i, 0, 0)), st_spec, st_spec, st_spec, st_spec),
        scratch_shapes=[pltpu.VMEM((PEER_CAND_ROWS, tb), F32)],
        compiler_params=_cparams(("parallel",)),
    )(h, mod, wq_hi, wq_lo, subkeys)


GELU_K1 = -2.0 * math.sqrt(2.0 / math.pi) * LOG2E
GELU_K2 = GELU_K1 * 0.044715


def _gelu_tanh(x):
    return x / (1.0 + jnp.exp2(x * (GELU_K1 + GELU_K2 * (x * x))))


def _peer_dense_kernel(xt_ref, u_ref, vt_ref, vtl_ref, rk_ref, b1_ref, nn_ref, az_ref, yt_ref, g_ref):
    c = pl.program_id(1)
    nk = PEER_NK
    nsub, _, tb = xt_ref.shape
    rows_per_step = u_ref.shape[0] // nk

    @pl.when(c == 0)
    def _():
        yt_ref[...] = jnp.zeros_like(yt_ref)
        g_ref[...] = jnp.zeros_like(g_ref)

    def sub_block(sb, carry):
        xt = xt_ref[sb]
        pre = lambda ii: jnp.dot(u_ref[ii * nk:(ii + 1) * nk, :], xt, preferred_element_type=F32)
        yt_ref[sb] += jnp.dot(vt_ref[...], g_ref[sb], preferred_element_type=F32)
        act_next = pre(0)
        for ii in range(rows_per_step):
            act = act_next
            if ii + 1 < rows_per_step:
                act_next = pre(ii + 1)
            w = None
            tile = (nk // BF16_ROWS, BF16_ROWS, tb)
            row = lambda ref, hd: jnp.broadcast_to(ref[sb, hd, ii:ii + 1, :], (BF16_ROWS, tb)).astype(BF16)[None]
            for hd in range(PEER_HEADS):
                t = jnp.where(rk_ref[sb, hd].reshape(tile) < row(nn_ref, hd), b1_ref[sb, hd].reshape(tile),
                              jnp.zeros((), BF16)) * row(az_ref, hd)
                w = t if w is None else w + t
            g_ref[sb, ii * nk:(ii + 1) * nk, :] = w.reshape(nk, tb) * _gelu_tanh(act.astype(BF16))
        return carry

    lax.fori_loop(0, nsub, sub_block, 0)

    @pl.when(c == pl.num_programs(1) - 1)
    def _():
        for sb in range(nsub):
            yt_ref[sb] += jnp.dot(vtl_ref[...], g_ref[sb], preferred_element_type=F32)


def _peer_dense(xt, u, vt, rk, b1, nn, az):
    nblk, d, tb = xt.shape
    n = u.shape[0]
    nsub = PEER_TOKENS // tb
    ec = PEER_EXPERTS
    nc = n // ec
    hh, nk = PEER_HEADS, PEER_NK
    st_spec = pl.BlockSpec((nsub, hh, nk, tb), lambda i, c: (i, 0, 0, 0))
    row_spec = pl.BlockSpec((nsub, hh, ec // nk, tb), lambda i, c: (i, 0, c, 0))
    return pl.pallas_call(
        _peer_dense_kernel,
        out_shape=jax.ShapeDtypeStruct((nblk, d, tb), F32),
        grid=(nblk // nsub, nc),
        in_specs=[pl.BlockSpec((nsub, d, tb), lambda i, c: (i, 0, 0)),
                  pl.BlockSpec((ec, d), lambda i, c: (c, 0)),
                  pl.BlockSpec((d, ec), lambda i, c: (0, jnp.maximum(c - 1, 0))),
                  pl.BlockSpec((d, ec), lambda i, c: (0, nc - 1)),
                  st_spec, st_spec, row_spec, row_spec],
        out_specs=pl.BlockSpec((nsub, d, tb), lambda i, c: (i, 0, 0)),
        scratch_shapes=[pltpu.VMEM((nsub, ec, tb), BF16)],
        compiler_params=_cparams(("parallel", "arbitrary")),
    )(xt, u, vt, vt, rk, b1, nn, az)


def _resid_ln_kernel(h_ref, yt_ref, mod_ref, ln_ref, o_ref, *, alpha, gate_row):
    z = alpha * h_ref[...] + mod_ref[gate_row:gate_row + 1, :] * yt_ref[...].T
    o_ref[...] = _layer_norm_rows(z, ln_ref[...])


def _resid_ln(h, yt, mod, group_of_block, ln, alpha, gate_row):
    t, d = h.shape
    tb = TOKEN_BLOCK
    row = pl.BlockSpec((tb, d), lambda i: (i, 0))
    return pl.pallas_call(
        functools.partial(_resid_ln_kernel, alpha=alpha, gate_row=gate_row),
        out_shape=jax.ShapeDtypeStruct((t, d), F32),
        grid=(t // tb,),
        in_specs=[row, pl.BlockSpec((None, d, tb), lambda i: (i, 0, 0)),
                  pl.BlockSpec((None, 6, d), lambda i: (group_of_block(i), 0, 0)),
                  pl.BlockSpec((2, d), lambda i: (0, 0))],
        out_specs=row,
        compiler_params=_cparams(("parallel",)),
    )(h, yt, mod, ln)


def _axial_tables(s, n_ctx, d):
    rows = s // GRID_W
    row = jnp.broadcast_to(jnp.arange(rows, dtype=F32)[:, None], (rows, GRID_W)).reshape(-1)
    col = jnp.broadcast_to(jnp.arange(GRID_W, dtype=F32)[None, :], (rows, GRID_W)).reshape(-1)
    quarter = d // 4
    inv = ROPE_THETA ** (-jnp.arange(quarter, dtype=F32) / quarter)
    ar, ac = row[:, None] * inv, col[:, None] * inv
    cos = jnp.concatenate([jnp.cos(ar), jnp.cos(ar), jnp.cos(ac), jnp.cos(ac)], axis=-1)
    sin = jnp.concatenate([-jnp.sin(ar), jnp.sin(ar), -jnp.sin(ac), jnp.sin(ac)], axis=-1)
    cos = jnp.concatenate([jnp.ones((n_ctx, d), F32), cos], axis=0)
    sin = jnp.concatenate([jnp.zeros((n_ctx, d), F32), sin], axis=0)
    return cos, sin


def _rope1d_tables(lt, d):
    half = d // 2
    inv = ROPE_THETA ** (-jnp.arange(half, dtype=F32) / half)
    ang = jnp.arange(lt, dtype=F32)[:, None] * inv
    return (jnp.concatenate([jnp.cos(ang), jnp.cos(ang)], axis=-1),
            jnp.concatenate([-jnp.sin(ang), jnp.sin(ang)], axis=-1))


def kernel(x, c, ctx, c_ctx, w_mod, b_mod, w_in, qk_gain, diff_lambda, diff_subln, win_sink, ret_decay,
           ret_norm, w_branch, w_out, ln_attn, ln_ffn, peer_wq, peer_subkeys, peer_u, peer_v):
    b, s, d = x.shape
    n_ctx = ctx.shape[1]
    depth = w_mod.shape[0]
    lt = n_ctx + s
    t = b * lt
    tb = TOKEN_BLOCK
    assert n_ctx % tb == 0 and s % tb == 0 and t % PEER_TOKENS == 0
    alpha = (2 * depth) ** 0.25
    blocks_per_batch = lt // tb
    ctx_blocks = n_ctx // tb

    def group_of_block(i):
        return jnp.where(i % blocks_per_batch < ctx_blocks, b, i // blocks_per_batch)

    cos64, sin64 = _axial_tables(s, n_ctx, HEAD_DIM)
    cos32, sin32 = _axial_tables(s, n_ctx, DIFF_DIM)
    cos1d, sin1d = _rope1d_tables(lt, RET_DK)
    sc_a = HEAD_DIM ** -0.5 * LOG2E
    sc_d = DIFF_DIM ** -0.5 * LOG2E
    rep = lambda a, n: jnp.tile(a, (1, n))
    tok_part = lambda t64, t32, t1d: [rep(t64, 2), rep(t32, 8), rep(t64, 4) * sc_a, rep(t64, 2), rep(t1d, 4)]
    feat_part = lambda t64, t32, t1d: [rep(t64, 4) * sc_a, rep(t32, 8) * sc_d, rep(t1d, 4) * RET_DK ** -0.5]
    ttok = jnp.concatenate(tok_part(cos64, cos32, cos1d) + tok_part(sin64, sin32, sin1d), axis=1)
    tfeat = jnp.concatenate(feat_part(cos64, cos32, cos1d) + feat_part(sin64, sin32, sin1d), axis=1).T
    swap64 = np.arange(HEAD_DIM) ^ 16
    head_ids = np.arange(2 * HEAD_DIM) // HEAD_DIM
    avg = jnp.asarray((head_ids[:, None] == head_ids[None, :]) / HEAD_DIM, BF16)
    cond8 = jnp.zeros((8, d), F32).at[:b].set(jax.nn.silu(c)).at[b].set(jax.nn.silu(c_ctx))

    h = jnp.concatenate([ctx, x], axis=1).reshape(t, d)

    for l in range(depth):
        mod = _modulation(cond8, w_mod[l], b_mod[l]).reshape(8, 6, d)
        w_mix = w_in[l, :, :MIX_COLS]
        w_gate = w_in[l, :, MIX_COLS:].astype(BF16)
        g0, g1 = qk_gain[l, 0].astype(F32), qk_gain[l, 1].astype(F32)
        gtok = jnp.stack([jnp.tile(g1, 2), jnp.tile(g1[swap64], 2)])
        gfeat = jnp.broadcast_to(jnp.concatenate([jnp.tile(g0, A_HEADS), jnp.tile(g0[swap64], A_HEADS)])[:, None],
                                 (2 * A_HEADS * HEAD_DIM, tb))
        kall, qt, vt, wq, wk, wv, rq, rkt, rv, rg = _mixer_in(
            h, mod, group_of_block, w_mix[:, TOK_COLS].astype(BF16), w_mix[:, FEAT_COLS].T.astype(BF16),
            ttok, tfeat, gtok, gfeat, avg, b, lt)
        gates = _gates(h, mod, group_of_block, w_gate)

        ot = _flash(qt, kall, vt, n_ctx)
        lam_init = 0.8 - 0.6 * math.exp(-0.3 * l)
        lp = diff_lambda[l].astype(F32)
        lam = (jnp.exp(jnp.sum(lp[0] * lp[1])) - jnp.exp(jnp.sum(lp[2] * lp[3])) + lam_init).reshape(1, 1)
        subln = jnp.broadcast_to((jnp.tile(diff_subln[l].astype(F32), DIFF_HEADS) * (1.0 - lam_init))[:, None],
                                 (DIFF_HEADS * HEAD_DIM, tb))

        sink = jnp.repeat(win_sink[l].astype(F32) * LOG2E, WINDOW).reshape(WIN_KV_HEADS, 2 * WINDOW, 1)
        ow = _window(wq, wk, wv, sink, n_ctx)

        lg = jax.nn.log_sigmoid(ret_decay[l].astype(F32))
        idx = jnp.arange(RET_CHUNK, dtype=F32)
        diff = idx[:, None] - idx[None, :]
        lg3 = lg[:, :, None, None]
        dm_f = jnp.exp(jnp.where(diff >= 0, diff * lg3[0], -jnp.inf))
        dm_b = jnp.exp(jnp.where(diff <= 0, -diff * lg3[1], -jnp.inf))
        dmat = jnp.stack([dm_f, dm_b])
        xi = jnp.stack([jnp.exp((idx + 1.0) * lg[0][:, None]), jnp.exp((RET_CHUNK - idx) * lg[1][:, None])])
        zeta = jnp.stack([jnp.exp((RET_CHUNK - 1.0 - idx) * lg[0][:, None]), jnp.exp(idx * lg[1][:, None])])
        gch = jnp.exp(RET_CHUNK * lg)
        o_ret = _retention_call(rq, rkt, rv, dmat, xi[..., None], zeta[:, :, None, :],
                                gch[:, :, None, None], n_ctx // RET_CHUNK)

        h = _merge(ot, ow, o_ret, rg, gates, h, mod, group_of_block, lam, subln, ret_norm[l].astype(F32),
                   w_branch[l].astype(BF16), w_out[l].astype(BF16), ln_attn[l], alpha, b, lt)

        wq_hi, wq_lo = _split_bf16(peer_wq[l])
        xt, rk, b1, nn, az = _peer_route(h, mod, group_of_block, wq_hi, wq_lo, peer_subkeys[l])
        yt = _peer_dense(xt, peer_u[l].astype(BF16), peer_v[l].T.astype(BF16), rk, b1, nn, az)
        h = _resid_ln(h, yt, mod, group_of_block, ln_ffn[l], alpha, 5)

    return h.reshape(b, lt, d)[:, n_ctx:, :]
```

```python
import functools
import math

import numpy as np
import jax
import jax.numpy as jnp
from jax import lax
from jax.experimental import pallas as pl
from jax.experimental.pallas import tpu as pltpu

GRID_W = 64
HEAD_DIM = 64
ROPE_THETA = 10000.0
A_HEADS = 4
A_KV_HEADS = 2
DIFF_HEADS = 4
DIFF_DIM = 32
WIN_HEADS = 4
WIN_KV_HEADS = 2
WINDOW = 128
RET_HEADS = 4
RET_DK = 64
RET_DV = 64
RET_CHUNK = 128
N_BRANCH = 4
PIECES = (
    A_HEADS * HEAD_DIM, A_KV_HEADS * HEAD_DIM, A_KV_HEADS * HEAD_DIM,
    2 * DIFF_HEADS * DIFF_DIM, 2 * DIFF_HEADS * DIFF_DIM, DIFF_HEADS * 2 * DIFF_DIM,
    WIN_HEADS * HEAD_DIM, WIN_KV_HEADS * HEAD_DIM, WIN_KV_HEADS * HEAD_DIM,
    RET_HEADS * RET_DK, RET_HEADS * RET_DK, RET_HEADS * RET_DV, RET_HEADS * RET_DV,
)
MIX_COLS = sum(PIECES)
PIECE_OFF = tuple(int(v) for v in np.cumsum((0,) + PIECES))
PEER_HEADS = 8
PEER_NK = 128
PEER_TOPK = 16
PEER_DQ = 128
PEER_CAND_ROWS = -(-sum((PEER_TOPK + 1) // (p + 1) for p in range(PEER_TOPK + 1)) // 8) * 8
LN_EPS = 1e-5
RMS_EPS = 1e-6
LOG2E = 1.4426950408889634

F32 = jnp.float32
BF16 = jnp.bfloat16

TOKEN_BLOCK = 256
ATT_TK = 512
ATT_UNROLL = 4
ATT_QTILES = 4
ATT_VPAD = 80
PEER_TOKENS = 512
PEER_EXPERTS = 2048
VMEM_LIMIT = 56 * 1024 * 1024
BF16_ROWS = 16


def _cparams(sem):
    return pltpu.CompilerParams(dimension_semantics=sem, vmem_limit_bytes=VMEM_LIMIT)


def _split_bf16(a):
    hi = a.astype(BF16)
    lo = (a - hi.astype(F32)).astype(BF16)
    return hi, lo


def _dot3(a, b, dims=(((1,), (0,)), ((), ()))):
    ah, al = _split_bf16(a)
    bh, bl = _split_bf16(b)
    d = functools.partial(lax.dot_general, dimension_numbers=dims, preferred_element_type=F32)
    return d(ah, bh) + (d(al, bh) + d(ah, bl))


def _mod_kernel(c_ref, w_ref, b_ref, o_ref):
    o_ref[...] = _dot3(c_ref[...], w_ref[...]) + b_ref[...]


def _modulation(cond8, w, b):
    d, n = w.shape
    tn = 1536
    return pl.pallas_call(
        _mod_kernel,
        out_shape=jax.ShapeDtypeStruct((8, n), F32),
        grid=(n // tn,),
        in_specs=[pl.BlockSpec((8, d), lambda j: (0, 0)),
                  pl.BlockSpec((d, tn), lambda j: (0, j)),
                  pl.BlockSpec((1, tn), lambda j: (0, j))],
        out_specs=pl.BlockSpec((8, tn), lambda j: (0, j)),
        compiler_params=_cparams(("arbitrary",)),
    )(cond8, w, b.reshape(1, n))


def _gates_kernel(x_ref, mod_ref, w_ref, o_ref, *, chunk):
    m = mod_ref[...]
    xm = (x_ref[...] * (1.0 + m[1:2, :]) + m[0:1, :]).astype(BF16)
    n = w_ref.shape[1]
    for j in range(n // chunk):
        acc = jnp.dot(xm, w_ref[:, j * chunk:(j + 1) * chunk], preferred_element_type=F32)
        o_ref[:, j * chunk:(j + 1) * chunk] = jax.nn.sigmoid(acc).astype(o_ref.dtype)


def _gates(h, mod, group_of_block, w):
    t, d = h.shape
    n = w.shape[1]
    tb = TOKEN_BLOCK
    return pl.pallas_call(
        functools.partial(_gates_kernel, chunk=512),
        out_shape=jax.ShapeDtypeStruct((t, n), BF16),
        grid=(t // tb,),
        in_specs=[pl.BlockSpec((tb, d), lambda i: (i, 0)),
                  pl.BlockSpec((None, 6, d), lambda i: (group_of_block(i), 0, 0)),
                  pl.BlockSpec((d, n), lambda i: (0, 0))],
        out_specs=pl.BlockSpec((tb, n), lambda i: (i, 0)),
        compiler_params=_cparams(("parallel",)),
    )(h, mod, w)


TOK_PIECES = ((1, True), (4, True), (6, True), (7, True), (8, False), (9, True), (11, False), (12, False))
FEAT_PIECES = ((0, True), (3, True), (2, False), (5, False), (10, True))


def _piece_cols(pieces):
    flips = {0: 16, 1: 16, 3: 8, 4: 8, 6: 16, 7: 16, 9: 32, 10: 32}
    cols, offs = [], {}
    n = 0
    for p, rotary in pieces:
        base = np.arange(PIECE_OFF[p], PIECE_OFF[p + 1])
        offs[p] = n
        cols.append(base)
        n += len(base)
        if rotary:
            cols.append(PIECE_OFF[p] + ((base - PIECE_OFF[p]) ^ flips[p]))
            n += len(base)
    return np.concatenate(cols), offs


TOK_COLS, TOK_OFF = _piece_cols(TOK_PIECES)
FEAT_COLS, FEAT_OFF = _piece_cols(FEAT_PIECES)
TTOK_OFF = {1: 0, 4: 128, 6: 384, 7: 640, 9: 768}
TTOK_W = 1024
TFEAT_OFF = {0: 0, 3: 256, 10: 512}
TFEAT_W = 768


def _mixer_in_kernel(x_ref, mod_ref, wt_ref, wf_ref, ttok_ref, tfeat_ref, gtok_ref, gfeat_ref, avg_ref,
                     kall_ref, qt_ref, vt_ref, wq_ref, wk_ref, wv_ref, rq_ref, rkt_ref, rv_ref, rg_ref):
    m = mod_ref[...]
    u = x_ref[...] * (1.0 + m[1:2, :]) + m[0:1, :]
    xm = u.astype(BF16)
    xmt = u.T.astype(BF16)
    tb = xm.shape[0]

    def tok(p, swapped=False):
        a = TOK_OFF[p] + (PIECES[p] if swapped else 0)
        return jnp.dot(xm, wt_ref[:, a:a + PIECES[p]], preferred_element_type=F32)

    def feat(p, swapped=False):
        a = FEAT_OFF[p] + (PIECES[p] if swapped else 0)
        return jnp.dot(wf_ref[a:a + PIECES[p], :], xmt, preferred_element_type=F32)

    def rope_tok(p, x, xs):
        a = TTOK_OFF[p]
        return x * ttok_ref[:, a:a + PIECES[p]] + xs * ttok_ref[:, TTOK_W + a:TTOK_W + a + PIECES[p]]

    def rope_feat(p, x, xs):
        a = TFEAT_OFF[p]
        return x * tfeat_ref[a:a + PIECES[p], :] + xs * tfeat_ref[TFEAT_W + a:TFEAT_W + a + PIECES[p], :]

    x, xs = tok(1), tok(1, True)
    sq_hi, sq_lo = _split_bf16(x * x)
    avg = avg_ref[...]
    ms = jnp.dot(sq_hi, avg, preferred_element_type=F32) + jnp.dot(sq_lo, avg, preferred_element_type=F32)
    r = lax.rsqrt(ms + RMS_EPS)
    ka = r * rope_tok(1, x * gtok_ref[0:1, :], xs * gtok_ref[1:2, :])
    kd = rope_tok(4, tok(4), tok(4, True))
    kall_ref[:, :PIECES[1]] = ka.astype(BF16)
    kall_ref[:, PIECES[1]:] = kd.astype(BF16)

    def split_heads(ref, val):
        for hd in range(ref.shape[0]):
            ref[hd] = val[:, hd * HEAD_DIM:(hd + 1) * HEAD_DIM].astype(ref.dtype)

    split_heads(wq_ref, rope_tok(6, tok(6), tok(6, True)))
    split_heads(wk_ref, rope_tok(7, tok(7), tok(7, True)))
    split_heads(wv_ref, tok(8))
    split_heads(rq_ref, rope_tok(9, tok(9), tok(9, True)))
    split_heads(rv_ref, tok(11))
    rg_ref[...] = tok(12)
    rkt = rope_feat(10, feat(10), feat(10, True))
    for hd in range(RET_HEADS):
        rkt_ref[hd] = rkt[hd * RET_DK:(hd + 1) * RET_DK]

    zero64 = jnp.zeros((HEAD_DIM, tb), F32)
    zero32 = jnp.zeros((DIFF_DIM, tb), F32)

    def place(q, upper):
        return jnp.concatenate([zero64, q] if upper else [q, zero64], axis=0).astype(BF16)

    xq, xqs = feat(0), feat(0, True)
    for hd in range(A_HEADS):
        rows = slice(hd * HEAD_DIM, (hd + 1) * HEAD_DIM)
        xh = xq[rows]
        rh = lax.rsqrt(jnp.mean(xh * xh, axis=0, keepdims=True) + RMS_EPS)
        swapped_rows = slice(A_HEADS * HEAD_DIM + hd * HEAD_DIM, A_HEADS * HEAD_DIM + (hd + 1) * HEAD_DIM)
        qh = rh * (xh * gfeat_ref[rows, :] * tfeat_ref[rows, :]
                   + xqs[rows] * gfeat_ref[swapped_rows, :]
                   * tfeat_ref[TFEAT_W + hd * HEAD_DIM:TFEAT_W + (hd + 1) * HEAD_DIM, :])
        qt_ref[hd // 2, hd % 2] = place(qh, hd // 2 == 1)
    qd = rope_feat(3, feat(3), feat(3, True))
    for hd in range(DIFF_HEADS):
        qh = qd[hd * HEAD_DIM:(hd + 1) * HEAD_DIM]
        q1 = jnp.concatenate([qh[:DIFF_DIM], zero32], axis=0)
        q2 = jnp.concatenate([zero32, qh[DIFF_DIM:]], axis=0)
        qt_ref[A_KV_HEADS + hd, 0] = place(q1, hd % 2 == 1)
        qt_ref[A_KV_HEADS + hd, 1] = place(q2, hd % 2 == 1)

    pad_rows = vt_ref.shape[1] - HEAD_DIM
    tail = (lax.broadcasted_iota(jnp.int32, (pad_rows, tb), 0) == 0).astype(F32)
    va, vd = feat(2), feat(5)
    for g in range(A_KV_HEADS + DIFF_HEADS):
        v = va[g * HEAD_DIM:(g + 1) * HEAD_DIM] if g < A_KV_HEADS else \
            vd[(g - A_KV_HEADS) * HEAD_DIM:(g - A_KV_HEADS + 1) * HEAD_DIM]
        vt_ref[g] = jnp.concatenate([v, tail], axis=0).astype(BF16)


def _mixer_in(h, mod, group_of_block, wt, wf, ttok, tfeat, gtok, gfeat, avg, b, lt):
    t, d = h.shape
    tb = TOKEN_BLOCK
    bpb = lt // tb
    n_groups = A_KV_HEADS + DIFF_HEADS
    kw = PIECES[1] + PIECES[4]
    row = lambda w: pl.BlockSpec((tb, w), lambda i: (i, 0))
    const = lambda a: pl.BlockSpec(a.shape, lambda i: (0,) * a.ndim)
    heads_shape = lambda n, dt: jax.ShapeDtypeStruct((b, n, lt, HEAD_DIM), dt)
    heads_spec = lambda n: pl.BlockSpec((None, n, tb, HEAD_DIM), lambda i: (i // bpb, 0, i % bpb, 0))
    return pl.pallas_call(
        _mixer_in_kernel,
        out_shape=(jax.ShapeDtypeStruct((b, lt, kw), BF16),
                   jax.ShapeDtypeStruct((b, n_groups, bpb, 2, 2 * HEAD_DIM, tb), BF16),
                   jax.ShapeDtypeStruct((b, n_groups, bpb, ATT_VPAD, tb), BF16),
                   heads_shape(WIN_HEADS, BF16), heads_shape(WIN_KV_HEADS, BF16), heads_shape(WIN_KV_HEADS, BF16),
                   heads_shape(RET_HEADS, F32), jax.ShapeDtypeStruct((b, RET_HEADS, RET_DK, lt), F32),
                   heads_shape(RET_HEADS, F32), jax.ShapeDtypeStruct((t, PIECES[12]), F32)),
        grid=(t // tb,),
        in_specs=[row(d),
                  pl.BlockSpec((None, 6, d), lambda i: (group_of_block(i), 0, 0)),
                  const(wt), const(wf),
                  pl.BlockSpec((tb, 2 * TTOK_W), lambda i: (i % bpb, 0)),
                  pl.BlockSpec((2 * TFEAT_W, tb), lambda i: (0, i % bpb)),
                  const(gtok), const(gfeat), const(avg)],
        out_specs=(pl.BlockSpec((None, tb, kw), lambda i: (i // bpb, i % bpb, 0)),
                   pl.BlockSpec((None, n_groups, None, 2, 2 * HEAD_DIM, tb),
                                lambda i: (i // bpb, 0, i % bpb, 0, 0, 0)),
                   pl.BlockSpec((None, n_groups, None, ATT_VPAD, tb), lambda i: (i // bpb, 0, i % bpb, 0, 0)),
                   heads_spec(WIN_HEADS), heads_spec(WIN_KV_HEADS), heads_spec(WIN_KV_HEADS),
                   heads_spec(RET_HEADS),
                   pl.BlockSpec((None, RET_HEADS, RET_DK, tb), lambda i: (i // bpb, 0, 0, i % bpb)),
                   heads_spec(RET_HEADS), row(PIECES[12])),
        compiler_params=_cparams(("parallel",)),
    )(h, mod, wt, wf, ttok, tfeat, gtok, gfeat, avg)


def _flash_kernel(*refs, n_ctx, tk, nt):
    q_refs, (k_ref, vt_ref, o_ref, s_ref, smax_ref, m_ref, acc_ref) = refs[:nt], refs[nt:]
    i = pl.program_id(2)
    tq = q_refs[0].shape[2]
    qt = jnp.concatenate([q[st] for st in range(2) for q in q_refs], axis=1)
    tile = vt_ref.shape[2]
    nlc = (k_ref.shape[0] - n_ctx) // tk

    def scores(row0, rows):
        return jnp.dot(k_ref[pl.ds(row0, rows), :], qt, preferred_element_type=F32)

    def stage(slot, chunk):
        s = scores(lat_row(chunk), tk)
        s_ref[slot] = s
        smax_ref[slot] = jnp.max(s, axis=0, keepdims=True)

    def absorb(s, smax, tile0):
        m = m_ref[...]
        m_new = jnp.maximum(m, smax)
        p = jnp.exp2(s - m_new).astype(BF16)
        acc = acc_ref[...] * jnp.exp2(m - m_new)
        for j in range(s.shape[0] // tile):
            acc = acc + jnp.dot(vt_ref[tile0 + j], p[j * tile:(j + 1) * tile], preferred_element_type=F32)
        m_ref[...] = m_new
        acc_ref[...] = acc

    m_ref[...] = jnp.full(m_ref.shape, -jnp.inf, F32)
    acc_ref[...] = jnp.zeros(acc_ref.shape, F32)
    s_ctx = scores(0, n_ctx)
    absorb(s_ctx, jnp.max(s_ctx, axis=0, keepdims=True), 0)
    lat_row = lambda c: pl.multiple_of(n_ctx + c * tk, tile)
    lat_tile = lambda c: (n_ctx + c * tk) // tile

    @pl.when(i > 0)
    def _():
        stage(0, 0)
        unroll = ATT_UNROLL

        def trip(j, carry):
            for e in range(unroll):
                c = unroll * j + e
                stage((e + 1) % 2, c + 1)
                absorb(s_ref[e % 2], smax_ref[e % 2], lat_tile(c))
            return carry

        lax.fori_loop(0, nlc // unroll - 1, trip, 0)
        for c in range(nlc - unroll, nlc):
            if c + 1 < nlc:
                stage((c + 1) % 2, c + 1)
            absorb(s_ref[c % 2], smax_ref[c % 2], lat_tile(c))

    acc = acc_ref[...]
    o = acc[:HEAD_DIM] * (1.0 / acc[HEAD_DIM:HEAD_DIM + 1])
    for j in range(nt):
        o_ref[j] = jnp.concatenate([o[:, j * tq:(j + 1) * tq], o[:, (nt + j) * tq:(nt + j + 1) * tq]], axis=1)


def _flash(qt, k_all, vt, n_ctx):
    b, g, tiles, _, dk, tq = qt.shape
    nt = ATT_QTILES
    lt = k_all.shape[1]
    s = lt - n_ctx
    tk = next(c for c in (ATT_TK, 512, 256) if s % (ATT_UNROLL * c) == 0)
    assert n_ctx == tq and tk % tq == 0 and (tiles - 1) % nt == 0
    steps = 1 + (tiles - 1) // nt
    key_block = lambda gi: jnp.where(gi < A_KV_HEADS, 0, 1 + (gi - A_KV_HEADS) // 2)
    qspec = lambda j: pl.BlockSpec(
        (None, None, None, 2, dk, tq),
        lambda bi, gi, i: (bi, gi, jnp.where(i == 0, 0, nt * i - (nt - 1) + j), 0, 0, 0))
    return pl.pallas_call(
        functools.partial(_flash_kernel, n_ctx=n_ctx, tk=tk, nt=nt),
        out_shape=jax.ShapeDtypeStruct((b, g, steps, nt, HEAD_DIM, 2 * tq), F32),
        grid=(b, g, steps),
        in_specs=[qspec(j) for j in range(nt)]
        + [pl.BlockSpec((None, lt, dk), lambda bi, gi, i: (bi, 0, key_block(gi))),
           pl.BlockSpec((None, None) + vt.shape[2:], lambda bi, gi, i: (bi, gi, 0, 0, 0))],
        out_specs=pl.BlockSpec((None, None, None, nt, HEAD_DIM, 2 * tq), lambda bi, gi, i: (bi, gi, i, 0, 0, 0)),
        scratch_shapes=[pltpu.VMEM((2, tk, 2 * nt * tq), F32), pltpu.VMEM((2, 1, 2 * nt * tq), F32),
                        pltpu.VMEM((1, 2 * nt * tq), F32), pltpu.VMEM((vt.shape[3], 2 * nt * tq), F32)],
        compiler_params=_cparams(("parallel", "parallel", "arbitrary")),
    )(*([qt] * nt), k_all, vt)


def _window_kernel(q_ref, kp_ref, kc_ref, kn_ref, vp_ref, vc_ref, vn_ref, kx_ref, vx_ref, sink_ref,
                   o_ref, *, n_ctx_blocks, n_blocks):
    step = pl.program_id(1)
    n_kv = kc_ref.shape[0]
    group = q_ref.shape[0] // n_kv
    rows = group * WINDOW
    nt = (((1,), (1,)), ((), ()))
    sdot = functools.partial(lax.dot_general, dimension_numbers=nt, preferred_element_type=F32)
    pv = functools.partial(jnp.dot, preferred_element_type=F32)
    qi = lax.broadcasted_iota(jnp.int32, (rows, WINDOW), 0) % WINDOW
    kj = lax.broadcasted_iota(jnp.int32, (rows, WINDOW), 1)
    neg = -jnp.inf
    for g in range(n_kv):
        sink = sink_ref[g]
        kx, vx = kx_ref[g], vx_ref[g]
        for j in range(2):
            qb = 2 * step + j
            cur = slice(j * WINDOW, (j + 1) * WINDOW)
            q = q_ref[g * group:(g + 1) * group, cur, :].reshape(rows, HEAD_DIM)
            if j == 0:
                kp, vp, kn, vn = kp_ref[g], vp_ref[g], kc_ref[g, WINDOW:, :], vc_ref[g, WINDOW:, :]
            else:
                kp, vp, kn, vn = kc_ref[g, :WINDOW, :], vc_ref[g, :WINDOW, :], kn_ref[g], vn_ref[g]
            off_p = jnp.where(qb >= n_ctx_blocks + 1, 0, 2 * WINDOW)
            off_c = jnp.where(qb >= n_ctx_blocks, 0, 2 * WINDOW)
            off_n = jnp.where(jnp.logical_and(qb >= n_ctx_blocks, qb <= n_blocks - 2), 0, 2 * WINDOW)
            s_p = jnp.where(kj >= qi + off_p, sdot(q, kp), neg)
            s_c = jnp.where(kj >= off_c, sdot(q, kc_ref[g, cur, :]), neg)
            s_n = jnp.where(kj <= qi - off_n, sdot(q, kn), neg)
            s_x = sdot(q, kx)
            rmax = lambda s: jnp.max(s, axis=1, keepdims=True)
            m = jnp.maximum(jnp.maximum(jnp.maximum(rmax(s_p), rmax(s_c)), jnp.maximum(rmax(s_n), rmax(s_x))), sink)
            e_p, e_c, e_n, e_x = (jnp.exp2(s - m) for s in (s_p, s_c, s_n, s_x))
            rsum = lambda e: jnp.sum(e, axis=1, keepdims=True)
            den = rsum(e_p) + rsum(e_c) + rsum(e_n) + rsum(e_x) + jnp.exp2(sink - m)
            o = (pv(e_p.astype(BF16), vp) + pv(e_c.astype(BF16), vc_ref[g, cur, :])
                 + pv(e_n.astype(BF16), vn) + pv(e_x.astype(BF16), vx))
            o_ref[g, j] = o / den


def _window(q, k, v, sink, n_ctx):
    b, hq, lt, dh = q.shape
    g = k.shape[1]
    blk = WINDOW
    nb = lt // blk
    rows = (hq // g) * blk
    n_ctx_blocks = n_ctx // blk
    assert n_ctx % (2 * blk) == 0 and nb % 2 == 0
    lo, hi = n_ctx_blocks, nb - 1
    pair_spec = lambda heads: pl.BlockSpec((None, heads, 2 * blk, dh), lambda bi, i: (bi, 0, i, 0))
    side_spec = lambda delta: pl.BlockSpec((None, g, blk, dh),
                                           lambda bi, i: (bi, 0, jnp.clip(2 * i + delta, lo, hi), 0))
    ctx_spec = pl.BlockSpec((None, g, n_ctx, dh), lambda bi, i: (bi, 0, 0, 0))
    return pl.pallas_call(
        functools.partial(_window_kernel, n_ctx_blocks=n_ctx_blocks, n_blocks=nb),
        out_shape=jax.ShapeDtypeStruct((b, g, nb, rows, dh), F32),
        grid=(b, nb // 2),
        in_specs=[pair_spec(hq), side_spec(-1), pair_spec(g), side_spec(2), side_spec(-1), pair_spec(g),
                  side_spec(2), ctx_spec, ctx_spec, pl.BlockSpec((g, rows, 1), lambda bi, i: (0, 0, 0))],
        out_specs=pl.BlockSpec((None, g, 2, rows, dh), lambda bi, i: (bi, 0, i, 0, 0)),
        compiler_params=_cparams(("parallel", "arbitrary")),
    )(q, k, k, k, v, v, v, k, v, sink)


def _retention_kernel(qf_ref, ktf_ref, vf_ref, qb_ref, ktb_ref, vb_ref, dmat_ref, xi_ref, zeta_ref, gch_ref,
                      of_ref, ob_ref, st_ref):
    t = pl.program_id(0)

    @pl.when(t == 0)
    def _():
        st_ref[...] = jnp.zeros_like(st_ref)

    for d, (q_ref, kt_ref, v_ref, o_ref) in enumerate(((qf_ref, ktf_ref, vf_ref, of_ref),
                                                       (qb_ref, ktb_ref, vb_ref, ob_ref))):
        for bi in range(q_ref.shape[0]):
            for hd in range(RET_HEADS):
                q = q_ref[bi, hd]
                kt = kt_ref[bi, hd]
                v = v_ref[bi, hd]
                st = st_ref[d, bi, hd]
                inner = _dot3(q, kt) * dmat_ref[d, hd]
                o_ref[bi, hd] = _dot3(inner, v) + _dot3(q, st) * xi_ref[d, hd]
                st_ref[d, bi, hd] = st * gch_ref[d, hd] + _dot3(kt * zeta_ref[d, hd], v)


def _retention_call(q, kt, v, dmat, xi, zeta, gch, n_ctx_chunks):
    b, hh, lt, dk = q.shape
    dv = v.shape[-1]
    c = RET_CHUNK
    nch = lt // c

    def back(ti):
        return jnp.where(ti < n_ctx_chunks, n_ctx_chunks - 1 - ti, nch - 1 - (ti - n_ctx_chunks))

    fwd = lambda ti: ti
    rows = lambda blk, w: pl.BlockSpec((b, hh, c, w), lambda ti: (0, 0, blk(ti), 0))
    cols = lambda blk: pl.BlockSpec((b, hh, dk, c), lambda ti: (0, 0, 0, blk(ti)))
    tab = lambda a: pl.BlockSpec(a.shape, lambda ti: (0,) * a.ndim)
    out = jax.ShapeDtypeStruct((b, hh, lt, dv), F32)
    return pl.pallas_call(
        _retention_kernel,
        out_shape=(out, out),
        grid=(nch,),
        in_specs=[rows(fwd, dk), cols(fwd), rows(fwd, dv), rows(back, dk), cols(back), rows(back, dv),
                  tab(dmat), tab(xi), tab(zeta), tab(gch)],
        out_specs=(rows(fwd, dv), rows(back, dv)),
        scratch_shapes=[pltpu.VMEM((2, b, hh, dk, dv), F32)],
        compiler_params=_cparams(("arbitrary",)),
    )(q, kt, v, q, kt, v, dmat, xi, zeta, gch)


def _layer_norm_rows(z, ln):
    mu = jnp.mean(z, axis=-1, keepdims=True)
    zc = z - mu
    var = jnp.mean(zc * zc, axis=-1, keepdims=True)
    return zc * lax.rsqrt(var + LN_EPS) * ln[0:1, :] + ln[1:2, :]


def _merge_kernel(fa_ref, fd0_ref, fd1_ref, win_ref, retf_ref, retb_ref, rg_ref, g_ref, h_ref, mod_ref, lam_ref,
                  subln_ref, rnorm_ref, wb_ref, wo_ref, ln_ref, o_ref, *, alpha):
    d = h_ref.shape[1]
    tb = h_ref.shape[0]
    hd = HEAD_DIM
    proj = functools.partial(jnp.dot, preferred_element_type=F32)
    gate = lambda i: g_ref[:, i * d:(i + 1) * d].astype(F32)

    oat = jnp.concatenate([fa_ref[g][:, st * tb:(st + 1) * tb] for g in range(A_KV_HEADS) for st in range(2)],
                          axis=0)
    m = gate(0) * proj(oat.T.astype(BF16), wb_ref[0])

    lam = lam_ref[...]
    heads = []
    for h4 in range(DIFF_HEADS):
        f = (fd0_ref if h4 < 2 else fd1_ref)[h4 % 2]
        o = f[:, :tb] - lam * f[:, tb:]
        heads.append(o * lax.rsqrt(jnp.mean(o * o, axis=0, keepdims=True) + RMS_EPS))
    obt = jnp.concatenate(heads, axis=0) * subln_ref[...]
    m = m + gate(1) * proj(obt.T.astype(BF16), wb_ref[1])

    half = tb // 2
    acc = None
    for h4 in range(WIN_HEADS):
        g, st = h4 // 2, h4 % 2
        o = jnp.concatenate([win_ref[g, 0][st * half:(st + 1) * half], win_ref[g, 1][st * half:(st + 1) * half]],
                            axis=0)
        t = proj(o.astype(BF16), wb_ref[2, h4 * hd:(h4 + 1) * hd, :])
        acc = t if acc is None else acc + t
    m = m + gate(2) * acc

    acc = None
    for h4 in range(RET_HEADS):
        cols = slice(h4 * RET_DV, (h4 + 1) * RET_DV)
        o = retf_ref[h4] + retb_ref[h4]
        mu = jnp.mean(o, axis=-1, keepdims=True)
        oc = o - mu
        var = jnp.mean(oc * oc, axis=-1, keepdims=True)
        on = oc * lax.rsqrt(var + LN_EPS) * rnorm_ref[0:1, cols] + rnorm_ref[1:2, cols]
        gt = rg_ref[:, cols]
        t = proj((on * (gt / (1.0 + jnp.exp(-gt)))).astype(BF16), wb_ref[3, cols, :])
        acc = t if acc is None else acc + t
    m = m + gate(3) * acc

    y = proj(m.astype(BF16), wo_ref[...])
    z = alpha * h_ref[...] + mod_ref[2:3, :] * y
    o_ref[...] = _layer_norm_rows(z, ln_ref[...])


def _merge(flash_out, win_out, ret_out, rg, gates, h, mod, group_of_block, lam, subln, rnorm, wb, wo, ln,
           alpha, b, lt):
    t, d = h.shape
    tb = TOKEN_BLOCK
    bpb = lt // tb
    row = lambda n: pl.BlockSpec((tb, n), lambda i: (i, 0))
    const = lambda a: pl.BlockSpec(a.shape, lambda i: (0,) * a.ndim)
    fspec = lambda gb: pl.BlockSpec(
        (None, 2, None, None) + flash_out.shape[4:],
        lambda i: (i // bpb, gb, (i % bpb + ATT_QTILES - 1) // ATT_QTILES, (i % bpb + ATT_QTILES - 1) % ATT_QTILES,
                   0, 0))
    ret_spec = pl.BlockSpec((None, RET_HEADS, tb, RET_DV), lambda i: (i // bpb, 0, i % bpb, 0))
    return pl.pallas_call(
        functools.partial(_merge_kernel, alpha=alpha),
        out_shape=jax.ShapeDtypeStruct((t, d), F32),
        grid=(t // tb,),
        in_specs=[fspec(0), fspec(1), fspec(2),
                  pl.BlockSpec((None, WIN_KV_HEADS, 2) + win_out.shape[3:], lambda i: (i // bpb, 0, i % bpb, 0, 0)),
                  ret_spec, ret_spec, row(rg.shape[1]), row(N_BRANCH * d), row(d),
                  pl.BlockSpec((None, 6, d), lambda i: (group_of_block(i), 0, 0)),
                  const(lam), const(subln), const(rnorm), const(wb), const(wo), const(ln)],
        out_specs=row(d),
        compiler_params=_cparams(("parallel",)),
    )(flash_out, flash_out, flash_out, win_out, *ret_out, rg, gates, h, mod, lam, subln, rnorm, wb, wo, ln)


def _top_rows(s, n, with_rank=False):
    out = []
    cur = s
    rank = jnp.full(s.shape, float(n), F32) if with_rank else None
    for r in range(n):
        mx = jnp.max(cur, axis=0, keepdims=True)
        out.append(mx)
        if with_rank:
            rank = jnp.where(cur == mx, float(r), rank)
        if r + 1 < n:
            cur = jnp.where(cur == mx, -jnp.inf, cur)
    return (out, rank) if with_rank else out


def _peer_route_kernel(h_ref, mod_ref, wh_ref, wl_ref, sk_ref, xt_ref, rk_ref, b1_ref, nn_ref, az_ref,
                       cand_ref):
    m = mod_ref[...]
    u = h_ref[...] * (1.0 + m[4:5, :]) + m[3:4, :]
    xt_ref[...] = u.T.astype(BF16)
    uh, ul = _split_bf16(u)
    d = functools.partial(jnp.dot, preferred_element_type=F32)
    nk = PEER_NK
    nt = (((1,), (1,)), ((), ()))
    k1 = PEER_TOPK + 1
    for hd in range(PEER_HEADS):
        c0 = 2 * hd * PEER_DQ
        wh = wh_ref[:, c0:c0 + 2 * PEER_DQ]
        wl = wl_ref[:, c0:c0 + 2 * PEER_DQ]
        q = d(uh, wh) + (d(ul, wh) + d(uh, wl))
        st = [_dot3(sk_ref[hd, p], q[:, p * PEER_DQ:(p + 1) * PEER_DQ], nt) for p in range(2)]
        top0 = _top_rows(st[0], k1)
        top1, rank1 = _top_rows(st[1], k1, with_rank=True)
        r = 0
        for p0 in range(k1):
            for p1 in range(k1 // (p0 + 1)):
                cand_ref[r:r + 1, :] = top0[p0] + top1[p1]
                r += 1
        cand_ref[r:, :] = jnp.full((cand_ref.shape[0] - r, cand_ref.shape[1]), -jnp.inf, F32)
        cand = cand_ref[...]
        ctop = _top_rows(cand, k1)
        tau = 0.5 * (ctop[PEER_TOPK - 1] + ctop[PEER_TOPK])
        mx = top0[0] + top1[0]
        z = jnp.sum(jnp.where(cand >= tau, jnp.exp(cand - mx), 0.0), axis=0, keepdims=True)
        th = tau - st[0]
        nn = jnp.zeros_like(th)
        for q in range(PEER_TOPK):
            nn = nn + jnp.where(top1[q] >= th, 1.0, 0.0)
        rk_ref[hd] = rank1.astype(BF16)
        b1_ref[hd] = jnp.exp(st[1] - top1[0]).astype(BF16)
        nn_ref[hd] = nn
        az_ref[hd] = jnp.exp(st[0] - top0[0]) / z


def _peer_route(h, mod, group_of_block, wq_hi, wq_lo, subkeys):
    t, d = h.shape
    tb = TOKEN_BLOCK
    hh, nk = PEER_HEADS, PEER_NK
    st_shape = lambda dt: jax.ShapeDtypeStruct((t // tb, hh, nk, tb), dt)
    st_spec = pl.BlockSpec((None, hh, nk, tb), lambda i: (i, 0, 0, 0))
    nq = wq_hi.shape[1]
    return pl.pallas_call(
        _peer_route_kernel,
        out_shape=(jax.ShapeDtypeStruct((t // tb, d, tb), BF16), st_shape(BF16), st_shape(BF16), st_shape(F32),
                   st_shape(F32)),
        grid=(t // tb,),
        in_specs=[pl.BlockSpec((tb, d), lambda i: (i, 0)),
                  pl.BlockSpec((None, 6, d), lambda i: (group_of_block(i), 0, 0)),
                  pl.BlockSpec((d, nq), lambda i: (0, 0)),
                  pl.BlockSpec((d, nq), lambda i: (0, 0)),
                  pl.BlockSpec((hh, 2, nk, PEER_DQ), lambda i: (0, 0, 0, 0))],
        out_specs=(pl.BlockSpec((None, d, tb), lambda i: (i, 0, 0)), st_spec, st_spec, st_spec, st_spec),
        scratch_shapes=[pltpu.VMEM((PEER_CAND_ROWS, tb), F32)],
        compiler_params=_cparams(("parallel",)),
    )(h, mod, wq_hi, wq_lo, subkeys)


GELU_K1 = -2.0 * math.sqrt(2.0 / math.pi) * LOG2E
GELU_K2 = GELU_K1 * 0.044715


def _gelu_tanh(x):
    return x / (1.0 + jnp.exp2(x * (GELU_K1 + GELU_K2 * (x * x))))


def _peer_dense_kernel(xt_ref, u_ref, vt_ref, vtl_ref, rk_ref, b1_ref, nn_ref, az_ref, yt_ref, g_ref):
    c = pl.program_id(1)
    nk = PEER_NK
    nsub, _, tb = xt_ref.shape
    rows_per_step = u_ref.shape[0] // nk

    @pl.when(c == 0)
    def _():
        yt_ref[...] = jnp.zeros_like(yt_ref)
        g_ref[...] = jnp.zeros_like(g_ref)

    def sub_block(sb, carry):
        xt = xt_ref[sb]
        pre = lambda ii: jnp.dot(u_ref[ii * nk:(ii + 1) * nk, :], xt, preferred_element_type=F32)
        yt_ref[sb] += jnp.dot(vt_ref[...], g_ref[sb], preferred_element_type=F32)
        act_next = pre(0)
        for ii in range(rows_per_step):
            act = act_next
            if ii + 1 < rows_per_step:
                act_next = pre(ii + 1)
            w = None
            tile = (nk // BF16_ROWS, BF16_ROWS, tb)
            row = lambda ref, hd: jnp.broadcast_to(ref[sb, hd, ii:ii + 1, :], (BF16_ROWS, tb)).astype(BF16)[None]
            for hd in range(PEER_HEADS):
                t = jnp.where(rk_ref[sb, hd].reshape(tile) < row(nn_ref, hd), b1_ref[sb, hd].reshape(tile),
                              jnp.zeros((), BF16)) * row(az_ref, hd)
                w = t if w is None else w + t
            g_ref[sb, ii * nk:(ii + 1) * nk, :] = w.reshape(nk, tb) * _gelu_tanh(act.astype(BF16))
        return carry

    lax.fori_loop(0, nsub, sub_block, 0)

    @pl.when(c == pl.num_programs(1) - 1)
    def _():
        for sb in range(nsub):
            yt_ref[sb] += jnp.dot(vtl_ref[...], g_ref[sb], preferred_element_type=F32)


def _peer_dense(xt, u, vt, rk, b1, nn, az):
    nblk, d, tb = xt.shape
    n = u.shape[0]
    nsub = PEER_TOKENS // tb
    ec = PEER_EXPERTS
    nc = n // ec
    hh, nk = PEER_HEADS, PEER_NK
    st_spec = pl.BlockSpec((nsub, hh, nk, tb), lambda i, c: (i, 0, 0, 0))
    row_spec = pl.BlockSpec((nsub, hh, ec // nk, tb), lambda i, c: (i, 0, c, 0))
    return pl.pallas_call(
        _peer_dense_kernel,
        out_shape=jax.ShapeDtypeStruct((nblk, d, tb), F32),
        grid=(nblk // nsub, nc),
        in_specs=[pl.BlockSpec((nsub, d, tb), lambda i, c: (i, 0, 0)),
                  pl.BlockSpec((ec, d), lambda i, c: (c, 0)),
                  pl.BlockSpec((d, ec), lambda i, c: (0, jnp.maximum(c - 1, 0))),
                  pl.BlockSpec((d, ec), lambda i, c: (0, nc - 1)),
                  st_spec, st_spec, row_spec, row_spec],
        out_specs=pl.BlockSpec((nsub, d, tb), lambda i, c: (i, 0, 0)),
        scratch_shapes=[pltpu.VMEM((nsub, ec, tb), BF16)],
        compiler_params=_cparams(("parallel", "arbitrary")),
    )(xt, u, vt, vt, rk, b1, nn, az)


def _resid_ln_kernel(h_ref, yt_ref, mod_ref, ln_ref, o_ref, *, alpha, gate_row):
    z = alpha * h_ref[...] + mod_ref[gate_row:gate_row + 1, :] * yt_ref[...].T
    o_ref[...] = _layer_norm_rows(z, ln_ref[...])


def _resid_ln(h, yt, mod, group_of_block, ln, alpha, gate_row):
    t, d = h.shape
    tb = TOKEN_BLOCK
    row = pl.BlockSpec((tb, d), lambda i: (i, 0))
    return pl.pallas_call(
        functools.partial(_resid_ln_kernel, alpha=alpha, gate_row=gate_row),
        out_shape=jax.ShapeDtypeStruct((t, d), F32),
        grid=(t // tb,),
        in_specs=[row, pl.BlockSpec((None, d, tb), lambda i: (i, 0, 0)),
                  pl.BlockSpec((None, 6, d), lambda i: (group_of_block(i), 0, 0)),
                  pl.BlockSpec((2, d), lambda i: (0, 0))],
        out_specs=row,
        compiler_params=_cparams(("parallel",)),
    )(h, yt, mod, ln)


def _axial_tables(s, n_ctx, d):
    rows = s // GRID_W
    row = jnp.broadcast_to(jnp.arange(rows, dtype=F32)[:, None], (rows, GRID_W)).reshape(-1)
    col = jnp.broadcast_to(jnp.arange(GRID_W, dtype=F32)[None, :], (rows, GRID_W)).reshape(-1)
    quarter = d // 4
    inv = ROPE_THETA ** (-jnp.arange(quarter, dtype=F32) / quarter)
    ar, ac = row[:, None] * inv, col[:, None] * inv
    cos = jnp.concatenate([jnp.cos(ar), jnp.cos(ar), jnp.cos(ac), jnp.cos(ac)], axis=-1)
    sin = jnp.concatenate([-jnp.sin(ar), jnp.sin(ar), -jnp.sin(ac), jnp.sin(ac)], axis=-1)
    cos = jnp.concatenate([jnp.ones((n_ctx, d), F32), cos], axis=0)
    sin = jnp.concatenate([jnp.zeros((n_ctx, d), F32), sin], axis=0)
    return cos, sin


def _rope1d_tables(lt, d):
    half = d // 2
    inv = ROPE_THETA ** (-jnp.arange(half, dtype=F32) / half)
    ang = jnp.arange(lt, dtype=F32)[:, None] * inv
    return (jnp.concatenate([jnp.cos(ang), jnp.cos(ang)], axis=-1),
            jnp.concatenate([-jnp.sin(ang), jnp.sin(ang)], axis=-1))


def kernel(x, c, ctx, c_ctx, w_mod, b_mod, w_in, qk_gain, diff_lambda, diff_subln, win_sink, ret_decay,
           ret_norm, w_branch, w_out, ln_attn, ln_ffn, peer_wq, peer_subkeys, peer_u, peer_v):
    b, s, d = x.shape
    n_ctx = ctx.shape[1]
    depth = w_mod.shape[0]
    lt = n_ctx + s
    t = b * lt
    tb = TOKEN_BLOCK
    assert n_ctx % tb == 0 and s % tb == 0 and t % PEER_TOKENS == 0
    alpha = (2 * depth) ** 0.25
    blocks_per_batch = lt // tb
    ctx_blocks = n_ctx // tb

    def group_of_block(i):
        return jnp.where(i % blocks_per_batch < ctx_blocks, b, i // blocks_per_batch)

    cos64, sin64 = _axial_tables(s, n_ctx, HEAD_DIM)
    cos32, sin32 = _axial_tables(s, n_ctx, DIFF_DIM)
    cos1d, sin1d = _rope1d_tables(lt, RET_DK)
    sc_a = HEAD_DIM ** -0.5 * LOG2E
    sc_d = DIFF_DIM ** -0.5 * LOG2E
    rep = lambda a, n: jnp.tile(a, (1, n))
    tok_part = lambda t64, t32, t1d: [rep(t64, 2), rep(t32, 8), rep(t64, 4) * sc_a, rep(t64, 2), rep(t1d, 4)]
    feat_part = lambda t64, t32, t1d: [rep(t64, 4) * sc_a, rep(t32, 8) * sc_d, rep(t1d, 4) * RET_DK ** -0.5]
    ttok = jnp.concatenate(tok_part(cos64, cos32, cos1d) + tok_part(sin64, sin32, sin1d), axis=1)
    tfeat = jnp.concatenate(feat_part(cos64, cos32, cos1d) + feat_part(sin64, sin32, sin1d), axis=1).T
    swap64 = np.arange(HEAD_DIM) ^ 16
    head_ids = np.arange(2 * HEAD_DIM) // HEAD_DIM
    avg = jnp.asarray((head_ids[:, None] == head_ids[None, :]) / HEAD_DIM, BF16)
    cond8 = jnp.zeros((8, d), F32).at[:b].set(jax.nn.silu(c)).at[b].set(jax.nn.silu(c_ctx))

    h = jnp.concatenate([ctx, x], axis=1).reshape(t, d)

    for l in range(depth):
        mod = _modulation(cond8, w_mod[l], b_mod[l]).reshape(8, 6, d)
        w_mix = w_in[l, :, :MIX_COLS]
        w_gate = w_in[l, :, MIX_COLS:].astype(BF16)
        g0, g1 = qk_gain[l, 0].astype(F32), qk_gain[l, 1].astype(F32)
        gtok = jnp.stack([jnp.tile(g1, 2), jnp.tile(g1[swap64], 2)])
        gfeat = jnp.broadcast_to(jnp.concatenate([jnp.tile(g0, A_HEADS), jnp.tile(g0[swap64], A_HEADS)])[:, None],
                                 (2 * A_HEADS * HEAD_DIM, tb))
        kall, qt, vt, wq, wk, wv, rq, rkt, rv, rg = _mixer_in(
            h, mod, group_of_block, w_mix[:, TOK_COLS].astype(BF16), w_mix[:, FEAT_COLS].T.astype(BF16),
            ttok, tfeat, gtok, gfeat, avg, b, lt)
        gates = _gates(h, mod, group_of_block, w_gate)

        ot = _flash(qt, kall, vt, n_ctx)
        lam_init = 0.8 - 0.6 * math.exp(-0.3 * l)
        lp = diff_lambda[l].astype(F32)
        lam = (jnp.exp(jnp.sum(lp[0] * lp[1])) - jnp.exp(jnp.sum(lp[2] * lp[3])) + lam_init).reshape(1, 1)
        subln = jnp.broadcast_to((jnp.tile(diff_subln[l].astype(F32), DIFF_HEADS) * (1.0 - lam_init))[:, None],
                                 (DIFF_HEADS * HEAD_DIM, tb))

        sink = jnp.repeat(win_sink[l].astype(F32) * LOG2E, WINDOW).reshape(WIN_KV_HEADS, 2 * WINDOW, 1)
        ow = _window(wq, wk, wv, sink, n_ctx)

        lg = jax.nn.log_sigmoid(ret_decay[l].astype(F32))
        idx = jnp.arange(RET_CHUNK, dtype=F32)
        diff = idx[:, None] - idx[None, :]
        lg3 = lg[:, :, None, None]
        dm_f = jnp.exp(jnp.where(diff >= 0, diff * lg3[0], -jnp.inf))
        dm_b = jnp.exp(jnp.where(diff <= 0, -diff * lg3[1], -jnp.inf))
        dmat = jnp.stack([dm_f, dm_b])
        xi = jnp.stack([jnp.exp((idx + 1.0) * lg[0][:, None]), jnp.exp((RET_CHUNK - idx) * lg[1][:, None])])
        zeta = jnp.stack([jnp.exp((RET_CHUNK - 1.0 - idx) * lg[0][:, None]), jnp.exp(idx * lg[1][:, None])])
        gch = jnp.exp(RET_CHUNK * lg)
        o_ret = _retention_call(rq, rkt, rv, dmat, xi[..., None], zeta[:, :, None, :],
                                gch[:, :, None, None], n_ctx // RET_CHUNK)

        h = _merge(ot, ow, o_ret, rg, gates, h, mod, group_of_block, lam, subln, ret_norm[l].astype(F32),
                   w_branch[l].astype(BF16), w_out[l].astype(BF16), ln_attn[l], alpha, b, lt)

        wq_hi, wq_lo = _split_bf16(peer_wq[l])
        xt, rk, b1, nn, az = _peer_route(h, mod, group_of_block, wq_hi, wq_lo, peer_subkeys[l])
        yt = _peer_dense(xt, peer_u[l].astype(BF16), peer_v[l].T.astype(BF16), rk, b1, nn, az)
        h = _resid_ln(h, yt, mod, group_of_block, ln_ffn[l], alpha, 5)

    return h.reshape(b, lt, d)[:, n_ctx:, :]
```
